```python
import math
import jax
import jax.numpy as jnp
from jax import lax
import numpy as np

D_MODEL = 1024
BATCH = 4
SEQ = 8192
DEPTH = 2

A_HEADS = 4
A_DK = 64
A_DV = 128
B_HEADS = 4
B_DK = 128
B_DV = 128
B_CONV = 5
C_WIDTH = 512
C_CONV = 31
D_HEADS = 8
D_DH = 64
D_GROUPS = ((128, 1), (512, 4), (2048, 16))
REL_BUCKETS = 32
REL_MAX_DIST = 1024
CHUNK = 64
EPS = 1e-6
NEG = -1e30
MIX_EVEN = A_HEADS * A_DV + B_HEADS * B_DV
MIX_ODD = C_WIDTH + D_HEADS * D_DH
B_QKV = B_HEADS * (2 * B_DK + B_DV)
EVEN_SPLITS = (A_HEADS * A_DK, A_HEADS * A_DK, A_HEADS * A_DV, A_HEADS * A_DV, 4 * A_HEADS, B_QKV, 4 * B_HEADS, MIX_EVEN)
ODD_SPLITS = (C_WIDTH, C_WIDTH, D_HEADS * D_DH, D_HEADS * D_DH, D_HEADS * D_DH, MIX_ODD)
EVEN_IN = sum(EVEN_SPLITS)
ODD_IN = sum(ODD_SPLITS)
N_EVEN = (DEPTH + 1) // 2
N_ODD = DEPTH // 2

kernel_name = 'bidir_hybrid_mlstm_gdn_conformer_dilated'


def _split(p, sizes):
    return jnp.split(p, np.cumsum(sizes)[:-1].tolist(), axis=-1)


def _rms(x, g):
    xf = x.astype(jnp.float32)
    y = xf * lax.rsqrt(jnp.mean(xf * xf, axis=-1, keepdims=True) + EPS)
    return (y * g.astype(jnp.float32)).astype(x.dtype)


def _layernorm(x, g, b):
    xf = x.astype(jnp.float32)
    xc = xf - jnp.mean(xf, axis=-1, keepdims=True)
    y = xc * lax.rsqrt(jnp.mean(xc * xc, axis=-1, keepdims=True) + EPS)
    return (y * g.astype(jnp.float32) + b.astype(jnp.float32)).astype(x.dtype)


def _head_rms(t, g):
    bsz, s, h, d = t.shape
    y = t * lax.rsqrt(jnp.mean(t * t, axis=-1, keepdims=True) + EPS)
    return y.reshape(bsz, s, h * d) * g.astype(jnp.float32)


def _l2n(t):
    return t * lax.rsqrt(jnp.sum(t * t, axis=-1, keepdims=True) + EPS)


def _dwconv(x, w):
    return lax.conv_general_dilated(x, w[:, None, :].astype(x.dtype), window_strides=(1,), padding='SAME', dimension_numbers=('NWC', 'WIO', 'NWC'), feature_group_count=x.shape[-1])


def _flip(t):
    return jnp.flip(t, axis=1)


def _to_chunks(t):
    bsz, s, h = t.shape[:3]
    t = t.reshape((bsz, s // CHUNK, CHUNK, h) + t.shape[3:])
    return jnp.moveaxis(t, (1, 3), (0, 2))


def _from_chunks(t):
    nc, bsz, h, l = t.shape[:4]
    t = jnp.moveaxis(t, (0, 2), (1, 3))
    return t.reshape((bsz, nc * l, h) + t.shape[4:])


def _mlstm_chunkwise(q, k, v, i_pre, logf):
    q, k, v, i_pre, logf = (_to_chunks(t) for t in (q, k, v, i_pre, logf))
    nc, bsz, h = q.shape[:3]
    causal = jnp.tril(jnp.ones((CHUNK, CHUNK), dtype=bool))
    b = jnp.cumsum(logf, axis=-1)
    dmat = jnp.where(causal, b[..., :, None] - b[..., None, :] + i_pre[..., None, :], -jnp.inf)
    dmax = jnp.max(dmat, axis=-1)
    qk = jnp.einsum('nbhld,nbhsd->nbhls', q, k)
    a_end = b[..., -1:] - b + i_pre

    def step(carry, xs):
        cmat, nvec, m = carry
        qc, kc, vc, bc, dc, dmc, qkc, aec = xs
        inter = bc + m[..., None]
        mt = jnp.maximum(inter, dmc)
        w_int = jnp.exp(inter - mt)
        sc = jnp.exp(dc - mt[..., None]) * qkc
        num = w_int[..., None] * jnp.einsum('bhld,bhde->bhle', qc, cmat) + jnp.einsum('bhls,bhse->bhle', sc, vc)
        den = w_int * jnp.einsum('bhld,bhd->bhl', qc, nvec) + jnp.sum(sc, axis=-1)
        hc = num / jnp.maximum(jnp.abs(den), jnp.exp(-mt))[..., None]
        m_new = jnp.maximum(bc[..., -1] + m, jnp.max(aec, axis=-1))
        w_old = jnp.exp(bc[..., -1] + m - m_new)
        kw = kc * jnp.exp(aec - m_new[..., None])[..., None]
        cmat = w_old[..., None, None] * cmat + jnp.einsum('bhld,bhle->bhde', kw, vc)
        nvec = w_old[..., None] * nvec + jnp.sum(kw, axis=-2)
        return (cmat, nvec, m_new), hc

    init = (jnp.zeros((bsz, h, A_DK, A_DV), jnp.float32), jnp.zeros((bsz, h, A_DK), jnp.float32), jnp.zeros((bsz, h), jnp.float32))
    _, hs = lax.scan(step, init, (q, k, v, b, dmat, dmax, qk, a_end))
    return _from_chunks(hs)


def _gdn_chunked(q, k, v, beta, g):
    q, k, v, beta, g = (_to_chunks(t) for t in (q, k, v, beta, g))
    nc, bsz, h = q.shape[:3]
    tril = jnp.tril(jnp.ones((CHUNK, CHUNK), dtype=bool))
    strict = jnp.tril(jnp.ones((CHUNK, CHUNK), dtype=bool), -1)
    gc = jnp.cumsum(g, axis=-1)
    gam = jnp.exp(jnp.where(tril, gc[..., :, None] - gc[..., None, :], -jnp.inf))
    a = jnp.where(strict, beta[..., :, None] * jnp.einsum('nbhid,nbhjd->nbhij', k, k) * gam, 0.0)
    tmat = a + jnp.eye(CHUNK, dtype=a.dtype)
    u = lax.linalg.triangular_solve(tmat, beta[..., None] * v, left_side=True, lower=True, unit_diagonal=True)
    w = lax.linalg.triangular_solve(tmat, (beta * jnp.exp(gc))[..., None] * k, left_side=True, lower=True, unit_diagonal=True)
    attn = jnp.einsum('nbhid,nbhjd->nbhij', q, k) * gam

    def step(state, xs):
        qc, kc, uc, wc, gcc, ac = xs
        v_new = uc - jnp.einsum('bhld,bhde->bhle', wc, state)
        o = jnp.einsum('bhld,bhde->bhle', qc * jnp.exp(gcc)[..., None], state) + jnp.einsum('bhls,bhse->bhle', ac, v_new)
        gl = gcc[..., -1]
        state = jnp.exp(gl)[..., None, None] * state + jnp.einsum('bhld,bhle->bhde', kc * jnp.exp(gl[..., None] - gcc)[..., None], v_new)
        return state, o

    _, os_ = lax.scan(step, jnp.zeros((bsz, h, B_DK, B_DV), jnp.float32), (q, k, u, w, gc, attn))
    return _from_chunks(os_)


def _t5_bucket(rel):
    half = REL_BUCKETS // 2
    exact = half // 2
    n = jnp.abs(rel)
    large = exact + (jnp.log(jnp.maximum(n, 1).astype(jnp.float32) / exact) / math.log(REL_MAX_DIST / exact) * (half - exact)).astype(jnp.int32)
    large = jnp.minimum(large, half - 1)
    return (rel > 0).astype(jnp.int32) * half + jnp.where(n < exact, n, large)


def _dilated_group(q, k, v, dilation, radius, rel_bias):
    bsz, s, h, dh = q.shape
    ls = s // dilation
    nb = -(-ls // radius)
    lp = nb * radius

    def sub(t, lo, hi):
        t = t.reshape(bsz, ls, dilation, h, dh).transpose(0, 3, 2, 1, 4)
        return jnp.pad(t, ((0, 0), (0, 0), (0, 0), (lo, hi), (0, 0)))

    qb = sub(q, 0, lp - ls).reshape(bsz, h, dilation, nb, radius, dh)

    def band(t):
        t = sub(t, radius, lp - ls + radius).reshape(bsz, h, dilation, nb + 2, radius, dh)
        return jnp.concatenate([t[:, :, :, :-2], t[:, :, :, 1:-1], t[:, :, :, 2:]], axis=4)

    kb, vb = band(k), band(v)
    qi = jnp.arange(radius)[:, None]
    kj = jnp.arange(3 * radius)[None, :]
    rel = kj - radius - qi
    kpos = jnp.arange(nb)[:, None, None] * radius + kj - radius
    valid = (jnp.abs(rel) <= radius) & (kpos >= 0) & (kpos < ls)
    bias = jnp.transpose(rel_bias[_t5_bucket(rel * dilation)], (2, 0, 1)).astype(jnp.float32)
    sc = jnp.einsum('bhrnid,bhrnjd->bhrnij', qb, kb).astype(jnp.float32) * (dh ** -0.5) + bias[:, None, None]
    sc = jnp.where(valid, sc, NEG)
    m = jnp.max(sc, axis=-1, keepdims=True)
    p = jnp.exp(sc - m)
    den = jnp.sum(p, axis=-1)
    o = jnp.einsum('bhrnij,bhrnjd->bhrnid', p, vb.astype(jnp.float32)) / den[..., None]
    lse = m[..., 0] + jnp.log(den)
    o = o.reshape(bsz, h, dilation, lp, dh)[:, :, :, :ls].transpose(0, 3, 2, 1, 4).reshape(bsz, s, h, dh)
    lse = lse.reshape(bsz, h, dilation, lp)[:, :, :, :ls].transpose(0, 3, 2, 1).reshape(bsz, s, h)
    return o, lse


def _dilated_attention(q, k, v, rel_bias):
    outs, lses = [], []
    for window, dilation in D_GROUPS:
        o, l = _dilated_group(q, k, v, dilation, window // (2 * dilation), rel_bias)
        outs.append(o)
        lses.append(l)
    wts = jax.nn.softmax(jnp.stack(lses, axis=0), axis=0)
    return jnp.sum(wts[..., None] * jnp.stack(outs, axis=0), axis=0)


def _even_mixer(h, w_in, m_gate_b, dn_dt_bias, dn_a_log, dn_conv_w, m_norm_g, dn_norm_g, w_out):
    bsz, s, _ = h.shape
    f32 = jnp.float32
    mq, mk, mv, mo, mg, dqkv, dg, z = _split(h @ w_in, EVEN_SPLITS)
    q = mq.reshape(bsz, s, A_HEADS, A_DK).astype(f32)
    k = mk.reshape(bsz, s, A_HEADS, A_DK).astype(f32) * (A_DK ** -0.5)
    v = mv.reshape(bsz, s, A_HEADS, A_DV).astype(f32)
    gt = mg.reshape(bsz, s, 4, A_HEADS).astype(f32) + m_gate_b.astype(f32)
    logf = jax.nn.log_sigmoid(gt[:, :, 2:4])
    h_fwd = _mlstm_chunkwise(q, k, v, gt[:, :, 0], logf[:, :, 0])
    h_bwd = _flip(_mlstm_chunkwise(_flip(q), _flip(k), _flip(v), _flip(gt[:, :, 1]), _flip(logf[:, :, 1])))
    out_a = jax.nn.sigmoid(mo.astype(f32)) * _head_rms(h_fwd + h_bwd, m_norm_g)
    qkv = jax.nn.silu(_dwconv(dqkv, dn_conv_w))
    bq, bk, bv = _split(qkv, (B_HEADS * B_DK, B_HEADS * B_DK, B_HEADS * B_DV))
    q = _l2n(bq.reshape(bsz, s, B_HEADS, B_DK).astype(f32)) * (B_DK ** -0.5)
    k = _l2n(bk.reshape(bsz, s, B_HEADS, B_DK).astype(f32))
    v = bv.reshape(bsz, s, B_HEADS, B_DV).astype(f32)
    gb = dg.reshape(bsz, s, 4, B_HEADS).astype(f32)
    beta = jax.nn.sigmoid(gb[:, :, 0:2])
    decay = -jnp.exp(dn_a_log.astype(f32)) * jax.nn.softplus(gb[:, :, 2:4] + dn_dt_bias.astype(f32))
    o_fwd = _gdn_chunked(q, k, v, beta[:, :, 0], decay[:, :, 0])
    o_bwd = _flip(_gdn_chunked(_flip(q), _flip(k), _flip(v), _flip(beta[:, :, 1]), _flip(decay[:, :, 1])))
    out_b = _head_rms(o_fwd + o_bwd, dn_norm_g)
    mix = jnp.concatenate([out_a, out_b], axis=-1).astype(h.dtype) * jax.nn.silu(z)
    return mix @ w_out


def _odd_mixer(h, w_in, dw_w, dw_b, ln_g, ln_b, rel_bias, w_out):
    bsz, s, _ = h.shape
    ga, gb, aq, ak, av, z = _split(h @ w_in, ODD_SPLITS)
    u = _dwconv(ga * jax.nn.sigmoid(gb), dw_w) + dw_b
    out_c = jax.nn.silu(_layernorm(u, ln_g, ln_b))
    shp = (bsz, s, D_HEADS, D_DH)
    out_d = _dilated_attention(aq.reshape(shp), ak.reshape(shp), av.reshape(shp), rel_bias).reshape(bsz, s, D_HEADS * D_DH)
    mix = jnp.concatenate([out_c, out_d.astype(h.dtype)], axis=-1) * jax.nn.silu(z)
    return mix @ w_out


def setup_inputs(seed: int = 0) -> dict:
    key = jax.random.key(seed)
    ks = jax.random.split(key, 24)
    f32 = jnp.float32

    def nrm(k, shape, sd):
        return jax.random.normal(k, shape, f32) * sd

    forget_b = jnp.linspace(3.0, 6.0, A_HEADS, dtype=f32)
    m_gate_b = jnp.concatenate([nrm(ks[6], (N_EVEN, 2, A_HEADS), 0.1), forget_b + nrm(ks[7], (N_EVEN, 2, A_HEADS), 0.1)], axis=1)
    dt = jnp.exp(jax.random.uniform(ks[8], (N_EVEN, 2, B_HEADS), f32, math.log(1e-3), math.log(1e-1)))
    dt_bias = dt + jnp.log(-jnp.expm1(-dt))
    a_log = jnp.log(jax.random.uniform(ks[9], (N_EVEN, 2, B_HEADS), f32, 1.0, 16.0))
    return {
        'x': nrm(ks[0], (BATCH, SEQ, D_MODEL), 1.0),
        'c': nrm(ks[1], (BATCH, D_MODEL), 1.0),
        'norm_g': 1.0 + nrm(ks[2], (DEPTH, D_MODEL), 0.02),
        'ada_w': nrm(ks[3], (DEPTH, D_MODEL, 3 * D_MODEL), D_MODEL ** -0.5),
        'ada_b': nrm(ks[4], (DEPTH, 3 * D_MODEL), 0.02),
        'ev_w_in': nrm(ks[5], (N_EVEN, D_MODEL, EVEN_IN), D_MODEL ** -0.5),
        'ev_m_gate_b': m_gate_b,
        'ev_dn_dt_bias': dt_bias,
        'ev_dn_a_log': a_log,
        'ev_dn_conv_w': nrm(ks[10], (N_EVEN, B_CONV, B_QKV), B_CONV ** -0.5),
        'ev_m_norm_g': 1.0 + nrm(ks[11], (N_EVEN, A_HEADS * A_DV), 0.02),
        'ev_dn_norm_g': 1.0 + nrm(ks[12], (N_EVEN, B_HEADS * B_DV), 0.02),
        'ev_w_out': nrm(ks[13], (N_EVEN, MIX_EVEN, D_MODEL), MIX_EVEN ** -0.5),
        'od_w_in': nrm(ks[14], (N_ODD, D_MODEL, ODD_IN), D_MODEL ** -0.5),
        'od_dw_w': nrm(ks[15], (N_ODD, C_CONV, C_WIDTH), C_CONV ** -0.5),
        'od_dw_b': nrm(ks[16], (N_ODD, C_WIDTH), 0.02),
        'od_ln_g': 1.0 + nrm(ks[17], (N_ODD, C_WIDTH), 0.02),
        'od_ln_b': nrm(ks[18], (N_ODD, C_WIDTH), 0.02),
        'od_w_out': nrm(ks[19], (N_ODD, MIX_ODD, D_MODEL), MIX_ODD ** -0.5),
        'rel_bias': nrm(ks[20], (REL_BUCKETS, D_HEADS), 0.5),
        'final_g': 1.0 + nrm(ks[21], (D_MODEL,), 0.02),
    }


def reference(x, c, norm_g, ada_w, ada_b, ev_w_in, ev_m_gate_b, ev_dn_dt_bias, ev_dn_a_log, ev_dn_conv_w, ev_m_norm_g, ev_dn_norm_g, ev_w_out, od_w_in, od_dw_w, od_dw_b, od_ln_g, od_ln_b, od_w_out, rel_bias, final_g):
    cs = jax.nn.silu(c)
    for layer in range(DEPTH):
        mod = (cs @ ada_w[layer] + ada_b[layer])[:, None, :]
        shift, scale, gate = jnp.split(mod, 3, axis=-1)
        h = _rms(x, norm_g[layer]) * (1.0 + scale) + shift
        j = layer // 2
        if layer % 2 == 0:
            y = _even_mixer(h, ev_w_in[j], ev_m_gate_b[j], ev_dn_dt_bias[j], ev_dn_a_log[j], ev_dn_conv_w[j], ev_m_norm_g[j], ev_dn_norm_g[j], ev_w_out[j])
        else:
            y = _odd_mixer(h, od_w_in[j], od_dw_w[j], od_dw_b[j], od_ln_g[j], od_ln_b[j], rel_bias, od_w_out[j])
        x = x + gate * y
    return _rms(x, final_g)
```

```python
import math
from functools import partial

import jax
import jax.numpy as jnp
import numpy as np
from jax import lax
from jax.experimental import pallas as pl
from jax.experimental.pallas import tpu as pltpu

D_MODEL = 1024
BATCH = 4
SEQ = 8192
DEPTH = 2
A_HEADS = 4
A_DK = 64
A_DV = 128
B_HEADS = 4
B_DK = 128
B_DV = 128
B_CONV = 5
C_WIDTH = 512
C_CONV = 31
D_HEADS = 8
D_DH = 64
D_GROUPS = ((128, 1), (512, 4), (2048, 16))
REL_BUCKETS = 32
REL_MAX_DIST = 1024
CHUNK = 64
EPS = 1e-6
NEG = -1e30
MIX_EVEN = A_HEADS * A_DV + B_HEADS * B_DV
MIX_ODD = C_WIDTH + D_HEADS * D_DH
B_QKV = B_HEADS * (2 * B_DK + B_DV)
EVEN_SPLITS = (A_HEADS * A_DK, A_HEADS * A_DK, A_HEADS * A_DV, A_HEADS * A_DV, 4 * A_HEADS, B_QKV, 4 * B_HEADS, MIX_EVEN)
ODD_SPLITS = (C_WIDTH, C_WIDTH, D_HEADS * D_DH, D_HEADS * D_DH, D_HEADS * D_DH, MIX_ODD)

VMEM_LIMIT = 56 * 1024 * 1024
TM_PROJ = 256


def _inproj_kernel(x_ref, g_ref, sc_ref, sh_ref, w_ref, o_ref):
    x = x_ref[0]
    y = x * lax.rsqrt(jnp.mean(x * x, axis=-1, keepdims=True) + EPS)
    h = (y * g_ref[...]) * (1.0 + sc_ref[0]) + sh_ref[0]
    o_ref[0] = jnp.dot(h.astype(jnp.bfloat16), w_ref[...], preferred_element_type=jnp.float32)


def _inproj(x, g, scale, shift, w_bf16):
    bsz, s, d = x.shape
    n = w_bf16.shape[1]
    return pl.pallas_call(
        _inproj_kernel,
        grid=(bsz, s // TM_PROJ),
        in_specs=[
            pl.BlockSpec((1, TM_PROJ, d), lambda b, i: (b, i, 0)),
            pl.BlockSpec((1, d), lambda b, i: (0, 0)),
            pl.BlockSpec((1, 1, d), lambda b, i: (b, 0, 0)),
            pl.BlockSpec((1, 1, d), lambda b, i: (b, 0, 0)),
            pl.BlockSpec((d, n), lambda b, i: (0, 0)),
        ],
        out_specs=pl.BlockSpec((1, TM_PROJ, n), lambda b, i: (b, i, 0)),
        out_shape=jax.ShapeDtypeStruct((bsz, s, n), jnp.float32),
        compiler_params=pltpu.CompilerParams(dimension_semantics=("parallel", "parallel"), vmem_limit_bytes=VMEM_LIMIT),
        name="inproj",
    )(x, g.reshape(1, d), scale, shift, w_bf16)


def _outproj_kernel(mix_ref, z_ref, x_ref, gate_ref, w_ref, o_ref):
    z = z_ref[0]
    m = mix_ref[0] * (z * jax.nn.sigmoid(z))
    y = jnp.dot(m.astype(jnp.bfloat16), w_ref[...], preferred_element_type=jnp.float32)
    o_ref[0] = x_ref[0] + gate_ref[0] * y


def _outproj(mix, z, x, gate, w_bf16):
    bsz, s, d = x.shape
    k = mix.shape[-1]
    return pl.pallas_call(
        _outproj_kernel,
        grid=(bsz, s // TM_PROJ),
        in_specs=[
            pl.BlockSpec((1, TM_PROJ, k), lambda b, i: (b, i, 0)),
            pl.BlockSpec((1, TM_PROJ, k), lambda b, i: (b, i, 0)),
            pl.BlockSpec((1, TM_PROJ, d), lambda b, i: (b, i, 0)),
            pl.BlockSpec((1, 1, d), lambda b, i: (b, 0, 0)),
            pl.BlockSpec((k, d), lambda b, i: (0, 0)),
        ],
        out_specs=pl.BlockSpec((1, TM_PROJ, d), lambda b, i: (b, i, 0)),
        out_shape=jax.ShapeDtypeStruct((bsz, s, d), jnp.float32),
        compiler_params=pltpu.CompilerParams(dimension_semantics=("parallel", "parallel"), vmem_limit_bytes=VMEM_LIMIT),
        name="outproj",
    )(mix, z, x, gate, w_bf16)


def _final_rms_kernel(x_ref, g_ref, o_ref):
    x = x_ref[0]
    o_ref[0] = x * lax.rsqrt(jnp.mean(x * x, axis=-1, keepdims=True) + EPS) * g_ref[...]


def _final_rms(x, g):
    bsz, s, d = x.shape
    tm = 512
    return pl.pallas_call(
        _final_rms_kernel,
        grid=(bsz, s // tm),
        in_specs=[pl.BlockSpec((1, tm, d), lambda b, i: (b, i, 0)), pl.BlockSpec((1, d), lambda b, i: (0, 0))],
        out_specs=pl.BlockSpec((1, tm, d), lambda b, i: (b, i, 0)),
        out_shape=jax.ShapeDtypeStruct((bsz, s, d), jnp.float32),
        compiler_params=pltpu.CompilerParams(dimension_semantics=("parallel", "parallel")),
        name="final_rms",
    )(x, g.reshape(1, d))


def _split(p, sizes):
    return jnp.split(p, np.cumsum(sizes)[:-1].tolist(), axis=-1)


def _layernorm(x, g, b):
    xc = x - jnp.mean(x, axis=-1, keepdims=True)
    y = xc * lax.rsqrt(jnp.mean(xc * xc, axis=-1, keepdims=True) + EPS)
    return y * g + b


def _head_rms(t, g):
    bsz, s, h, d = t.shape
    y = t * lax.rsqrt(jnp.mean(t * t, axis=-1, keepdims=True) + EPS)
    return y.reshape(bsz, s, h * d) * g


def _l2n(t):
    return t * lax.rsqrt(jnp.sum(t * t, axis=-1, keepdims=True) + EPS)


def _dwconv(x, w):
    return lax.conv_general_dilated(x, w[:, None, :].astype(x.dtype), window_strides=(1,), padding='SAME', dimension_numbers=('NWC', 'WIO', 'NWC'), feature_group_count=x.shape[-1])


def _flip(t):
    return jnp.flip(t, axis=1)


def _to_chunks(t):
    bsz, s, h = t.shape[:3]
    t = t.reshape((bsz, s // CHUNK, CHUNK, h) + t.shape[3:])
    return jnp.moveaxis(t, (1, 3), (0, 2))


def _from_chunks(t):
    nc, bsz, h, l = t.shape[:4]
    t = jnp.moveaxis(t, (0, 2), (1, 3))
    return t.reshape((bsz, nc * l, h) + t.shape[4:])


def _mlstm_chunkwise(q, k, v, i_pre, logf):
    q, k, v, i_pre, logf = (_to_chunks(t) for t in (q, k, v, i_pre, logf))
    nc, bsz, h = q.shape[:3]
    causal = jnp.tril(jnp.ones((CHUNK, CHUNK), dtype=bool))
    b = jnp.cumsum(logf, axis=-1)
    dmat = jnp.where(causal, b[..., :, None] - b[..., None, :] + i_pre[..., None, :], -jnp.inf)
    dmax = jnp.max(dmat, axis=-1)
    qk = jnp.einsum('nbhld,nbhsd->nbhls', q, k)
    a_end = b[..., -1:] - b + i_pre

    def step(carry, xs):
        cmat, nvec, m = carry
        qc, kc, vc, bc, dc, dmc, qkc, aec = xs
        inter = bc + m[..., None]
        mt = jnp.maximum(inter, dmc)
        w_int = jnp.exp(inter - mt)
        sc = jnp.exp(dc - mt[..., None]) * qkc
        num = w_int[..., None] * jnp.einsum('bhld,bhde->bhle', qc, cmat) + jnp.einsum('bhls,bhse->bhle', sc, vc)
        den = w_int * jnp.einsum('bhld,bhd->bhl', qc, nvec) + jnp.sum(sc, axis=-1)
        hc = num / jnp.maximum(jnp.abs(den), jnp.exp(-mt))[..., None]
        m_new = jnp.maximum(bc[..., -1] + m, jnp.max(aec, axis=-1))
        w_old = jnp.exp(bc[..., -1] + m - m_new)
        kw = kc * jnp.exp(aec - m_new[..., None])[..., None]
        cmat = w_old[..., None, None] * cmat + jnp.einsum('bhld,bhle->bhde', kw, vc)
        nvec = w_old[..., None] * nvec + jnp.sum(kw, axis=-2)
        return (cmat, nvec, m_new), hc

    init = (jnp.zeros((bsz, h, A_DK, A_DV), jnp.float32), jnp.zeros((bsz, h, A_DK), jnp.float32), jnp.zeros((bsz, h), jnp.float32))
    _, hs = lax.scan(step, init, (q, k, v, b, dmat, dmax, qk, a_end))
    return _from_chunks(hs)


def _gdn_chunked(q, k, v, beta, g):
    q, k, v, beta, g = (_to_chunks(t) for t in (q, k, v, beta, g))
    nc, bsz, h = q.shape[:3]
    tril = jnp.tril(jnp.ones((CHUNK, CHUNK), dtype=bool))
    strict = jnp.tril(jnp.ones((CHUNK, CHUNK), dtype=bool), -1)
    gc = jnp.cumsum(g, axis=-1)
    gam = jnp.exp(jnp.where(tril, gc[..., :, None] - gc[..., None, :], -jnp.inf))
    a = jnp.where(strict, beta[..., :, None] * jnp.einsum('nbhid,nbhjd->nbhij', k, k) * gam, 0.0)
    tmat = a + jnp.eye(CHUNK, dtype=a.dtype)
    u = lax.linalg.triangular_solve(tmat, beta[..., None] * v, left_side=True, lower=True, unit_diagonal=True)
    w = lax.linalg.triangular_solve(tmat, (beta * jnp.exp(gc))[..., None] * k, left_side=True, lower=True, unit_diagonal=True)
    attn = jnp.einsum('nbhid,nbhjd->nbhij', q, k) * gam

    def step(state, xs):
        qc, kc, uc, wc, gcc, ac = xs
        v_new = uc - jnp.einsum('bhld,bhde->bhle', wc, state)
        o = jnp.einsum('bhld,bhde->bhle', qc * jnp.exp(gcc)[..., None], state) + jnp.einsum('bhls,bhse->bhle', ac, v_new)
        gl = gcc[..., -1]
        state = jnp.exp(gl)[..., None, None] * state + jnp.einsum('bhld,bhle->bhde', kc * jnp.exp(gl[..., None] - gcc)[..., None], v_new)
        return state, o

    _, os_ = lax.scan(step, jnp.zeros((bsz, h, B_DK, B_DV), jnp.float32), (q, k, u, w, gc, attn))
    return _from_chunks(os_)


def _t5_bucket(rel):
    half = REL_BUCKETS // 2
    exact = half // 2
    n = jnp.abs(rel)
    large = exact + (jnp.log(jnp.maximum(n, 1).astype(jnp.float32) / exact) / math.log(REL_MAX_DIST / exact) * (half - exact)).astype(jnp.int32)
    large = jnp.minimum(large, half - 1)
    return (rel > 0).astype(jnp.int32) * half + jnp.where(n < exact, n, large)


def _dilated_group(q, k, v, dilation, radius, rel_bias):
    bsz, s, h, dh = q.shape
    ls = s // dilation
    nb = -(-ls // radius)
    lp = nb * radius

    def sub(t, lo, hi):
        t = t.reshape(bsz, ls, dilation, h, dh).transpose(0, 3, 2, 1, 4)
        return jnp.pad(t, ((0, 0), (0, 0), (0, 0), (lo, hi), (0, 0)))

    qb = sub(q, 0, lp - ls).reshape(bsz, h, dilation, nb, radius, dh)

    def band(t):
        t = sub(t, radius, lp - ls + radius).reshape(bsz, h, dilation, nb + 2, radius, dh)
        return jnp.concatenate([t[:, :, :, :-2], t[:, :, :, 1:-1], t[:, :, :, 2:]], axis=4)

    kb, vb = band(k), band(v)
    qi = jnp.arange(radius)[:, None]
    kj = jnp.arange(3 * radius)[None, :]
    rel = kj - radius - qi
    kpos = jnp.arange(nb)[:, None, None] * radius + kj - radius
    valid = (jnp.abs(rel) <= radius) & (kpos >= 0) & (kpos < ls)
    bias = jnp.transpose(rel_bias[_t5_bucket(rel * dilation)], (2, 0, 1)).astype(jnp.float32)
    sc = jnp.einsum('bhrnid,bhrnjd->bhrnij', qb, kb).astype(jnp.float32) * (dh ** -0.5) + bias[:, None, None]
    sc = jnp.where(valid, sc, NEG)
    m = jnp.max(sc, axis=-1, keepdims=True)
    p = jnp.exp(sc - m)
    den = jnp.sum(p, axis=-1)
    o = jnp.einsum('bhrnij,bhrnjd->bhrnid', p, vb.astype(jnp.float32)) / den[..., None]
    lse = m[..., 0] + jnp.log(den)
    o = o.reshape(bsz, h, dilation, lp, dh)[:, :, :, :ls].transpose(0, 3, 2, 1, 4).reshape(bsz, s, h, dh)
    lse = lse.reshape(bsz, h, dilation, lp)[:, :, :, :ls].transpose(0, 3, 2, 1).reshape(bsz, s, h)
    return o, lse


def _dilated_attention(q, k, v, rel_bias):
    outs, lses = [], []
    for window, dilation in D_GROUPS:
        o, l = _dilated_group(q, k, v, dilation, window // (2 * dilation), rel_bias)
        outs.append(o)
        lses.append(l)
    wts = jax.nn.softmax(jnp.stack(lses, axis=0), axis=0)
    return jnp.sum(wts[..., None] * jnp.stack(outs, axis=0), axis=0)


def _even_mixer_core(p, m_gate_b, dn_dt_bias, dn_a_log, dn_conv_w, m_norm_g, dn_norm_g):
    bsz, s, _ = p.shape
    f32 = jnp.float32
    mq, mk, mv, mo, mg, dqkv, dg, z = _split(p, EVEN_SPLITS)
    q = mq.reshape(bsz, s, A_HEADS, A_DK)
    k = mk.reshape(bsz, s, A_HEADS, A_DK) * (A_DK ** -0.5)
    v = mv.reshape(bsz, s, A_HEADS, A_DV)
    gt = mg.reshape(bsz, s, 4, A_HEADS) + m_gate_b
    logf = jax.nn.log_sigmoid(gt[:, :, 2:4])
    h_fwd = _mlstm_chunkwise(q, k, v, gt[:, :, 0], logf[:, :, 0])
    h_bwd = _flip(_mlstm_chunkwise(_flip(q), _flip(k), _flip(v), _flip(gt[:, :, 1]), _flip(logf[:, :, 1])))
    out_a = jax.nn.sigmoid(mo) * _head_rms(h_fwd + h_bwd, m_norm_g)
    qkv = jax.nn.silu(_dwconv(dqkv, dn_conv_w))
    bq, bk, bv = _split(qkv, (B_HEADS * B_DK, B_HEADS * B_DK, B_HEADS * B_DV))
    q = _l2n(bq.reshape(bsz, s, B_HEADS, B_DK)) * (B_DK ** -0.5)
    k = _l2n(bk.reshape(bsz, s, B_HEADS, B_DK))
    v = bv.reshape(bsz, s, B_HEADS, B_DV)
    gb = dg.reshape(bsz, s, 4, B_HEADS)
    beta = jax.nn.sigmoid(gb[:, :, 0:2])
    decay = -jnp.exp(dn_a_log) * jax.nn.softplus(gb[:, :, 2:4] + dn_dt_bias)
    o_fwd = _gdn_chunked(q, k, v, beta[:, :, 0], decay[:, :, 0])
    o_bwd = _flip(_gdn_chunked(_flip(q), _flip(k), _flip(v), _flip(beta[:, :, 1]), _flip(decay[:, :, 1])))
    out_b = _head_rms(o_fwd + o_bwd, dn_norm_g)
    return jnp.concatenate([out_a, out_b], axis=-1), z


def _odd_mixer_core(p, dw_w, dw_b, ln_g, ln_b, rel_bias):
    bsz, s, _ = p.shape
    ga, gb, aq, ak, av, z = _split(p, ODD_SPLITS)
    u = _dwconv(ga * jax.nn.sigmoid(gb), dw_w) + dw_b
    out_c = jax.nn.silu(_layernorm(u, ln_g, ln_b))
    shp = (bsz, s, D_HEADS, D_DH)
    out_d = _dilated_attention(aq.reshape(shp), ak.reshape(shp), av.reshape(shp), rel_bias).reshape(bsz, s, D_HEADS * D_DH)
    return jnp.concatenate([out_c, out_d], axis=-1), z


def kernel(x, c, norm_g, ada_w, ada_b, ev_w_in, ev_m_gate_b, ev_dn_dt_bias, ev_dn_a_log, ev_dn_conv_w, ev_m_norm_g, ev_dn_norm_g, ev_w_out, od_w_in, od_dw_w, od_dw_b, od_ln_g, od_ln_b, od_w_out, rel_bias, final_g):
    bf16 = jnp.bfloat16
    cs = jax.nn.silu(c)
    for layer in range(DEPTH):
        mod = (cs @ ada_w[layer] + ada_b[layer])[:, None, :]
        shift, scale, gate = jnp.split(mod, 3, axis=-1)
        j = layer // 2
        if layer % 2 == 0:
            w_in = ev_w_in[j]
            n = w_in.shape[1]
            npad = -(-n // 128) * 128
            w_in = jnp.pad(w_in, ((0, 0), (0, npad - n))).astype(bf16)
            p = _inproj(x, norm_g[layer], scale, shift, w_in)[..., :n]
            mix, z = _even_mixer_core(p, ev_m_gate_b[j], ev_dn_dt_bias[j], ev_dn_a_log[j], ev_dn_conv_w[j], ev_m_norm_g[j], ev_dn_norm_g[j])
            x = _outproj(mix, z, x, gate, ev_w_out[j].astype(bf16))
        else:
            p = _inproj(x, norm_g[layer], scale, shift, od_w_in[j].astype(bf16))
            mix, z = _odd_mixer_core(p, od_dw_w[j], od_dw_b[j], od_ln_g[j], od_ln_b[j], rel_bias)
            x = _outproj(mix, z, x, gate, od_w_out[j].astype(bf16))
    return _final_rms(x, final_g)
```

```python
import math
from functools import partial

import jax
import jax.numpy as jnp
import numpy as np
from jax import lax
from jax.experimental import pallas as pl
from jax.experimental.pallas import tpu as pltpu

D_MODEL = 1024
BATCH = 4
SEQ = 8192
DEPTH = 2
A_HEADS = 4
A_DK = 64
A_DV = 128
B_HEADS = 4
B_DK = 128
B_DV = 128
B_CONV = 5
C_WIDTH = 512
C_CONV = 31
D_HEADS = 8
D_DH = 64
D_GROUPS = ((128, 1), (512, 4), (2048, 16))
REL_BUCKETS = 32
REL_MAX_DIST = 1024
CHUNK = 64
EPS = 1e-6
NEG = -1e30
MIX_EVEN = A_HEADS * A_DV + B_HEADS * B_DV
MIX_ODD = C_WIDTH + D_HEADS * D_DH
B_QKV = B_HEADS * (2 * B_DK + B_DV)
EVEN_SPLITS = (A_HEADS * A_DK, A_HEADS * A_DK, A_HEADS * A_DV, A_HEADS * A_DV, 4 * A_HEADS, B_QKV, 4 * B_HEADS, MIX_EVEN)
ODD_SPLITS = (C_WIDTH, C_WIDTH, D_HEADS * D_DH, D_HEADS * D_DH, D_HEADS * D_DH, MIX_ODD)

VMEM_LIMIT = 56 * 1024 * 1024
TM_PROJ = 256


def _adaln_kernel(c_ref, w_ref, b_ref, o_ref):
    c = c_ref[...]
    cs = (c * jax.nn.sigmoid(c)).astype(jnp.bfloat16)
    o_ref[0] = jnp.dot(cs, w_ref[0].astype(jnp.bfloat16), preferred_element_type=jnp.float32) + b_ref[0]


def _adaln(c, ada_w, ada_b):
    depth, d, n3 = ada_w.shape
    bsz = c.shape[0]
    tn = 1024
    return pl.pallas_call(
        _adaln_kernel,
        grid=(depth, n3 // tn),
        in_specs=[pl.BlockSpec((bsz, d), lambda l, j: (0, 0)), pl.BlockSpec((1, d, tn), lambda l, j: (l, 0, j)),
                  pl.BlockSpec((1, 1, tn), lambda l, j: (l, 0, j))],
        out_specs=pl.BlockSpec((1, bsz, tn), lambda l, j: (l, 0, j)),
        out_shape=jax.ShapeDtypeStruct((depth, bsz, n3), jnp.float32),
        compiler_params=pltpu.CompilerParams(dimension_semantics=("parallel", "parallel")),
        name="adaln",
    )(c, ada_w, ada_b.reshape(depth, 1, n3))


def _modulated_rms(x_ref, g_ref, sc_ref, sh_ref):
    x = x_ref[0]
    y = x * lax.rsqrt(jnp.mean(x * x, axis=-1, keepdims=True) + EPS)
    return ((y * g_ref[...]) * (1.0 + sc_ref[0]) + sh_ref[0]).astype(jnp.bfloat16)


_EV_COLS = {"mq": (0, 256), "mk": (256, 512), "mv": (512, 1024), "mo": (1024, 1536), "dqkv": (1536, 3072), "z": (3072, 4096)}
_OD_COLS = {"ga": (0, 512), "gb": (512, 1024), "aq": (1024, 1536), "ak": (1536, 2048), "av": (2048, 2560), "z": (2560, 3584)}


def _inproj_even_kernel(x_ref, g_ref, sc_ref, sh_ref, w_ref, wg_ref, wgt_ref,
                        mq_ref, mk_ref, mv_ref, mo_ref, dqkv_ref, z_ref, mg_ref, dg_ref, mgt_ref, dgt_ref):
    f32 = jnp.float32
    h = _modulated_rms(x_ref, g_ref, sc_ref, sh_ref)
    for name, o_ref in (("mq", mq_ref), ("mk", mk_ref), ("mv", mv_ref), ("mo", mo_ref), ("dqkv", dqkv_ref), ("z", z_ref)):
        lo, hi = _EV_COLS[name]
        o_ref[0] = jnp.dot(h, w_ref[:, lo:hi], preferred_element_type=f32).astype(o_ref.dtype)
    gates = jnp.dot(h, wg_ref[...], preferred_element_type=f32)
    gates_t = lax.dot_general(wgt_ref[...], h, (((1,), (1,)), ((), ())), preferred_element_type=f32)
    mg_ref[0] = gates[:, :16]
    dg_ref[0] = gates[:, 16:]
    mgt_ref[0] = gates_t[:16]
    dgt_ref[0] = gates_t[16:]


def _inproj_even(x, g, scale, shift, w, wg, wgt):
    bsz, s, d = x.shape
    tm = TM_PROJ
    tok = lambda b, i: (b, i, 0)
    tok_t = lambda b, i: (b, 0, i)
    const = lambda b, i: (0, 0)
    bvec = lambda b, i: (b, 0, 0)
    bf16, f32 = jnp.bfloat16, jnp.float32
    outs = [("mq", bf16), ("mk", bf16), ("mv", bf16), ("mo", f32), ("dqkv", f32), ("z", f32)]
    widths = [_EV_COLS[n][1] - _EV_COLS[n][0] for n, _ in outs]
    return pl.pallas_call(
        _inproj_even_kernel,
        grid=(bsz, s // tm),
        in_specs=[pl.BlockSpec((1, tm, d), tok), pl.BlockSpec((1, d), const), pl.BlockSpec((1, 1, d), bvec), pl.BlockSpec((1, 1, d), bvec),
                  pl.BlockSpec(w.shape, const), pl.BlockSpec(wg.shape, const), pl.BlockSpec(wgt.shape, const)],
        out_specs=[pl.BlockSpec((1, tm, wd), tok) for wd in widths] + [pl.BlockSpec((1, tm, 16), tok)] * 2 + [pl.BlockSpec((1, 16, tm), tok_t)] * 2,
        out_shape=[jax.ShapeDtypeStruct((bsz, s, wd), dt) for wd, (_, dt) in zip(widths, outs)]
        + [jax.ShapeDtypeStruct((bsz, s, 16), f32)] * 2 + [jax.ShapeDtypeStruct((bsz, 16, s), f32)] * 2,
        compiler_params=pltpu.CompilerParams(dimension_semantics=("parallel", "parallel"), vmem_limit_bytes=VMEM_LIMIT),
        name="inproj_even",
    )(x, g.reshape(1, d), scale, shift, w, wg, wgt)


def _inproj_odd_kernel(x_ref, g_ref, sc_ref, sh_ref, w_ref, glu_ref, aq_ref, ak_ref, av_ref, z_ref):
    f32 = jnp.float32
    h = _modulated_rms(x_ref, g_ref, sc_ref, sh_ref)
    dot = lambda name: jnp.dot(h, w_ref[:, _OD_COLS[name][0]:_OD_COLS[name][1]], preferred_element_type=f32)
    glu_ref[0] = dot("ga") * jax.nn.sigmoid(dot("gb"))
    for name, o_ref in (("aq", aq_ref), ("ak", ak_ref), ("av", av_ref), ("z", z_ref)):
        o_ref[0] = dot(name).astype(o_ref.dtype)


def _inproj_odd(x, g, scale, shift, w):
    bsz, s, d = x.shape
    tm = TM_PROJ
    tok = lambda b, i: (b, i, 0)
    const = lambda b, i: (0, 0)
    bvec = lambda b, i: (b, 0, 0)
    bf16, f32 = jnp.bfloat16, jnp.float32
    outs = [(C_WIDTH, f32), (512, bf16), (512, bf16), (512, bf16), (MIX_ODD, f32)]
    return pl.pallas_call(
        _inproj_odd_kernel,
        grid=(bsz, s // tm),
        in_specs=[pl.BlockSpec((1, tm, d), tok), pl.BlockSpec((1, d), const), pl.BlockSpec((1, 1, d), bvec), pl.BlockSpec((1, 1, d), bvec),
                  pl.BlockSpec(w.shape, const)],
        out_specs=[pl.BlockSpec((1, tm, wd), tok) for wd, _ in outs],
        out_shape=[jax.ShapeDtypeStruct((bsz, s, wd), dt) for wd, dt in outs],
        compiler_params=pltpu.CompilerParams(dimension_semantics=("parallel", "parallel"), vmem_limit_bytes=VMEM_LIMIT),
        name="inproj_odd",
    )(x, g.reshape(1, d), scale, shift, w)


def _head_rms_cols(t, g, width):
    parts = []
    for h in range(t.shape[1] // width):
        th = t[:, h * width:(h + 1) * width]
        parts.append(th * lax.rsqrt(jnp.mean(th * th, axis=-1, keepdims=True) + EPS))
    return jnp.concatenate(parts, axis=1) * g


def _outproj_even_kernel(hf_ref, hb_ref, of_ref, ob_ref, mo_ref, z_ref, x_ref, gate_ref, mg_ref, dg_ref, w_ref, o_ref):
    f32, bf16 = jnp.float32, jnp.bfloat16
    z = z_ref[0]
    sz = z * jax.nn.sigmoid(z)
    na = A_HEADS * A_DV
    out_a = jax.nn.sigmoid(mo_ref[0]) * _head_rms_cols(hf_ref[0] + hb_ref[0], mg_ref[...], A_DV)
    out_b = _head_rms_cols(of_ref[0] + ob_ref[0], dg_ref[...], B_DV)
    y = jnp.dot((out_a * sz[:, :na]).astype(bf16), w_ref[:na, :], preferred_element_type=f32)
    y = y + jnp.dot((out_b * sz[:, na:]).astype(bf16), w_ref[na:, :], preferred_element_type=f32)
    o_ref[0] = x_ref[0] + gate_ref[0] * y


def _outproj_even(hf, hb, of, ob, mo, z, x, gate, m_norm_g, dn_norm_g, w):
    bsz, s, d = x.shape
    tm = TM_PROJ
    tok = lambda b, i: (b, i, 0)
    const = lambda b, i: (0, 0)
    bvec = lambda b, i: (b, 0, 0)
    na, nb = A_HEADS * A_DV, B_HEADS * B_DV
    return pl.pallas_call(
        _outproj_even_kernel,
        grid=(bsz, s // tm),
        in_specs=[pl.BlockSpec((1, tm, na), tok)] * 2 + [pl.BlockSpec((1, tm, nb), tok)] * 2 + [pl.BlockSpec((1, tm, na), tok),
                  pl.BlockSpec((1, tm, na + nb), tok), pl.BlockSpec((1, tm, d), tok), pl.BlockSpec((1, 1, d), bvec),
                  pl.BlockSpec((1, na), const), pl.BlockSpec((1, nb), const), pl.BlockSpec(w.shape, const)],
        out_specs=pl.BlockSpec((1, tm, d), tok),
        out_shape=jax.ShapeDtypeStruct((bsz, s, d), jnp.float32),
        compiler_params=pltpu.CompilerParams(dimension_semantics=("parallel", "parallel"), vmem_limit_bytes=VMEM_LIMIT),
        name="outproj_even",
    )(hf, hb, of, ob, mo, z, x, gate, m_norm_g.reshape(1, na), dn_norm_g.reshape(1, nb), w)


def _outproj_odd_kernel(oc_ref, o1_ref, o2_ref, o3_ref, l1_ref, l2_ref, l3_ref, z_ref, x_ref, gate_ref, fg_ref, w_ref, o_ref):
    f32, bf16 = jnp.float32, jnp.bfloat16
    z = z_ref[0]
    sz = z * jax.nn.sigmoid(z)
    l1, l2, l3 = l1_ref[0], l2_ref[0], l3_ref[0]
    lm = jnp.maximum(jnp.maximum(l1, l2), l3)
    e1, e2, e3 = jnp.exp(l1 - lm), jnp.exp(l2 - lm), jnp.exp(l3 - lm)
    inv = 1.0 / (e1 + e2 + e3)
    w1, w2, w3 = e1 * inv, e2 * inv, e3 * inv
    o1, o2, o3 = o1_ref[0], o2_ref[0], o3_ref[0]
    parts = []
    for h in range(D_HEADS):
        cs = slice(h * D_DH, (h + 1) * D_DH)
        parts.append(w1[:, h:h + 1] * o1[:, cs] + w2[:, h:h + 1] * o2[:, cs] + w3[:, h:h + 1] * o3[:, cs])
    out_d = jnp.concatenate(parts, axis=1)
    y = jnp.dot((oc_ref[0] * sz[:, :C_WIDTH]).astype(bf16), w_ref[:C_WIDTH, :], preferred_element_type=f32)
    y = y + jnp.dot((out_d * sz[:, C_WIDTH:]).astype(bf16), w_ref[C_WIDTH:, :], preferred_element_type=f32)
    xn = x_ref[0] + gate_ref[0] * y
    o_ref[0] = xn * lax.rsqrt(jnp.mean(xn * xn, axis=-1, keepdims=True) + EPS) * fg_ref[...]


def _outproj_odd_final(oc, og, lg, z, x, gate, final_g, w):
    bsz, s, d = x.shape
    tm = TM_PROJ
    tok = lambda b, i: (b, i, 0)
    const = lambda b, i: (0, 0)
    bvec = lambda b, i: (b, 0, 0)
    nd = D_HEADS * D_DH
    return pl.pallas_call(
        _outproj_odd_kernel,
        grid=(bsz, s // tm),
        in_specs=[pl.BlockSpec((1, tm, C_WIDTH), tok)] + [pl.BlockSpec((1, tm, nd), tok)] * 3 + [pl.BlockSpec((1, tm, 128), tok)] * 3
        + [pl.BlockSpec((1, tm, MIX_ODD), tok), pl.BlockSpec((1, tm, d), tok), pl.BlockSpec((1, 1, d), bvec), pl.BlockSpec((1, d), const),
           pl.BlockSpec(w.shape, const)],
        out_specs=pl.BlockSpec((1, tm, d), tok),
        out_shape=jax.ShapeDtypeStruct((bsz, s, d), jnp.float32),
        compiler_params=pltpu.CompilerParams(dimension_semantics=("parallel", "parallel"), vmem_limit_bytes=VMEM_LIMIT),
        name="outproj_odd",
    )(oc, *og, *lg, z, x, gate, final_g.reshape(1, d), w)


T_CONV = 512
HALO_C = 16
SUB_C = 64


def _conformer_kernel(x_ref, xp_ref, xn_ref, w_ref, b_ref, lg_ref, lb_ref, o_ref, xe_ref):
    i = pl.program_id(1)
    nt = pl.num_programs(1)
    t = x_ref.shape[1]
    xe_ref[0:HALO_C, :] = jnp.where(i > 0, xp_ref[0], 0.0)
    xe_ref[HALO_C:HALO_C + t, :] = x_ref[0]
    xe_ref[HALO_C + t:, :] = jnp.where(i < nt - 1, xn_ref[0], 0.0)
    half = C_CONV // 2
    for r0 in range(0, t, SUB_C):
        acc = None
        for j in range(C_CONV):
            lo = HALO_C - half + j + r0
            term = xe_ref[lo:lo + SUB_C, :] * w_ref[j:j + 1, :]
            acc = term if acc is None else acc + term
        u = acc + b_ref[...]
        uc = u - jnp.mean(u, axis=-1, keepdims=True)
        y = uc * lax.rsqrt(jnp.mean(uc * uc, axis=-1, keepdims=True) + EPS) * lg_ref[...] + lb_ref[...]
        o_ref[0, r0:r0 + SUB_C, :] = y * jax.nn.sigmoid(y)


def _conformer(glu, dw_w, dw_b, ln_g, ln_b):
    bsz, s, cw = glu.shape
    t = min(T_CONV, s)
    hb = t // HALO_C
    cur = lambda b, i: (b, i, 0)
    const = lambda b, i: (0, 0)
    return pl.pallas_call(
        _conformer_kernel,
        grid=(bsz, s // t),
        in_specs=[pl.BlockSpec((1, t, cw), cur),
                  pl.BlockSpec((1, HALO_C, cw), lambda b, i: (b, jnp.maximum(i * hb - 1, 0), 0)),
                  pl.BlockSpec((1, HALO_C, cw), lambda b, i: (b, jnp.minimum((i + 1) * hb, s // HALO_C - 1), 0)),
                  pl.BlockSpec((C_CONV, cw), const)] + [pl.BlockSpec((1, cw), const)] * 3,
        out_specs=pl.BlockSpec((1, t, cw), cur),
        out_shape=jax.ShapeDtypeStruct((bsz, s, cw), jnp.float32),
        scratch_shapes=[pltpu.VMEM((t + 2 * HALO_C, cw), jnp.float32)],
        compiler_params=pltpu.CompilerParams(dimension_semantics=("parallel", "parallel"), vmem_limit_bytes=VMEM_LIMIT),
        name="conformer",
    )(glu, glu, glu, dw_w, dw_b.reshape(1, cw), ln_g.reshape(1, cw), ln_b.reshape(1, cw))


TQ_ATT = 128
R_ATT = 64


def _dilated_kernel(q_ref, kc_ref, kp_ref, kn_ref, vc_ref, vp_ref, vn_ref, bias_ref, o_ref, lse_ref, kx_ref, vx_ref):
    i = pl.program_id(2)
    nt = pl.num_programs(2)
    tq = q_ref.shape[1]
    nk = tq + 2 * R_ATT
    f32, bf16 = jnp.float32, jnp.bfloat16
    kx_ref[0:R_ATT, :] = kp_ref[0]
    kx_ref[R_ATT:R_ATT + tq, :] = kc_ref[0]
    kx_ref[R_ATT + tq:, :] = kn_ref[0]
    vx_ref[0:R_ATT, :] = vp_ref[0]
    vx_ref[R_ATT:R_ATT + tq, :] = vc_ref[0]
    vx_ref[R_ATT + tq:, :] = vn_ref[0]
    kj = lax.broadcasted_iota(jnp.int32, (tq, nk), 1)
    outside = ((kj < R_ATT) & (i == 0)) | ((kj >= R_ATT + tq) & (i == nt - 1))
    lane = lax.broadcasted_iota(jnp.int32, (tq, 128), 1)
    lse_all = jnp.zeros((tq, 128), f32)
    for h in range(D_HEADS):
        cs = slice(h * D_DH, (h + 1) * D_DH)
        sc = lax.dot_general(q_ref[0, :, cs], kx_ref[:, cs], (((1,), (1,)), ((), ())), preferred_element_type=f32) + bias_ref[h]
        sc = jnp.where(outside, NEG, sc)
        m = jnp.max(sc, axis=-1, keepdims=True)
        p = jnp.exp(sc - m)
        den = jnp.sum(p, axis=-1, keepdims=True)
        o_ref[0, :, cs] = jnp.dot(p.astype(bf16), vx_ref[:, cs], preferred_element_type=f32) / den
        lse_all = jnp.where(lane == h, m + jnp.log(den), lse_all)
    lse_ref[0] = lse_all


def _dilated_bias(rel_bias, dilation, tq):
    half = REL_BUCKETS // 2
    exact = half // 2
    qi = jnp.arange(tq)[:, None]
    kj = jnp.arange(tq + 2 * R_ATT)[None, :]
    rel = kj - R_ATT - qi
    reld = rel * dilation
    n = jnp.abs(reld)
    large = exact + (jnp.log(jnp.maximum(n, 1).astype(jnp.float32) / exact) / math.log(REL_MAX_DIST / exact) * (half - exact)).astype(jnp.int32)
    large = jnp.minimum(large, half - 1)
    bucket = (reld > 0).astype(jnp.int32) * half + jnp.where(n < exact, n, large)
    bias = jnp.transpose(rel_bias[bucket], (2, 0, 1)).astype(jnp.float32)
    return jnp.where((jnp.abs(rel) <= R_ATT)[None], bias, NEG)


def _dilated_group_call(q, k, v, rel_bias, dilation):
    bsz, s, nd = q.shape
    ls = s // dilation
    tq = min(TQ_ATT, ls)
    nt = ls // tq
    hb = tq // R_ATT
    nk = tq + 2 * R_ATT
    view = lambda t: t.reshape(bsz, ls, dilation * t.shape[-1])
    cur = lambda b, r, i: (b, i, r)
    prev = lambda b, r, i: (b, jnp.maximum(i * hb - 1, 0), r)
    nxt = lambda b, r, i: (b, jnp.minimum((i + 1) * hb, ls // R_ATT - 1), r)
    qv, kv, vv = view(q), view(k), view(v)
    kv_specs = [pl.BlockSpec((1, tq, nd), cur), pl.BlockSpec((1, R_ATT, nd), prev), pl.BlockSpec((1, R_ATT, nd), nxt)]
    o, lse = pl.pallas_call(
        _dilated_kernel,
        grid=(bsz, dilation, nt),
        in_specs=[pl.BlockSpec((1, tq, nd), cur)] + kv_specs + kv_specs + [pl.BlockSpec((D_HEADS, tq, nk), lambda b, r, i: (0, 0, 0))],
        out_specs=[pl.BlockSpec((1, tq, nd), cur), pl.BlockSpec((1, tq, 128), cur)],
        out_shape=[jax.ShapeDtypeStruct((bsz, ls, dilation * nd), jnp.float32), jax.ShapeDtypeStruct((bsz, ls, dilation * 128), jnp.float32)],
        scratch_shapes=[pltpu.VMEM((nk, nd), jnp.bfloat16)] * 2,
        compiler_params=pltpu.CompilerParams(dimension_semantics=("parallel", "parallel", "parallel"), vmem_limit_bytes=VMEM_LIMIT),
        name=f"dilated_d{dilation}",
    )(qv, kv, kv, kv, vv, vv, vv, _dilated_bias(rel_bias, dilation, tq))
    return o.reshape(bsz, s, nd), lse.reshape(bsz, s, 128)


def _inproj(x, g, scale, shift, w_bf16):
    bsz, s, d = x.shape
    n = w_bf16.shape[1]
    return pl.pallas_call(
        _inproj_kernel,
        grid=(bsz, s // TM_PROJ),
        in_specs=[
            pl.BlockSpec((1, TM_PROJ, d), lambda b, i: (b, i, 0)),
            pl.BlockSpec((1, d), lambda b, i: (0, 0)),
            pl.BlockSpec((1, 1, d), lambda b, i: (b, 0, 0)),
            pl.BlockSpec((1, 1, d), lambda b, i: (b, 0, 0)),
            pl.BlockSpec((d, n), lambda b, i: (0, 0)),
        ],
        out_specs=pl.BlockSpec((1, TM_PROJ, n), lambda b, i: (b, i, 0)),
        out_shape=jax.ShapeDtypeStruct((bsz, s, n), jnp.float32),
        compiler_params=pltpu.CompilerParams(dimension_semantics=("parallel", "parallel"), vmem_limit_bytes=VMEM_LIMIT),
        name="inproj",
    )(x, g.reshape(1, d), scale, shift, w_bf16)


def _outproj_kernel(mix_ref, z_ref, x_ref, gate_ref, w_ref, o_ref):
    z = z_ref[0]
    m = mix_ref[0] * (z * jax.nn.sigmoid(z))
    y = jnp.dot(m.astype(jnp.bfloat16), w_ref[...], preferred_element_type=jnp.float32)
    o_ref[0] = x_ref[0] + gate_ref[0] * y


def _outproj(mix, z, x, gate, w_bf16):
    bsz, s, d = x.shape
    k = mix.shape[-1]
    return pl.pallas_call(
        _outproj_kernel,
        grid=(bsz, s // TM_PROJ),
        in_specs=[
            pl.BlockSpec((1, TM_PROJ, k), lambda b, i: (b, i, 0)),
            pl.BlockSpec((1, TM_PROJ, k), lambda b, i: (b, i, 0)),
            pl.BlockSpec((1, TM_PROJ, d), lambda b, i: (b, i, 0)),
            pl.BlockSpec((1, 1, d), lambda b, i: (b, 0, 0)),
            pl.BlockSpec((k, d), lambda b, i: (0, 0)),
        ],
        out_specs=pl.BlockSpec((1, TM_PROJ, d), lambda b, i: (b, i, 0)),
        out_shape=jax.ShapeDtypeStruct((bsz, s, d), jnp.float32),
        compiler_params=pltpu.CompilerParams(dimension_semantics=("parallel", "parallel"), vmem_limit_bytes=VMEM_LIMIT),
        name="outproj",
    )(mix, z, x, gate, w_bf16)


def _final_rms_kernel(x_ref, g_ref, o_ref):
    x = x_ref[0]
    o_ref[0] = x * lax.rsqrt(jnp.mean(x * x, axis=-1, keepdims=True) + EPS) * g_ref[...]


def _final_rms(x, g):
    bsz, s, d = x.shape
    tm = 512
    return pl.pallas_call(
        _final_rms_kernel,
        grid=(bsz, s // tm),
        in_specs=[pl.BlockSpec((1, tm, d), lambda b, i: (b, i, 0)), pl.BlockSpec((1, d), lambda b, i: (0, 0))],
        out_specs=pl.BlockSpec((1, tm, d), lambda b, i: (b, i, 0)),
        out_shape=jax.ShapeDtypeStruct((bsz, s, d), jnp.float32),
        compiler_params=pltpu.CompilerParams(dimension_semantics=("parallel", "parallel")),
        name="final_rms",
    )(x, g.reshape(1, d))


L_MLSTM = 256
_HI = lax.Precision.HIGHEST


def _log_sigmoid(t):
    return jnp.minimum(t, 0.0) - jnp.log(1.0 + jnp.exp(-jnp.abs(t)))


def _mlstm_kernel(qf_ref, kf_ref, vf_ref, gf_ref, gtf_ref, qb_ref, kb_ref, vb_ref, gb_ref, gtb_ref,
                  bias_ref, biast_ref, hf_ref, hb_ref, c_ref, m_ref):
    n = pl.program_id(1)
    ln = qf_ref.shape[1]
    f32, bf16 = jnp.float32, jnp.bfloat16

    @pl.when(n == 0)
    def _():
        c_ref[...] = jnp.zeros_like(c_ref)
        m_ref[...] = jnp.zeros_like(m_ref)

    row = lax.broadcasted_iota(jnp.int32, (ln, ln), 0)
    col = lax.broadcasted_iota(jnp.int32, (ln, ln), 1)
    ones_blk = jnp.ones((ln, A_DV), bf16)
    dirs = ((0, qf_ref, kf_ref, vf_ref, gf_ref, gtf_ref, hf_ref), (1, qb_ref, kb_ref, vb_ref, gb_ref, gtb_ref, hb_ref))
    for d, q_ref, k_ref, v_ref, g_ref, gt_ref, h_ref in dirs:
        mask = (row >= col) if d == 0 else (row <= col)
        tri = mask.astype(f32)
        tri_t = ((row <= col) if d == 0 else (row >= col)).astype(f32)
        g = g_ref[0] + bias_ref[...]
        gt = gt_ref[0] + biast_ref[...]
        ic = g[:, 4 * d:4 * d + 4]
        it = gt[4 * d:4 * d + 4, :]
        bc = jnp.dot(tri, _log_sigmoid(g[:, 8 + 4 * d:12 + 4 * d]), precision=_HI, preferred_element_type=f32)
        bt = jnp.dot(_log_sigmoid(gt[8 + 4 * d:12 + 4 * d, :]), tri_t, precision=_HI, preferred_element_type=f32)
        last = ln - 1 if d == 0 else 0
        for h in range(A_HEADS):
            r = d * A_HEADS + h
            q = q_ref[0, :, h * A_DK:(h + 1) * A_DK]
            k = k_ref[0, :, h * A_DK:(h + 1) * A_DK]
            vaug = jnp.concatenate([v_ref[0, :, h * A_DV:(h + 1) * A_DV], ones_blk], axis=1)
            bcol, icol = bc[:, h:h + 1], ic[:, h:h + 1]
            brow, irow = bt[h:h + 1, :], it[h:h + 1, :]
            m_old = m_ref[r:r + 1, 0:1]
            caug = c_ref[r]
            dmat = jnp.where(mask, bcol - brow + irow, -jnp.inf)
            dmax = jnp.max(dmat, axis=-1, keepdims=True)
            inter = bcol + m_old
            mt = jnp.maximum(inter, dmax)
            w_int = jnp.exp(inter - mt)
            qk = lax.dot_general(q, k, (((1,), (1,)), ((), ())), preferred_element_type=f32)
            sc = jnp.exp(dmat - mt) * qk
            tot = w_int * jnp.dot(q, caug.astype(bf16), preferred_element_type=f32) + jnp.dot(sc.astype(bf16), vaug, preferred_element_type=f32)
            den = jnp.maximum(jnp.abs(tot[:, A_DV:]), jnp.exp(-mt))
            h_ref[0, :, h * A_DV:(h + 1) * A_DV] = tot[:, :A_DV] / den
            btot_c = bcol[last:last + 1, :]
            btot_r = brow[:, last:last + 1]
            m_new = jnp.maximum(btot_r + m_old, jnp.max(btot_r - brow + irow, axis=-1, keepdims=True))
            w_old = jnp.exp(btot_r + m_old - m_new)
            kw = (k.astype(f32) * jnp.exp(btot_c - bcol + icol - m_new)).astype(bf16)
            c_ref[r] = w_old * caug + lax.dot_general(kw, vaug, (((0,), (0,)), ((), ())), preferred_element_type=f32)
            m_ref[r:r + 1, :] = jnp.broadcast_to(m_new, (1, m_ref.shape[1]))


def _mlstm(q, k, v, g, gt, bias):
    bsz, s, _ = q.shape
    ln = min(L_MLSTM, s)
    nc = s // ln
    hk, hv = A_HEADS * A_DK, A_HEADS * A_DV
    fwd = lambda b, n: (b, n, 0)
    bwd = lambda b, n: (b, nc - 1 - n, 0)
    fwd_t = lambda b, n: (b, 0, n)
    bwd_t = lambda b, n: (b, 0, nc - 1 - n)
    const = lambda b, n: (0, 0)
    def specs(im, im_t):
        return [pl.BlockSpec((1, ln, hk), im), pl.BlockSpec((1, ln, hk), im), pl.BlockSpec((1, ln, hv), im),
                pl.BlockSpec((1, ln, 16), im), pl.BlockSpec((1, 16, ln), im_t)]
    return pl.pallas_call(
        _mlstm_kernel,
        grid=(bsz, nc),
        in_specs=specs(fwd, fwd_t) + specs(bwd, bwd_t) + [pl.BlockSpec((1, 16), const), pl.BlockSpec((16, 1), const)],
        out_specs=[pl.BlockSpec((1, ln, hv), fwd), pl.BlockSpec((1, ln, hv), bwd)],
        out_shape=[jax.ShapeDtypeStruct((bsz, s, hv), jnp.float32)] * 2,
        scratch_shapes=[pltpu.VMEM((2 * A_HEADS, A_DK, 2 * A_DV), jnp.float32), pltpu.VMEM((2 * A_HEADS, 128), jnp.float32)],
        compiler_params=pltpu.CompilerParams(dimension_semantics=("parallel", "arbitrary"), vmem_limit_bytes=VMEM_LIMIT),
        name="mlstm",
    )(q, k, v, g, gt, q, k, v, g, gt, bias.reshape(1, 16), bias.reshape(16, 1))


T_GDN = 256
C_GDN = 64
HALO = 8


def _softplus(t):
    return jnp.maximum(t, 0.0) + jnp.log1p(jnp.exp(-jnp.abs(t)))


def _gdn_prep_kernel(x_ref, xp_ref, xn_ref, g_ref, gt_ref, w_ref, a_ref, at_ref, dt_ref, dtt_ref,
                     q_ref, k_ref, v_ref, gc_ref, gr_ref, xe_ref):
    i = pl.program_id(1)
    nt = pl.num_programs(1)
    t = x_ref.shape[1]
    f32 = jnp.float32
    hd = B_HEADS * B_DK
    xe_ref[0:HALO, :] = jnp.where(i > 0, xp_ref[0], 0.0)
    xe_ref[HALO:HALO + t, :] = x_ref[0]
    xe_ref[HALO + t:, :] = jnp.where(i < nt - 1, xn_ref[0], 0.0)
    half = B_CONV // 2
    for part, o_ref in enumerate((q_ref, k_ref, v_ref)):
        cs = slice(part * hd, (part + 1) * hd)
        acc = None
        for j in range(B_CONV):
            term = xe_ref[HALO - half + j:HALO - half + j + t, cs] * w_ref[j:j + 1, cs]
            acc = term if acc is None else acc + term
        y = acc * jax.nn.sigmoid(acc)
        for h in range(B_HEADS):
            yh = y[:, h * B_DK:(h + 1) * B_DK]
            if part == 0:
                yh = yh * lax.rsqrt(jnp.sum(yh * yh, axis=-1, keepdims=True) + EPS) * (B_DK ** -0.5)
            elif part == 1:
                yh = yh * lax.rsqrt(jnp.sum(yh * yh, axis=-1, keepdims=True) + EPS)
            o_ref[0, :, h * B_DK:(h + 1) * B_DK] = yh.astype(o_ref.dtype)
    row = lax.broadcasted_iota(jnp.int32, (t, t), 0)
    col = lax.broadcasted_iota(jnp.int32, (t, t), 1)
    same = (row // C_GDN) == (col // C_GDN)
    lower = (same & (row >= col)).astype(f32)
    upper = (same & (row <= col)).astype(f32)
    g = g_ref[0]
    gt = gt_ref[0]
    nh = B_HEADS
    dec = -jnp.exp(a_ref[...]) * _softplus(g[:, 2 * nh:] + dt_ref[...])
    dect = -jnp.exp(at_ref[...]) * _softplus(gt[2 * nh:, :] + dtt_ref[...])
    gc_ref[0, :, 0:2 * nh] = jax.nn.sigmoid(g[:, 0:2 * nh])
    gc_ref[0, :, 2 * nh:3 * nh] = jnp.dot(lower, dec[:, 0:nh], precision=_HI, preferred_element_type=f32)
    gc_ref[0, :, 3 * nh:] = jnp.dot(upper, dec[:, nh:], precision=_HI, preferred_element_type=f32)
    gr_ref[0, 0:2 * nh, :] = jax.nn.sigmoid(gt[0:2 * nh, :])
    gr_ref[0, 2 * nh:3 * nh, :] = jnp.dot(dect[0:nh, :], upper, precision=_HI, preferred_element_type=f32)
    gr_ref[0, 3 * nh:, :] = jnp.dot(dect[nh:, :], lower, precision=_HI, preferred_element_type=f32)


def _gdn_prep(dqkv, g, gt, conv_w, a_log, dt_bias):
    bsz, s, n3 = dqkv.shape
    t = min(T_GDN, s)
    nt = s // t
    hd = B_HEADS * B_DK
    hb = t // HALO
    cur = lambda b, i: (b, i, 0)
    const = lambda b, i: (0, 0)
    bf16 = jnp.bfloat16
    return pl.pallas_call(
        _gdn_prep_kernel,
        grid=(bsz, nt),
        in_specs=[
            pl.BlockSpec((1, t, n3), cur),
            pl.BlockSpec((1, HALO, n3), lambda b, i: (b, jnp.maximum(i * hb - 1, 0), 0)),
            pl.BlockSpec((1, HALO, n3), lambda b, i: (b, jnp.minimum((i + 1) * hb, s // HALO - 1), 0)),
            pl.BlockSpec((1, t, 16), cur),
            pl.BlockSpec((1, 16, t), lambda b, i: (b, 0, i)),
            pl.BlockSpec((B_CONV, n3), const),
            pl.BlockSpec((1, 8), const), pl.BlockSpec((8, 1), const),
            pl.BlockSpec((1, 8), const), pl.BlockSpec((8, 1), const),
        ],
        out_specs=[pl.BlockSpec((1, t, hd), cur)] * 3 + [pl.BlockSpec((1, t, 16), cur), pl.BlockSpec((1, 16, t), lambda b, i: (b, 0, i))],
        out_shape=[jax.ShapeDtypeStruct((bsz, s, hd), bf16)] * 3 + [jax.ShapeDtypeStruct((bsz, s, 16), jnp.float32), jax.ShapeDtypeStruct((bsz, 16, s), jnp.float32)],
        scratch_shapes=[pltpu.VMEM((t + 2 * HALO, n3), jnp.float32)],
        compiler_params=pltpu.CompilerParams(dimension_semantics=("parallel", "parallel"), vmem_limit_bytes=VMEM_LIMIT),
        name="gdn_prep",
    )(dqkv, dqkv, dqkv, g, gt, conv_w, a_log.reshape(1, 8), a_log.reshape(8, 1), dt_bias.reshape(1, 8), dt_bias.reshape(8, 1))


def _tri_inverse(a, masks):
    eye, m16, m32, m64 = masks
    f32, bf16 = jnp.float32, jnp.bfloat16
    mm = lambda x, y: jnp.dot(x.astype(bf16), y.astype(bf16), preferred_element_type=f32)
    ad = jnp.where(m16, a, 0.0)
    x = eye - ad
    p = mm(ad, ad)
    x = x + mm(x, p)
    p = mm(p, p)
    x = x + mm(x, p)
    p = mm(p, p)
    x = x + mm(x, p)
    for lo, hi in ((m16, m32), (m32, m64)):
        nb = jnp.where(hi & ~lo, a, 0.0)
        x = x - mm(x, mm(nb, x))
    return x


def _gdn_kernel(qf_ref, kf_ref, vf_ref, gcf_ref, grf_ref, qb_ref, kb_ref, vb_ref, gcb_ref, grb_ref, of_ref, ob_ref, s_ref):
    n = pl.program_id(2)
    t = qf_ref.shape[1]
    c = C_GDN
    f32, bf16 = jnp.float32, jnp.bfloat16

    @pl.when(n == 0)
    def _():
        s_ref[...] = jnp.zeros_like(s_ref)

    row = lax.broadcasted_iota(jnp.int32, (c, c), 0)
    col = lax.broadcasted_iota(jnp.int32, (c, c), 1)
    eye = (row == col).astype(f32)
    blk = lambda w: (row // w) == (col // w)
    dirs = ((0, qf_ref, kf_ref, vf_ref, gcf_ref, grf_ref, of_ref), (1, qb_ref, kb_ref, vb_ref, gcb_ref, grb_ref, ob_ref))
    for d, q_ref, k_ref, v_ref, gc_ref, gr_ref, o_ref in dirs:
        incl = (row >= col) if d == 0 else (row <= col)
        strict = (row > col) if d == 0 else (row < col)
        masks = (eye, blk(16), blk(32), blk(64))
        last = c - 1 if d == 0 else 0
        state = s_ref[d]
        order = range(t // c) if d == 0 else range(t // c - 1, -1, -1)
        for ci in order:
            rs = slice(ci * c, (ci + 1) * c)
            q, k, v = q_ref[0, rs, :], k_ref[0, rs, :], v_ref[0, rs, :]
            beta = gc_ref[0, 0, rs, d:d + 1]
            gcol = gc_ref[0, 0, rs, 2 + d:3 + d]
            grow = gr_ref[0, 0, 2 + d:3 + d, rs]
            gam = jnp.exp(jnp.where(incl, gcol - grow, -jnp.inf))
            kk = lax.dot_general(k, k, (((1,), (1,)), ((), ())), preferred_element_type=f32)
            qk = lax.dot_general(q, k, (((1,), (1,)), ((), ())), preferred_element_type=f32)
            a = jnp.where(strict, beta * kk * gam, 0.0)
            tinv = _tri_inverse(a, masks)
            kf, vf = k.astype(f32), v.astype(f32)
            egc = jnp.exp(gcol)
            rhs = jnp.concatenate([beta * vf, (beta * egc) * kf], axis=1).astype(bf16)
            uw = jnp.dot(tinv.astype(bf16), rhs, preferred_element_type=f32)
            u, w = uw[:, :B_DV], uw[:, B_DV:]
            sb = state.astype(bf16)
            wq = jnp.concatenate([w, q.astype(f32) * egc], axis=0).astype(bf16)
            ws = jnp.dot(wq, sb, preferred_element_type=f32)
            v_new = (u - ws[:c]).astype(bf16)
            o_ref[0, rs, :] = ws[c:] + jnp.dot((qk * gam).astype(bf16), v_new, preferred_element_type=f32)
            gl = gcol[last:last + 1, :]
            kd = (kf * jnp.exp(gl - gcol)).astype(bf16)
            state = jnp.exp(gl) * state + lax.dot_general(kd, v_new, (((0,), (0,)), ((), ())), preferred_element_type=f32)
        s_ref[d] = state


def _gdn(q, k, v, gcol, grow):
    bsz, s, _ = q.shape
    t = min(T_GDN, s)
    nb = s // t
    fwd = lambda b, h, n: (b, n, h)
    bwd = lambda b, h, n: (b, nb - 1 - n, h)
    def specs(im, sgn):
        return [pl.BlockSpec((1, t, B_DK), im)] * 3 + [
            pl.BlockSpec((1, 1, t, 4), lambda b, h, n: (b, h, n if sgn else nb - 1 - n, 0)),
            pl.BlockSpec((1, 1, 4, t), lambda b, h, n: (b, h, 0, n if sgn else nb - 1 - n))]
    return pl.pallas_call(
        _gdn_kernel,
        grid=(bsz, B_HEADS, nb),
        in_specs=specs(fwd, True) + specs(bwd, False),
        out_specs=[pl.BlockSpec((1, t, B_DV), fwd), pl.BlockSpec((1, t, B_DV), bwd)],
        out_shape=[jax.ShapeDtypeStruct((bsz, s, B_HEADS * B_DV), jnp.float32)] * 2,
        scratch_shapes=[pltpu.VMEM((2, B_DK, B_DV), jnp.float32)],
        compiler_params=pltpu.CompilerParams(dimension_semantics=("parallel", "parallel", "arbitrary"), vmem_limit_bytes=VMEM_LIMIT),
        name="gdn",
    )(q, k, v, gcol, grow, q, k, v, gcol, grow)


def _split(p, sizes):
    return jnp.split(p, np.cumsum(sizes)[:-1].tolist(), axis=-1)


def _layernorm(x, g, b):
    xc = x - jnp.mean(x, axis=-1, keepdims=True)
    y = xc * lax.rsqrt(jnp.mean(xc * xc, axis=-1, keepdims=True) + EPS)
    return y * g + b


def _head_rms(t, g):
    bsz, s, h, d = t.shape
    y = t * lax.rsqrt(jnp.mean(t * t, axis=-1, keepdims=True) + EPS)
    return y.reshape(bsz, s, h * d) * g


def _l2n(t):
    return t * lax.rsqrt(jnp.sum(t * t, axis=-1, keepdims=True) + EPS)


def _dwconv(x, w):
    return lax.conv_general_dilated(x, w[:, None, :].astype(x.dtype), window_strides=(1,), padding='SAME', dimension_numbers=('NWC', 'WIO', 'NWC'), feature_group_count=x.shape[-1])


def _flip(t):
    return jnp.flip(t, axis=1)


def _to_chunks(t):
    bsz, s, h = t.shape[:3]
    t = t.reshape((bsz, s // CHUNK, CHUNK, h) + t.shape[3:])
    return jnp.moveaxis(t, (1, 3), (0, 2))


def _from_chunks(t):
    nc, bsz, h, l = t.shape[:4]
    t = jnp.moveaxis(t, (0, 2), (1, 3))
    return t.reshape((bsz, nc * l, h) + t.shape[4:])


def _mlstm_chunkwise(q, k, v, i_pre, logf):
    q, k, v, i_pre, logf = (_to_chunks(t) for t in (q, k, v, i_pre, logf))
    nc, bsz, h = q.shape[:3]
    causal = jnp.tril(jnp.ones((CHUNK, CHUNK), dtype=bool))
    b = jnp.cumsum(logf, axis=-1)
    dmat = jnp.where(causal, b[..., :, None] - b[..., None, :] + i_pre[..., None, :], -jnp.inf)
    dmax = jnp.max(dmat, axis=-1)
    qk = jnp.einsum('nbhld,nbhsd->nbhls', q, k)
    a_end = b[..., -1:] - b + i_pre

    def step(carry, xs):
        cmat, nvec, m = carry
        qc, kc, vc, bc, dc, dmc, qkc, aec = xs
        inter = bc + m[..., None]
        mt = jnp.maximum(inter, dmc)
        w_int = jnp.exp(inter - mt)
        sc = jnp.exp(dc - mt[..., None]) * qkc
        num = w_int[..., None] * jnp.einsum('bhld,bhde->bhle', qc, cmat) + jnp.einsum('bhls,bhse->bhle', sc, vc)
        den = w_int * jnp.einsum('bhld,bhd->bhl', qc, nvec) + jnp.sum(sc, axis=-1)
        hc = num / jnp.maximum(jnp.abs(den), jnp.exp(-mt))[..., None]
        m_new = jnp.maximum(bc[..., -1] + m, jnp.max(aec, axis=-1))
        w_old = jnp.exp(bc[..., -1] + m - m_new)
        kw = kc * jnp.exp(aec - m_new[..., None])[..., None]
        cmat = w_old[..., None, None] * cmat + jnp.einsum('bhld,bhle->bhde', kw, vc)
        nvec = w_old[..., None] * nvec + jnp.sum(kw, axis=-2)
        return (cmat, nvec, m_new), hc

    init = (jnp.zeros((bsz, h, A_DK, A_DV), jnp.float32), jnp.zeros((bsz, h, A_DK), jnp.float32), jnp.zeros((bsz, h), jnp.float32))
    _, hs = lax.scan(step, init, (q, k, v, b, dmat, dmax, qk, a_end))
    return _from_chunks(hs)


def _gdn_chunked(q, k, v, beta, g):
    q, k, v, beta, g = (_to_chunks(t) for t in (q, k, v, beta, g))
    nc, bsz, h = q.shape[:3]
    tril = jnp.tril(jnp.ones((CHUNK, CHUNK), dtype=bool))
    strict = jnp.tril(jnp.ones((CHUNK, CHUNK), dtype=bool), -1)
    gc = jnp.cumsum(g, axis=-1)
    gam = jnp.exp(jnp.where(tril, gc[..., :, None] - gc[..., None, :], -jnp.inf))
    a = jnp.where(strict, beta[..., :, None] * jnp.einsum('nbhid,nbhjd->nbhij', k, k) * gam, 0.0)
    tmat = a + jnp.eye(CHUNK, dtype=a.dtype)
    u = lax.linalg.triangular_solve(tmat, beta[..., None] * v, left_side=True, lower=True, unit_diagonal=True)
    w = lax.linalg.triangular_solve(tmat, (beta * jnp.exp(gc))[..., None] * k, left_side=True, lower=True, unit_diagonal=True)
    attn = jnp.einsum('nbhid,nbhjd->nbhij', q, k) * gam

    def step(state, xs):
        qc, kc, uc, wc, gcc, ac = xs
        v_new = uc - jnp.einsum('bhld,bhde->bhle', wc, state)
        o = jnp.einsum('bhld,bhde->bhle', qc * jnp.exp(gcc)[..., None], state) + jnp.einsum('bhls,bhse->bhle', ac, v_new)
        gl = gcc[..., -1]
        state = jnp.exp(gl)[..., None, None] * state + jnp.einsum('bhld,bhle->bhde', kc * jnp.exp(gl[..., None] - gcc)[..., None], v_new)
        return state, o

    _, os_ = lax.scan(step, jnp.zeros((bsz, h, B_DK, B_DV), jnp.float32), (q, k, u, w, gc, attn))
    return _from_chunks(os_)


def _t5_bucket(rel):
    half = REL_BUCKETS // 2
    exact = half // 2
    n = jnp.abs(rel)
    large = exact + (jnp.log(jnp.maximum(n, 1).astype(jnp.float32) / exact) / math.log(REL_MAX_DIST / exact) * (half - exact)).astype(jnp.int32)
    large = jnp.minimum(large, half - 1)
    return (rel > 0).astype(jnp.int32) * half + jnp.where(n < exact, n, large)


def _dilated_group(q, k, v, dilation, radius, rel_bias):
    bsz, s, h, dh = q.shape
    ls = s // dilation
    nb = -(-ls // radius)
    lp = nb * radius

    def sub(t, lo, hi):
        t = t.reshape(bsz, ls, dilation, h, dh).transpose(0, 3, 2, 1, 4)
        return jnp.pad(t, ((0, 0), (0, 0), (0, 0), (lo, hi), (0, 0)))

    qb = sub(q, 0, lp - ls).reshape(bsz, h, dilation, nb, radius, dh)

    def band(t):
        t = sub(t, radius, lp - ls + radius).reshape(bsz, h, dilation, nb + 2, radius, dh)
        return jnp.concatenate([t[:, :, :, :-2], t[:, :, :, 1:-1], t[:, :, :, 2:]], axis=4)

    kb, vb = band(k), band(v)
    qi = jnp.arange(radius)[:, None]
    kj = jnp.arange(3 * radius)[None, :]
    rel = kj - radius - qi
    kpos = jnp.arange(nb)[:, None, None] * radius + kj - radius
    valid = (jnp.abs(rel) <= radius) & (kpos >= 0) & (kpos < ls)
    bias = jnp.transpose(rel_bias[_t5_bucket(rel * dilation)], (2, 0, 1)).astype(jnp.float32)
    sc = jnp.einsum('bhrnid,bhrnjd->bhrnij', qb, kb).astype(jnp.float32) * (dh ** -0.5) + bias[:, None, None]
    sc = jnp.where(valid, sc, NEG)
    m = jnp.max(sc, axis=-1, keepdims=True)
    p = jnp.exp(sc - m)
    den = jnp.sum(p, axis=-1)
    o = jnp.einsum('bhrnij,bhrnjd->bhrnid', p, vb.astype(jnp.float32)) / den[..., None]
    lse = m[..., 0] + jnp.log(den)
    o = o.reshape(bsz, h, dilation, lp, dh)[:, :, :, :ls].transpose(0, 3, 2, 1, 4).reshape(bsz, s, h, dh)
    lse = lse.reshape(bsz, h, dilation, lp)[:, :, :, :ls].transpose(0, 3, 2, 1).reshape(bsz, s, h)
    return o, lse


def _dilated_attention(q, k, v, rel_bias):
    outs, lses = [], []
    for window, dilation in D_GROUPS:
        o, l = _dilated_group(q, k, v, dilation, window // (2 * dilation), rel_bias)
        outs.append(o)
        lses.append(l)
    wts = jax.nn.softmax(jnp.stack(lses, axis=0), axis=0)
    return jnp.sum(wts[..., None] * jnp.stack(outs, axis=0), axis=0)


def _even_mixer_core(p, m_gate_b, dn_dt_bias, dn_a_log, dn_conv_w, m_norm_g, dn_norm_g):
    bsz, s, _ = p.shape
    f32 = jnp.float32
    mq, mk, mv, mo, mg, dqkv, dg, z = _split(p, EVEN_SPLITS)
    q = mq.reshape(bsz, s, A_HEADS, A_DK)
    k = mk.reshape(bsz, s, A_HEADS, A_DK) * (A_DK ** -0.5)
    v = mv.reshape(bsz, s, A_HEADS, A_DV)
    gt = mg.reshape(bsz, s, 4, A_HEADS) + m_gate_b
    logf = jax.nn.log_sigmoid(gt[:, :, 2:4])
    h_fwd = _mlstm_chunkwise(q, k, v, gt[:, :, 0], logf[:, :, 0])
    h_bwd = _flip(_mlstm_chunkwise(_flip(q), _flip(k), _flip(v), _flip(gt[:, :, 1]), _flip(logf[:, :, 1])))
    out_a = jax.nn.sigmoid(mo) * _head_rms(h_fwd + h_bwd, m_norm_g)
    qkv = jax.nn.silu(_dwconv(dqkv, dn_conv_w))
    bq, bk, bv = _split(qkv, (B_HEADS * B_DK, B_HEADS * B_DK, B_HEADS * B_DV))
    q = _l2n(bq.reshape(bsz, s, B_HEADS, B_DK)) * (B_DK ** -0.5)
    k = _l2n(bk.reshape(bsz, s, B_HEADS, B_DK))
    v = bv.reshape(bsz, s, B_HEADS, B_DV)
    gb = dg.reshape(bsz, s, 4, B_HEADS)
    beta = jax.nn.sigmoid(gb[:, :, 0:2])
    decay = -jnp.exp(dn_a_log) * jax.nn.softplus(gb[:, :, 2:4] + dn_dt_bias)
    o_fwd = _gdn_chunked(q, k, v, beta[:, :, 0], decay[:, :, 0])
    o_bwd = _flip(_gdn_chunked(_flip(q), _flip(k), _flip(v), _flip(beta[:, :, 1]), _flip(decay[:, :, 1])))
    out_b = _head_rms(o_fwd + o_bwd, dn_norm_g)
    return jnp.concatenate([out_a, out_b], axis=-1), z


def _odd_mixer_core(p, dw_w, dw_b, ln_g, ln_b, rel_bias):
    bsz, s, _ = p.shape
    ga, gb, aq, ak, av, z = _split(p, ODD_SPLITS)
    u = _dwconv(ga * jax.nn.sigmoid(gb), dw_w) + dw_b
    out_c = jax.nn.silu(_layernorm(u, ln_g, ln_b))
    shp = (bsz, s, D_HEADS, D_DH)
    out_d = _dilated_attention(aq.reshape(shp), ak.reshape(shp), av.reshape(shp), rel_bias).reshape(bsz, s, D_HEADS * D_DH)
    return jnp.concatenate([out_c, out_d], axis=-1), z


def kernel(x, c, norm_g, ada_w, ada_b, ev_w_in, ev_m_gate_b, ev_dn_dt_bias, ev_dn_a_log, ev_dn_conv_w, ev_m_norm_g, ev_dn_norm_g, ev_w_out, od_w_in, od_dw_w, od_dw_b, od_ln_g, od_ln_b, od_w_out, rel_bias, final_g):
    assert DEPTH == 2, "the final RMSNorm is fused into the (last) odd layer's output projection"
    bf16 = jnp.bfloat16
    bsz, s, d = x.shape
    mod = _adaln(c, ada_w, ada_b)
    for layer in range(DEPTH):
        shift, scale, gate = (mod[layer, :, i * d:(i + 1) * d][:, None, :] for i in range(3))
        j = layer // 2
        if layer % 2 == 0:
            mq, mk, mv, mo, mg, dqkv, dg, z = _split(ev_w_in[j], EVEN_SPLITS)
            w = jnp.concatenate([mq, mk * (A_DK ** -0.5), mv, mo, dqkv, z], axis=1).astype(bf16)
            wg = jnp.concatenate([mg, dg], axis=1).astype(bf16)
            pq, pk, pv, po, pdqkv, pz, g_m, g_d, gt_m, gt_d = _inproj_even(x, norm_g[layer], scale, shift, w, wg, wg.T)
            hf, hb = _mlstm(pq, pk, pv, g_m, gt_m, ev_m_gate_b[j].reshape(16))
            bq, bk, bv, gc, gr = _gdn_prep(pdqkv, g_d, gt_d, ev_dn_conv_w[j], ev_dn_a_log[j], ev_dn_dt_bias[j])
            gcol = gc.reshape(bsz, s, 4, B_HEADS).transpose(0, 3, 1, 2)
            grow = gr.reshape(bsz, 4, B_HEADS, s).transpose(0, 2, 1, 3)
            of, ob = _gdn(bq, bk, bv, gcol, grow)
            x = _outproj_even(hf, hb, of, ob, po, pz, x, gate, ev_m_norm_g[j], ev_dn_norm_g[j], ev_w_out[j].astype(bf16))
        else:
            ga, gb, aq, ak, av, z = _split(od_w_in[j], ODD_SPLITS)
            w = jnp.concatenate([ga, gb, aq * (D_DH ** -0.5), ak, av, z], axis=1).astype(bf16)
            glu, pq, pk, pv, pz = _inproj_odd(x, norm_g[layer], scale, shift, w)
            out_c = _conformer(glu, od_dw_w[j], od_dw_b[j], od_ln_g[j], od_ln_b[j])
            og, lg = zip(*[_dilated_group_call(pq, pk, pv, rel_bias, dilation) for _, dilation in D_GROUPS])
            x = _outproj_odd_final(out_c, og, lg, pz, x, gate, final_g, od_w_out[j].astype(bf16))
    return x
```

```python
import math
from functools import partial

import jax
import jax.numpy as jnp
import numpy as np
from jax import lax
from jax.experimental import pallas as pl
from jax.experimental.pallas import tpu as pltpu

D_MODEL = 1024
BATCH = 4
SEQ = 8192
DEPTH = 2
A_HEADS = 4
A_DK = 64
A_DV = 128
B_HEADS = 4
B_DK = 128
B_DV = 128
B_CONV = 5
C_WIDTH = 512
C_CONV = 31
D_HEADS = 8
D_DH = 64
D_GROUPS = ((128, 1), (512, 4), (2048, 16))
REL_BUCKETS = 32
REL_MAX_DIST = 1024
CHUNK = 64
EPS = 1e-6
NEG = -1e30
MIX_EVEN = A_HEADS * A_DV + B_HEADS * B_DV
MIX_ODD = C_WIDTH + D_HEADS * D_DH
B_QKV = B_HEADS * (2 * B_DK + B_DV)
EVEN_SPLITS = (A_HEADS * A_DK, A_HEADS * A_DK, A_HEADS * A_DV, A_HEADS * A_DV, 4 * A_HEADS, B_QKV, 4 * B_HEADS, MIX_EVEN)
ODD_SPLITS = (C_WIDTH, C_WIDTH, D_HEADS * D_DH, D_HEADS * D_DH, D_HEADS * D_DH, MIX_ODD)

VMEM_LIMIT = 56 * 1024 * 1024
TM_PROJ = 256


def _adaln_kernel(c_ref, w_ref, b_ref, o_ref):
    c = c_ref[...]
    cs = (c * jax.nn.sigmoid(c)).astype(jnp.bfloat16)
    o_ref[0] = jnp.dot(cs, w_ref[0].astype(jnp.bfloat16), preferred_element_type=jnp.float32) + b_ref[0]


def _adaln(c, ada_w, ada_b):
    depth, d, n3 = ada_w.shape
    bsz = c.shape[0]
    tn = 1024
    return pl.pallas_call(
        _adaln_kernel,
        grid=(depth, n3 // tn),
        in_specs=[pl.BlockSpec((bsz, d), lambda l, j: (0, 0)), pl.BlockSpec((1, d, tn), lambda l, j: (l, 0, j)),
                  pl.BlockSpec((1, 1, tn), lambda l, j: (l, 0, j))],
        out_specs=pl.BlockSpec((1, bsz, tn), lambda l, j: (l, 0, j)),
        out_shape=jax.ShapeDtypeStruct((depth, bsz, n3), jnp.float32),
        compiler_params=pltpu.CompilerParams(dimension_semantics=("parallel", "parallel")),
        name="adaln",
    )(c, ada_w, ada_b.reshape(depth, 1, n3))


def _modulated_rms(x_ref, g_ref, sc_ref, sh_ref):
    x = x_ref[0]
    y = x * lax.rsqrt(jnp.mean(x * x, axis=-1, keepdims=True) + EPS)
    return ((y * g_ref[...]) * (1.0 + sc_ref[0]) + sh_ref[0]).astype(jnp.bfloat16)


_EV_COLS = {"mq": (0, 256), "mk": (256, 512), "mv": (512, 1024), "mo": (1024, 1536), "dqkv": (1536, 3072), "z": (3072, 4096)}
_OD_COLS = {"ga": (0, 512), "gb": (512, 1024), "aq": (1024, 1536), "ak": (1536, 2048), "av": (2048, 2560), "z": (2560, 3584)}


def _inproj_even_kernel(x_ref, g_ref, sc_ref, sh_ref, w_ref, wg_ref, wgt_ref,
                        mq_ref, mk_ref, mv_ref, mo_ref, dqkv_ref, z_ref, mg_ref, dg_ref, mgt_ref, dgt_ref):
    f32 = jnp.float32
    h = _modulated_rms(x_ref, g_ref, sc_ref, sh_ref)
    for name, o_ref in (("mq", mq_ref), ("mk", mk_ref), ("mv", mv_ref), ("mo", mo_ref), ("dqkv", dqkv_ref), ("z", z_ref)):
        lo, hi = _EV_COLS[name]
        o_ref[0] = jnp.dot(h, w_ref[:, lo:hi], preferred_element_type=f32).astype(o_ref.dtype)
    gates = jnp.dot(h, wg_ref[...], preferred_element_type=f32)
    gates_t = lax.dot_general(wgt_ref[...], h, (((1,), (1,)), ((), ())), preferred_element_type=f32)
    mg_ref[0] = gates[:, :16]
    dg_ref[0] = gates[:, 16:]
    mgt_ref[0] = gates_t[:16]
    dgt_ref[0] = gates_t[16:]


def _inproj_even(x, g, scale, shift, w, wg, wgt):
    bsz, s, d = x.shape
    tm = TM_PROJ
    tok = lambda b, i: (b, i, 0)
    tok_t = lambda b, i: (b, 0, i)
    const = lambda b, i: (0, 0)
    bvec = lambda b, i: (b, 0, 0)
    bf16, f32 = jnp.bfloat16, jnp.float32
    outs = [("mq", bf16), ("mk", bf16), ("mv", bf16), ("mo", f32), ("dqkv", f32), ("z", f32)]
    widths = [_EV_COLS[n][1] - _EV_COLS[n][0] for n, _ in outs]
    return pl.pallas_call(
        _inproj_even_kernel,
        grid=(bsz, s // tm),
        in_specs=[pl.BlockSpec((1, tm, d), tok), pl.BlockSpec((1, d), const), pl.BlockSpec((1, 1, d), bvec), pl.BlockSpec((1, 1, d), bvec),
                  pl.BlockSpec(w.shape, const), pl.BlockSpec(wg.shape, const), pl.BlockSpec(wgt.shape, const)],
        out_specs=[pl.BlockSpec((1, tm, wd), tok) for wd in widths] + [pl.BlockSpec((1, tm, 16), tok)] * 2 + [pl.BlockSpec((1, 16, tm), tok_t)] * 2,
        out_shape=[jax.ShapeDtypeStruct((bsz, s, wd), dt) for wd, (_, dt) in zip(widths, outs)]
        + [jax.ShapeDtypeStruct((bsz, s, 16), f32)] * 2 + [jax.ShapeDtypeStruct((bsz, 16, s), f32)] * 2,
        compiler_params=pltpu.CompilerParams(dimension_semantics=("parallel", "parallel"), vmem_limit_bytes=VMEM_LIMIT),
        name="inproj_even",
    )(x, g.reshape(1, d), scale, shift, w, wg, wgt)


def _inproj_odd_kernel(x_ref, g_ref, sc_ref, sh_ref, w_ref, glu_ref, aq_ref, ak_ref, av_ref, z_ref):
    f32 = jnp.float32
    h = _modulated_rms(x_ref, g_ref, sc_ref, sh_ref)
    dot = lambda name: jnp.dot(h, w_ref[:, _OD_COLS[name][0]:_OD_COLS[name][1]], preferred_element_type=f32)
    glu_ref[0] = dot("ga") * jax.nn.sigmoid(dot("gb"))
    for name, o_ref in (("aq", aq_ref), ("ak", ak_ref), ("av", av_ref), ("z", z_ref)):
        o_ref[0] = dot(name).astype(o_ref.dtype)


def _inproj_odd(x, g, scale, shift, w):
    bsz, s, d = x.shape
    tm = TM_PROJ
    tok = lambda b, i: (b, i, 0)
    const = lambda b, i: (0, 0)
    bvec = lambda b, i: (b, 0, 0)
    bf16, f32 = jnp.bfloat16, jnp.float32
    outs = [(C_WIDTH, f32), (512, bf16), (512, bf16), (512, bf16), (MIX_ODD, f32)]
    return pl.pallas_call(
        _inproj_odd_kernel,
        grid=(bsz, s // tm),
        in_specs=[pl.BlockSpec((1, tm, d), tok), pl.BlockSpec((1, d), const), pl.BlockSpec((1, 1, d), bvec), pl.BlockSpec((1, 1, d), bvec),
                  pl.BlockSpec(w.shape, const)],
        out_specs=[pl.BlockSpec((1, tm, wd), tok) for wd, _ in outs],
        out_shape=[jax.ShapeDtypeStruct((bsz, s, wd), dt) for wd, dt in outs],
        compiler_params=pltpu.CompilerParams(dimension_semantics=("parallel", "parallel"), vmem_limit_bytes=VMEM_LIMIT),
        name="inproj_odd",
    )(x, g.reshape(1, d), scale, shift, w)


def _head_rms_cols(t, g, width):
    parts = []
    for h in range(t.shape[1] // width):
        th = t[:, h * width:(h + 1) * width]
        parts.append(th * lax.rsqrt(jnp.mean(th * th, axis=-1, keepdims=True) + EPS))
    return jnp.concatenate(parts, axis=1) * g


def _outproj_even_kernel(hf_ref, hb_ref, of_ref, ob_ref, mo_ref, z_ref, x_ref, gate_ref, mg_ref, dg_ref, w_ref, o_ref):
    f32, bf16 = jnp.float32, jnp.bfloat16
    z = z_ref[0]
    sz = z * jax.nn.sigmoid(z)
    na = A_HEADS * A_DV
    out_a = jax.nn.sigmoid(mo_ref[0]) * _head_rms_cols(hf_ref[0] + hb_ref[0], mg_ref[...], A_DV)
    out_b = _head_rms_cols(of_ref[0] + ob_ref[0], dg_ref[...], B_DV)
    y = jnp.dot((out_a * sz[:, :na]).astype(bf16), w_ref[:na, :], preferred_element_type=f32)
    y = y + jnp.dot((out_b * sz[:, na:]).astype(bf16), w_ref[na:, :], preferred_element_type=f32)
    o_ref[0] = x_ref[0] + gate_ref[0] * y


def _outproj_even(hf, hb, of, ob, mo, z, x, gate, m_norm_g, dn_norm_g, w):
    bsz, s, d = x.shape
    tm = TM_PROJ
    tok = lambda b, i: (b, i, 0)
    const = lambda b, i: (0, 0)
    bvec = lambda b, i: (b, 0, 0)
    na, nb = A_HEADS * A_DV, B_HEADS * B_DV
    return pl.pallas_call(
        _outproj_even_kernel,
        grid=(bsz, s // tm),
        in_specs=[pl.BlockSpec((1, tm, na), tok)] * 2 + [pl.BlockSpec((1, tm, nb), tok)] * 2 + [pl.BlockSpec((1, tm, na), tok),
                  pl.BlockSpec((1, tm, na + nb), tok), pl.BlockSpec((1, tm, d), tok), pl.BlockSpec((1, 1, d), bvec),
                  pl.BlockSpec((1, na), const), pl.BlockSpec((1, nb), const), pl.BlockSpec(w.shape, const)],
        out_specs=pl.BlockSpec((1, tm, d), tok),
        out_shape=jax.ShapeDtypeStruct((bsz, s, d), jnp.float32),
        compiler_params=pltpu.CompilerParams(dimension_semantics=("parallel", "parallel"), vmem_limit_bytes=VMEM_LIMIT),
        name="outproj_even",
    )(hf, hb, of, ob, mo, z, x, gate, m_norm_g.reshape(1, na), dn_norm_g.reshape(1, nb), w)


def _outproj_odd_kernel(oc_ref, o1_ref, o2_ref, o3_ref, l1_ref, l2_ref, l3_ref, z_ref, x_ref, gate_ref, fg_ref, w_ref, o_ref):
    f32, bf16 = jnp.float32, jnp.bfloat16
    z = z_ref[0]
    sz = z * jax.nn.sigmoid(z)
    l1, l2, l3 = l1_ref[0], l2_ref[0], l3_ref[0]
    lm = jnp.maximum(jnp.maximum(l1, l2), l3)
    e1, e2, e3 = jnp.exp(l1 - lm), jnp.exp(l2 - lm), jnp.exp(l3 - lm)
    inv = 1.0 / (e1 + e2 + e3)
    w1, w2, w3 = e1 * inv, e2 * inv, e3 * inv
    o1, o2, o3 = o1_ref[0], o2_ref[0], o3_ref[0]
    parts = []
    for h in range(D_HEADS):
        cs = slice(h * D_DH, (h + 1) * D_DH)
        parts.append(w1[:, h:h + 1] * o1[:, cs] + w2[:, h:h + 1] * o2[:, cs] + w3[:, h:h + 1] * o3[:, cs])
    out_d = jnp.concatenate(parts, axis=1)
    y = jnp.dot((oc_ref[0] * sz[:, :C_WIDTH]).astype(bf16), w_ref[:C_WIDTH, :], preferred_element_type=f32)
    y = y + jnp.dot((out_d * sz[:, C_WIDTH:]).astype(bf16), w_ref[C_WIDTH:, :], preferred_element_type=f32)
    xn = x_ref[0] + gate_ref[0] * y
    o_ref[0] = xn * lax.rsqrt(jnp.mean(xn * xn, axis=-1, keepdims=True) + EPS) * fg_ref[...]


def _outproj_odd_final(oc, og, lg, z, x, gate, final_g, w):
    bsz, s, d = x.shape
    tm = TM_PROJ
    tok = lambda b, i: (b, i, 0)
    const = lambda b, i: (0, 0)
    bvec = lambda b, i: (b, 0, 0)
    nd = D_HEADS * D_DH
    return pl.pallas_call(
        _outproj_odd_kernel,
        grid=(bsz, s // tm),
        in_specs=[pl.BlockSpec((1, tm, C_WIDTH), tok)] + [pl.BlockSpec((1, tm, nd), tok)] * 3 + [pl.BlockSpec((1, tm, 128), tok)] * 3
        + [pl.BlockSpec((1, tm, MIX_ODD), tok), pl.BlockSpec((1, tm, d), tok), pl.BlockSpec((1, 1, d), bvec), pl.BlockSpec((1, d), const),
           pl.BlockSpec(w.shape, const)],
        out_specs=pl.BlockSpec((1, tm, d), tok),
        out_shape=jax.ShapeDtypeStruct((bsz, s, d), jnp.float32),
        compiler_params=pltpu.CompilerParams(dimension_semantics=("parallel", "parallel"), vmem_limit_bytes=VMEM_LIMIT),
        name="outproj_odd",
    )(oc, *og, *lg, z, x, gate, final_g.reshape(1, d), w)


T_CONV = 512
HALO_C = 16
SUB_C = 64


def _conformer_kernel(x_ref, xp_ref, xn_ref, w_ref, b_ref, lg_ref, lb_ref, o_ref, xe_ref):
    i = pl.program_id(1)
    nt = pl.num_programs(1)
    t = x_ref.shape[1]
    xe_ref[0:HALO_C, :] = jnp.where(i > 0, xp_ref[0], 0.0)
    xe_ref[HALO_C:HALO_C + t, :] = x_ref[0]
    xe_ref[HALO_C + t:, :] = jnp.where(i < nt - 1, xn_ref[0], 0.0)
    half = C_CONV // 2
    for r0 in range(0, t, SUB_C):
        acc = None
        for j in range(C_CONV):
            lo = HALO_C - half + j + r0
            term = xe_ref[lo:lo + SUB_C, :] * w_ref[j:j + 1, :]
            acc = term if acc is None else acc + term
        u = acc + b_ref[...]
        uc = u - jnp.mean(u, axis=-1, keepdims=True)
        y = uc * lax.rsqrt(jnp.mean(uc * uc, axis=-1, keepdims=True) + EPS) * lg_ref[...] + lb_ref[...]
        o_ref[0, r0:r0 + SUB_C, :] = y * jax.nn.sigmoid(y)


def _conformer(glu, dw_w, dw_b, ln_g, ln_b):
    bsz, s, cw = glu.shape
    t = min(T_CONV, s)
    hb = t // HALO_C
    cur = lambda b, i: (b, i, 0)
    const = lambda b, i: (0, 0)
    return pl.pallas_call(
        _conformer_kernel,
        grid=(bsz, s // t),
        in_specs=[pl.BlockSpec((1, t, cw), cur),
                  pl.BlockSpec((1, HALO_C, cw), lambda b, i: (b, jnp.maximum(i * hb - 1, 0), 0)),
                  pl.BlockSpec((1, HALO_C, cw), lambda b, i: (b, jnp.minimum((i + 1) * hb, s // HALO_C - 1), 0)),
                  pl.BlockSpec((C_CONV, cw), const)] + [pl.BlockSpec((1, cw), const)] * 3,
        out_specs=pl.BlockSpec((1, t, cw), cur),
        out_shape=jax.ShapeDtypeStruct((bsz, s, cw), jnp.float32),
        scratch_shapes=[pltpu.VMEM((t + 2 * HALO_C, cw), jnp.float32)],
        compiler_params=pltpu.CompilerParams(dimension_semantics=("parallel", "parallel"), vmem_limit_bytes=VMEM_LIMIT),
        name="conformer",
    )(glu, glu, glu, dw_w, dw_b.reshape(1, cw), ln_g.reshape(1, cw), ln_b.reshape(1, cw))


TQ_ATT = 128
R_ATT = 64


def _dilated_kernel(q_ref, kc_ref, kp_ref, kn_ref, vc_ref, vp_ref, vn_ref, bias_ref, o_ref, lse_ref, kx_ref, vx_ref):
    i = pl.program_id(2)
    nt = pl.num_programs(2)
    tq = q_ref.shape[1]
    nk = tq + 2 * R_ATT
    f32, bf16 = jnp.float32, jnp.bfloat16
    kx_ref[0:R_ATT, :] = kp_ref[0]
    kx_ref[R_ATT:R_ATT + tq, :] = kc_ref[0]
    kx_ref[R_ATT + tq:, :] = kn_ref[0]
    vx_ref[0:R_ATT, :] = vp_ref[0]
    vx_ref[R_ATT:R_ATT + tq, :] = vc_ref[0]
    vx_ref[R_ATT + tq:, :] = vn_ref[0]
    kj = lax.broadcasted_iota(jnp.int32, (tq, nk), 1)
    outside = ((kj < R_ATT) & (i == 0)) | ((kj >= R_ATT + tq) & (i == nt - 1))
    lane = lax.broadcasted_iota(jnp.int32, (tq, 128), 1)
    lse_all = jnp.zeros((tq, 128), f32)
    for h in range(D_HEADS):
        cs = slice(h * D_DH, (h + 1) * D_DH)
        sc = lax.dot_general(q_ref[0, :, cs], kx_ref[:, cs], (((1,), (1,)), ((), ())), preferred_element_type=f32) + bias_ref[h]
        sc = jnp.where(outside, NEG, sc)
        m = jnp.max(sc, axis=-1, keepdims=True)
        p = jnp.exp(sc - m)
        den = jnp.sum(p, axis=-1, keepdims=True)
        o_ref[0, :, cs] = jnp.dot(p.astype(bf16), vx_ref[:, cs], preferred_element_type=f32) / den
        lse_all = jnp.where(lane == h, m + jnp.log(den), lse_all)
    lse_ref[0] = lse_all


def _dilated_bias(rel_bias, dilation, tq):
    half = REL_BUCKETS // 2
    exact = half // 2
    qi = jnp.arange(tq)[:, None]
    kj = jnp.arange(tq + 2 * R_ATT)[None, :]
    rel = kj - R_ATT - qi
    reld = rel * dilation
    n = jnp.abs(reld)
    large = exact + (jnp.log(jnp.maximum(n, 1).astype(jnp.float32) / exact) / math.log(REL_MAX_DIST / exact) * (half - exact)).astype(jnp.int32)
    large = jnp.minimum(large, half - 1)
    bucket = (reld > 0).astype(jnp.int32) * half + jnp.where(n < exact, n, large)
    bias = jnp.zeros((rel_bias.shape[1],) + bucket.shape, jnp.float32)
    for b in range(REL_BUCKETS):
        bias = jnp.where((bucket == b)[None], rel_bias[b].astype(jnp.float32)[:, None, None], bias)
    return jnp.where((jnp.abs(rel) <= R_ATT)[None], bias, NEG)


def _dilated_group_call(q, k, v, rel_bias, dilation):
    bsz, s, nd = q.shape
    ls = s // dilation
    tq = min(TQ_ATT, ls)
    nt = ls // tq
    hb = tq // R_ATT
    nk = tq + 2 * R_ATT
    view = lambda t: t.reshape(bsz, ls, dilation * t.shape[-1])
    cur = lambda b, r, i: (b, i, r)
    prev = lambda b, r, i: (b, jnp.maximum(i * hb - 1, 0), r)
    nxt = lambda b, r, i: (b, jnp.minimum((i + 1) * hb, ls // R_ATT - 1), r)
    qv, kv, vv = view(q), view(k), view(v)
    kv_specs = [pl.BlockSpec((1, tq, nd), cur), pl.BlockSpec((1, R_ATT, nd), prev), pl.BlockSpec((1, R_ATT, nd), nxt)]
    o, lse = pl.pallas_call(
        _dilated_kernel,
        grid=(bsz, dilation, nt),
        in_specs=[pl.BlockSpec((1, tq, nd), cur)] + kv_specs + kv_specs + [pl.BlockSpec((D_HEADS, tq, nk), lambda b, r, i: (0, 0, 0))],
        out_specs=[pl.BlockSpec((1, tq, nd), cur), pl.BlockSpec((1, tq, 128), cur)],
        out_shape=[jax.ShapeDtypeStruct((bsz, ls, dilation * nd), jnp.float32), jax.ShapeDtypeStruct((bsz, ls, dilation * 128), jnp.float32)],
        scratch_shapes=[pltpu.VMEM((nk, nd), jnp.bfloat16)] * 2,
        compiler_params=pltpu.CompilerParams(dimension_semantics=("parallel", "parallel", "parallel"), vmem_limit_bytes=VMEM_LIMIT),
        name=f"dilated_d{dilation}",
    )(qv, kv, kv, kv, vv, vv, vv, _dilated_bias(rel_bias, dilation, tq))
    return o.reshape(bsz, s, nd), lse.reshape(bsz, s, 128)


def _inproj(x, g, scale, shift, w_bf16):
    bsz, s, d = x.shape
    n = w_bf16.shape[1]
    return pl.pallas_call(
        _inproj_kernel,
        grid=(bsz, s // TM_PROJ),
        in_specs=[
            pl.BlockSpec((1, TM_PROJ, d), lambda b, i: (b, i, 0)),
            pl.BlockSpec((1, d), lambda b, i: (0, 0)),
            pl.BlockSpec((1, 1, d), lambda b, i: (b, 0, 0)),
            pl.BlockSpec((1, 1, d), lambda b, i: (b, 0, 0)),
            pl.BlockSpec((d, n), lambda b, i: (0, 0)),
        ],
        out_specs=pl.BlockSpec((1, TM_PROJ, n), lambda b, i: (b, i, 0)),
        out_shape=jax.ShapeDtypeStruct((bsz, s, n), jnp.float32),
        compiler_params=pltpu.CompilerParams(dimension_semantics=("parallel", "parallel"), vmem_limit_bytes=VMEM_LIMIT),
        name="inproj",
    )(x, g.reshape(1, d), scale, shift, w_bf16)


def _outproj_kernel(mix_ref, z_ref, x_ref, gate_ref, w_ref, o_ref):
    z = z_ref[0]
    m = mix_ref[0] * (z * jax.nn.sigmoid(z))
    y = jnp.dot(m.astype(jnp.bfloat16), w_ref[...], preferred_element_type=jnp.float32)
    o_ref[0] = x_ref[0] + gate_ref[0] * y


def _outproj(mix, z, x, gate, w_bf16):
    bsz, s, d = x.shape
    k = mix.shape[-1]
    return pl.pallas_call(
        _outproj_kernel,
        grid=(bsz, s // TM_PROJ),
        in_specs=[
            pl.BlockSpec((1, TM_PROJ, k), lambda b, i: (b, i, 0)),
            pl.BlockSpec((1, TM_PROJ, k), lambda b, i: (b, i, 0)),
            pl.BlockSpec((1, TM_PROJ, d), lambda b, i: (b, i, 0)),
            pl.BlockSpec((1, 1, d), lambda b, i: (b, 0, 0)),
            pl.BlockSpec((k, d), lambda b, i: (0, 0)),
        ],
        out_specs=pl.BlockSpec((1, TM_PROJ, d), lambda b, i: (b, i, 0)),
        out_shape=jax.ShapeDtypeStruct((bsz, s, d), jnp.float32),
        compiler_params=pltpu.CompilerParams(dimension_semantics=("parallel", "parallel"), vmem_limit_bytes=VMEM_LIMIT),
        name="outproj",
    )(mix, z, x, gate, w_bf16)


def _final_rms_kernel(x_ref, g_ref, o_ref):
    x = x_ref[0]
    o_ref[0] = x * lax.rsqrt(jnp.mean(x * x, axis=-1, keepdims=True) + EPS) * g_ref[...]


def _final_rms(x, g):
    bsz, s, d = x.shape
    tm = 512
    return pl.pallas_call(
        _final_rms_kernel,
        grid=(bsz, s // tm),
        in_specs=[pl.BlockSpec((1, tm, d), lambda b, i: (b, i, 0)), pl.BlockSpec((1, d), lambda b, i: (0, 0))],
        out_specs=pl.BlockSpec((1, tm, d), lambda b, i: (b, i, 0)),
        out_shape=jax.ShapeDtypeStruct((bsz, s, d), jnp.float32),
        compiler_params=pltpu.CompilerParams(dimension_semantics=("parallel", "parallel")),
        name="final_rms",
    )(x, g.reshape(1, d))


L_MLSTM = 256
_HI = lax.Precision.HIGHEST


def _log_sigmoid(t):
    return jnp.minimum(t, 0.0) - jnp.log(1.0 + jnp.exp(-jnp.abs(t)))


def _mlstm_kernel(qf_ref, kf_ref, vf_ref, gf_ref, gtf_ref, qb_ref, kb_ref, vb_ref, gb_ref, gtb_ref,
                  bias_ref, biast_ref, hf_ref, hb_ref, c_ref, m_ref):
    n = pl.program_id(1)
    ln = qf_ref.shape[1]
    f32, bf16 = jnp.float32, jnp.bfloat16

    @pl.when(n == 0)
    def _():
        c_ref[...] = jnp.zeros_like(c_ref)
        m_ref[...] = jnp.zeros_like(m_ref)

    row = lax.broadcasted_iota(jnp.int32, (ln, ln), 0)
    col = lax.broadcasted_iota(jnp.int32, (ln, ln), 1)
    ones_blk = jnp.ones((ln, A_DV), bf16)
    dirs = ((0, qf_ref, kf_ref, vf_ref, gf_ref, gtf_ref, hf_ref), (1, qb_ref, kb_ref, vb_ref, gb_ref, gtb_ref, hb_ref))
    for d, q_ref, k_ref, v_ref, g_ref, gt_ref, h_ref in dirs:
        mask = (row >= col) if d == 0 else (row <= col)
        tri = mask.astype(f32)
        tri_t = ((row <= col) if d == 0 else (row >= col)).astype(f32)
        g = g_ref[0] + bias_ref[...]
        gt = gt_ref[0] + biast_ref[...]
        ic = g[:, 4 * d:4 * d + 4]
        it = gt[4 * d:4 * d + 4, :]
        bc = jnp.dot(tri, _log_sigmoid(g[:, 8 + 4 * d:12 + 4 * d]), precision=_HI, preferred_element_type=f32)
        bt = jnp.dot(_log_sigmoid(gt[8 + 4 * d:12 + 4 * d, :]), tri_t, precision=_HI, preferred_element_type=f32)
        last = ln - 1 if d == 0 else 0
        for h in range(A_HEADS):
            r = d * A_HEADS + h
            q = q_ref[0, :, h * A_DK:(h + 1) * A_DK]
            k = k_ref[0, :, h * A_DK:(h + 1) * A_DK]
            vaug = jnp.concatenate([v_ref[0, :, h * A_DV:(h + 1) * A_DV], ones_blk], axis=1)
            bcol, icol = bc[:, h:h + 1], ic[:, h:h + 1]
            brow, irow = bt[h:h + 1, :], it[h:h + 1, :]
            m_old = m_ref[r:r + 1, 0:1]
            caug = c_ref[r]
            dmat = jnp.where(mask, bcol - brow + irow, -jnp.inf)
            dmax = jnp.max(dmat, axis=-1, keepdims=True)
            inter = bcol + m_old
            mt = jnp.maximum(inter, dmax)
            w_int = jnp.exp(inter - mt)
            qk = lax.dot_general(q, k, (((1,), (1,)), ((), ())), preferred_element_type=f32)
            sc = jnp.exp(dmat - mt) * qk
            tot = w_int * jnp.dot(q, caug.astype(bf16), preferred_element_type=f32) + jnp.dot(sc.astype(bf16), vaug, preferred_element_type=f32)
            den = jnp.maximum(jnp.abs(tot[:, A_DV:]), jnp.exp(-mt))
            h_ref[0, :, h * A_DV:(h + 1) * A_DV] = tot[:, :A_DV] / den
            btot_c = bcol[last:last + 1, :]
            btot_r = brow[:, last:last + 1]
            m_new = jnp.maximum(btot_r + m_old, jnp.max(btot_r - brow + irow, axis=-1, keepdims=True))
            w_old = jnp.exp(btot_r + m_old - m_new)
            kw = (k.astype(f32) * jnp.exp(btot_c - bcol + icol - m_new)).astype(bf16)
            c_ref[r] = w_old * caug + lax.dot_general(kw, vaug, (((0,), (0,)), ((), ())), preferred_element_type=f32)
            m_ref[r:r + 1, :] = jnp.broadcast_to(m_new, (1, m_ref.shape[1]))


def _mlstm(q, k, v, g, gt, bias):
    bsz, s, _ = q.shape
    ln = min(L_MLSTM, s)
    nc = s // ln
    hk, hv = A_HEADS * A_DK, A_HEADS * A_DV
    fwd = lambda b, n: (b, n, 0)
    bwd = lambda b, n: (b, nc - 1 - n, 0)
    fwd_t = lambda b, n: (b, 0, n)
    bwd_t = lambda b, n: (b, 0, nc - 1 - n)
    const = lambda b, n: (0, 0)
    def specs(im, im_t):
        return [pl.BlockSpec((1, ln, hk), im), pl.BlockSpec((1, ln, hk), im), pl.BlockSpec((1, ln, hv), im),
                pl.BlockSpec((1, ln, 16), im), pl.BlockSpec((1, 16, ln), im_t)]
    return pl.pallas_call(
        _mlstm_kernel,
        grid=(bsz, nc),
        in_specs=specs(fwd, fwd_t) + specs(bwd, bwd_t) + [pl.BlockSpec((1, 16), const), pl.BlockSpec((16, 1), const)],
        out_specs=[pl.BlockSpec((1, ln, hv), fwd), pl.BlockSpec((1, ln, hv), bwd)],
        out_shape=[jax.ShapeDtypeStruct((bsz, s, hv), jnp.float32)] * 2,
        scratch_shapes=[pltpu.VMEM((2 * A_HEADS, A_DK, 2 * A_DV), jnp.float32), pltpu.VMEM((2 * A_HEADS, 128), jnp.float32)],
        compiler_params=pltpu.CompilerParams(dimension_semantics=("parallel", "arbitrary"), vmem_limit_bytes=VMEM_LIMIT),
        name="mlstm",
    )(q, k, v, g, gt, q, k, v, g, gt, bias.reshape(1, 16), bias.reshape(16, 1))


T_GDN = 256
T_GDN_STEP = 256
C_GDN = 64
HALO = 8


def _softplus(t):
    return jnp.maximum(t, 0.0) + jnp.log1p(jnp.exp(-jnp.abs(t)))


def _gdn_prep_kernel(x_ref, xp_ref, xn_ref, g_ref, gt_ref, w_ref, a_ref, at_ref, dt_ref, dtt_ref,
                     q_ref, k_ref, v_ref, gc_ref, gr_ref, xe_ref):
    i = pl.program_id(1)
    nt = pl.num_programs(1)
    t = x_ref.shape[1]
    f32 = jnp.float32
    hd = B_HEADS * B_DK
    xe_ref[0:HALO, :] = jnp.where(i > 0, xp_ref[0], 0.0)
    xe_ref[HALO:HALO + t, :] = x_ref[0]
    xe_ref[HALO + t:, :] = jnp.where(i < nt - 1, xn_ref[0], 0.0)
    half = B_CONV // 2
    for part, o_ref in enumerate((q_ref, k_ref, v_ref)):
        cs = slice(part * hd, (part + 1) * hd)
        acc = None
        for j in range(B_CONV):
            term = xe_ref[HALO - half + j:HALO - half + j + t, cs] * w_ref[j:j + 1, cs]
            acc = term if acc is None else acc + term
        y = acc * jax.nn.sigmoid(acc)
        for h in range(B_HEADS):
            yh = y[:, h * B_DK:(h + 1) * B_DK]
            if part == 0:
                yh = yh * lax.rsqrt(jnp.sum(yh * yh, axis=-1, keepdims=True) + EPS) * (B_DK ** -0.5)
            elif part == 1:
                yh = yh * lax.rsqrt(jnp.sum(yh * yh, axis=-1, keepdims=True) + EPS)
            o_ref[0, :, h * B_DK:(h + 1) * B_DK] = yh.astype(o_ref.dtype)
    row = lax.broadcasted_iota(jnp.int32, (t, t), 0)
    col = lax.broadcasted_iota(jnp.int32, (t, t), 1)
    same = (row // C_GDN) == (col // C_GDN)
    lower = (same & (row >= col)).astype(f32)
    upper = (same & (row <= col)).astype(f32)
    g = g_ref[0]
    gt = gt_ref[0]
    nh = B_HEADS
    dec = -jnp.exp(a_ref[...]) * _softplus(g[:, 2 * nh:] + dt_ref[...])
    dect = -jnp.exp(at_ref[...]) * _softplus(gt[2 * nh:, :] + dtt_ref[...])
    gc_ref[0, :, 0:2 * nh] = jax.nn.sigmoid(g[:, 0:2 * nh])
    gc_ref[0, :, 2 * nh:3 * nh] = jnp.dot(lower, dec[:, 0:nh], precision=_HI, preferred_element_type=f32)
    gc_ref[0, :, 3 * nh:] = jnp.dot(upper, dec[:, nh:], precision=_HI, preferred_element_type=f32)
    gr_ref[0, 0:2 * nh, :] = jax.nn.sigmoid(gt[0:2 * nh, :])
    gr_ref[0, 2 * nh:3 * nh, :] = jnp.dot(dect[0:nh, :], upper, precision=_HI, preferred_element_type=f32)
    gr_ref[0, 3 * nh:, :] = jnp.dot(dect[nh:, :], lower, precision=_HI, preferred_element_type=f32)


def _gdn_prep(dqkv, g, gt, conv_w, a_log, dt_bias):
    bsz, s, n3 = dqkv.shape
    t = min(T_GDN, s)
    nt = s // t
    hd = B_HEADS * B_DK
    hb = t // HALO
    cur = lambda b, i: (b, i, 0)
    const = lambda b, i: (0, 0)
    bf16 = jnp.bfloat16
    return pl.pallas_call(
        _gdn_prep_kernel,
        grid=(bsz, nt),
        in_specs=[
            pl.BlockSpec((1, t, n3), cur),
            pl.BlockSpec((1, HALO, n3), lambda b, i: (b, jnp.maximum(i * hb - 1, 0), 0)),
            pl.BlockSpec((1, HALO, n3), lambda b, i: (b, jnp.minimum((i + 1) * hb, s // HALO - 1), 0)),
            pl.BlockSpec((1, t, 16), cur),
            pl.BlockSpec((1, 16, t), lambda b, i: (b, 0, i)),
            pl.BlockSpec((B_CONV, n3), const),
            pl.BlockSpec((1, 8), const), pl.BlockSpec((8, 1), const),
            pl.BlockSpec((1, 8), const), pl.BlockSpec((8, 1), const),
        ],
        out_specs=[pl.BlockSpec((1, t, hd), cur)] * 3 + [pl.BlockSpec((1, t, 16), cur), pl.BlockSpec((1, 16, t), lambda b, i: (b, 0, i))],
        out_shape=[jax.ShapeDtypeStruct((bsz, s, hd), bf16)] * 3 + [jax.ShapeDtypeStruct((bsz, s, 16), jnp.float32), jax.ShapeDtypeStruct((bsz, 16, s), jnp.float32)],
        scratch_shapes=[pltpu.VMEM((t + 2 * HALO, n3), jnp.float32)],
        compiler_params=pltpu.CompilerParams(dimension_semantics=("parallel", "parallel"), vmem_limit_bytes=VMEM_LIMIT),
        name="gdn_prep",
    )(dqkv, dqkv, dqkv, g, gt, conv_w, a_log.reshape(1, 8), a_log.reshape(8, 1), dt_bias.reshape(1, 8), dt_bias.reshape(8, 1))


def _tri_inverse_many(a_list, masks):
    eye, m16, m32, m64 = masks
    f32, bf16 = jnp.float32, jnp.bfloat16
    mm = lambda x, y: jnp.dot(x.astype(bf16), y.astype(bf16), preferred_element_type=f32)
    ads = [jnp.where(m16, a, 0.0) for a in a_list]
    xs = [eye - ad for ad in ads]
    ps = [mm(ad, ad) for ad in ads]
    for stage in range(3):
        xs = [x + mm(x, p) for x, p in zip(xs, ps)]
        if stage < 2:
            ps = [mm(p, p) for p in ps]
    for lo, hi in ((m16, m32), (m32, m64)):
        off = hi & ~lo
        ys = [mm(jnp.where(off, a, 0.0), x) for a, x in zip(a_list, xs)]
        xs = [x - mm(x, y) for x, y in zip(xs, ys)]
    return xs


def _gdn_kernel(qf_ref, kf_ref, vf_ref, gcf_ref, grf_ref, qb_ref, kb_ref, vb_ref, gcb_ref, grb_ref, of_ref, ob_ref, s_ref):
    n = pl.program_id(1)
    t = qf_ref.shape[1]
    c = C_GDN
    f32, bf16 = jnp.float32, jnp.bfloat16

    @pl.when(n == 0)
    def _():
        s_ref[...] = jnp.zeros_like(s_ref)

    row = lax.broadcasted_iota(jnp.int32, (c, c), 0)
    col = lax.broadcasted_iota(jnp.int32, (c, c), 1)
    eye = (row == col).astype(f32)
    blk = lambda w: (row // w) == (col // w)
    masks = (eye, blk(16), blk(32), blk(64))
    nh, nchunk = B_HEADS, t // c
    dir_refs = ((qf_ref, kf_ref, vf_ref, gcf_ref, grf_ref, of_ref), (qb_ref, kb_ref, vb_ref, gcb_ref, grb_ref, ob_ref))
    probs = [(d, h, ci) for d in range(2) for h in range(nh) for ci in range(nchunk)]
    xpose = (((1,), (1,)), ((), ()))

    def load(d, h, ci):
        q_ref, k_ref, v_ref, gc_ref, gr_ref, _ = dir_refs[d]
        rs, cs = slice(ci * c, (ci + 1) * c), slice(h * B_DK, (h + 1) * B_DK)
        beta = gc_ref[0, rs, d * nh + h:d * nh + h + 1]
        gcol = gc_ref[0, rs, (2 + d) * nh + h:(2 + d) * nh + h + 1]
        grow = gr_ref[0, (2 + d) * nh + h:(2 + d) * nh + h + 1, rs]
        return q_ref[0, rs, cs], k_ref[0, rs, cs], v_ref[0, rs, cs], beta, gcol, grow

    data = [load(*p) for p in probs]
    gams = []
    for (d, _, _), (_, _, _, _, gcol, grow) in zip(probs, data):
        incl = (row >= col) if d == 0 else (row <= col)
        gams.append(jnp.exp(jnp.where(incl, gcol - grow, -jnp.inf)))
    kks = [lax.dot_general(k, k, xpose, preferred_element_type=f32) for (_, k, _, _, _, _) in data]
    qks = [lax.dot_general(q, k, xpose, preferred_element_type=f32) for (q, k, _, _, _, _) in data]
    a_list = []
    for (d, _, _), (_, _, _, beta, _, _), kk, gam in zip(probs, data, kks, gams):
        strict = (row > col) if d == 0 else (row < col)
        a_list.append(jnp.where(strict, beta * kk * gam, 0.0))
    tinvs = _tri_inverse_many(a_list, masks)
    egcs = [jnp.exp(gcol) for (_, _, _, _, gcol, _) in data]
    uws = []
    for (q, k, v, beta, gcol, _), tinv, egc in zip(data, tinvs, egcs):
        rhs = jnp.concatenate([beta * v.astype(f32), (beta * egc) * k.astype(f32)], axis=1).astype(bf16)
        uws.append(jnp.dot(tinv.astype(bf16), rhs, preferred_element_type=f32))
    attns = [(qk * gam).astype(bf16) for qk, gam in zip(qks, gams)]
    index = {p: i for i, p in enumerate(probs)}
    chains = [(d, h) for d in range(2) for h in range(nh)]
    states = [s_ref[d * nh + h] for d, h in chains]
    for step in range(nchunk):
        ids = [index[(d, h, step if d == 0 else nchunk - 1 - step)] for d, h in chains]
        wss = []
        for i, state in zip(ids, states):
            q, _, _, _, _, _ = data[i]
            wq = jnp.concatenate([uws[i][:, B_DV:], q.astype(f32) * egcs[i]], axis=0).astype(bf16)
            wss.append(jnp.dot(wq, state.astype(bf16), preferred_element_type=f32))
        v_news = [(uws[i][:, :B_DV] - ws[:c]).astype(bf16) for i, ws in zip(ids, wss)]
        for (d, h), i, ws, v_new in zip(chains, ids, wss, v_news):
            ci = probs[i][2]
            dir_refs[d][5][0, ci * c:(ci + 1) * c, h * B_DV:(h + 1) * B_DV] = ws[c:] + jnp.dot(attns[i], v_new, preferred_element_type=f32)
        new_states = []
        for (d, h), i, state, v_new in zip(chains, ids, states, v_news):
            _, k, _, _, gcol, _ = data[i]
            last = c - 1 if d == 0 else 0
            gl = gcol[last:last + 1, :]
            kd = (k.astype(f32) * jnp.exp(gl - gcol)).astype(bf16)
            new_states.append(jnp.exp(gl) * state + lax.dot_general(kd, v_new, (((0,), (0,)), ((), ())), preferred_element_type=f32))
        states = new_states
    for (d, h), state in zip(chains, states):
        s_ref[d * nh + h] = state


def _gdn(q, k, v, gc, gr):
    bsz, s, hd = q.shape
    t = min(T_GDN_STEP, s)
    nb = s // t
    fwd = lambda b, n: (b, n, 0)
    bwd = lambda b, n: (b, nb - 1 - n, 0)
    def specs(im, im_t):
        return [pl.BlockSpec((1, t, hd), im)] * 3 + [pl.BlockSpec((1, t, 16), im), pl.BlockSpec((1, 16, t), im_t)]
    return pl.pallas_call(
        _gdn_kernel,
        grid=(bsz, nb),
        in_specs=specs(fwd, lambda b, n: (b, 0, n)) + specs(bwd, lambda b, n: (b, 0, nb - 1 - n)),
        out_specs=[pl.BlockSpec((1, t, hd), fwd), pl.BlockSpec((1, t, hd), bwd)],
        out_shape=[jax.ShapeDtypeStruct((bsz, s, hd), jnp.float32)] * 2,
        scratch_shapes=[pltpu.VMEM((2 * B_HEADS, B_DK, B_DV), jnp.float32)],
        compiler_params=pltpu.CompilerParams(dimension_semantics=("parallel", "arbitrary"), vmem_limit_bytes=VMEM_LIMIT),
        name="gdn",
    )(q, k, v, gc, gr, q, k, v, gc, gr)


def _split(p, sizes):
    return jnp.split(p, np.cumsum(sizes)[:-1].tolist(), axis=-1)


def _layernorm(x, g, b):
    xc = x - jnp.mean(x, axis=-1, keepdims=True)
    y = xc * lax.rsqrt(jnp.mean(xc * xc, axis=-1, keepdims=True) + EPS)
    return y * g + b


def _head_rms(t, g):
    bsz, s, h, d = t.shape
    y = t * lax.rsqrt(jnp.mean(t * t, axis=-1, keepdims=True) + EPS)
    return y.reshape(bsz, s, h * d) * g


def _l2n(t):
    return t * lax.rsqrt(jnp.sum(t * t, axis=-1, keepdims=True) + EPS)


def _dwconv(x, w):
    return lax.conv_general_dilated(x, w[:, None, :].astype(x.dtype), window_strides=(1,), padding='SAME', dimension_numbers=('NWC', 'WIO', 'NWC'), feature_group_count=x.shape[-1])


def _flip(t):
    return jnp.flip(t, axis=1)


def _to_chunks(t):
    bsz, s, h = t.shape[:3]
    t = t.reshape((bsz, s // CHUNK, CHUNK, h) + t.shape[3:])
    return jnp.moveaxis(t, (1, 3), (0, 2))


def _from_chunks(t):
    nc, bsz, h, l = t.shape[:4]
    t = jnp.moveaxis(t, (0, 2), (1, 3))
    return t.reshape((bsz, nc * l, h) + t.shape[4:])


def _mlstm_chunkwise(q, k, v, i_pre, logf):
    q, k, v, i_pre, logf = (_to_chunks(t) for t in (q, k, v, i_pre, logf))
    nc, bsz, h = q.shape[:3]
    causal = jnp.tril(jnp.ones((CHUNK, CHUNK), dtype=bool))
    b = jnp.cumsum(logf, axis=-1)
    dmat = jnp.where(causal, b[..., :, None] - b[..., None, :] + i_pre[..., None, :], -jnp.inf)
    dmax = jnp.max(dmat, axis=-1)
    qk = jnp.einsum('nbhld,nbhsd->nbhls', q, k)
    a_end = b[..., -1:] - b + i_pre

    def step(carry, xs):
        cmat, nvec, m = carry
        qc, kc, vc, bc, dc, dmc, qkc, aec = xs
        inter = bc + m[..., None]
        mt = jnp.maximum(inter, dmc)
        w_int = jnp.exp(inter - mt)
        sc = jnp.exp(dc - mt[..., None]) * qkc
        num = w_int[..., None] * jnp.einsum('bhld,bhde->bhle', qc, cmat) + jnp.einsum('bhls,bhse->bhle', sc, vc)
        den = w_int * jnp.einsum('bhld,bhd->bhl', qc, nvec) + jnp.sum(sc, axis=-1)
        hc = num / jnp.maximum(jnp.abs(den), jnp.exp(-mt))[..., None]
        m_new = jnp.maximum(bc[..., -1] + m, jnp.max(aec, axis=-1))
        w_old = jnp.exp(bc[..., -1] + m - m_new)
        kw = kc * jnp.exp(aec - m_new[..., None])[..., None]
        cmat = w_old[..., None, None] * cmat + jnp.einsum('bhld,bhle->bhde', kw, vc)
        nvec = w_old[..., None] * nvec + jnp.sum(kw, axis=-2)
        return (cmat, nvec, m_new), hc

    init = (jnp.zeros((bsz, h, A_DK, A_DV), jnp.float32), jnp.zeros((bsz, h, A_DK), jnp.float32), jnp.zeros((bsz, h), jnp.float32))
    _, hs = lax.scan(step, init, (q, k, v, b, dmat, dmax, qk, a_end))
    return _from_chunks(hs)


def _gdn_chunked(q, k, v, beta, g):
    q, k, v, beta, g = (_to_chunks(t) for t in (q, k, v, beta, g))
    nc, bsz, h = q.shape[:3]
    tril = jnp.tril(jnp.ones((CHUNK, CHUNK), dtype=bool))
    strict = jnp.tril(jnp.ones((CHUNK, CHUNK), dtype=bool), -1)
    gc = jnp.cumsum(g, axis=-1)
    gam = jnp.exp(jnp.where(tril, gc[..., :, None] - gc[..., None, :], -jnp.inf))
    a = jnp.where(strict, beta[..., :, None] * jnp.einsum('nbhid,nbhjd->nbhij', k, k) * gam, 0.0)
    tmat = a + jnp.eye(CHUNK, dtype=a.dtype)
    u = lax.linalg.triangular_solve(tmat, beta[..., None] * v, left_side=True, lower=True, unit_diagonal=True)
    w = lax.linalg.triangular_solve(tmat, (beta * jnp.exp(gc))[..., None] * k, left_side=True, lower=True, unit_diagonal=True)
    attn = jnp.einsum('nbhid,nbhjd->nbhij', q, k) * gam

    def step(state, xs):
        qc, kc, uc, wc, gcc, ac = xs
        v_new = uc - jnp.einsum('bhld,bhde->bhle', wc, state)
        o = jnp.einsum('bhld,bhde->bhle', qc * jnp.exp(gcc)[..., None], state) + jnp.einsum('bhls,bhse->bhle', ac, v_new)
        gl = gcc[..., -1]
        state = jnp.exp(gl)[..., None, None] * state + jnp.einsum('bhld,bhle->bhde', kc * jnp.exp(gl[..., None] - gcc)[..., None], v_new)
        return state, o

    _, os_ = lax.scan(step, jnp.zeros((bsz, h, B_DK, B_DV), jnp.float32), (q, k, u, w, gc, attn))
    return _from_chunks(os_)


def _t5_bucket(rel):
    half = REL_BUCKETS // 2
    exact = half // 2
    n = jnp.abs(rel)
    large = exact + (jnp.log(jnp.maximum(n, 1).astype(jnp.float32) / exact) / math.log(REL_MAX_DIST / exact) * (half - exact)).astype(jnp.int32)
    large = jnp.minimum(large, half - 1)
    return (rel > 0).astype(jnp.int32) * half + jnp.where(n < exact, n, large)


def _dilated_group(q, k, v, dilation, radius, rel_bias):
    bsz, s, h, dh = q.shape
    ls = s // dilation
    nb = -(-ls // radius)
    lp = nb * radius

    def sub(t, lo, hi):
        t = t.reshape(bsz, ls, dilation, h, dh).transpose(0, 3, 2, 1, 4)
        return jnp.pad(t, ((0, 0), (0, 0), (0, 0), (lo, hi), (0, 0)))

    qb = sub(q, 0, lp - ls).reshape(bsz, h, dilation, nb, radius, dh)

    def band(t):
        t = sub(t, radius, lp - ls + radius).reshape(bsz, h, dilation, nb + 2, radius, dh)
        return jnp.concatenate([t[:, :, :, :-2], t[:, :, :, 1:-1], t[:, :, :, 2:]], axis=4)

    kb, vb = band(k), band(v)
    qi = jnp.arange(radius)[:, None]
    kj = jnp.arange(3 * radius)[None, :]
    rel = kj - radius - qi
    kpos = jnp.arange(nb)[:, None, None] * radius + kj - radius
    valid = (jnp.abs(rel) <= radius) & (kpos >= 0) & (kpos < ls)
    bias = jnp.transpose(rel_bias[_t5_bucket(rel * dilation)], (2, 0, 1)).astype(jnp.float32)
    sc = jnp.einsum('bhrnid,bhrnjd->bhrnij', qb, kb).astype(jnp.float32) * (dh ** -0.5) + bias[:, None, None]
    sc = jnp.where(valid, sc, NEG)
    m = jnp.max(sc, axis=-1, keepdims=True)
    p = jnp.exp(sc - m)
    den = jnp.sum(p, axis=-1)
    o = jnp.einsum('bhrnij,bhrnjd->bhrnid', p, vb.astype(jnp.float32)) / den[..., None]
    lse = m[..., 0] + jnp.log(den)
    o = o.reshape(bsz, h, dilation, lp, dh)[:, :, :, :ls].transpose(0, 3, 2, 1, 4).reshape(bsz, s, h, dh)
    lse = lse.reshape(bsz, h, dilation, lp)[:, :, :, :ls].transpose(0, 3, 2, 1).reshape(bsz, s, h)
    return o, lse


def _dilated_attention(q, k, v, rel_bias):
    outs, lses = [], []
    for window, dilation in D_GROUPS:
        o, l = _dilated_group(q, k, v, dilation, window // (2 * dilation), rel_bias)
        outs.append(o)
        lses.append(l)
    wts = jax.nn.softmax(jnp.stack(lses, axis=0), axis=0)
    return jnp.sum(wts[..., None] * jnp.stack(outs, axis=0), axis=0)


def _even_mixer_core(p, m_gate_b, dn_dt_bias, dn_a_log, dn_conv_w, m_norm_g, dn_norm_g):
    bsz, s, _ = p.shape
    f32 = jnp.float32
    mq, mk, mv, mo, mg, dqkv, dg, z = _split(p, EVEN_SPLITS)
    q = mq.reshape(bsz, s, A_HEADS, A_DK)
    k = mk.reshape(bsz, s, A_HEADS, A_DK) * (A_DK ** -0.5)
    v = mv.reshape(bsz, s, A_HEADS, A_DV)
    gt = mg.reshape(bsz, s, 4, A_HEADS) + m_gate_b
    logf = jax.nn.log_sigmoid(gt[:, :, 2:4])
    h_fwd = _mlstm_chunkwise(q, k, v, gt[:, :, 0], logf[:, :, 0])
    h_bwd = _flip(_mlstm_chunkwise(_flip(q), _flip(k), _flip(v), _flip(gt[:, :, 1]), _flip(logf[:, :, 1])))
    out_a = jax.nn.sigmoid(mo) * _head_rms(h_fwd + h_bwd, m_norm_g)
    qkv = jax.nn.silu(_dwconv(dqkv, dn_conv_w))
    bq, bk, bv = _split(qkv, (B_HEADS * B_DK, B_HEADS * B_DK, B_HEADS * B_DV))
    q = _l2n(bq.reshape(bsz, s, B_HEADS, B_DK)) * (B_DK ** -0.5)
    k = _l2n(bk.reshape(bsz, s, B_HEADS, B_DK))
    v = bv.reshape(bsz, s, B_HEADS, B_DV)
    gb = dg.reshape(bsz, s, 4, B_HEADS)
    beta = jax.nn.sigmoid(gb[:, :, 0:2])
    decay = -jnp.exp(dn_a_log) * jax.nn.softplus(gb[:, :, 2:4] + dn_dt_bias)
    o_fwd = _gdn_chunked(q, k, v, beta[:, :, 0], decay[:, :, 0])
    o_bwd = _flip(_gdn_chunked(_flip(q), _flip(k), _flip(v), _flip(beta[:, :, 1]), _flip(decay[:, :, 1])))
    out_b = _head_rms(o_fwd + o_bwd, dn_norm_g)
    return jnp.concatenate([out_a, out_b], axis=-1), z


def _odd_mixer_core(p, dw_w, dw_b, ln_g, ln_b, rel_bias):
    bsz, s, _ = p.shape
    ga, gb, aq, ak, av, z = _split(p, ODD_SPLITS)
    u = _dwconv(ga * jax.nn.sigmoid(gb), dw_w) + dw_b
    out_c = jax.nn.silu(_layernorm(u, ln_g, ln_b))
    shp = (bsz, s, D_HEADS, D_DH)
    out_d = _dilated_attention(aq.reshape(shp), ak.reshape(shp), av.reshape(shp), rel_bias).reshape(bsz, s, D_HEADS * D_DH)
    return jnp.concatenate([out_c, out_d], axis=-1), z


def kernel(x, c, norm_g, ada_w, ada_b, ev_w_in, ev_m_gate_b, ev_dn_dt_bias, ev_dn_a_log, ev_dn_conv_w, ev_m_norm_g, ev_dn_norm_g, ev_w_out, od_w_in, od_dw_w, od_dw_b, od_ln_g, od_ln_b, od_w_out, rel_bias, final_g):
    assert DEPTH == 2, "the final RMSNorm is fused into the (last) odd layer's output projection"
    bf16 = jnp.bfloat16
    bsz, s, d = x.shape
    mod = _adaln(c, ada_w, ada_b)
    for layer in range(DEPTH):
        shift, scale, gate = (mod[layer, :, i * d:(i + 1) * d][:, None, :] for i in range(3))
        j = layer // 2
        if layer % 2 == 0:
            mq, mk, mv, mo, mg, dqkv, dg, z = _split(ev_w_in[j], EVEN_SPLITS)
            w = jnp.concatenate([mq, mk * (A_DK ** -0.5), mv, mo, dqkv, z], axis=1).astype(bf16)
            wg = jnp.concatenate([mg, dg], axis=1).astype(bf16)
            pq, pk, pv, po, pdqkv, pz, g_m, g_d, gt_m, gt_d = _inproj_even(x, norm_g[layer], scale, shift, w, wg, wg.T)
            hf, hb = _mlstm(pq, pk, pv, g_m, gt_m, ev_m_gate_b[j].reshape(16))
            bq, bk, bv, gc, gr = _gdn_prep(pdqkv, g_d, gt_d, ev_dn_conv_w[j], ev_dn_a_log[j], ev_dn_dt_bias[j])
            of, ob = _gdn(bq, bk, bv, gc, gr)
            x = _outproj_even(hf, hb, of, ob, po, pz, x, gate, ev_m_norm_g[j], ev_dn_norm_g[j], ev_w_out[j].astype(bf16))
        else:
            ga, gb, aq, ak, av, z = _split(od_w_in[j], ODD_SPLITS)
            w = jnp.concatenate([ga, gb, aq * (D_DH ** -0.5), ak, av, z], axis=1).astype(bf16)
            glu, pq, pk, pv, pz = _inproj_odd(x, norm_g[layer], scale, shift, w)
            out_c = _conformer(glu, od_dw_w[j], od_dw_b[j], od_ln_g[j], od_ln_b[j])
            og, lg = zip(*[_dilated_group_call(pq, pk, pv, rel_bias, dilation) for _, dilation in D_GROUPS])
            x = _outproj_odd_final(out_c, og, lg, pz, x, gate, final_g, od_w_out[j].astype(bf16))
    return x
```

```python
import math
from functools import partial

import jax
import jax.numpy as jnp
import numpy as np
from jax import lax
from jax.experimental import pallas as pl
from jax.experimental.pallas import tpu as pltpu

D_MODEL = 1024
BATCH = 4
SEQ = 8192
DEPTH = 2
A_HEADS = 4
A_DK = 64
A_DV = 128
B_HEADS = 4
B_DK = 128
B_DV = 128
B_CONV = 5
C_WIDTH = 512
C_CONV = 31
D_HEADS = 8
D_DH = 64
D_GROUPS = ((128, 1), (512, 4), (2048, 16))
REL_BUCKETS = 32
REL_MAX_DIST = 1024
CHUNK = 64
EPS = 1e-6
NEG = -1e30
MIX_EVEN = A_HEADS * A_DV + B_HEADS * B_DV
MIX_ODD = C_WIDTH + D_HEADS * D_DH
B_QKV = B_HEADS * (2 * B_DK + B_DV)
EVEN_SPLITS = (A_HEADS * A_DK, A_HEADS * A_DK, A_HEADS * A_DV, A_HEADS * A_DV, 4 * A_HEADS, B_QKV, 4 * B_HEADS, MIX_EVEN)
ODD_SPLITS = (C_WIDTH, C_WIDTH, D_HEADS * D_DH, D_HEADS * D_DH, D_HEADS * D_DH, MIX_ODD)

VMEM_LIMIT = 56 * 1024 * 1024
TM_PROJ = 256


def _adaln_kernel(c_ref, w_ref, b_ref, o_ref):
    c = c_ref[...]
    cs = (c * jax.nn.sigmoid(c)).astype(jnp.bfloat16)
    o_ref[0] = jnp.dot(cs, w_ref[0].astype(jnp.bfloat16), preferred_element_type=jnp.float32) + b_ref[0]


def _adaln(c, ada_w, ada_b):
    depth, d, n3 = ada_w.shape
    bsz = c.shape[0]
    tn = 1024
    return pl.pallas_call(
        _adaln_kernel,
        grid=(depth, n3 // tn),
        in_specs=[pl.BlockSpec((bsz, d), lambda l, j: (0, 0)), pl.BlockSpec((1, d, tn), lambda l, j: (l, 0, j)),
                  pl.BlockSpec((1, 1, tn), lambda l, j: (l, 0, j))],
        out_specs=pl.BlockSpec((1, bsz, tn), lambda l, j: (l, 0, j)),
        out_shape=jax.ShapeDtypeStruct((depth, bsz, n3), jnp.float32),
        compiler_params=pltpu.CompilerParams(dimension_semantics=("parallel", "parallel")),
        name="adaln",
    )(c, ada_w, ada_b.reshape(depth, 1, n3))


def _modulated_rms(x_ref, g_ref, sc_ref, sh_ref):
    x = x_ref[0]
    y = x * lax.rsqrt(jnp.mean(x * x, axis=-1, keepdims=True) + EPS)
    return ((y * g_ref[...]) * (1.0 + sc_ref[0]) + sh_ref[0]).astype(jnp.bfloat16)


_EV_COLS = {"mq": (0, 256), "mk": (256, 512), "mv": (512, 1024), "mo": (1024, 1536), "dqkv": (1536, 3072), "z": (3072, 4096)}
_OD_COLS = {"ga": (0, 512), "gb": (512, 1024), "aq": (1024, 1536), "ak": (1536, 2048), "av": (2048, 2560), "z": (2560, 3584)}


def _inproj_even_kernel(x_ref, g_ref, sc_ref, sh_ref, w_ref, wg_ref, wgt_ref,
                        mq_ref, mk_ref, mv_ref, mo_ref, dqkv_ref, z_ref, mg_ref, dg_ref, mgt_ref, dgt_ref):
    f32 = jnp.float32
    h = _modulated_rms(x_ref, g_ref, sc_ref, sh_ref)
    for name, o_ref in (("mq", mq_ref), ("mk", mk_ref), ("mv", mv_ref), ("mo", mo_ref), ("dqkv", dqkv_ref), ("z", z_ref)):
        lo, hi = _EV_COLS[name]
        o_ref[0] = jnp.dot(h, w_ref[:, lo:hi], preferred_element_type=f32).astype(o_ref.dtype)
    gates = jnp.dot(h, wg_ref[...], preferred_element_type=f32)
    gates_t = lax.dot_general(wgt_ref[...], h, (((1,), (1,)), ((), ())), preferred_element_type=f32)
    mg_ref[0] = gates[:, :16]
    dg_ref[0] = gates[:, 16:]
    mgt_ref[0] = gates_t[:16]
    dgt_ref[0] = gates_t[16:]


def _inproj_even(x, g, scale, shift, w, wg, wgt):
    bsz, s, d = x.shape
    tm = TM_PROJ
    tok = lambda b, i: (b, i, 0)
    tok_t = lambda b, i: (b, 0, i)
    const = lambda b, i: (0, 0)
    bvec = lambda b, i: (b, 0, 0)
    bf16, f32 = jnp.bfloat16, jnp.float32
    outs = [("mq", bf16), ("mk", bf16), ("mv", bf16), ("mo", f32), ("dqkv", f32), ("z", f32)]
    widths = [_EV_COLS[n][1] - _EV_COLS[n][0] for n, _ in outs]
    return pl.pallas_call(
        _inproj_even_kernel,
        grid=(bsz, s // tm),
        in_specs=[pl.BlockSpec((1, tm, d), tok), pl.BlockSpec((1, d), const), pl.BlockSpec((1, 1, d), bvec), pl.BlockSpec((1, 1, d), bvec),
                  pl.BlockSpec(w.shape, const), pl.BlockSpec(wg.shape, const), pl.BlockSpec(wgt.shape, const)],
        out_specs=[pl.BlockSpec((1, tm, wd), tok) for wd in widths] + [pl.BlockSpec((1, tm, 16), tok)] * 2 + [pl.BlockSpec((1, 16, tm), tok_t)] * 2,
        out_shape=[jax.ShapeDtypeStruct((bsz, s, wd), dt) for wd, (_, dt) in zip(widths, outs)]
        + [jax.ShapeDtypeStruct((bsz, s, 16), f32)] * 2 + [jax.ShapeDtypeStruct((bsz, 16, s), f32)] * 2,
        compiler_params=pltpu.CompilerParams(dimension_semantics=("parallel", "parallel"), vmem_limit_bytes=VMEM_LIMIT),
        name="inproj_even",
    )(x, g.reshape(1, d), scale, shift, w, wg, wgt)


DILATIONS = tuple(dil for _, dil in D_GROUPS)
LANES = 128


def _inproj_odd_kernel(x_ref, g_ref, sc_ref, sh_ref, w_ref, glu_ref, z_ref, *rest):
    out_refs, plane_ref = rest[:-1], rest[-1]
    f32 = jnp.float32
    tm = x_ref.shape[1]
    nd = D_HEADS * D_DH
    h = _modulated_rms(x_ref, g_ref, sc_ref, sh_ref)
    dot = lambda name: jnp.dot(h, w_ref[:, _OD_COLS[name][0]:_OD_COLS[name][1]], preferred_element_type=f32)
    glu_ref[0] = dot("ga") * jax.nn.sigmoid(dot("gb"))
    z_ref[0] = dot("z")
    for a, name in enumerate(("aq", "ak", "av")):
        r = dot(name)
        group_refs = out_refs[a * len(DILATIONS):(a + 1) * len(DILATIONS)]
        for j in range(nd // LANES):
            plane_ref[a, j] = r[:, j * LANES:(j + 1) * LANES]
        for dil, o_ref in zip(DILATIONS, group_refs):
            if dil == 1:
                o_ref[0, 0] = r.astype(o_ref.dtype)
                continue
            for res in range(dil):
                for j in range(nd // LANES):
                    o_ref[0, res, :, j * LANES:(j + 1) * LANES] = plane_ref[a, j, pl.ds(res, tm // dil, stride=dil), :].astype(o_ref.dtype)


def _inproj_odd(x, g, scale, shift, w):
    bsz, s, d = x.shape
    tm = TM_PROJ
    tok = lambda b, i: (b, i, 0)
    const = lambda b, i: (0, 0)
    bvec = lambda b, i: (b, 0, 0)
    bf16, f32 = jnp.bfloat16, jnp.float32
    nd = D_HEADS * D_DH
    att_specs = [pl.BlockSpec((1, dil, tm // dil, nd), lambda b, i: (b, 0, i, 0)) for dil in DILATIONS] * 3
    att_shapes = [jax.ShapeDtypeStruct((bsz, dil, s // dil, nd), bf16) for dil in DILATIONS] * 3
    outs = pl.pallas_call(
        _inproj_odd_kernel,
        grid=(bsz, s // tm),
        in_specs=[pl.BlockSpec((1, tm, d), tok), pl.BlockSpec((1, d), const), pl.BlockSpec((1, 1, d), bvec), pl.BlockSpec((1, 1, d), bvec),
                  pl.BlockSpec(w.shape, const)],
        out_specs=[pl.BlockSpec((1, tm, C_WIDTH), tok), pl.BlockSpec((1, tm, MIX_ODD), tok)] + att_specs,
        out_shape=[jax.ShapeDtypeStruct((bsz, s, C_WIDTH), f32), jax.ShapeDtypeStruct((bsz, s, MIX_ODD), f32)] + att_shapes,
        scratch_shapes=[pltpu.VMEM((3, nd // LANES, tm, LANES), f32)],
        compiler_params=pltpu.CompilerParams(dimension_semantics=("parallel", "parallel"), vmem_limit_bytes=VMEM_LIMIT),
        name="inproj_odd",
    )(x, g.reshape(1, d), scale, shift, w)
    ng = len(DILATIONS)
    return outs[0], outs[1], outs[2:2 + ng], outs[2 + ng:2 + 2 * ng], outs[2 + 2 * ng:]


def _head_rms_cols(t, g, width):
    parts = []
    for h in range(t.shape[1] // width):
        th = t[:, h * width:(h + 1) * width]
        parts.append(th * lax.rsqrt(jnp.mean(th * th, axis=-1, keepdims=True) + EPS))
    return jnp.concatenate(parts, axis=1) * g


def _outproj_even_kernel(hf_ref, hb_ref, of_ref, ob_ref, mo_ref, z_ref, x_ref, gate_ref, mg_ref, dg_ref, w_ref, o_ref):
    f32, bf16 = jnp.float32, jnp.bfloat16
    z = z_ref[0]
    sz = z * jax.nn.sigmoid(z)
    na = A_HEADS * A_DV
    out_a = jax.nn.sigmoid(mo_ref[0]) * _head_rms_cols(hf_ref[0] + hb_ref[0], mg_ref[...], A_DV)
    out_b = _head_rms_cols(of_ref[0] + ob_ref[0], dg_ref[...], B_DV)
    y = jnp.dot((out_a * sz[:, :na]).astype(bf16), w_ref[:na, :], preferred_element_type=f32)
    y = y + jnp.dot((out_b * sz[:, na:]).astype(bf16), w_ref[na:, :], preferred_element_type=f32)
    o_ref[0] = x_ref[0] + gate_ref[0] * y


def _outproj_even(hf, hb, of, ob, mo, z, x, gate, m_norm_g, dn_norm_g, w):
    bsz, s, d = x.shape
    tm = TM_PROJ
    tok = lambda b, i: (b, i, 0)
    const = lambda b, i: (0, 0)
    bvec = lambda b, i: (b, 0, 0)
    na, nb = A_HEADS * A_DV, B_HEADS * B_DV
    return pl.pallas_call(
        _outproj_even_kernel,
        grid=(bsz, s // tm),
        in_specs=[pl.BlockSpec((1, tm, na), tok)] * 2 + [pl.BlockSpec((1, tm, nb), tok)] * 2 + [pl.BlockSpec((1, tm, na), tok),
                  pl.BlockSpec((1, tm, na + nb), tok), pl.BlockSpec((1, tm, d), tok), pl.BlockSpec((1, 1, d), bvec),
                  pl.BlockSpec((1, na), const), pl.BlockSpec((1, nb), const), pl.BlockSpec(w.shape, const)],
        out_specs=pl.BlockSpec((1, tm, d), tok),
        out_shape=jax.ShapeDtypeStruct((bsz, s, d), jnp.float32),
        compiler_params=pltpu.CompilerParams(dimension_semantics=("parallel", "parallel"), vmem_limit_bytes=VMEM_LIMIT),
        name="outproj_even",
    )(hf, hb, of, ob, mo, z, x, gate, m_norm_g.reshape(1, na), dn_norm_g.reshape(1, nb), w)


def _outproj_odd_kernel(oc_ref, o1_ref, o2_ref, o3_ref, l1_ref, l2_ref, l3_ref, z_ref, x_ref, gate_ref, fg_ref, w_ref, o_ref, nat_ref):
    f32, bf16 = jnp.float32, jnp.bfloat16
    tm = x_ref.shape[1]
    npl = D_HEADS * D_DH // LANES
    z = z_ref[0]
    sz = z * jax.nn.sigmoid(z)
    groups = []
    for gi, (dil, og_ref, lg_ref) in enumerate(zip(DILATIONS, (o1_ref, o2_ref, o3_ref), (l1_ref, l2_ref, l3_ref))):
        if dil == 1:
            groups.append(([og_ref[0, 0, :, j * LANES:(j + 1) * LANES] for j in range(npl)], lg_ref[0, 0]))
            continue
        for res in range(dil):
            rows = pl.ds(res, tm // dil, stride=dil)
            for j in range(npl):
                nat_ref[gi, j, rows, :] = og_ref[0, res, :, j * LANES:(j + 1) * LANES]
            nat_ref[gi, npl, rows, :] = lg_ref[0, res]
        groups.append(([nat_ref[gi, j] for j in range(npl)], nat_ref[gi, npl]))
    (p1, l1), (p2, l2), (p3, l3) = groups
    lm = jnp.maximum(jnp.maximum(l1, l2), l3)
    e1, e2, e3 = jnp.exp(l1 - lm), jnp.exp(l2 - lm), jnp.exp(l3 - lm)
    inv = 1.0 / (e1 + e2 + e3)
    low = lax.broadcasted_iota(jnp.int32, (tm, LANES), 1) < D_DH
    planes = []
    for j in range(npl):
        acc = None
        for e, p in ((e1, p1), (e2, p2), (e3, p3)):
            wgt = e * inv
            term = jnp.where(low, wgt[:, 2 * j:2 * j + 1], wgt[:, 2 * j + 1:2 * j + 2]) * p[j]
            acc = term if acc is None else acc + term
        planes.append(acc)
    out_d = jnp.concatenate(planes, axis=1)
    y = jnp.dot((oc_ref[0] * sz[:, :C_WIDTH]).astype(bf16), w_ref[:C_WIDTH, :], preferred_element_type=f32)
    y = y + jnp.dot((out_d * sz[:, C_WIDTH:]).astype(bf16), w_ref[C_WIDTH:, :], preferred_element_type=f32)
    xn = x_ref[0] + gate_ref[0] * y
    o_ref[0] = xn * lax.rsqrt(jnp.mean(xn * xn, axis=-1, keepdims=True) + EPS) * fg_ref[...]


def _outproj_odd_final(oc, og, lg, z, x, gate, final_g, w):
    bsz, s, d = x.shape
    tm = TM_PROJ
    tok = lambda b, i: (b, i, 0)
    const = lambda b, i: (0, 0)
    bvec = lambda b, i: (b, 0, 0)
    nd = D_HEADS * D_DH
    res_major = lambda width: [pl.BlockSpec((1, dil, tm // dil, width), lambda b, i: (b, 0, i, 0)) for dil in DILATIONS]
    return pl.pallas_call(
        _outproj_odd_kernel,
        grid=(bsz, s // tm),
        in_specs=[pl.BlockSpec((1, tm, C_WIDTH), tok)] + res_major(nd) + res_major(LANES)
        + [pl.BlockSpec((1, tm, MIX_ODD), tok), pl.BlockSpec((1, tm, d), tok), pl.BlockSpec((1, 1, d), bvec), pl.BlockSpec((1, d), const),
           pl.BlockSpec(w.shape, const)],
        out_specs=pl.BlockSpec((1, tm, d), tok),
        out_shape=jax.ShapeDtypeStruct((bsz, s, d), jnp.float32),
        scratch_shapes=[pltpu.VMEM((len(DILATIONS), nd // LANES + 1, tm, LANES), jnp.float32)],
        compiler_params=pltpu.CompilerParams(dimension_semantics=("parallel", "parallel"), vmem_limit_bytes=VMEM_LIMIT),
        name="outproj_odd",
    )(oc, *og, *lg, z, x, gate, final_g.reshape(1, d), w)


T_CONV = 512
HALO_C = 16
SUB_C = 64


def _conformer_kernel(x_ref, xp_ref, xn_ref, w_ref, b_ref, lg_ref, lb_ref, o_ref, xe_ref):
    i = pl.program_id(1)
    nt = pl.num_programs(1)
    t = x_ref.shape[1]
    xe_ref[0:HALO_C, :] = jnp.where(i > 0, xp_ref[0], 0.0)
    xe_ref[HALO_C:HALO_C + t, :] = x_ref[0]
    xe_ref[HALO_C + t:, :] = jnp.where(i < nt - 1, xn_ref[0], 0.0)
    half = C_CONV // 2
    for r0 in range(0, t, SUB_C):
        acc = None
        for j in range(C_CONV):
            lo = HALO_C - half + j + r0
            term = xe_ref[lo:lo + SUB_C, :] * w_ref[j:j + 1, :]
            acc = term if acc is None else acc + term
        u = acc + b_ref[...]
        uc = u - jnp.mean(u, axis=-1, keepdims=True)
        y = uc * lax.rsqrt(jnp.mean(uc * uc, axis=-1, keepdims=True) + EPS) * lg_ref[...] + lb_ref[...]
        o_ref[0, r0:r0 + SUB_C, :] = y * jax.nn.sigmoid(y)


def _conformer(glu, dw_w, dw_b, ln_g, ln_b):
    bsz, s, cw = glu.shape
    t = min(T_CONV, s)
    hb = t // HALO_C
    cur = lambda b, i: (b, i, 0)
    const = lambda b, i: (0, 0)
    return pl.pallas_call(
        _conformer_kernel,
        grid=(bsz, s // t),
        in_specs=[pl.BlockSpec((1, t, cw), cur),
                  pl.BlockSpec((1, HALO_C, cw), lambda b, i: (b, jnp.maximum(i * hb - 1, 0), 0)),
                  pl.BlockSpec((1, HALO_C, cw), lambda b, i: (b, jnp.minimum((i + 1) * hb, s // HALO_C - 1), 0)),
                  pl.BlockSpec((C_CONV, cw), const)] + [pl.BlockSpec((1, cw), const)] * 3,
        out_specs=pl.BlockSpec((1, t, cw), cur),
        out_shape=jax.ShapeDtypeStruct((bsz, s, cw), jnp.float32),
        scratch_shapes=[pltpu.VMEM((t + 2 * HALO_C, cw), jnp.float32)],
        compiler_params=pltpu.CompilerParams(dimension_semantics=("parallel", "parallel"), vmem_limit_bytes=VMEM_LIMIT),
        name="conformer",
    )(glu, glu, glu, dw_w, dw_b.reshape(1, cw), ln_g.reshape(1, cw), ln_b.reshape(1, cw))


TQ_ATT = 128
R_ATT = 64


def _dilated_kernel(q_ref, kc_ref, kp_ref, kn_ref, vc_ref, vp_ref, vn_ref, bias_ref, o_ref, lse_ref, kx_ref, vx_ref):
    i = pl.program_id(2)
    nt = pl.num_programs(2)
    tq = q_ref.shape[1]
    nk = tq + 2 * R_ATT
    f32, bf16 = jnp.float32, jnp.bfloat16
    kx_ref[0:R_ATT, :] = kp_ref[0]
    kx_ref[R_ATT:R_ATT + tq, :] = kc_ref[0]
    kx_ref[R_ATT + tq:, :] = kn_ref[0]
    vx_ref[0:R_ATT, :] = vp_ref[0]
    vx_ref[R_ATT:R_ATT + tq, :] = vc_ref[0]
    vx_ref[R_ATT + tq:, :] = vn_ref[0]
    kj = lax.broadcasted_iota(jnp.int32, (tq, nk), 1)
    outside = ((kj < R_ATT) & (i == 0)) | ((kj >= R_ATT + tq) & (i == nt - 1))
    lane = lax.broadcasted_iota(jnp.int32, (tq, 128), 1)
    low = lane < D_DH
    heads = [(pr, hi) for pr in range(D_HEADS // 2) for hi in (False, True)]
    scs = []
    for pr, hi in heads:
        ps = slice(pr * 128, (pr + 1) * 128)
        qp = q_ref[0, :, ps]
        qh = jnp.where(low != hi, qp, jnp.zeros_like(qp))
        sc = lax.dot_general(qh, kx_ref[:, ps], (((1,), (1,)), ((), ())), preferred_element_type=f32) + bias_ref[2 * pr + int(hi)]
        scs.append(jnp.where(outside, NEG, sc))
    ms = [jnp.max(sc, axis=-1, keepdims=True) for sc in scs]
    ps_ = [jnp.exp(sc - m) for sc, m in zip(scs, ms)]
    dens = [jnp.sum(p, axis=-1, keepdims=True) for p in ps_]
    pvs = [jnp.dot(p.astype(bf16), vx_ref[:, pr * 128:(pr + 1) * 128], preferred_element_type=f32) for (pr, _), p in zip(heads, ps_)]
    lse_all = jnp.zeros((tq, 128), f32)
    for pr in range(D_HEADS // 2):
        lo, hi = 2 * pr, 2 * pr + 1
        o_ref[0, :, pr * 128:(pr + 1) * 128] = jnp.where(low, pvs[lo] / dens[lo], pvs[hi] / dens[hi])
        lse_all = jnp.where(lane == lo, ms[lo] + jnp.log(dens[lo]), lse_all)
        lse_all = jnp.where(lane == hi, ms[hi] + jnp.log(dens[hi]), lse_all)
    lse_ref[0] = lse_all


def _dilated_bias(rel_bias, dilation, tq):
    half = REL_BUCKETS // 2
    exact = half // 2
    qi = jnp.arange(tq)[:, None]
    kj = jnp.arange(tq + 2 * R_ATT)[None, :]
    rel = kj - R_ATT - qi
    reld = rel * dilation
    n = jnp.abs(reld)
    large = exact + (jnp.log(jnp.maximum(n, 1).astype(jnp.float32) / exact) / math.log(REL_MAX_DIST / exact) * (half - exact)).astype(jnp.int32)
    large = jnp.minimum(large, half - 1)
    bucket = (reld > 0).astype(jnp.int32) * half + jnp.where(n < exact, n, large)
    bias = jnp.zeros((rel_bias.shape[1],) + bucket.shape, jnp.float32)
    for b in range(REL_BUCKETS):
        bias = jnp.where((bucket == b)[None], rel_bias[b].astype(jnp.float32)[:, None, None], bias)
    return jnp.where((jnp.abs(rel) <= R_ATT)[None], bias, NEG)


def _dilated_group_call(q, k, v, rel_bias, dilation):
    bsz, dil, ls, nd = q.shape
    assert dil == dilation
    tq = min(TQ_ATT, ls)
    nt = ls // tq
    hb = tq // R_ATT
    nk = tq + 2 * R_ATT
    cur = lambda b, r, i: (b, r, i, 0)
    prev = lambda b, r, i: (b, r, jnp.maximum(i * hb - 1, 0), 0)
    nxt = lambda b, r, i: (b, r, jnp.minimum((i + 1) * hb, ls // R_ATT - 1), 0)
    kv_specs = [pl.BlockSpec((1, None, tq, nd), cur), pl.BlockSpec((1, None, R_ATT, nd), prev), pl.BlockSpec((1, None, R_ATT, nd), nxt)]
    return pl.pallas_call(
        _dilated_kernel,
        grid=(bsz, dilation, nt),
        in_specs=[pl.BlockSpec((1, None, tq, nd), cur)] + kv_specs + kv_specs + [pl.BlockSpec((D_HEADS, tq, nk), lambda b, r, i: (0, 0, 0))],
        out_specs=[pl.BlockSpec((1, None, tq, nd), cur), pl.BlockSpec((1, None, tq, 128), cur)],
        out_shape=[jax.ShapeDtypeStruct((bsz, dilation, ls, nd), jnp.float32), jax.ShapeDtypeStruct((bsz, dilation, ls, 128), jnp.float32)],
        scratch_shapes=[pltpu.VMEM((nk, nd), jnp.bfloat16)] * 2,
        compiler_params=pltpu.CompilerParams(dimension_semantics=("parallel", "parallel", "parallel"), vmem_limit_bytes=VMEM_LIMIT),
        name=f"dilated_d{dilation}",
    )(q, k, k, k, v, v, v, _dilated_bias(rel_bias, dilation, tq))


def _inproj(x, g, scale, shift, w_bf16):
    bsz, s, d = x.shape
    n = w_bf16.shape[1]
    return pl.pallas_call(
        _inproj_kernel,
        grid=(bsz, s // TM_PROJ),
        in_specs=[
            pl.BlockSpec((1, TM_PROJ, d), lambda b, i: (b, i, 0)),
            pl.BlockSpec((1, d), lambda b, i: (0, 0)),
            pl.BlockSpec((1, 1, d), lambda b, i: (b, 0, 0)),
            pl.BlockSpec((1, 1, d), lambda b, i: (b, 0, 0)),
            pl.BlockSpec((d, n), lambda b, i: (0, 0)),
        ],
        out_specs=pl.BlockSpec((1, TM_PROJ, n), lambda b, i: (b, i, 0)),
        out_shape=jax.ShapeDtypeStruct((bsz, s, n), jnp.float32),
        compiler_params=pltpu.CompilerParams(dimension_semantics=("parallel", "parallel"), vmem_limit_bytes=VMEM_LIMIT),
        name="inproj",
    )(x, g.reshape(1, d), scale, shift, w_bf16)


def _outproj_kernel(mix_ref, z_ref, x_ref, gate_ref, w_ref, o_ref):
    z = z_ref[0]
    m = mix_ref[0] * (z * jax.nn.sigmoid(z))
    y = jnp.dot(m.astype(jnp.bfloat16), w_ref[...], preferred_element_type=jnp.float32)
    o_ref[0] = x_ref[0] + gate_ref[0] * y


def _outproj(mix, z, x, gate, w_bf16):
    bsz, s, d = x.shape
    k = mix.shape[-1]
    return pl.pallas_call(
        _outproj_kernel,
        grid=(bsz, s // TM_PROJ),
        in_specs=[
            pl.BlockSpec((1, TM_PROJ, k), lambda b, i: (b, i, 0)),
            pl.BlockSpec((1, TM_PROJ, k), lambda b, i: (b, i, 0)),
            pl.BlockSpec((1, TM_PROJ, d), lambda b, i: (b, i, 0)),
            pl.BlockSpec((1, 1, d), lambda b, i: (b, 0, 0)),
            pl.BlockSpec((k, d), lambda b, i: (0, 0)),
        ],
        out_specs=pl.BlockSpec((1, TM_PROJ, d), lambda b, i: (b, i, 0)),
        out_shape=jax.ShapeDtypeStruct((bsz, s, d), jnp.float32),
        compiler_params=pltpu.CompilerParams(dimension_semantics=("parallel", "parallel"), vmem_limit_bytes=VMEM_LIMIT),
        name="outproj",
    )(mix, z, x, gate, w_bf16)


def _final_rms_kernel(x_ref, g_ref, o_ref):
    x = x_ref[0]
    o_ref[0] = x * lax.rsqrt(jnp.mean(x * x, axis=-1, keepdims=True) + EPS) * g_ref[...]


def _final_rms(x, g):
    bsz, s, d = x.shape
    tm = 512
    return pl.pallas_call(
        _final_rms_kernel,
        grid=(bsz, s // tm),
        in_specs=[pl.BlockSpec((1, tm, d), lambda b, i: (b, i, 0)), pl.BlockSpec((1, d), lambda b, i: (0, 0))],
        out_specs=pl.BlockSpec((1, tm, d), lambda b, i: (b, i, 0)),
        out_shape=jax.ShapeDtypeStruct((bsz, s, d), jnp.float32),
        compiler_params=pltpu.CompilerParams(dimension_semantics=("parallel", "parallel")),
        name="final_rms",
    )(x, g.reshape(1, d))


L_MLSTM = 256
_HI = lax.Precision.HIGHEST


def _log_sigmoid(t):
    return jnp.minimum(t, 0.0) - jnp.log(1.0 + jnp.exp(-jnp.abs(t)))


def _mlstm_kernel(qf_ref, kf_ref, vf_ref, gf_ref, gtf_ref, qb_ref, kb_ref, vb_ref, gb_ref, gtb_ref,
                  bias_ref, biast_ref, hf_ref, hb_ref, c_ref, m_ref):
    n = pl.program_id(1)
    ln = qf_ref.shape[1]
    f32, bf16 = jnp.float32, jnp.bfloat16

    @pl.when(n == 0)
    def _():
        c_ref[...] = jnp.zeros_like(c_ref)
        m_ref[...] = jnp.zeros_like(m_ref)

    row = lax.broadcasted_iota(jnp.int32, (ln, ln), 0)
    col = lax.broadcasted_iota(jnp.int32, (ln, ln), 1)
    ones_blk = jnp.ones((ln, A_DV), bf16)
    dirs = ((0, qf_ref, kf_ref, vf_ref, gf_ref, gtf_ref, hf_ref), (1, qb_ref, kb_ref, vb_ref, gb_ref, gtb_ref, hb_ref))
    for d, q_ref, k_ref, v_ref, g_ref, gt_ref, h_ref in dirs:
        mask = (row >= col) if d == 0 else (row <= col)
        tri = mask.astype(f32)
        tri_t = ((row <= col) if d == 0 else (row >= col)).astype(f32)
        g = g_ref[0] + bias_ref[...]
        gt = gt_ref[0] + biast_ref[...]
        ic = g[:, 4 * d:4 * d + 4]
        it = gt[4 * d:4 * d + 4, :]
        bc = jnp.dot(tri, _log_sigmoid(g[:, 8 + 4 * d:12 + 4 * d]), precision=_HI, preferred_element_type=f32)
        bt = jnp.dot(_log_sigmoid(gt[8 + 4 * d:12 + 4 * d, :]), tri_t, precision=_HI, preferred_element_type=f32)
        last = ln - 1 if d == 0 else 0
        for h in range(A_HEADS):
            r = d * A_HEADS + h
            q = q_ref[0, :, h * A_DK:(h + 1) * A_DK]
            k = k_ref[0, :, h * A_DK:(h + 1) * A_DK]
            vaug = jnp.concatenate([v_ref[0, :, h * A_DV:(h + 1) * A_DV], ones_blk], axis=1)
            bcol, icol = bc[:, h:h + 1], ic[:, h:h + 1]
            brow, irow = bt[h:h + 1, :], it[h:h + 1, :]
            m_old = m_ref[r:r + 1, 0:1]
            caug = c_ref[r]
            dmat = jnp.where(mask, bcol - brow + irow, -jnp.inf)
            dmax = jnp.max(dmat, axis=-1, keepdims=True)
            inter = bcol + m_old
            mt = jnp.maximum(inter, dmax)
            w_int = jnp.exp(inter - mt)
            qk = lax.dot_general(q, k, (((1,), (1,)), ((), ())), preferred_element_type=f32)
            sc = jnp.exp(dmat - mt) * qk
            tot = w_int * jnp.dot(q, caug.astype(bf16), preferred_element_type=f32) + jnp.dot(sc.astype(bf16), vaug, preferred_element_type=f32)
            den = jnp.maximum(jnp.abs(tot[:, A_DV:]), jnp.exp(-mt))
            h_ref[0, :, h * A_DV:(h + 1) * A_DV] = tot[:, :A_DV] / den
            btot_c = bcol[last:last + 1, :]
            btot_r = brow[:, last:last + 1]
            m_new = jnp.maximum(btot_r + m_old, jnp.max(btot_r - brow + irow, axis=-1, keepdims=True))
            w_old = jnp.exp(btot_r + m_old - m_new)
            kw = (k.astype(f32) * jnp.exp(btot_c - bcol + icol - m_new)).astype(bf16)
            c_ref[r] = w_old * caug + lax.dot_general(kw, vaug, (((0,), (0,)), ((), ())), preferred_element_type=f32)
            m_ref[r:r + 1, :] = jnp.broadcast_to(m_new, (1, m_ref.shape[1]))


def _mlstm(q, k, v, g, gt, bias):
    bsz, s, _ = q.shape
    ln = min(L_MLSTM, s)
    nc = s // ln
    hk, hv = A_HEADS * A_DK, A_HEADS * A_DV
    fwd = lambda b, n: (b, n, 0)
    bwd = lambda b, n: (b, nc - 1 - n, 0)
    fwd_t = lambda b, n: (b, 0, n)
    bwd_t = lambda b, n: (b, 0, nc - 1 - n)
    const = lambda b, n: (0, 0)
    def specs(im, im_t):
        return [pl.BlockSpec((1, ln, hk), im), pl.BlockSpec((1, ln, hk), im), pl.BlockSpec((1, ln, hv), im),
                pl.BlockSpec((1, ln, 16), im), pl.BlockSpec((1, 16, ln), im_t)]
    return pl.pallas_call(
        _mlstm_kernel,
        grid=(bsz, nc),
        in_specs=specs(fwd, fwd_t) + specs(bwd, bwd_t) + [pl.BlockSpec((1, 16), const), pl.BlockSpec((16, 1), const)],
        out_specs=[pl.BlockSpec((1, ln, hv), fwd), pl.BlockSpec((1, ln, hv), bwd)],
        out_shape=[jax.ShapeDtypeStruct((bsz, s, hv), jnp.float32)] * 2,
        scratch_shapes=[pltpu.VMEM((2 * A_HEADS, A_DK, 2 * A_DV), jnp.float32), pltpu.VMEM((2 * A_HEADS, 128), jnp.float32)],
        compiler_params=pltpu.CompilerParams(dimension_semantics=("parallel", "arbitrary"), vmem_limit_bytes=VMEM_LIMIT),
        name="mlstm",
    )(q, k, v, g, gt, q, k, v, g, gt, bias.reshape(1, 16), bias.reshape(16, 1))


T_GDN = 256
T_GDN_STEP = 256
C_GDN = 64
HALO = 8


def _softplus(t):
    return jnp.maximum(t, 0.0) + jnp.log1p(jnp.exp(-jnp.abs(t)))


def _gdn_prep_kernel(x_ref, xp_ref, xn_ref, g_ref, gt_ref, w_ref, a_ref, at_ref, dt_ref, dtt_ref,
                     q_ref, k_ref, v_ref, gc_ref, gr_ref, xe_ref):
    i = pl.program_id(1)
    nt = pl.num_programs(1)
    t = x_ref.shape[1]
    f32 = jnp.float32
    hd = B_HEADS * B_DK
    xe_ref[0:HALO, :] = jnp.where(i > 0, xp_ref[0], 0.0)
    xe_ref[HALO:HALO + t, :] = x_ref[0]
    xe_ref[HALO + t:, :] = jnp.where(i < nt - 1, xn_ref[0], 0.0)
    half = B_CONV // 2
    for part, o_ref in enumerate((q_ref, k_ref, v_ref)):
        cs = slice(part * hd, (part + 1) * hd)
        acc = None
        for j in range(B_CONV):
            term = xe_ref[HALO - half + j:HALO - half + j + t, cs] * w_ref[j:j + 1, cs]
            acc = term if acc is None else acc + term
        y = acc * jax.nn.sigmoid(acc)
        for h in range(B_HEADS):
            yh = y[:, h * B_DK:(h + 1) * B_DK]
            if part == 0:
                yh = yh * lax.rsqrt(jnp.sum(yh * yh, axis=-1, keepdims=True) + EPS) * (B_DK ** -0.5)
            elif part == 1:
                yh = yh * lax.rsqrt(jnp.sum(yh * yh, axis=-1, keepdims=True) + EPS)
            o_ref[0, :, h * B_DK:(h + 1) * B_DK] = yh.astype(o_ref.dtype)
    row = lax.broadcasted_iota(jnp.int32, (t, t), 0)
    col = lax.broadcasted_iota(jnp.int32, (t, t), 1)
    same = (row // C_GDN) == (col // C_GDN)
    lower = (same & (row >= col)).astype(f32)
    upper = (same & (row <= col)).astype(f32)
    g = g_ref[0]
    gt = gt_ref[0]
    nh = B_HEADS
    dec = -jnp.exp(a_ref[...]) * _softplus(g[:, 2 * nh:] + dt_ref[...])
    dect = -jnp.exp(at_ref[...]) * _softplus(gt[2 * nh:, :] + dtt_ref[...])
    gc_ref[0, :, 0:2 * nh] = jax.nn.sigmoid(g[:, 0:2 * nh])
    gc_ref[0, :, 2 * nh:3 * nh] = jnp.dot(lower, dec[:, 0:nh], precision=_HI, preferred_element_type=f32)
    gc_ref[0, :, 3 * nh:] = jnp.dot(upper, dec[:, nh:], precision=_HI, preferred_element_type=f32)
    gr_ref[0, 0:2 * nh, :] = jax.nn.sigmoid(gt[0:2 * nh, :])
    gr_ref[0, 2 * nh:3 * nh, :] = jnp.dot(dect[0:nh, :], upper, precision=_HI, preferred_element_type=f32)
    gr_ref[0, 3 * nh:, :] = jnp.dot(dect[nh:, :], lower, precision=_HI, preferred_element_type=f32)


def _gdn_prep(dqkv, g, gt, conv_w, a_log, dt_bias):
    bsz, s, n3 = dqkv.shape
    t = min(T_GDN, s)
    nt = s // t
    hd = B_HEADS * B_DK
    hb = t // HALO
    cur = lambda b, i: (b, i, 0)
    const = lambda b, i: (0, 0)
    bf16 = jnp.bfloat16
    return pl.pallas_call(
        _gdn_prep_kernel,
        grid=(bsz, nt),
        in_specs=[
            pl.BlockSpec((1, t, n3), cur),
            pl.BlockSpec((1, HALO, n3), lambda b, i: (b, jnp.maximum(i * hb - 1, 0), 0)),
            pl.BlockSpec((1, HALO, n3), lambda b, i: (b, jnp.minimum((i + 1) * hb, s // HALO - 1), 0)),
            pl.BlockSpec((1, t, 16), cur),
            pl.BlockSpec((1, 16, t), lambda b, i: (b, 0, i)),
            pl.BlockSpec((B_CONV, n3), const),
            pl.BlockSpec((1, 8), const), pl.BlockSpec((8, 1), const),
            pl.BlockSpec((1, 8), const), pl.BlockSpec((8, 1), const),
        ],
        out_specs=[pl.BlockSpec((1, t, hd), cur)] * 3 + [pl.BlockSpec((1, t, 16), cur), pl.BlockSpec((1, 16, t), lambda b, i: (b, 0, i))],
        out_shape=[jax.ShapeDtypeStruct((bsz, s, hd), bf16)] * 3 + [jax.ShapeDtypeStruct((bsz, s, 16), jnp.float32), jax.ShapeDtypeStruct((bsz, 16, s), jnp.float32)],
        scratch_shapes=[pltpu.VMEM((t + 2 * HALO, n3), jnp.float32)],
        compiler_params=pltpu.CompilerParams(dimension_semantics=("parallel", "parallel"), vmem_limit_bytes=VMEM_LIMIT),
        name="gdn_prep",
    )(dqkv, dqkv, dqkv, g, gt, conv_w, a_log.reshape(1, 8), a_log.reshape(8, 1), dt_bias.reshape(1, 8), dt_bias.reshape(8, 1))


def _tri_inverse_many(a_list, masks):
    eye, m16, m32, m64 = masks
    f32, bf16 = jnp.float32, jnp.bfloat16
    mm = lambda x, y: jnp.dot(x.astype(bf16), y.astype(bf16), preferred_element_type=f32)
    ads = [jnp.where(m16, a, 0.0) for a in a_list]
    xs = [eye - ad for ad in ads]
    ps = [mm(ad, ad) for ad in ads]
    for stage in range(3):
        xs = [x + mm(x, p) for x, p in zip(xs, ps)]
        if stage < 2:
            ps = [mm(p, p) for p in ps]
    for lo, hi in ((m16, m32), (m32, m64)):
        off = hi & ~lo
        ys = [mm(jnp.where(off, a, 0.0), x) for a, x in zip(a_list, xs)]
        xs = [x - mm(x, y) for x, y in zip(xs, ys)]
    return xs


def _gdn_kernel(qf_ref, kf_ref, vf_ref, gcf_ref, grf_ref, qb_ref, kb_ref, vb_ref, gcb_ref, grb_ref, of_ref, ob_ref, s_ref):
    n = pl.program_id(1)
    t = qf_ref.shape[1]
    c = C_GDN
    f32, bf16 = jnp.float32, jnp.bfloat16

    @pl.when(n == 0)
    def _():
        s_ref[...] = jnp.zeros_like(s_ref)

    row = lax.broadcasted_iota(jnp.int32, (c, c), 0)
    col = lax.broadcasted_iota(jnp.int32, (c, c), 1)
    eye = (row == col).astype(f32)
    blk = lambda w: (row // w) == (col // w)
    masks = (eye, blk(16), blk(32), blk(64))
    nh, nchunk = B_HEADS, t // c
    dir_refs = ((qf_ref, kf_ref, vf_ref, gcf_ref, grf_ref, of_ref), (qb_ref, kb_ref, vb_ref, gcb_ref, grb_ref, ob_ref))
    probs = [(d, h, ci) for d in range(2) for h in range(nh) for ci in range(nchunk)]
    xpose = (((1,), (1,)), ((), ()))

    def load(d, h, ci):
        q_ref, k_ref, v_ref, gc_ref, gr_ref, _ = dir_refs[d]
        rs, cs = slice(ci * c, (ci + 1) * c), slice(h * B_DK, (h + 1) * B_DK)
        beta = gc_ref[0, rs, d * nh + h:d * nh + h + 1]
        gcol = gc_ref[0, rs, (2 + d) * nh + h:(2 + d) * nh + h + 1]
        grow = gr_ref[0, (2 + d) * nh + h:(2 + d) * nh + h + 1, rs]
        return q_ref[0, rs, cs], k_ref[0, rs, cs], v_ref[0, rs, cs], beta, gcol, grow

    data = [load(*p) for p in probs]
    gams = []
    for (d, _, _), (_, _, _, _, gcol, grow) in zip(probs, data):
        incl = (row >= col) if d == 0 else (row <= col)
        gams.append(jnp.exp(jnp.where(incl, gcol - grow, -jnp.inf)))
    kks = [lax.dot_general(k, k, xpose, preferred_element_type=f32) for (_, k, _, _, _, _) in data]
    qks = [lax.dot_general(q, k, xpose, preferred_element_type=f32) for (q, k, _, _, _, _) in data]
    a_list = []
    for (d, _, _), (_, _, _, beta, _, _), kk, gam in zip(probs, data, kks, gams):
        strict = (row > col) if d == 0 else (row < col)
        a_list.append(jnp.where(strict, beta * kk * gam, 0.0))
    tinvs = _tri_inverse_many(a_list, masks)
    egcs = [jnp.exp(gcol) for (_, _, _, _, gcol, _) in data]
    uws = []
    for (q, k, v, beta, gcol, _), tinv, egc in zip(data, tinvs, egcs):
        rhs = jnp.concatenate([beta * v.astype(f32), (beta * egc) * k.astype(f32)], axis=1).astype(bf16)
        uws.append(jnp.dot(tinv.astype(bf16), rhs, preferred_element_type=f32))
    attns = [(qk * gam).astype(bf16) for qk, gam in zip(qks, gams)]
    index = {p: i for i, p in enumerate(probs)}
    chains = [(d, h) for d in range(2) for h in range(nh)]
    states = [s_ref[d * nh + h] for d, h in chains]
    for step in range(nchunk):
        ids = [index[(d, h, step if d == 0 else nchunk - 1 - step)] for d, h in chains]
        wss = []
        for i, state in zip(ids, states):
            q, _, _, _, _, _ = data[i]
            wq = jnp.concatenate([uws[i][:, B_DV:], q.astype(f32) * egcs[i]], axis=0).astype(bf16)
            wss.append(jnp.dot(wq, state.astype(bf16), preferred_element_type=f32))
        v_news = [(uws[i][:, :B_DV] - ws[:c]).astype(bf16) for i, ws in zip(ids, wss)]
        for (d, h), i, ws, v_new in zip(chains, ids, wss, v_news):
            ci = probs[i][2]
            dir_refs[d][5][0, ci * c:(ci + 1) * c, h * B_DV:(h + 1) * B_DV] = ws[c:] + jnp.dot(attns[i], v_new, preferred_element_type=f32)
        new_states = []
        for (d, h), i, state, v_new in zip(chains, ids, states, v_news):
            _, k, _, _, gcol, _ = data[i]
            last = c - 1 if d == 0 else 0
            gl = gcol[last:last + 1, :]
            kd = (k.astype(f32) * jnp.exp(gl - gcol)).astype(bf16)
            new_states.append(jnp.exp(gl) * state + lax.dot_general(kd, v_new, (((0,), (0,)), ((), ())), preferred_element_type=f32))
        states = new_states
    for (d, h), state in zip(chains, states):
        s_ref[d * nh + h] = state


def _gdn(q, k, v, gc, gr):
    bsz, s, hd = q.shape
    t = min(T_GDN_STEP, s)
    nb = s // t
    fwd = lambda b, n: (b, n, 0)
    bwd = lambda b, n: (b, nb - 1 - n, 0)
    def specs(im, im_t):
        return [pl.BlockSpec((1, t, hd), im)] * 3 + [pl.BlockSpec((1, t, 16), im), pl.BlockSpec((1, 16, t), im_t)]
    return pl.pallas_call(
        _gdn_kernel,
        grid=(bsz, nb),
        in_specs=specs(fwd, lambda b, n: (b, 0, n)) + specs(bwd, lambda b, n: (b, 0, nb - 1 - n)),
        out_specs=[pl.BlockSpec((1, t, hd), fwd), pl.BlockSpec((1, t, hd), bwd)],
        out_shape=[jax.ShapeDtypeStruct((bsz, s, hd), jnp.float32)] * 2,
        scratch_shapes=[pltpu.VMEM((2 * B_HEADS, B_DK, B_DV), jnp.float32)],
        compiler_params=pltpu.CompilerParams(dimension_semantics=("parallel", "arbitrary"), vmem_limit_bytes=VMEM_LIMIT),
        name="gdn",
    )(q, k, v, gc, gr, q, k, v, gc, gr)


def _split(p, sizes):
    return jnp.split(p, np.cumsum(sizes)[:-1].tolist(), axis=-1)


def _layernorm(x, g, b):
    xc = x - jnp.mean(x, axis=-1, keepdims=True)
    y = xc * lax.rsqrt(jnp.mean(xc * xc, axis=-1, keepdims=True) + EPS)
    return y * g + b


def _head_rms(t, g):
    bsz, s, h, d = t.shape
    y = t * lax.rsqrt(jnp.mean(t * t, axis=-1, keepdims=True) + EPS)
    return y.reshape(bsz, s, h * d) * g


def _l2n(t):
    return t * lax.rsqrt(jnp.sum(t * t, axis=-1, keepdims=True) + EPS)


def _dwconv(x, w):
    return lax.conv_general_dilated(x, w[:, None, :].astype(x.dtype), window_strides=(1,), padding='SAME', dimension_numbers=('NWC', 'WIO', 'NWC'), feature_group_count=x.shape[-1])


def _flip(t):
    return jnp.flip(t, axis=1)


def _to_chunks(t):
    bsz, s, h = t.shape[:3]
    t = t.reshape((bsz, s // CHUNK, CHUNK, h) + t.shape[3:])
    return jnp.moveaxis(t, (1, 3), (0, 2))


def _from_chunks(t):
    nc, bsz, h, l = t.shape[:4]
    t = jnp.moveaxis(t, (0, 2), (1, 3))
    return t.reshape((bsz, nc * l, h) + t.shape[4:])


def _mlstm_chunkwise(q, k, v, i_pre, logf):
    q, k, v, i_pre, logf = (_to_chunks(t) for t in (q, k, v, i_pre, logf))
    nc, bsz, h = q.shape[:3]
    causal = jnp.tril(jnp.ones((CHUNK, CHUNK), dtype=bool))
    b = jnp.cumsum(logf, axis=-1)
    dmat = jnp.where(causal, b[..., :, None] - b[..., None, :] + i_pre[..., None, :], -jnp.inf)
    dmax = jnp.max(dmat, axis=-1)
    qk = jnp.einsum('nbhld,nbhsd->nbhls', q, k)
    a_end = b[..., -1:] - b + i_pre

    def step(carry, xs):
        cmat, nvec, m = carry
        qc, kc, vc, bc, dc, dmc, qkc, aec = xs
        inter = bc + m[..., None]
        mt = jnp.maximum(inter, dmc)
        w_int = jnp.exp(inter - mt)
        sc = jnp.exp(dc - mt[..., None]) * qkc
        num = w_int[..., None] * jnp.einsum('bhld,bhde->bhle', qc, cmat) + jnp.einsum('bhls,bhse->bhle', sc, vc)
        den = w_int * jnp.einsum('bhld,bhd->bhl', qc, nvec) + jnp.sum(sc, axis=-1)
        hc = num / jnp.maximum(jnp.abs(den), jnp.exp(-mt))[..., None]
        m_new = jnp.maximum(bc[..., -1] + m, jnp.max(aec, axis=-1))
        w_old = jnp.exp(bc[..., -1] + m - m_new)
        kw = kc * jnp.exp(aec - m_new[..., None])[..., None]
        cmat = w_old[..., None, None] * cmat + jnp.einsum('bhld,bhle->bhde', kw, vc)
        nvec = w_old[..., None] * nvec + jnp.sum(kw, axis=-2)
        return (cmat, nvec, m_new), hc

    init = (jnp.zeros((bsz, h, A_DK, A_DV), jnp.float32), jnp.zeros((bsz, h, A_DK), jnp.float32), jnp.zeros((bsz, h), jnp.float32))
    _, hs = lax.scan(step, init, (q, k, v, b, dmat, dmax, qk, a_end))
    return _from_chunks(hs)


def _gdn_chunked(q, k, v, beta, g):
    q, k, v, beta, g = (_to_chunks(t) for t in (q, k, v, beta, g))
    nc, bsz, h = q.shape[:3]
    tril = jnp.tril(jnp.ones((CHUNK, CHUNK), dtype=bool))
    strict = jnp.tril(jnp.ones((CHUNK, CHUNK), dtype=bool), -1)
    gc = jnp.cumsum(g, axis=-1)
    gam = jnp.exp(jnp.where(tril, gc[..., :, None] - gc[..., None, :], -jnp.inf))
    a = jnp.where(strict, beta[..., :, None] * jnp.einsum('nbhid,nbhjd->nbhij', k, k) * gam, 0.0)
    tmat = a + jnp.eye(CHUNK, dtype=a.dtype)
    u = lax.linalg.triangular_solve(tmat, beta[..., None] * v, left_side=True, lower=True, unit_diagonal=True)
    w = lax.linalg.triangular_solve(tmat, (beta * jnp.exp(gc))[..., None] * k, left_side=True, lower=True, unit_diagonal=True)
    attn = jnp.einsum('nbhid,nbhjd->nbhij', q, k) * gam

    def step(state, xs):
        qc, kc, uc, wc, gcc, ac = xs
        v_new = uc - jnp.einsum('bhld,bhde->bhle', wc, state)
        o = jnp.einsum('bhld,bhde->bhle', qc * jnp.exp(gcc)[..., None], state) + jnp.einsum('bhls,bhse->bhle', ac, v_new)
        gl = gcc[..., -1]
        state = jnp.exp(gl)[..., None, None] * state + jnp.einsum('bhld,bhle->bhde', kc * jnp.exp(gl[..., None] - gcc)[..., None], v_new)
        return state, o

    _, os_ = lax.scan(step, jnp.zeros((bsz, h, B_DK, B_DV), jnp.float32), (q, k, u, w, gc, attn))
    return _from_chunks(os_)


def _t5_bucket(rel):
    half = REL_BUCKETS // 2
    exact = half // 2
    n = jnp.abs(rel)
    large = exact + (jnp.log(jnp.maximum(n, 1).astype(jnp.float32) / exact) / math.log(REL_MAX_DIST / exact) * (half - exact)).astype(jnp.int32)
    large = jnp.minimum(large, half - 1)
    return (rel > 0).astype(jnp.int32) * half + jnp.where(n < exact, n, large)


def _dilated_group(q, k, v, dilation, radius, rel_bias):
    bsz, s, h, dh = q.shape
    ls = s // dilation
    nb = -(-ls // radius)
    lp = nb * radius

    def sub(t, lo, hi):
        t = t.reshape(bsz, ls, dilation, h, dh).transpose(0, 3, 2, 1, 4)
        return jnp.pad(t, ((0, 0), (0, 0), (0, 0), (lo, hi), (0, 0)))

    qb = sub(q, 0, lp - ls).reshape(bsz, h, dilation, nb, radius, dh)

    def band(t):
        t = sub(t, radius, lp - ls + radius).reshape(bsz, h, dilation, nb + 2, radius, dh)
        return jnp.concatenate([t[:, :, :, :-2], t[:, :, :, 1:-1], t[:, :, :, 2:]], axis=4)

    kb, vb = band(k), band(v)
    qi = jnp.arange(radius)[:, None]
    kj = jnp.arange(3 * radius)[None, :]
    rel = kj - radius - qi
    kpos = jnp.arange(nb)[:, None, None] * radius + kj - radius
    valid = (jnp.abs(rel) <= radius) & (kpos >= 0) & (kpos < ls)
    bias = jnp.transpose(rel_bias[_t5_bucket(rel * dilation)], (2, 0, 1)).astype(jnp.float32)
    sc = jnp.einsum('bhrnid,bhrnjd->bhrnij', qb, kb).astype(jnp.float32) * (dh ** -0.5) + bias[:, None, None]
    sc = jnp.where(valid, sc, NEG)
    m = jnp.max(sc, axis=-1, keepdims=True)
    p = jnp.exp(sc - m)
    den = jnp.sum(p, axis=-1)
    o = jnp.einsum('bhrnij,bhrnjd->bhrnid', p, vb.astype(jnp.float32)) / den[..., None]
    lse = m[..., 0] + jnp.log(den)
    o = o.reshape(bsz, h, dilation, lp, dh)[:, :, :, :ls].transpose(0, 3, 2, 1, 4).reshape(bsz, s, h, dh)
    lse = lse.reshape(bsz, h, dilation, lp)[:, :, :, :ls].transpose(0, 3, 2, 1).reshape(bsz, s, h)
    return o, lse


def _dilated_attention(q, k, v, rel_bias):
    outs, lses = [], []
    for window, dilation in D_GROUPS:
        o, l = _dilated_group(q, k, v, dilation, window // (2 * dilation), rel_bias)
        outs.append(o)
        lses.append(l)
    wts = jax.nn.softmax(jnp.stack(lses, axis=0), axis=0)
    return jnp.sum(wts[..., None] * jnp.stack(outs, axis=0), axis=0)


def _even_mixer_core(p, m_gate_b, dn_dt_bias, dn_a_log, dn_conv_w, m_norm_g, dn_norm_g):
    bsz, s, _ = p.shape
    f32 = jnp.float32
    mq, mk, mv, mo, mg, dqkv, dg, z = _split(p, EVEN_SPLITS)
    q = mq.reshape(bsz, s, A_HEADS, A_DK)
    k = mk.reshape(bsz, s, A_HEADS, A_DK) * (A_DK ** -0.5)
    v = mv.reshape(bsz, s, A_HEADS, A_DV)
    gt = mg.reshape(bsz, s, 4, A_HEADS) + m_gate_b
    logf = jax.nn.log_sigmoid(gt[:, :, 2:4])
    h_fwd = _mlstm_chunkwise(q, k, v, gt[:, :, 0], logf[:, :, 0])
    h_bwd = _flip(_mlstm_chunkwise(_flip(q), _flip(k), _flip(v), _flip(gt[:, :, 1]), _flip(logf[:, :, 1])))
    out_a = jax.nn.sigmoid(mo) * _head_rms(h_fwd + h_bwd, m_norm_g)
    qkv = jax.nn.silu(_dwconv(dqkv, dn_conv_w))
    bq, bk, bv = _split(qkv, (B_HEADS * B_DK, B_HEADS * B_DK, B_HEADS * B_DV))
    q = _l2n(bq.reshape(bsz, s, B_HEADS, B_DK)) * (B_DK ** -0.5)
    k = _l2n(bk.reshape(bsz, s, B_HEADS, B_DK))
    v = bv.reshape(bsz, s, B_HEADS, B_DV)
    gb = dg.reshape(bsz, s, 4, B_HEADS)
    beta = jax.nn.sigmoid(gb[:, :, 0:2])
    decay = -jnp.exp(dn_a_log) * jax.nn.softplus(gb[:, :, 2:4] + dn_dt_bias)
    o_fwd = _gdn_chunked(q, k, v, beta[:, :, 0], decay[:, :, 0])
    o_bwd = _flip(_gdn_chunked(_flip(q), _flip(k), _flip(v), _flip(beta[:, :, 1]), _flip(decay[:, :, 1])))
    out_b = _head_rms(o_fwd + o_bwd, dn_norm_g)
    return jnp.concatenate([out_a, out_b], axis=-1), z


def _odd_mixer_core(p, dw_w, dw_b, ln_g, ln_b, rel_bias):
    bsz, s, _ = p.shape
    ga, gb, aq, ak, av, z = _split(p, ODD_SPLITS)
    u = _dwconv(ga * jax.nn.sigmoid(gb), dw_w) + dw_b
    out_c = jax.nn.silu(_layernorm(u, ln_g, ln_b))
    shp = (bsz, s, D_HEADS, D_DH)
    out_d = _dilated_attention(aq.reshape(shp), ak.reshape(shp), av.reshape(shp), rel_bias).reshape(bsz, s, D_HEADS * D_DH)
    return jnp.concatenate([out_c, out_d], axis=-1), z


def kernel(x, c, norm_g, ada_w, ada_b, ev_w_in, ev_m_gate_b, ev_dn_dt_bias, ev_dn_a_log, ev_dn_conv_w, ev_m_norm_g, ev_dn_norm_g, ev_w_out, od_w_in, od_dw_w, od_dw_b, od_ln_g, od_ln_b, od_w_out, rel_bias, final_g):
    assert DEPTH == 2, "the final RMSNorm is fused into the (last) odd layer's output projection"
    assert all(window // (2 * dil) == R_ATT for window, dil in D_GROUPS)
    d = x.shape[-1]
    mod = _adaln(c, ada_w, ada_b)
    for layer in range(DEPTH):
        shift, scale, gate = (mod[layer, :, i * d:(i + 1) * d][:, None, :] for i in range(3))
        j = layer // 2
        if layer % 2 == 0:
            x = _even_layer(x, norm_g[layer], scale, shift, gate, ev_w_in[j], ev_m_gate_b[j], ev_dn_dt_bias[j], ev_dn_a_log[j],
                            ev_dn_conv_w[j], ev_m_norm_g[j], ev_dn_norm_g[j], ev_w_out[j])
        else:
            x = _odd_layer_final(x, norm_g[layer], scale, shift, gate, od_w_in[j], od_dw_w[j], od_dw_b[j], od_ln_g[j], od_ln_b[j],
                                 rel_bias, od_w_out[j], final_g)
    return x


def _even_layer(x, norm_g, scale, shift, gate, w_in, m_gate_b, dn_dt_bias, dn_a_log, dn_conv_w, m_norm_g, dn_norm_g, w_out):
    bf16 = jnp.bfloat16
    mq, mk, mv, mo, mg, dqkv, dg, z = _split(w_in, EVEN_SPLITS)
    w = jnp.concatenate([mq, mk * (A_DK ** -0.5), mv, mo, dqkv, z], axis=1).astype(bf16)
    wg = jnp.concatenate([mg, dg], axis=1).astype(bf16)
    pq, pk, pv, po, pdqkv, pz, g_m, g_d, gt_m, gt_d = _inproj_even(x, norm_g, scale, shift, w, wg, wg.T)
    hf, hb = _mlstm(pq, pk, pv, g_m, gt_m, m_gate_b.reshape(16))
    bq, bk, bv, gc, gr = _gdn_prep(pdqkv, g_d, gt_d, dn_conv_w, dn_a_log, dn_dt_bias)
    of, ob = _gdn(bq, bk, bv, gc, gr)
    return _outproj_even(hf, hb, of, ob, po, pz, x, gate, m_norm_g, dn_norm_g, w_out.astype(bf16))


def _odd_layer_final(x, norm_g, scale, shift, gate, w_in, dw_w, dw_b, ln_g, ln_b, rel_bias, w_out, final_g):
    bf16 = jnp.bfloat16
    ga, gb, aq, ak, av, z = _split(w_in, ODD_SPLITS)
    w = jnp.concatenate([ga, gb, aq * (D_DH ** -0.5), ak, av, z], axis=1).astype(bf16)
    glu, pz, pq, pk, pv = _inproj_odd(x, norm_g, scale, shift, w)
    out_c = _conformer(glu, dw_w, dw_b, ln_g, ln_b)
    og, lg = zip(*[_dilated_group_call(qd, kd, vd, rel_bias, dil) for qd, kd, vd, dil in zip(pq, pk, pv, DILATIONS)])
    return _outproj_odd_final(out_c, og, lg, pz, x, gate, final_g, w_out.astype(bf16))
```

```python
import math
from functools import partial

import jax
import jax.numpy as jnp
import numpy as np
from jax import lax
from jax.experimental import pallas as pl
from jax.experimental.pallas import tpu as pltpu

D_MODEL = 1024
BATCH = 4
SEQ = 8192
DEPTH = 2
A_HEADS = 4
A_DK = 64
A_DV = 128
B_HEADS = 4
B_DK = 128
B_DV = 128
B_CONV = 5
C_WIDTH = 512
C_CONV = 31
D_HEADS = 8
D_DH = 64
D_GROUPS = ((128, 1), (512, 4), (2048, 16))
REL_BUCKETS = 32
REL_MAX_DIST = 1024
CHUNK = 64
EPS = 1e-6
NEG = -1e30
MIX_EVEN = A_HEADS * A_DV + B_HEADS * B_DV
MIX_ODD = C_WIDTH + D_HEADS * D_DH
B_QKV = B_HEADS * (2 * B_DK + B_DV)
EVEN_SPLITS = (A_HEADS * A_DK, A_HEADS * A_DK, A_HEADS * A_DV, A_HEADS * A_DV, 4 * A_HEADS, B_QKV, 4 * B_HEADS, MIX_EVEN)
ODD_SPLITS = (C_WIDTH, C_WIDTH, D_HEADS * D_DH, D_HEADS * D_DH, D_HEADS * D_DH, MIX_ODD)

VMEM_LIMIT = 56 * 1024 * 1024
TM_PROJ = 256


def _adaln_kernel(c_ref, w_ref, b_ref, o_ref):
    c = c_ref[...]
    cs = (c * jax.nn.sigmoid(c)).astype(jnp.bfloat16)
    o_ref[0] = jnp.dot(cs, w_ref[0].astype(jnp.bfloat16), preferred_element_type=jnp.float32) + b_ref[0]


def _adaln(c, ada_w, ada_b):
    depth, d, n3 = ada_w.shape
    bsz = c.shape[0]
    tn = 1024
    return pl.pallas_call(
        _adaln_kernel,
        grid=(depth, n3 // tn),
        in_specs=[pl.BlockSpec((bsz, d), lambda l, j: (0, 0)), pl.BlockSpec((1, d, tn), lambda l, j: (l, 0, j)),
                  pl.BlockSpec((1, 1, tn), lambda l, j: (l, 0, j))],
        out_specs=pl.BlockSpec((1, bsz, tn), lambda l, j: (l, 0, j)),
        out_shape=jax.ShapeDtypeStruct((depth, bsz, n3), jnp.float32),
        compiler_params=pltpu.CompilerParams(dimension_semantics=("parallel", "parallel")),
        name="adaln",
    )(c, ada_w, ada_b.reshape(depth, 1, n3))


def _modulated_rms(x_ref, g_ref, sc_ref, sh_ref):
    x = x_ref[0]
    y = x * lax.rsqrt(jnp.mean(x * x, axis=-1, keepdims=True) + EPS)
    return ((y * g_ref[...]) * (1.0 + sc_ref[0]) + sh_ref[0]).astype(jnp.bfloat16)


_EV_COLS = {"mq": (0, 256), "mk": (256, 512), "mv": (512, 1024), "dqkv": (1024, 2560)}
_OD_COLS = {"ga": (0, 512), "gb": (512, 1024), "aq": (1024, 1536), "ak": (1536, 2048), "av": (2048, 2560)}


def _inproj_even_kernel(x_ref, g_ref, sc_ref, sh_ref, w_ref, wg_ref, wgt_ref,
                        mq_ref, mk_ref, mv_ref, dqkv_ref, mg_ref, dg_ref, mgt_ref, dgt_ref):
    f32 = jnp.float32
    h = _modulated_rms(x_ref, g_ref, sc_ref, sh_ref)
    for name, o_ref in (("mq", mq_ref), ("mk", mk_ref), ("mv", mv_ref), ("dqkv", dqkv_ref)):
        lo, hi = _EV_COLS[name]
        o_ref[0] = jnp.dot(h, w_ref[:, lo:hi], preferred_element_type=f32).astype(o_ref.dtype)
    gates = jnp.dot(h, wg_ref[...], preferred_element_type=f32)
    gates_t = lax.dot_general(wgt_ref[...], h, (((1,), (1,)), ((), ())), preferred_element_type=f32)
    mg_ref[0] = gates[:, :16]
    dg_ref[0] = gates[:, 16:]
    mgt_ref[0] = gates_t[:16]
    dgt_ref[0] = gates_t[16:]


def _inproj_even(x, g, scale, shift, w, wg, wgt):
    bsz, s, d = x.shape
    tm = TM_PROJ
    tok = lambda b, i: (b, i, 0)
    tok_t = lambda b, i: (b, 0, i)
    const = lambda b, i: (0, 0)
    bvec = lambda b, i: (b, 0, 0)
    bf16, f32 = jnp.bfloat16, jnp.float32
    outs = [("mq", bf16), ("mk", bf16), ("mv", bf16), ("dqkv", f32)]
    widths = [_EV_COLS[n][1] - _EV_COLS[n][0] for n, _ in outs]
    return pl.pallas_call(
        _inproj_even_kernel,
        grid=(bsz, s // tm),
        in_specs=[pl.BlockSpec((1, tm, d), tok), pl.BlockSpec((1, d), const), pl.BlockSpec((1, 1, d), bvec), pl.BlockSpec((1, 1, d), bvec),
                  pl.BlockSpec(w.shape, const), pl.BlockSpec(wg.shape, const), pl.BlockSpec(wgt.shape, const)],
        out_specs=[pl.BlockSpec((1, tm, wd), tok) for wd in widths] + [pl.BlockSpec((1, tm, 16), tok)] * 2 + [pl.BlockSpec((1, 16, tm), tok_t)] * 2,
        out_shape=[jax.ShapeDtypeStruct((bsz, s, wd), dt) for wd, (_, dt) in zip(widths, outs)]
        + [jax.ShapeDtypeStruct((bsz, s, 16), f32)] * 2 + [jax.ShapeDtypeStruct((bsz, 16, s), f32)] * 2,
        compiler_params=pltpu.CompilerParams(dimension_semantics=("parallel", "parallel"), vmem_limit_bytes=VMEM_LIMIT),
        name="inproj_even",
    )(x, g.reshape(1, d), scale, shift, w, wg, wgt)


DILATIONS = tuple(dil for _, dil in D_GROUPS)
LANES = 128


def _inproj_odd_kernel(x_ref, g_ref, sc_ref, sh_ref, w_ref, glu_ref, *rest):
    out_refs, plane_ref = rest[:-1], rest[-1]
    f32 = jnp.float32
    tm = x_ref.shape[1]
    nd = D_HEADS * D_DH
    h = _modulated_rms(x_ref, g_ref, sc_ref, sh_ref)
    dot = lambda name: jnp.dot(h, w_ref[:, _OD_COLS[name][0]:_OD_COLS[name][1]], preferred_element_type=f32)
    glu_ref[0] = dot("ga") * jax.nn.sigmoid(dot("gb"))
    for a, name in enumerate(("aq", "ak", "av")):
        r = dot(name)
        group_refs = out_refs[a * len(DILATIONS):(a + 1) * len(DILATIONS)]
        for j in range(nd // LANES):
            plane_ref[a, j] = r[:, j * LANES:(j + 1) * LANES]
        for dil, o_ref in zip(DILATIONS, group_refs):
            if dil == 1:
                o_ref[0, 0] = r.astype(o_ref.dtype)
                continue
            for res in range(dil):
                for j in range(nd // LANES):
                    o_ref[0, res, :, j * LANES:(j + 1) * LANES] = plane_ref[a, j, pl.ds(res, tm // dil, stride=dil), :].astype(o_ref.dtype)


def _inproj_odd(x, g, scale, shift, w):
    bsz, s, d = x.shape
    tm = TM_PROJ
    tok = lambda b, i: (b, i, 0)
    const = lambda b, i: (0, 0)
    bvec = lambda b, i: (b, 0, 0)
    bf16, f32 = jnp.bfloat16, jnp.float32
    nd = D_HEADS * D_DH
    att_specs = [pl.BlockSpec((1, dil, tm // dil, nd), lambda b, i: (b, 0, i, 0)) for dil in DILATIONS] * 3
    att_shapes = [jax.ShapeDtypeStruct((bsz, dil, s // dil, nd), bf16) for dil in DILATIONS] * 3
    outs = pl.pallas_call(
        _inproj_odd_kernel,
        grid=(bsz, s // tm),
        in_specs=[pl.BlockSpec((1, tm, d), tok), pl.BlockSpec((1, d), const), pl.BlockSpec((1, 1, d), bvec), pl.BlockSpec((1, 1, d), bvec),
                  pl.BlockSpec(w.shape, const)],
        out_specs=[pl.BlockSpec((1, tm, C_WIDTH), tok)] + att_specs,
        out_shape=[jax.ShapeDtypeStruct((bsz, s, C_WIDTH), f32)] + att_shapes,
        scratch_shapes=[pltpu.VMEM((3, nd // LANES, tm, LANES), f32)],
        compiler_params=pltpu.CompilerParams(dimension_semantics=("parallel", "parallel"), vmem_limit_bytes=VMEM_LIMIT),
        name="inproj_odd",
    )(x, g.reshape(1, d), scale, shift, w)
    ng = len(DILATIONS)
    return outs[0], outs[1:1 + ng], outs[1 + ng:1 + 2 * ng], outs[1 + 2 * ng:]


def _head_rms_cols(t, g, width):
    parts = []
    for h in range(t.shape[1] // width):
        th = t[:, h * width:(h + 1) * width]
        parts.append(th * lax.rsqrt(jnp.mean(th * th, axis=-1, keepdims=True) + EPS))
    return jnp.concatenate(parts, axis=1) * g


def _outproj_even_kernel(hf_ref, hb_ref, of_ref, ob_ref, x_ref, g_ref, sc_ref, sh_ref, gate_ref, mg_ref, dg_ref, wz_ref, w_ref, o_ref):
    f32, bf16 = jnp.float32, jnp.bfloat16
    na = A_HEADS * A_DV
    h = _modulated_rms(x_ref, g_ref, sc_ref, sh_ref)
    mo = jnp.dot(h, wz_ref[:, :na], preferred_element_type=f32)
    z = jnp.dot(h, wz_ref[:, na:], preferred_element_type=f32)
    sz = z * jax.nn.sigmoid(z)
    out_a = jax.nn.sigmoid(mo) * _head_rms_cols(hf_ref[0] + hb_ref[0], mg_ref[...], A_DV)
    out_b = _head_rms_cols(of_ref[0] + ob_ref[0], dg_ref[...], B_DV)
    y = jnp.dot((out_a * sz[:, :na]).astype(bf16), w_ref[:na, :], preferred_element_type=f32)
    y = y + jnp.dot((out_b * sz[:, na:]).astype(bf16), w_ref[na:, :], preferred_element_type=f32)
    o_ref[0] = x_ref[0] + gate_ref[0] * y


def _outproj_even(hf, hb, of, ob, x, norm_g, scale, shift, gate, m_norm_g, dn_norm_g, wz, w):
    bsz, s, d = x.shape
    tm = TM_PROJ
    tok = lambda b, i: (b, i, 0)
    const = lambda b, i: (0, 0)
    bvec = lambda b, i: (b, 0, 0)
    na, nb = A_HEADS * A_DV, B_HEADS * B_DV
    return pl.pallas_call(
        _outproj_even_kernel,
        grid=(bsz, s // tm),
        in_specs=[pl.BlockSpec((1, tm, na), tok)] * 2 + [pl.BlockSpec((1, tm, nb), tok)] * 2 + [pl.BlockSpec((1, tm, d), tok),
                  pl.BlockSpec((1, d), const), pl.BlockSpec((1, 1, d), bvec), pl.BlockSpec((1, 1, d), bvec), pl.BlockSpec((1, 1, d), bvec),
                  pl.BlockSpec((1, na), const), pl.BlockSpec((1, nb), const), pl.BlockSpec(wz.shape, const), pl.BlockSpec(w.shape, const)],
        out_specs=pl.BlockSpec((1, tm, d), tok),
        out_shape=jax.ShapeDtypeStruct((bsz, s, d), jnp.float32),
        compiler_params=pltpu.CompilerParams(dimension_semantics=("parallel", "parallel"), vmem_limit_bytes=VMEM_LIMIT),
        name="outproj_even",
    )(hf, hb, of, ob, x, norm_g.reshape(1, d), scale, shift, gate, m_norm_g.reshape(1, na), dn_norm_g.reshape(1, nb), wz, w)


def _outproj_odd_kernel(oc_ref, o1_ref, o2_ref, o3_ref, l1_ref, l2_ref, l3_ref, x_ref, g_ref, sc_ref, sh_ref, gate_ref, fg_ref,
                        wz_ref, w_ref, o_ref, nat_ref):
    f32, bf16 = jnp.float32, jnp.bfloat16
    tm = x_ref.shape[1]
    npl = D_HEADS * D_DH // LANES
    z = jnp.dot(_modulated_rms(x_ref, g_ref, sc_ref, sh_ref), wz_ref[...], preferred_element_type=f32)
    sz = z * jax.nn.sigmoid(z)
    groups = []
    for gi, (dil, og_ref, lg_ref) in enumerate(zip(DILATIONS, (o1_ref, o2_ref, o3_ref), (l1_ref, l2_ref, l3_ref))):
        if dil == 1:
            groups.append(([og_ref[0, 0, :, j * LANES:(j + 1) * LANES] for j in range(npl)], lg_ref[0, 0]))
            continue
        for res in range(dil):
            rows = pl.ds(res, tm // dil, stride=dil)
            for j in range(npl):
                nat_ref[gi, j, rows, :] = og_ref[0, res, :, j * LANES:(j + 1) * LANES]
            nat_ref[gi, npl, rows, :] = lg_ref[0, res]
        groups.append(([nat_ref[gi, j] for j in range(npl)], nat_ref[gi, npl]))
    (p1, l1), (p2, l2), (p3, l3) = groups
    lm = jnp.maximum(jnp.maximum(l1, l2), l3)
    e1, e2, e3 = jnp.exp(l1 - lm), jnp.exp(l2 - lm), jnp.exp(l3 - lm)
    inv = 1.0 / (e1 + e2 + e3)
    low = lax.broadcasted_iota(jnp.int32, (tm, LANES), 1) < D_DH
    planes = []
    for j in range(npl):
        acc = None
        for e, p in ((e1, p1), (e2, p2), (e3, p3)):
            wgt = e * inv
            term = jnp.where(low, wgt[:, 2 * j:2 * j + 1], wgt[:, 2 * j + 1:2 * j + 2]) * p[j]
            acc = term if acc is None else acc + term
        planes.append(acc)
    out_d = jnp.concatenate(planes, axis=1)
    y = jnp.dot((oc_ref[0] * sz[:, :C_WIDTH]).astype(bf16), w_ref[:C_WIDTH, :], preferred_element_type=f32)
    y = y + jnp.dot((out_d * sz[:, C_WIDTH:]).astype(bf16), w_ref[C_WIDTH:, :], preferred_element_type=f32)
    xn = x_ref[0] + gate_ref[0] * y
    o_ref[0] = xn * lax.rsqrt(jnp.mean(xn * xn, axis=-1, keepdims=True) + EPS) * fg_ref[...]


def _outproj_odd_final(oc, og, lg, x, norm_g, scale, shift, gate, final_g, wz, w):
    bsz, s, d = x.shape
    tm = TM_PROJ
    tok = lambda b, i: (b, i, 0)
    const = lambda b, i: (0, 0)
    bvec = lambda b, i: (b, 0, 0)
    nd = D_HEADS * D_DH
    res_major = lambda width: [pl.BlockSpec((1, dil, tm // dil, width), lambda b, i: (b, 0, i, 0)) for dil in DILATIONS]
    return pl.pallas_call(
        _outproj_odd_kernel,
        grid=(bsz, s // tm),
        in_specs=[pl.BlockSpec((1, tm, C_WIDTH), tok)] + res_major(nd) + res_major(LANES)
        + [pl.BlockSpec((1, tm, d), tok), pl.BlockSpec((1, d), const), pl.BlockSpec((1, 1, d), bvec), pl.BlockSpec((1, 1, d), bvec),
           pl.BlockSpec((1, 1, d), bvec), pl.BlockSpec((1, d), const), pl.BlockSpec(wz.shape, const), pl.BlockSpec(w.shape, const)],
        out_specs=pl.BlockSpec((1, tm, d), tok),
        out_shape=jax.ShapeDtypeStruct((bsz, s, d), jnp.float32),
        scratch_shapes=[pltpu.VMEM((len(DILATIONS), nd // LANES + 1, tm, LANES), jnp.float32)],
        compiler_params=pltpu.CompilerParams(dimension_semantics=("parallel", "parallel"), vmem_limit_bytes=VMEM_LIMIT),
        name="outproj_odd",
    )(oc, *og, *lg, x, norm_g.reshape(1, d), scale, shift, gate, final_g.reshape(1, d), wz, w)


T_CONV = 512
HALO_C = 16
SUB_C = 64


def _conformer_kernel(x_ref, xp_ref, xn_ref, w_ref, b_ref, lg_ref, lb_ref, o_ref, xe_ref):
    i = pl.program_id(1)
    nt = pl.num_programs(1)
    t = x_ref.shape[1]
    xe_ref[0:HALO_C, :] = jnp.where(i > 0, xp_ref[0], 0.0)
    xe_ref[HALO_C:HALO_C + t, :] = x_ref[0]
    xe_ref[HALO_C + t:, :] = jnp.where(i < nt - 1, xn_ref[0], 0.0)
    half = C_CONV // 2
    for r0 in range(0, t, SUB_C):
        acc = None
        for j in range(C_CONV):
            lo = HALO_C - half + j + r0
            term = xe_ref[lo:lo + SUB_C, :] * w_ref[j:j + 1, :]
            acc = term if acc is None else acc + term
        u = acc + b_ref[...]
        uc = u - jnp.mean(u, axis=-1, keepdims=True)
        y = uc * lax.rsqrt(jnp.mean(uc * uc, axis=-1, keepdims=True) + EPS) * lg_ref[...] + lb_ref[...]
        o_ref[0, r0:r0 + SUB_C, :] = y * jax.nn.sigmoid(y)


def _conformer(glu, dw_w, dw_b, ln_g, ln_b):
    bsz, s, cw = glu.shape
    t = min(T_CONV, s)
    hb = t // HALO_C
    cur = lambda b, i: (b, i, 0)
    const = lambda b, i: (0, 0)
    return pl.pallas_call(
        _conformer_kernel,
        grid=(bsz, s // t),
        in_specs=[pl.BlockSpec((1, t, cw), cur),
                  pl.BlockSpec((1, HALO_C, cw), lambda b, i: (b, jnp.maximum(i * hb - 1, 0), 0)),
                  pl.BlockSpec((1, HALO_C, cw), lambda b, i: (b, jnp.minimum((i + 1) * hb, s // HALO_C - 1), 0)),
                  pl.BlockSpec((C_CONV, cw), const)] + [pl.BlockSpec((1, cw), const)] * 3,
        out_specs=pl.BlockSpec((1, t, cw), cur),
        out_shape=jax.ShapeDtypeStruct((bsz, s, cw), jnp.float32),
        scratch_shapes=[pltpu.VMEM((t + 2 * HALO_C, cw), jnp.float32)],
        compiler_params=pltpu.CompilerParams(dimension_semantics=("parallel", "parallel"), vmem_limit_bytes=VMEM_LIMIT),
        name="conformer",
    )(glu, glu, glu, dw_w, dw_b.reshape(1, cw), ln_g.reshape(1, cw), ln_b.reshape(1, cw))


TQ_ATT = 128
R_ATT = 64


def _dilated_kernel(q_ref, kc_ref, kp_ref, kn_ref, vc_ref, vp_ref, vn_ref, bias_ref, o_ref, lse_ref, kx_ref, vx_ref):
    i = pl.program_id(2)
    nt = pl.num_programs(2)
    tq = q_ref.shape[1]
    nk = tq + 2 * R_ATT
    f32, bf16 = jnp.float32, jnp.bfloat16
    kx_ref[0:R_ATT, :] = kp_ref[0]
    kx_ref[R_ATT:R_ATT + tq, :] = kc_ref[0]
    kx_ref[R_ATT + tq:, :] = kn_ref[0]
    vx_ref[0:R_ATT, :] = vp_ref[0]
    vx_ref[R_ATT:R_ATT + tq, :] = vc_ref[0]
    vx_ref[R_ATT + tq:, :] = vn_ref[0]
    kj = lax.broadcasted_iota(jnp.int32, (tq, nk), 1)
    outside = ((kj < R_ATT) & (i == 0)) | ((kj >= R_ATT + tq) & (i == nt - 1))
    lane = lax.broadcasted_iota(jnp.int32, (tq, 128), 1)
    low = lane < D_DH
    heads = [(pr, hi) for pr in range(D_HEADS // 2) for hi in (False, True)]
    scs = []
    for pr, hi in heads:
        ps = slice(pr * 128, (pr + 1) * 128)
        qp = q_ref[0, :, ps]
        qh = jnp.where(low != hi, qp, jnp.zeros_like(qp))
        sc = lax.dot_general(qh, kx_ref[:, ps], (((1,), (1,)), ((), ())), preferred_element_type=f32) + bias_ref[2 * pr + int(hi)]
        scs.append(jnp.where(outside, NEG, sc))
    ms = [jnp.max(sc, axis=-1, keepdims=True) for sc in scs]
    ps_ = [jnp.exp(sc - m) for sc, m in zip(scs, ms)]
    dens = [jnp.sum(p, axis=-1, keepdims=True) for p in ps_]
    pvs = [jnp.dot(p.astype(bf16), vx_ref[:, pr * 128:(pr + 1) * 128], preferred_element_type=f32) for (pr, _), p in zip(heads, ps_)]
    lse_all = jnp.zeros((tq, 128), f32)
    for pr in range(D_HEADS // 2):
        lo, hi = 2 * pr, 2 * pr + 1
        o_ref[0, :, pr * 128:(pr + 1) * 128] = jnp.where(low, pvs[lo] / dens[lo], pvs[hi] / dens[hi])
        lse_all = jnp.where(lane == lo, ms[lo] + jnp.log(dens[lo]), lse_all)
        lse_all = jnp.where(lane == hi, ms[hi] + jnp.log(dens[hi]), lse_all)
    lse_ref[0] = lse_all


def _dilated_bias(rel_bias, dilation, tq):
    half = REL_BUCKETS // 2
    exact = half // 2
    qi = jnp.arange(tq)[:, None]
    kj = jnp.arange(tq + 2 * R_ATT)[None, :]
    rel = kj - R_ATT - qi
    reld = rel * dilation
    n = jnp.abs(reld)
    large = exact + (jnp.log(jnp.maximum(n, 1).astype(jnp.float32) / exact) / math.log(REL_MAX_DIST / exact) * (half - exact)).astype(jnp.int32)
    large = jnp.minimum(large, half - 1)
    bucket = (reld > 0).astype(jnp.int32) * half + jnp.where(n < exact, n, large)
    bias = jnp.zeros((rel_bias.shape[1],) + bucket.shape, jnp.float32)
    for b in range(REL_BUCKETS):
        bias = jnp.where((bucket == b)[None], rel_bias[b].astype(jnp.float32)[:, None, None], bias)
    return jnp.where((jnp.abs(rel) <= R_ATT)[None], bias, NEG)


def _dilated_group_call(q, k, v, rel_bias, dilation):
    bsz, dil, ls, nd = q.shape
    assert dil == dilation
    tq = min(TQ_ATT, ls)
    nt = ls // tq
    hb = tq // R_ATT
    nk = tq + 2 * R_ATT
    cur = lambda b, r, i: (b, r, i, 0)
    prev = lambda b, r, i: (b, r, jnp.maximum(i * hb - 1, 0), 0)
    nxt = lambda b, r, i: (b, r, jnp.minimum((i + 1) * hb, ls // R_ATT - 1), 0)
    kv_specs = [pl.BlockSpec((1, None, tq, nd), cur), pl.BlockSpec((1, None, R_ATT, nd), prev), pl.BlockSpec((1, None, R_ATT, nd), nxt)]
    return pl.pallas_call(
        _dilated_kernel,
        grid=(bsz, dilation, nt),
        in_specs=[pl.BlockSpec((1, None, tq, nd), cur)] + kv_specs + kv_specs + [pl.BlockSpec((D_HEADS, tq, nk), lambda b, r, i: (0, 0, 0))],
        out_specs=[pl.BlockSpec((1, None, tq, nd), cur), pl.BlockSpec((1, None, tq, 128), cur)],
        out_shape=[jax.ShapeDtypeStruct((bsz, dilation, ls, nd), jnp.float32), jax.ShapeDtypeStruct((bsz, dilation, ls, 128), jnp.float32)],
        scratch_shapes=[pltpu.VMEM((nk, nd), jnp.bfloat16)] * 2,
        compiler_params=pltpu.CompilerParams(dimension_semantics=("parallel", "parallel", "parallel"), vmem_limit_bytes=VMEM_LIMIT),
        name=f"dilated_d{dilation}",
    )(q, k, k, k, v, v, v, _dilated_bias(rel_bias, dilation, tq))


def _inproj(x, g, scale, shift, w_bf16):
    bsz, s, d = x.shape
    n = w_bf16.shape[1]
    return pl.pallas_call(
        _inproj_kernel,
        grid=(bsz, s // TM_PROJ),
        in_specs=[
            pl.BlockSpec((1, TM_PROJ, d), lambda b, i: (b, i, 0)),
            pl.BlockSpec((1, d), lambda b, i: (0, 0)),
            pl.BlockSpec((1, 1, d), lambda b, i: (b, 0, 0)),
            pl.BlockSpec((1, 1, d), lambda b, i: (b, 0, 0)),
            pl.BlockSpec((d, n), lambda b, i: (0, 0)),
        ],
        out_specs=pl.BlockSpec((1, TM_PROJ, n), lambda b, i: (b, i, 0)),
        out_shape=jax.ShapeDtypeStruct((bsz, s, n), jnp.float32),
        compiler_params=pltpu.CompilerParams(dimension_semantics=("parallel", "parallel"), vmem_limit_bytes=VMEM_LIMIT),
        name="inproj",
    )(x, g.reshape(1, d), scale, shift, w_bf16)


def _outproj_kernel(mix_ref, z_ref, x_ref, gate_ref, w_ref, o_ref):
    z = z_ref[0]
    m = mix_ref[0] * (z * jax.nn.sigmoid(z))
    y = jnp.dot(m.astype(jnp.bfloat16), w_ref[...], preferred_element_type=jnp.float32)
    o_ref[0] = x_ref[0] + gate_ref[0] * y


def _outproj(mix, z, x, gate, w_bf16):
    bsz, s, d = x.shape
    k = mix.shape[-1]
    return pl.pallas_call(
        _outproj_kernel,
        grid=(bsz, s // TM_PROJ),
        in_specs=[
            pl.BlockSpec((1, TM_PROJ, k), lambda b, i: (b, i, 0)),
            pl.BlockSpec((1, TM_PROJ, k), lambda b, i: (b, i, 0)),
            pl.BlockSpec((1, TM_PROJ, d), lambda b, i: (b, i, 0)),
            pl.BlockSpec((1, 1, d), lambda b, i: (b, 0, 0)),
            pl.BlockSpec((k, d), lambda b, i: (0, 0)),
        ],
        out_specs=pl.BlockSpec((1, TM_PROJ, d), lambda b, i: (b, i, 0)),
        out_shape=jax.ShapeDtypeStruct((bsz, s, d), jnp.float32),
        compiler_params=pltpu.CompilerParams(dimension_semantics=("parallel", "parallel"), vmem_limit_bytes=VMEM_LIMIT),
        name="outproj",
    )(mix, z, x, gate, w_bf16)


def _final_rms_kernel(x_ref, g_ref, o_ref):
    x = x_ref[0]
    o_ref[0] = x * lax.rsqrt(jnp.mean(x * x, axis=-1, keepdims=True) + EPS) * g_ref[...]


def _final_rms(x, g):
    bsz, s, d = x.shape
    tm = 512
    return pl.pallas_call(
        _final_rms_kernel,
        grid=(bsz, s // tm),
        in_specs=[pl.BlockSpec((1, tm, d), lambda b, i: (b, i, 0)), pl.BlockSpec((1, d), lambda b, i: (0, 0))],
        out_specs=pl.BlockSpec((1, tm, d), lambda b, i: (b, i, 0)),
        out_shape=jax.ShapeDtypeStruct((bsz, s, d), jnp.float32),
        compiler_params=pltpu.CompilerParams(dimension_semantics=("parallel", "parallel")),
        name="final_rms",
    )(x, g.reshape(1, d))


L_MLSTM = 256
_HI = lax.Precision.HIGHEST


def _log_sigmoid(t):
    return jnp.minimum(t, 0.0) - jnp.log(1.0 + jnp.exp(-jnp.abs(t)))


def _mlstm_kernel(qf_ref, kf_ref, vf_ref, gf_ref, gtf_ref, qb_ref, kb_ref, vb_ref, gb_ref, gtb_ref,
                  bias_ref, biast_ref, hf_ref, hb_ref, c_ref, m_ref):
    n = pl.program_id(1)
    ln = qf_ref.shape[1]
    f32, bf16 = jnp.float32, jnp.bfloat16

    @pl.when(n == 0)
    def _():
        c_ref[...] = jnp.zeros_like(c_ref)
        m_ref[...] = jnp.zeros_like(m_ref)

    row = lax.broadcasted_iota(jnp.int32, (ln, ln), 0)
    col = lax.broadcasted_iota(jnp.int32, (ln, ln), 1)
    ones_blk = jnp.ones((ln, A_DV), bf16)
    dirs = ((0, qf_ref, kf_ref, vf_ref, gf_ref, gtf_ref, hf_ref), (1, qb_ref, kb_ref, vb_ref, gb_ref, gtb_ref, hb_ref))
    probs = []
    for d, q_ref, k_ref, v_ref, g_ref, gt_ref, h_ref in dirs:
        mask = (row >= col) if d == 0 else (row <= col)
        tri = mask.astype(f32)
        tri_t = ((row <= col) if d == 0 else (row >= col)).astype(f32)
        g = g_ref[0] + bias_ref[...]
        gt = gt_ref[0] + biast_ref[...]
        ic = g[:, 4 * d:4 * d + 4]
        it = gt[4 * d:4 * d + 4, :]
        bc = jnp.dot(tri, _log_sigmoid(g[:, 8 + 4 * d:12 + 4 * d]), precision=_HI, preferred_element_type=f32)
        bt = jnp.dot(_log_sigmoid(gt[8 + 4 * d:12 + 4 * d, :]), tri_t, precision=_HI, preferred_element_type=f32)
        for h in range(A_HEADS):
            probs.append(dict(
                r=d * A_HEADS + h, h=h, mask=mask, last=ln - 1 if d == 0 else 0, h_ref=h_ref,
                q=q_ref[0, :, h * A_DK:(h + 1) * A_DK], k=k_ref[0, :, h * A_DK:(h + 1) * A_DK],
                vaug=jnp.concatenate([v_ref[0, :, h * A_DV:(h + 1) * A_DV], ones_blk], axis=1),
                bcol=bc[:, h:h + 1], icol=ic[:, h:h + 1], brow=bt[h:h + 1, :], irow=it[h:h + 1, :]))
    for p in probs:
        p["m_old"] = m_ref[p["r"]:p["r"] + 1, 0:1]
        p["caug"] = c_ref[p["r"]]
        p["qk"] = lax.dot_general(p["q"], p["k"], (((1,), (1,)), ((), ())), preferred_element_type=f32)
    for p in probs:
        p["qc"] = jnp.dot(p["q"], p["caug"].astype(bf16), preferred_element_type=f32)
    for p in probs:
        dmat = jnp.where(p["mask"], p["bcol"] - p["brow"] + p["irow"], -jnp.inf)
        inter = p["bcol"] + p["m_old"]
        mt = jnp.maximum(inter, jnp.max(dmat, axis=-1, keepdims=True))
        p["mt"], p["w_int"] = mt, jnp.exp(inter - mt)
        p["sc"] = (jnp.exp(dmat - mt) * p["qk"]).astype(bf16)
    for p in probs:
        tot = p["w_int"] * p["qc"] + jnp.dot(p["sc"], p["vaug"], preferred_element_type=f32)
        den = jnp.maximum(jnp.abs(tot[:, A_DV:]), jnp.exp(-p["mt"]))
        p["h_ref"][0, :, p["h"] * A_DV:(p["h"] + 1) * A_DV] = tot[:, :A_DV] / den
    for p in probs:
        last, bcol, brow = p["last"], p["bcol"], p["brow"]
        btot_c = bcol[last:last + 1, :]
        btot_r = brow[:, last:last + 1]
        m_new = jnp.maximum(btot_r + p["m_old"], jnp.max(btot_r - brow + p["irow"], axis=-1, keepdims=True))
        w_old = jnp.exp(btot_r + p["m_old"] - m_new)
        kw = (p["k"].astype(f32) * jnp.exp(btot_c - bcol + p["icol"] - m_new)).astype(bf16)
        c_ref[p["r"]] = w_old * p["caug"] + lax.dot_general(kw, p["vaug"], (((0,), (0,)), ((), ())), preferred_element_type=f32)
        m_ref[p["r"]:p["r"] + 1, :] = jnp.broadcast_to(m_new, (1, m_ref.shape[1]))


def _mlstm(q, k, v, g, gt, bias):
    bsz, s, _ = q.shape
    ln = min(L_MLSTM, s)
    nc = s // ln
    hk, hv = A_HEADS * A_DK, A_HEADS * A_DV
    fwd = lambda b, n: (b, n, 0)
    bwd = lambda b, n: (b, nc - 1 - n, 0)
    fwd_t = lambda b, n: (b, 0, n)
    bwd_t = lambda b, n: (b, 0, nc - 1 - n)
    const = lambda b, n: (0, 0)
    def specs(im, im_t):
        return [pl.BlockSpec((1, ln, hk), im), pl.BlockSpec((1, ln, hk), im), pl.BlockSpec((1, ln, hv), im),
                pl.BlockSpec((1, ln, 16), im), pl.BlockSpec((1, 16, ln), im_t)]
    return pl.pallas_call(
        _mlstm_kernel,
        grid=(bsz, nc),
        in_specs=specs(fwd, fwd_t) + specs(bwd, bwd_t) + [pl.BlockSpec((1, 16), const), pl.BlockSpec((16, 1), const)],
        out_specs=[pl.BlockSpec((1, ln, hv), fwd), pl.BlockSpec((1, ln, hv), bwd)],
        out_shape=[jax.ShapeDtypeStruct((bsz, s, hv), jnp.float32)] * 2,
        scratch_shapes=[pltpu.VMEM((2 * A_HEADS, A_DK, 2 * A_DV), jnp.float32), pltpu.VMEM((2 * A_HEADS, 128), jnp.float32)],
        compiler_params=pltpu.CompilerParams(dimension_semantics=("parallel", "arbitrary"), vmem_limit_bytes=VMEM_LIMIT),
        name="mlstm",
    )(q, k, v, g, gt, q, k, v, g, gt, bias.reshape(1, 16), bias.reshape(16, 1))


T_GDN = 256
T_GDN_STEP = 256
C_GDN = 64
HALO = 8


def _softplus(t):
    return jnp.maximum(t, 0.0) + jnp.log1p(jnp.exp(-jnp.abs(t)))


def _gdn_prep_kernel(x_ref, xp_ref, xn_ref, g_ref, gt_ref, w_ref, a_ref, at_ref, dt_ref, dtt_ref,
                     q_ref, k_ref, v_ref, gc_ref, gr_ref, xe_ref):
    i = pl.program_id(1)
    nt = pl.num_programs(1)
    t = x_ref.shape[1]
    f32 = jnp.float32
    hd = B_HEADS * B_DK
    xe_ref[0:HALO, :] = jnp.where(i > 0, xp_ref[0], 0.0)
    xe_ref[HALO:HALO + t, :] = x_ref[0]
    xe_ref[HALO + t:, :] = jnp.where(i < nt - 1, xn_ref[0], 0.0)
    half = B_CONV // 2
    for part, o_ref in enumerate((q_ref, k_ref, v_ref)):
        cs = slice(part * hd, (part + 1) * hd)
        acc = None
        for j in range(B_CONV):
            term = xe_ref[HALO - half + j:HALO - half + j + t, cs] * w_ref[j:j + 1, cs]
            acc = term if acc is None else acc + term
        y = acc * jax.nn.sigmoid(acc)
        for h in range(B_HEADS):
            yh = y[:, h * B_DK:(h + 1) * B_DK]
            if part == 0:
                yh = yh * lax.rsqrt(jnp.sum(yh * yh, axis=-1, keepdims=True) + EPS) * (B_DK ** -0.5)
            elif part == 1:
                yh = yh * lax.rsqrt(jnp.sum(yh * yh, axis=-1, keepdims=True) + EPS)
            o_ref[0, :, h * B_DK:(h + 1) * B_DK] = yh.astype(o_ref.dtype)
    row = lax.broadcasted_iota(jnp.int32, (t, t), 0)
    col = lax.broadcasted_iota(jnp.int32, (t, t), 1)
    same = (row // C_GDN) == (col // C_GDN)
    lower = (same & (row >= col)).astype(f32)
    upper = (same & (row <= col)).astype(f32)
    g = g_ref[0]
    gt = gt_ref[0]
    nh = B_HEADS
    dec = -jnp.exp(a_ref[...]) * _softplus(g[:, 2 * nh:] + dt_ref[...])
    dect = -jnp.exp(at_ref[...]) * _softplus(gt[2 * nh:, :] + dtt_ref[...])
    gc_ref[0, :, 0:2 * nh] = jax.nn.sigmoid(g[:, 0:2 * nh])
    gc_ref[0, :, 2 * nh:3 * nh] = jnp.dot(lower, dec[:, 0:nh], precision=_HI, preferred_element_type=f32)
    gc_ref[0, :, 3 * nh:] = jnp.dot(upper, dec[:, nh:], precision=_HI, preferred_element_type=f32)
    gr_ref[0, 0:2 * nh, :] = jax.nn.sigmoid(gt[0:2 * nh, :])
    gr_ref[0, 2 * nh:3 * nh, :] = jnp.dot(dect[0:nh, :], upper, precision=_HI, preferred_element_type=f32)
    gr_ref[0, 3 * nh:, :] = jnp.dot(dect[nh:, :], lower, precision=_HI, preferred_element_type=f32)


def _gdn_prep(dqkv, g, gt, conv_w, a_log, dt_bias):
    bsz, s, n3 = dqkv.shape
    t = min(T_GDN, s)
    nt = s // t
    hd = B_HEADS * B_DK
    hb = t // HALO
    cur = lambda b, i: (b, i, 0)
    const = lambda b, i: (0, 0)
    bf16 = jnp.bfloat16
    return pl.pallas_call(
        _gdn_prep_kernel,
        grid=(bsz, nt),
        in_specs=[
            pl.BlockSpec((1, t, n3), cur),
            pl.BlockSpec((1, HALO, n3), lambda b, i: (b, jnp.maximum(i * hb - 1, 0), 0)),
            pl.BlockSpec((1, HALO, n3), lambda b, i: (b, jnp.minimum((i + 1) * hb, s // HALO - 1), 0)),
            pl.BlockSpec((1, t, 16), cur),
            pl.BlockSpec((1, 16, t), lambda b, i: (b, 0, i)),
            pl.BlockSpec((B_CONV, n3), const),
            pl.BlockSpec((1, 8), const), pl.BlockSpec((8, 1), const),
            pl.BlockSpec((1, 8), const), pl.BlockSpec((8, 1), const),
        ],
        out_specs=[pl.BlockSpec((1, t, hd), cur)] * 3 + [pl.BlockSpec((1, t, 16), cur), pl.BlockSpec((1, 16, t), lambda b, i: (b, 0, i))],
        out_shape=[jax.ShapeDtypeStruct((bsz, s, hd), bf16)] * 3 + [jax.ShapeDtypeStruct((bsz, s, 16), jnp.float32), jax.ShapeDtypeStruct((bsz, 16, s), jnp.float32)],
        scratch_shapes=[pltpu.VMEM((t + 2 * HALO, n3), jnp.float32)],
        compiler_params=pltpu.CompilerParams(dimension_semantics=("parallel", "parallel"), vmem_limit_bytes=VMEM_LIMIT),
        name="gdn_prep",
    )(dqkv, dqkv, dqkv, g, gt, conv_w, a_log.reshape(1, 8), a_log.reshape(8, 1), dt_bias.reshape(1, 8), dt_bias.reshape(8, 1))


def _tri_inverse_many(a_list, masks):
    eye, m16, m32, m64 = masks
    f32, bf16 = jnp.float32, jnp.bfloat16
    mm = lambda x, y: jnp.dot(x.astype(bf16), y.astype(bf16), preferred_element_type=f32)
    ads = [jnp.where(m16, a, 0.0) for a in a_list]
    xs = [eye - ad for ad in ads]
    ps = [mm(ad, ad) for ad in ads]
    for stage in range(3):
        xs = [x + mm(x, p) for x, p in zip(xs, ps)]
        if stage < 2:
            ps = [mm(p, p) for p in ps]
    for lo, hi in ((m16, m32), (m32, m64)):
        off = hi & ~lo
        ys = [mm(jnp.where(off, a, 0.0), x) for a, x in zip(a_list, xs)]
        xs = [x - mm(x, y) for x, y in zip(xs, ys)]
    return xs


def _gdn_kernel(qf_ref, kf_ref, vf_ref, gcf_ref, grf_ref, qb_ref, kb_ref, vb_ref, gcb_ref, grb_ref, of_ref, ob_ref, s_ref):
    n = pl.program_id(1)
    t = qf_ref.shape[1]
    c = C_GDN
    f32, bf16 = jnp.float32, jnp.bfloat16

    @pl.when(n == 0)
    def _():
        s_ref[...] = jnp.zeros_like(s_ref)

    row = lax.broadcasted_iota(jnp.int32, (c, c), 0)
    col = lax.broadcasted_iota(jnp.int32, (c, c), 1)
    eye = (row == col).astype(f32)
    blk = lambda w: (row // w) == (col // w)
    masks = (eye, blk(16), blk(32), blk(64))
    nh, nchunk = B_HEADS, t // c
    dir_refs = ((qf_ref, kf_ref, vf_ref, gcf_ref, grf_ref, of_ref), (qb_ref, kb_ref, vb_ref, gcb_ref, grb_ref, ob_ref))
    probs = [(d, h, ci) for d in range(2) for h in range(nh) for ci in range(nchunk)]
    xpose = (((1,), (1,)), ((), ()))

    def load(d, h, ci):
        q_ref, k_ref, v_ref, gc_ref, gr_ref, _ = dir_refs[d]
        rs, cs = slice(ci * c, (ci + 1) * c), slice(h * B_DK, (h + 1) * B_DK)
        beta = gc_ref[0, rs, d * nh + h:d * nh + h + 1]
        gcol = gc_ref[0, rs, (2 + d) * nh + h:(2 + d) * nh + h + 1]
        grow = gr_ref[0, (2 + d) * nh + h:(2 + d) * nh + h + 1, rs]
        return q_ref[0, rs, cs], k_ref[0, rs, cs], v_ref[0, rs, cs], beta, gcol, grow

    data = [load(*p) for p in probs]
    gams = []
    for (d, _, _), (_, _, _, _, gcol, grow) in zip(probs, data):
        incl = (row >= col) if d == 0 else (row <= col)
        gams.append(jnp.exp(jnp.where(incl, gcol - grow, -jnp.inf)))
    kks = [lax.dot_general(k, k, xpose, preferred_element_type=f32) for (_, k, _, _, _, _) in data]
    qks = [lax.dot_general(q, k, xpose, preferred_element_type=f32) for (q, k, _, _, _, _) in data]
    a_list = []
    for (d, _, _), (_, _, _, beta, _, _), kk, gam in zip(probs, data, kks, gams):
        strict = (row > col) if d == 0 else (row < col)
        a_list.append(jnp.where(strict, beta * kk * gam, 0.0))
    tinvs = _tri_inverse_many(a_list, masks)
    egcs = [jnp.exp(gcol) for (_, _, _, _, gcol, _) in data]
    uws = []
    for (q, k, v, beta, gcol, _), tinv, egc in zip(data, tinvs, egcs):
        rhs = jnp.concatenate([beta * v.astype(f32), (beta * egc) * k.astype(f32)], axis=1).astype(bf16)
        uws.append(jnp.dot(tinv.astype(bf16), rhs, preferred_element_type=f32))
    attns = [(qk * gam).astype(bf16) for qk, gam in zip(qks, gams)]
    index = {p: i for i, p in enumerate(probs)}
    chains = [(d, h) for d in range(2) for h in range(nh)]
    states = [s_ref[d * nh + h] for d, h in chains]
    for step in range(nchunk):
        ids = [index[(d, h, step if d == 0 else nchunk - 1 - step)] for d, h in chains]
        wss = []
        for i, state in zip(ids, states):
            q, _, _, _, _, _ = data[i]
            wq = jnp.concatenate([uws[i][:, B_DV:], q.astype(f32) * egcs[i]], axis=0).astype(bf16)
            wss.append(jnp.dot(wq, state.astype(bf16), preferred_element_type=f32))
        v_news = [(uws[i][:, :B_DV] - ws[:c]).astype(bf16) for i, ws in zip(ids, wss)]
        for (d, h), i, ws, v_new in zip(chains, ids, wss, v_news):
            ci = probs[i][2]
            dir_refs[d][5][0, ci * c:(ci + 1) * c, h * B_DV:(h + 1) * B_DV] = ws[c:] + jnp.dot(attns[i], v_new, preferred_element_type=f32)
        new_states = []
        for (d, h), i, state, v_new in zip(chains, ids, states, v_news):
            _, k, _, _, gcol, _ = data[i]
            last = c - 1 if d == 0 else 0
            gl = gcol[last:last + 1, :]
            kd = (k.astype(f32) * jnp.exp(gl - gcol)).astype(bf16)
            new_states.append(jnp.exp(gl) * state + lax.dot_general(kd, v_new, (((0,), (0,)), ((), ())), preferred_element_type=f32))
        states = new_states
    for (d, h), state in zip(chains, states):
        s_ref[d * nh + h] = state


def _gdn(q, k, v, gc, gr):
    bsz, s, hd = q.shape
    t = min(T_GDN_STEP, s)
    nb = s // t
    fwd = lambda b, n: (b, n, 0)
    bwd = lambda b, n: (b, nb - 1 - n, 0)
    def specs(im, im_t):
        return [pl.BlockSpec((1, t, hd), im)] * 3 + [pl.BlockSpec((1, t, 16), im), pl.BlockSpec((1, 16, t), im_t)]
    return pl.pallas_call(
        _gdn_kernel,
        grid=(bsz, nb),
        in_specs=specs(fwd, lambda b, n: (b, 0, n)) + specs(bwd, lambda b, n: (b, 0, nb - 1 - n)),
        out_specs=[pl.BlockSpec((1, t, hd), fwd), pl.BlockSpec((1, t, hd), bwd)],
        out_shape=[jax.ShapeDtypeStruct((bsz, s, hd), jnp.float32)] * 2,
        scratch_shapes=[pltpu.VMEM((2 * B_HEADS, B_DK, B_DV), jnp.float32)],
        compiler_params=pltpu.CompilerParams(dimension_semantics=("parallel", "arbitrary"), vmem_limit_bytes=VMEM_LIMIT),
        name="gdn",
    )(q, k, v, gc, gr, q, k, v, gc, gr)


def _split(p, sizes):
    return jnp.split(p, np.cumsum(sizes)[:-1].tolist(), axis=-1)


def _layernorm(x, g, b):
    xc = x - jnp.mean(x, axis=-1, keepdims=True)
    y = xc * lax.rsqrt(jnp.mean(xc * xc, axis=-1, keepdims=True) + EPS)
    return y * g + b


def _head_rms(t, g):
    bsz, s, h, d = t.shape
    y = t * lax.rsqrt(jnp.mean(t * t, axis=-1, keepdims=True) + EPS)
    return y.reshape(bsz, s, h * d) * g


def _l2n(t):
    return t * lax.rsqrt(jnp.sum(t * t, axis=-1, keepdims=True) + EPS)


def _dwconv(x, w):
    return lax.conv_general_dilated(x, w[:, None, :].astype(x.dtype), window_strides=(1,), padding='SAME', dimension_numbers=('NWC', 'WIO', 'NWC'), feature_group_count=x.shape[-1])


def _flip(t):
    return jnp.flip(t, axis=1)


def _to_chunks(t):
    bsz, s, h = t.shape[:3]
    t = t.reshape((bsz, s // CHUNK, CHUNK, h) + t.shape[3:])
    return jnp.moveaxis(t, (1, 3), (0, 2))


def _from_chunks(t):
    nc, bsz, h, l = t.shape[:4]
    t = jnp.moveaxis(t, (0, 2), (1, 3))
    return t.reshape((bsz, nc * l, h) + t.shape[4:])


def _mlstm_chunkwise(q, k, v, i_pre, logf):
    q, k, v, i_pre, logf = (_to_chunks(t) for t in (q, k, v, i_pre, logf))
    nc, bsz, h = q.shape[:3]
    causal = jnp.tril(jnp.ones((CHUNK, CHUNK), dtype=bool))
    b = jnp.cumsum(logf, axis=-1)
    dmat = jnp.where(causal, b[..., :, None] - b[..., None, :] + i_pre[..., None, :], -jnp.inf)
    dmax = jnp.max(dmat, axis=-1)
    qk = jnp.einsum('nbhld,nbhsd->nbhls', q, k)
    a_end = b[..., -1:] - b + i_pre

    def step(carry, xs):
        cmat, nvec, m = carry
        qc, kc, vc, bc, dc, dmc, qkc, aec = xs
        inter = bc + m[..., None]
        mt = jnp.maximum(inter, dmc)
        w_int = jnp.exp(inter - mt)
        sc = jnp.exp(dc - mt[..., None]) * qkc
        num = w_int[..., None] * jnp.einsum('bhld,bhde->bhle', qc, cmat) + jnp.einsum('bhls,bhse->bhle', sc, vc)
        den = w_int * jnp.einsum('bhld,bhd->bhl', qc, nvec) + jnp.sum(sc, axis=-1)
        hc = num / jnp.maximum(jnp.abs(den), jnp.exp(-mt))[..., None]
        m_new = jnp.maximum(bc[..., -1] + m, jnp.max(aec, axis=-1))
        w_old = jnp.exp(bc[..., -1] + m - m_new)
        kw = kc * jnp.exp(aec - m_new[..., None])[..., None]
        cmat = w_old[..., None, None] * cmat + jnp.einsum('bhld,bhle->bhde', kw, vc)
        nvec = w_old[..., None] * nvec + jnp.sum(kw, axis=-2)
        return (cmat, nvec, m_new), hc

    init = (jnp.zeros((bsz, h, A_DK, A_DV), jnp.float32), jnp.zeros((bsz, h, A_DK), jnp.float32), jnp.zeros((bsz, h), jnp.float32))
    _, hs = lax.scan(step, init, (q, k, v, b, dmat, dmax, qk, a_end))
    return _from_chunks(hs)


def _gdn_chunked(q, k, v, beta, g):
    q, k, v, beta, g = (_to_chunks(t) for t in (q, k, v, beta, g))
    nc, bsz, h = q.shape[:3]
    tril = jnp.tril(jnp.ones((CHUNK, CHUNK), dtype=bool))
    strict = jnp.tril(jnp.ones((CHUNK, CHUNK), dtype=bool), -1)
    gc = jnp.cumsum(g, axis=-1)
    gam = jnp.exp(jnp.where(tril, gc[..., :, None] - gc[..., None, :], -jnp.inf))
    a = jnp.where(strict, beta[..., :, None] * jnp.einsum('nbhid,nbhjd->nbhij', k, k) * gam, 0.0)
    tmat = a + jnp.eye(CHUNK, dtype=a.dtype)
    u = lax.linalg.triangular_solve(tmat, beta[..., None] * v, left_side=True, lower=True, unit_diagonal=True)
    w = lax.linalg.triangular_solve(tmat, (beta * jnp.exp(gc))[..., None] * k, left_side=True, lower=True, unit_diagonal=True)
    attn = jnp.einsum('nbhid,nbhjd->nbhij', q, k) * gam

    def step(state, xs):
        qc, kc, uc, wc, gcc, ac = xs
        v_new = uc - jnp.einsum('bhld,bhde->bhle', wc, state)
        o = jnp.einsum('bhld,bhde->bhle', qc * jnp.exp(gcc)[..., None], state) + jnp.einsum('bhls,bhse->bhle', ac, v_new)
        gl = gcc[..., -1]
        state = jnp.exp(gl)[..., None, None] * state + jnp.einsum('bhld,bhle->bhde', kc * jnp.exp(gl[..., None] - gcc)[..., None], v_new)
        return state, o

    _, os_ = lax.scan(step, jnp.zeros((bsz, h, B_DK, B_DV), jnp.float32), (q, k, u, w, gc, attn))
    return _from_chunks(os_)


def _t5_bucket(rel):
    half = REL_BUCKETS // 2
    exact = half // 2
    n = jnp.abs(rel)
    large = exact + (jnp.log(jnp.maximum(n, 1).astype(jnp.float32) / exact) / math.log(REL_MAX_DIST / exact) * (half - exact)).astype(jnp.int32)
    large = jnp.minimum(large, half - 1)
    return (rel > 0).astype(jnp.int32) * half + jnp.where(n < exact, n, large)


def _dilated_group(q, k, v, dilation, radius, rel_bias):
    bsz, s, h, dh = q.shape
    ls = s // dilation
    nb = -(-ls // radius)
    lp = nb * radius

    def sub(t, lo, hi):
        t = t.reshape(bsz, ls, dilation, h, dh).transpose(0, 3, 2, 1, 4)
        return jnp.pad(t, ((0, 0), (0, 0), (0, 0), (lo, hi), (0, 0)))

    qb = sub(q, 0, lp - ls).reshape(bsz, h, dilation, nb, radius, dh)

    def band(t):
        t = sub(t, radius, lp - ls + radius).reshape(bsz, h, dilation, nb + 2, radius, dh)
        return jnp.concatenate([t[:, :, :, :-2], t[:, :, :, 1:-1], t[:, :, :, 2:]], axis=4)

    kb, vb = band(k), band(v)
    qi = jnp.arange(radius)[:, None]
    kj = jnp.arange(3 * radius)[None, :]
    rel = kj - radius - qi
    kpos = jnp.arange(nb)[:, None, None] * radius + kj - radius
    valid = (jnp.abs(rel) <= radius) & (kpos >= 0) & (kpos < ls)
    bias = jnp.transpose(rel_bias[_t5_bucket(rel * dilation)], (2, 0, 1)).astype(jnp.float32)
    sc = jnp.einsum('bhrnid,bhrnjd->bhrnij', qb, kb).astype(jnp.float32) * (dh ** -0.5) + bias[:, None, None]
    sc = jnp.where(valid, sc, NEG)
    m = jnp.max(sc, axis=-1, keepdims=True)
    p = jnp.exp(sc - m)
    den = jnp.sum(p, axis=-1)
    o = jnp.einsum('bhrnij,bhrnjd->bhrnid', p, vb.astype(jnp.float32)) / den[..., None]
    lse = m[..., 0] + jnp.log(den)
    o = o.reshape(bsz, h, dilation, lp, dh)[:, :, :, :ls].transpose(0, 3, 2, 1, 4).reshape(bsz, s, h, dh)
    lse = lse.reshape(bsz, h, dilation, lp)[:, :, :, :ls].transpose(0, 3, 2, 1).reshape(bsz, s, h)
    return o, lse


def _dilated_attention(q, k, v, rel_bias):
    outs, lses = [], []
    for window, dilation in D_GROUPS:
        o, l = _dilated_group(q, k, v, dilation, window // (2 * dilation), rel_bias)
        outs.append(o)
        lses.append(l)
    wts = jax.nn.softmax(jnp.stack(lses, axis=0), axis=0)
    return jnp.sum(wts[..., None] * jnp.stack(outs, axis=0), axis=0)


def _even_mixer_core(p, m_gate_b, dn_dt_bias, dn_a_log, dn_conv_w, m_norm_g, dn_norm_g):
    bsz, s, _ = p.shape
    f32 = jnp.float32
    mq, mk, mv, mo, mg, dqkv, dg, z = _split(p, EVEN_SPLITS)
    q = mq.reshape(bsz, s, A_HEADS, A_DK)
    k = mk.reshape(bsz, s, A_HEADS, A_DK) * (A_DK ** -0.5)
    v = mv.reshape(bsz, s, A_HEADS, A_DV)
    gt = mg.reshape(bsz, s, 4, A_HEADS) + m_gate_b
    logf = jax.nn.log_sigmoid(gt[:, :, 2:4])
    h_fwd = _mlstm_chunkwise(q, k, v, gt[:, :, 0], logf[:, :, 0])
    h_bwd = _flip(_mlstm_chunkwise(_flip(q), _flip(k), _flip(v), _flip(gt[:, :, 1]), _flip(logf[:, :, 1])))
    out_a = jax.nn.sigmoid(mo) * _head_rms(h_fwd + h_bwd, m_norm_g)
    qkv = jax.nn.silu(_dwconv(dqkv, dn_conv_w))
    bq, bk, bv = _split(qkv, (B_HEADS * B_DK, B_HEADS * B_DK, B_HEADS * B_DV))
    q = _l2n(bq.reshape(bsz, s, B_HEADS, B_DK)) * (B_DK ** -0.5)
    k = _l2n(bk.reshape(bsz, s, B_HEADS, B_DK))
    v = bv.reshape(bsz, s, B_HEADS, B_DV)
    gb = dg.reshape(bsz, s, 4, B_HEADS)
    beta = jax.nn.sigmoid(gb[:, :, 0:2])
    decay = -jnp.exp(dn_a_log) * jax.nn.softplus(gb[:, :, 2:4] + dn_dt_bias)
    o_fwd = _gdn_chunked(q, k, v, beta[:, :, 0], decay[:, :, 0])
    o_bwd = _flip(_gdn_chunked(_flip(q), _flip(k), _flip(v), _flip(beta[:, :, 1]), _flip(decay[:, :, 1])))
    out_b = _head_rms(o_fwd + o_bwd, dn_norm_g)
    return jnp.concatenate([out_a, out_b], axis=-1), z


def _odd_mixer_core(p, dw_w, dw_b, ln_g, ln_b, rel_bias):
    bsz, s, _ = p.shape
    ga, gb, aq, ak, av, z = _split(p, ODD_SPLITS)
    u = _dwconv(ga * jax.nn.sigmoid(gb), dw_w) + dw_b
    out_c = jax.nn.silu(_layernorm(u, ln_g, ln_b))
    shp = (bsz, s, D_HEADS, D_DH)
    out_d = _dilated_attention(aq.reshape(shp), ak.reshape(shp), av.reshape(shp), rel_bias).reshape(bsz, s, D_HEADS * D_DH)
    return jnp.concatenate([out_c, out_d], axis=-1), z


def kernel(x, c, norm_g, ada_w, ada_b, ev_w_in, ev_m_gate_b, ev_dn_dt_bias, ev_dn_a_log, ev_dn_conv_w, ev_m_norm_g, ev_dn_norm_g, ev_w_out, od_w_in, od_dw_w, od_dw_b, od_ln_g, od_ln_b, od_w_out, rel_bias, final_g):
    assert DEPTH == 2, "the final RMSNorm is fused into the (last) odd layer's output projection"
    assert all(window // (2 * dil) == R_ATT for window, dil in D_GROUPS)
    d = x.shape[-1]
    mod = _adaln(c, ada_w, ada_b)
    for layer in range(DEPTH):
        shift, scale, gate = (mod[layer, :, i * d:(i + 1) * d][:, None, :] for i in range(3))
        j = layer // 2
        if layer % 2 == 0:
            x = _even_layer(x, norm_g[layer], scale, shift, gate, ev_w_in[j], ev_m_gate_b[j], ev_dn_dt_bias[j], ev_dn_a_log[j],
                            ev_dn_conv_w[j], ev_m_norm_g[j], ev_dn_norm_g[j], ev_w_out[j])
        else:
            x = _odd_layer_final(x, norm_g[layer], scale, shift, gate, od_w_in[j], od_dw_w[j], od_dw_b[j], od_ln_g[j], od_ln_b[j],
                                 rel_bias, od_w_out[j], final_g)
    return x


def _even_layer(x, norm_g, scale, shift, gate, w_in, m_gate_b, dn_dt_bias, dn_a_log, dn_conv_w, m_norm_g, dn_norm_g, w_out):
    bf16 = jnp.bfloat16
    mq, mk, mv, mo, mg, dqkv, dg, z = _split(w_in, EVEN_SPLITS)
    w = jnp.concatenate([mq, mk * (A_DK ** -0.5), mv, dqkv], axis=1).astype(bf16)
    wg = jnp.concatenate([mg, dg], axis=1).astype(bf16)
    wz = jnp.concatenate([mo, z], axis=1).astype(bf16)
    pq, pk, pv, pdqkv, g_m, g_d, gt_m, gt_d = _inproj_even(x, norm_g, scale, shift, w, wg, wg.T)
    hf, hb = _mlstm(pq, pk, pv, g_m, gt_m, m_gate_b.reshape(16))
    bq, bk, bv, gc, gr = _gdn_prep(pdqkv, g_d, gt_d, dn_conv_w, dn_a_log, dn_dt_bias)
    of, ob = _gdn(bq, bk, bv, gc, gr)
    return _outproj_even(hf, hb, of, ob, x, norm_g, scale, shift, gate, m_norm_g, dn_norm_g, wz, w_out.astype(bf16))


def _odd_layer_final(x, norm_g, scale, shift, gate, w_in, dw_w, dw_b, ln_g, ln_b, rel_bias, w_out, final_g):
    bf16 = jnp.bfloat16
    ga, gb, aq, ak, av, z = _split(w_in, ODD_SPLITS)
    w = jnp.concatenate([ga, gb, aq * (D_DH ** -0.5), ak, av], axis=1).astype(bf16)
    glu, pq, pk, pv = _inproj_odd(x, norm_g, scale, shift, w)
    out_c = _conformer(glu, dw_w, dw_b, ln_g, ln_b)
    og, lg = zip(*[_dilated_group_call(qd, kd, vd, rel_bias, dil) for qd, kd, vd, dil in zip(pq, pk, pv, DILATIONS)])
    return _outproj_odd_final(out_c, og, lg, x, norm_g, scale, shift, gate, final_g, z.astype(bf16), w_out.astype(bf16))
```

```python
import math
from functools import partial

import jax
import jax.numpy as jnp
import numpy as np
from jax import lax
from jax.experimental import pallas as pl
from jax.experimental.pallas import tpu as pltpu

D_MODEL = 1024
BATCH = 4
SEQ = 8192
DEPTH = 2
A_HEADS = 4
A_DK = 64
A_DV = 128
B_HEADS = 4
B_DK = 128
B_DV = 128
B_CONV = 5
C_WIDTH = 512
C_CONV = 31
D_HEADS = 8
D_DH = 64
D_GROUPS = ((128, 1), (512, 4), (2048, 16))
REL_BUCKETS = 32
REL_MAX_DIST = 1024
CHUNK = 64
EPS = 1e-6
NEG = -1e30
MIX_EVEN = A_HEADS * A_DV + B_HEADS * B_DV
MIX_ODD = C_WIDTH + D_HEADS * D_DH
B_QKV = B_HEADS * (2 * B_DK + B_DV)
EVEN_SPLITS = (A_HEADS * A_DK, A_HEADS * A_DK, A_HEADS * A_DV, A_HEADS * A_DV, 4 * A_HEADS, B_QKV, 4 * B_HEADS, MIX_EVEN)
ODD_SPLITS = (C_WIDTH, C_WIDTH, D_HEADS * D_DH, D_HEADS * D_DH, D_HEADS * D_DH, MIX_ODD)

VMEM_LIMIT = 56 * 1024 * 1024
TM_PROJ = 256


def _adaln_kernel(c_ref, w_ref, b_ref, o_ref):
    c = c_ref[...]
    cs = (c * jax.nn.sigmoid(c)).astype(jnp.bfloat16)
    o_ref[0] = jnp.dot(cs, w_ref[0].astype(jnp.bfloat16), preferred_element_type=jnp.float32) + b_ref[0]


def _adaln(c, ada_w, ada_b):
    depth, d, n3 = ada_w.shape
    bsz = c.shape[0]
    tn = 1024
    return pl.pallas_call(
        _adaln_kernel,
        grid=(depth, n3 // tn),
        in_specs=[pl.BlockSpec((bsz, d), lambda l, j: (0, 0)), pl.BlockSpec((1, d, tn), lambda l, j: (l, 0, j)),
                  pl.BlockSpec((1, 1, tn), lambda l, j: (l, 0, j))],
        out_specs=pl.BlockSpec((1, bsz, tn), lambda l, j: (l, 0, j)),
        out_shape=jax.ShapeDtypeStruct((depth, bsz, n3), jnp.float32),
        compiler_params=pltpu.CompilerParams(dimension_semantics=("parallel", "parallel")),
        name="adaln",
    )(c, ada_w, ada_b.reshape(depth, 1, n3))


def _modulated_rms(x_ref, g_ref, sc_ref, sh_ref):
    x = x_ref[0]
    y = x * lax.rsqrt(jnp.mean(x * x, axis=-1, keepdims=True) + EPS)
    return ((y * g_ref[...]) * (1.0 + sc_ref[0]) + sh_ref[0]).astype(jnp.bfloat16)


_EV_COLS = {"mq": (0, 256), "mk": (256, 512), "mv": (512, 1024), "dqkv": (1024, 2560)}
_OD_COLS = {"ga": (0, 512), "gb": (512, 1024), "aq": (1024, 1536), "ak": (1536, 2048), "av": (2048, 2560)}


def _inproj_even_kernel(x_ref, g_ref, sc_ref, sh_ref, w_ref, wg_ref, wgt_ref,
                        mq_ref, mk_ref, mv_ref, dqkv_ref, mg_ref, dg_ref, mgt_ref, dgt_ref):
    f32 = jnp.float32
    h = _modulated_rms(x_ref, g_ref, sc_ref, sh_ref)
    for name, o_ref in (("mq", mq_ref), ("mk", mk_ref), ("mv", mv_ref), ("dqkv", dqkv_ref)):
        lo, hi = _EV_COLS[name]
        o_ref[0] = jnp.dot(h, w_ref[:, lo:hi], preferred_element_type=f32).astype(o_ref.dtype)
    gates = jnp.dot(h, wg_ref[...], preferred_element_type=f32)
    gates_t = lax.dot_general(wgt_ref[...], h, (((1,), (1,)), ((), ())), preferred_element_type=f32)
    mg_ref[0] = gates[:, :16]
    dg_ref[0] = gates[:, 16:]
    mgt_ref[0] = gates_t[:16]
    dgt_ref[0] = gates_t[16:]


def _inproj_even(x, g, scale, shift, w, wg, wgt):
    bsz, s, d = x.shape
    tm = TM_PROJ
    tok = lambda b, i: (b, i, 0)
    tok_t = lambda b, i: (b, 0, i)
    const = lambda b, i: (0, 0)
    bvec = lambda b, i: (b, 0, 0)
    bf16, f32 = jnp.bfloat16, jnp.float32
    outs = [("mq", bf16), ("mk", bf16), ("mv", bf16), ("dqkv", f32)]
    widths = [_EV_COLS[n][1] - _EV_COLS[n][0] for n, _ in outs]
    return pl.pallas_call(
        _inproj_even_kernel,
        grid=(bsz, s // tm),
        in_specs=[pl.BlockSpec((1, tm, d), tok), pl.BlockSpec((1, d), const), pl.BlockSpec((1, 1, d), bvec), pl.BlockSpec((1, 1, d), bvec),
                  pl.BlockSpec(w.shape, const), pl.BlockSpec(wg.shape, const), pl.BlockSpec(wgt.shape, const)],
        out_specs=[pl.BlockSpec((1, tm, wd), tok) for wd in widths] + [pl.BlockSpec((1, tm, 16), tok)] * 2 + [pl.BlockSpec((1, 16, tm), tok_t)] * 2,
        out_shape=[jax.ShapeDtypeStruct((bsz, s, wd), dt) for wd, (_, dt) in zip(widths, outs)]
        + [jax.ShapeDtypeStruct((bsz, s, 16), f32)] * 2 + [jax.ShapeDtypeStruct((bsz, 16, s), f32)] * 2,
        compiler_params=pltpu.CompilerParams(dimension_semantics=("parallel", "parallel"), vmem_limit_bytes=VMEM_LIMIT),
        name="inproj_even",
    )(x, g.reshape(1, d), scale, shift, w, wg, wgt)


DILATIONS = tuple(dil for _, dil in D_GROUPS)
LANES = 128
SUBLANES = 8


def _inproj_odd_kernel(x_ref, g_ref, sc_ref, sh_ref, w_ref, glu_ref, *rest):
    out_refs, plane_ref = rest[:-1], rest[-1]
    f32 = jnp.float32
    tm = x_ref.shape[1]
    nd = D_HEADS * D_DH
    h = _modulated_rms(x_ref, g_ref, sc_ref, sh_ref)
    dot = lambda name: jnp.dot(h, w_ref[:, _OD_COLS[name][0]:_OD_COLS[name][1]], preferred_element_type=f32)
    glu_ref[0] = dot("ga") * jax.nn.sigmoid(dot("gb"))
    for a, name in enumerate(("aq", "ak", "av")):
        r = dot(name)
        group_refs = out_refs[a * len(DILATIONS):(a + 1) * len(DILATIONS)]
        for j in range(nd // LANES):
            plane_ref[a, j] = r[:, j * LANES:(j + 1) * LANES]
        for dil, o_ref in zip(DILATIONS, group_refs):
            if dil == 1:
                o_ref[0, 0] = r.astype(o_ref.dtype)
                continue
            for res in range(dil):
                for j in range(nd // LANES):
                    o_ref[0, res, :, j * LANES:(j + 1) * LANES] = plane_ref[a, j, pl.ds(res, tm // dil, stride=dil), :].astype(o_ref.dtype)


def _inproj_odd(x, g, scale, shift, w):
    bsz, s, d = x.shape
    tm = TM_PROJ
    tok = lambda b, i: (b, i, 0)
    const = lambda b, i: (0, 0)
    bvec = lambda b, i: (b, 0, 0)
    bf16, f32 = jnp.bfloat16, jnp.float32
    nd = D_HEADS * D_DH
    att_specs = [pl.BlockSpec((1, dil, tm // dil, nd), lambda b, i: (b, 0, i, 0)) for dil in DILATIONS] * 3
    att_shapes = [jax.ShapeDtypeStruct((bsz, dil, s // dil, nd), bf16) for dil in DILATIONS] * 3
    outs = pl.pallas_call(
        _inproj_odd_kernel,
        grid=(bsz, s // tm),
        in_specs=[pl.BlockSpec((1, tm, d), tok), pl.BlockSpec((1, d), const), pl.BlockSpec((1, 1, d), bvec), pl.BlockSpec((1, 1, d), bvec),
                  pl.BlockSpec(w.shape, const)],
        out_specs=[pl.BlockSpec((1, tm, C_WIDTH), tok)] + att_specs,
        out_shape=[jax.ShapeDtypeStruct((bsz, s, C_WIDTH), f32)] + att_shapes,
        scratch_shapes=[pltpu.VMEM((3, nd // LANES, tm, LANES), f32)],
        compiler_params=pltpu.CompilerParams(dimension_semantics=("parallel", "parallel"), vmem_limit_bytes=VMEM_LIMIT),
        name="inproj_odd",
    )(x, g.reshape(1, d), scale, shift, w)
    ng = len(DILATIONS)
    return outs[0], outs[1:1 + ng], outs[1 + ng:1 + 2 * ng], outs[1 + 2 * ng:]


def _head_rms_cols(t, g, width):
    parts = []
    for h in range(t.shape[1] // width):
        th = t[:, h * width:(h + 1) * width]
        parts.append(th * lax.rsqrt(jnp.mean(th * th, axis=-1, keepdims=True) + EPS))
    return jnp.concatenate(parts, axis=1) * g


def _outproj_even_kernel(hf_ref, hb_ref, of_ref, ob_ref, x_ref, g_ref, sc_ref, sh_ref, gate_ref, mg_ref, dg_ref, wz_ref, w_ref, o_ref):
    f32, bf16 = jnp.float32, jnp.bfloat16
    na = A_HEADS * A_DV
    h = _modulated_rms(x_ref, g_ref, sc_ref, sh_ref)
    mo = jnp.dot(h, wz_ref[:, :na], preferred_element_type=f32)
    z = jnp.dot(h, wz_ref[:, na:], preferred_element_type=f32)
    sz = z * jax.nn.sigmoid(z)
    out_a = jax.nn.sigmoid(mo) * _head_rms_cols(hf_ref[0] + hb_ref[0], mg_ref[...], A_DV)
    out_b = _head_rms_cols(of_ref[0] + ob_ref[0], dg_ref[...], B_DV)
    y = jnp.dot((out_a * sz[:, :na]).astype(bf16), w_ref[:na, :], preferred_element_type=f32)
    y = y + jnp.dot((out_b * sz[:, na:]).astype(bf16), w_ref[na:, :], preferred_element_type=f32)
    o_ref[0] = x_ref[0] + gate_ref[0] * y


def _outproj_even(hf, hb, of, ob, x, norm_g, scale, shift, gate, m_norm_g, dn_norm_g, wz, w):
    bsz, s, d = x.shape
    tm = TM_PROJ
    tok = lambda b, i: (b, i, 0)
    const = lambda b, i: (0, 0)
    bvec = lambda b, i: (b, 0, 0)
    na, nb = A_HEADS * A_DV, B_HEADS * B_DV
    return pl.pallas_call(
        _outproj_even_kernel,
        grid=(bsz, s // tm),
        in_specs=[pl.BlockSpec((1, tm, na), tok)] * 2 + [pl.BlockSpec((1, tm, nb), tok)] * 2 + [pl.BlockSpec((1, tm, d), tok),
                  pl.BlockSpec((1, d), const), pl.BlockSpec((1, 1, d), bvec), pl.BlockSpec((1, 1, d), bvec), pl.BlockSpec((1, 1, d), bvec),
                  pl.BlockSpec((1, na), const), pl.BlockSpec((1, nb), const), pl.BlockSpec(wz.shape, const), pl.BlockSpec(w.shape, const)],
        out_specs=pl.BlockSpec((1, tm, d), tok),
        out_shape=jax.ShapeDtypeStruct((bsz, s, d), jnp.float32),
        compiler_params=pltpu.CompilerParams(dimension_semantics=("parallel", "parallel"), vmem_limit_bytes=VMEM_LIMIT),
        name="outproj_even",
    )(hf, hb, of, ob, x, norm_g.reshape(1, d), scale, shift, gate, m_norm_g.reshape(1, na), dn_norm_g.reshape(1, nb), wz, w)


def _outproj_odd_kernel(oc_ref, o1_ref, o2_ref, o3_ref, l1_ref, l2_ref, l3_ref, x_ref, g_ref, sc_ref, sh_ref, gate_ref, fg_ref,
                        wz_ref, w_ref, o_ref, nat_ref):
    f32, bf16 = jnp.float32, jnp.bfloat16
    tm = x_ref.shape[1]
    npl = D_HEADS * D_DH // LANES
    z = jnp.dot(_modulated_rms(x_ref, g_ref, sc_ref, sh_ref), wz_ref[...], preferred_element_type=f32)
    sz = z * jax.nn.sigmoid(z)
    groups = []
    for gi, (dil, og_ref, lg_ref) in enumerate(zip(DILATIONS, (o1_ref, o2_ref, o3_ref), (l1_ref, l2_ref, l3_ref))):
        if dil == 1:
            groups.append(([og_ref[0, 0, :, j * LANES:(j + 1) * LANES] for j in range(npl)], lg_ref[0, 0]))
            continue
        for res in range(dil):
            rows = pl.ds(res, tm // dil, stride=dil)
            for j in range(npl):
                nat_ref[gi, j, rows, :] = og_ref[0, res, :, j * LANES:(j + 1) * LANES]
            nat_ref[gi, npl, rows, :] = lg_ref[0, res]
        groups.append(([nat_ref[gi, j] for j in range(npl)], nat_ref[gi, npl]))
    (p1, l1), (p2, l2), (p3, l3) = groups
    lm = jnp.maximum(jnp.maximum(l1, l2), l3)
    e1, e2, e3 = jnp.exp(l1 - lm), jnp.exp(l2 - lm), jnp.exp(l3 - lm)
    inv = 1.0 / (e1 + e2 + e3)
    low = lax.broadcasted_iota(jnp.int32, (tm, LANES), 1) < D_DH
    planes = []
    for j in range(npl):
        acc = None
        for e, p in ((e1, p1), (e2, p2), (e3, p3)):
            wgt = e * inv
            term = jnp.where(low, wgt[:, 2 * j:2 * j + 1], wgt[:, 2 * j + 1:2 * j + 2]) * p[j]
            acc = term if acc is None else acc + term
        planes.append(acc)
    out_d = jnp.concatenate(planes, axis=1)
    y = jnp.dot((oc_ref[0] * sz[:, :C_WIDTH]).astype(bf16), w_ref[:C_WIDTH, :], preferred_element_type=f32)
    y = y + jnp.dot((out_d * sz[:, C_WIDTH:]).astype(bf16), w_ref[C_WIDTH:, :], preferred_element_type=f32)
    xn = x_ref[0] + gate_ref[0] * y
    o_ref[0] = xn * lax.rsqrt(jnp.mean(xn * xn, axis=-1, keepdims=True) + EPS) * fg_ref[...]


def _outproj_odd_final(oc, og, lg, x, norm_g, scale, shift, gate, final_g, wz, w):
    bsz, s, d = x.shape
    tm = TM_PROJ
    tok = lambda b, i: (b, i, 0)
    const = lambda b, i: (0, 0)
    bvec = lambda b, i: (b, 0, 0)
    nd = D_HEADS * D_DH
    res_major = lambda width: [pl.BlockSpec((1, dil, tm // dil, width), lambda b, i: (b, 0, i, 0)) for dil in DILATIONS]
    return pl.pallas_call(
        _outproj_odd_kernel,
        grid=(bsz, s // tm),
        in_specs=[pl.BlockSpec((1, tm, C_WIDTH), tok)] + res_major(nd) + res_major(LANES)
        + [pl.BlockSpec((1, tm, d), tok), pl.BlockSpec((1, d), const), pl.BlockSpec((1, 1, d), bvec), pl.BlockSpec((1, 1, d), bvec),
           pl.BlockSpec((1, 1, d), bvec), pl.BlockSpec((1, d), const), pl.BlockSpec(wz.shape, const), pl.BlockSpec(w.shape, const)],
        out_specs=pl.BlockSpec((1, tm, d), tok),
        out_shape=jax.ShapeDtypeStruct((bsz, s, d), jnp.float32),
        scratch_shapes=[pltpu.VMEM((len(DILATIONS), nd // LANES + 1, tm, LANES), jnp.float32)],
        compiler_params=pltpu.CompilerParams(dimension_semantics=("parallel", "parallel"), vmem_limit_bytes=VMEM_LIMIT),
        name="outproj_odd",
    )(oc, *og, *lg, x, norm_g.reshape(1, d), scale, shift, gate, final_g.reshape(1, d), wz, w)


T_CONV = 512
HALO_C = 16
SUB_C = 64


def _conformer_kernel(x_ref, xp_ref, xn_ref, w_ref, b_ref, lg_ref, lb_ref, o_ref, xe_ref, ph_ref):
    i = pl.program_id(1)
    nt = pl.num_programs(1)
    t = x_ref.shape[1]
    xe_ref[0:HALO_C, :] = jnp.where(i > 0, xp_ref[0], 0.0)
    xe_ref[HALO_C:HALO_C + t, :] = x_ref[0]
    xe_ref[HALO_C + t:, :] = jnp.where(i < nt - 1, xn_ref[0], 0.0)
    half = C_CONV // 2
    n = t + 2 * HALO_C
    xe = xe_ref[...]
    for b in range(1, SUBLANES):
        ph_ref[b - 1] = pltpu.roll(xe, n - b, axis=0)
    for r0 in range(0, t, SUB_C):
        acc = None
        for j in range(C_CONV):
            a, b = divmod(HALO_C - half + j, SUBLANES)
            lo = a * SUBLANES + r0
            src = xe_ref[lo:lo + SUB_C, :] if b == 0 else ph_ref[b - 1, lo:lo + SUB_C, :]
            term = src * w_ref[j:j + 1, :]
            acc = term if acc is None else acc + term
        u = acc + b_ref[...]
        uc = u - jnp.mean(u, axis=-1, keepdims=True)
        y = uc * lax.rsqrt(jnp.mean(uc * uc, axis=-1, keepdims=True) + EPS) * lg_ref[...] + lb_ref[...]
        o_ref[0, r0:r0 + SUB_C, :] = y * jax.nn.sigmoid(y)


def _conformer(glu, dw_w, dw_b, ln_g, ln_b):
    bsz, s, cw = glu.shape
    t = min(T_CONV, s)
    hb = t // HALO_C
    cur = lambda b, i: (b, i, 0)
    const = lambda b, i: (0, 0)
    return pl.pallas_call(
        _conformer_kernel,
        grid=(bsz, s // t),
        in_specs=[pl.BlockSpec((1, t, cw), cur),
                  pl.BlockSpec((1, HALO_C, cw), lambda b, i: (b, jnp.maximum(i * hb - 1, 0), 0)),
                  pl.BlockSpec((1, HALO_C, cw), lambda b, i: (b, jnp.minimum((i + 1) * hb, s // HALO_C - 1), 0)),
                  pl.BlockSpec((C_CONV, cw), const)] + [pl.BlockSpec((1, cw), const)] * 3,
        out_specs=pl.BlockSpec((1, t, cw), cur),
        out_shape=jax.ShapeDtypeStruct((bsz, s, cw), jnp.float32),
        scratch_shapes=[pltpu.VMEM((t + 2 * HALO_C, cw), jnp.float32), pltpu.VMEM((SUBLANES - 1, t + 2 * HALO_C, cw), jnp.float32)],
        compiler_params=pltpu.CompilerParams(dimension_semantics=("parallel", "parallel"), vmem_limit_bytes=VMEM_LIMIT),
        name="conformer",
    )(glu, glu, glu, dw_w, dw_b.reshape(1, cw), ln_g.reshape(1, cw), ln_b.reshape(1, cw))


TQ_ATT = 128
TB_ATT = 512
R_ATT = 64


def _dilated_kernel(q_ref, kc_ref, kp_ref, kn_ref, vc_ref, vp_ref, vn_ref, bias_ref, o_ref, lse_ref, kx_ref, vx_ref):
    i = pl.program_id(2)
    nt = pl.num_programs(2)
    tb = q_ref.shape[1]
    tq = TQ_ATT
    nk = tq + 2 * R_ATT
    nsub = tb // tq
    f32, bf16 = jnp.float32, jnp.bfloat16
    kx_ref[0:R_ATT, :] = kp_ref[0]
    kx_ref[R_ATT:R_ATT + tb, :] = kc_ref[0]
    kx_ref[R_ATT + tb:, :] = kn_ref[0]
    vx_ref[0:R_ATT, :] = vp_ref[0]
    vx_ref[R_ATT:R_ATT + tb, :] = vc_ref[0]
    vx_ref[R_ATT + tb:, :] = vn_ref[0]
    kj = lax.broadcasted_iota(jnp.int32, (tq, nk), 1)
    lane = lax.broadcasted_iota(jnp.int32, (tq, 128), 1)
    low = lane < D_DH
    heads = [(pr, hi) for pr in range(D_HEADS // 2) for hi in (False, True)]
    for sub in range(nsub):
        qs = slice(sub * tq, (sub + 1) * tq)
        ks = slice(sub * tq, sub * tq + nk)
        outside = None
        if sub == 0:
            outside = (kj < R_ATT) & (i == 0)
        if sub == nsub - 1:
            after = (kj >= R_ATT + tq) & (i == nt - 1)
            outside = after if outside is None else outside | after
        scs = []
        for pr, hi in heads:
            ps = slice(pr * 128, (pr + 1) * 128)
            qp = q_ref[0, qs, ps]
            qh = jnp.where(low != hi, qp, jnp.zeros_like(qp))
            sc = lax.dot_general(qh, kx_ref[ks, ps], (((1,), (1,)), ((), ())), preferred_element_type=f32) + bias_ref[2 * pr + int(hi)]
            scs.append(sc if outside is None else jnp.where(outside, NEG, sc))
        ms = [jnp.max(sc, axis=-1, keepdims=True) for sc in scs]
        ps_ = [jnp.exp(sc - m) for sc, m in zip(scs, ms)]
        dens = [jnp.sum(p, axis=-1, keepdims=True) for p in ps_]
        pvs = [jnp.dot(p.astype(bf16), vx_ref[ks, pr * 128:(pr + 1) * 128], preferred_element_type=f32) for (pr, _), p in zip(heads, ps_)]
        lse_all = jnp.zeros((tq, 128), f32)
        for pr in range(D_HEADS // 2):
            lo, hi = 2 * pr, 2 * pr + 1
            o_ref[0, qs, pr * 128:(pr + 1) * 128] = jnp.where(low, pvs[lo] / dens[lo], pvs[hi] / dens[hi])
            lse_all = jnp.where(lane == lo, ms[lo] + jnp.log(dens[lo]), lse_all)
            lse_all = jnp.where(lane == hi, ms[hi] + jnp.log(dens[hi]), lse_all)
        lse_ref[0, qs, :] = lse_all


def _dilated_bias(rel_bias, dilation, tq):
    half = REL_BUCKETS // 2
    exact = half // 2
    qi = jnp.arange(tq)[:, None]
    kj = jnp.arange(tq + 2 * R_ATT)[None, :]
    rel = kj - R_ATT - qi
    reld = rel * dilation
    n = jnp.abs(reld)
    large = exact + (jnp.log(jnp.maximum(n, 1).astype(jnp.float32) / exact) / math.log(REL_MAX_DIST / exact) * (half - exact)).astype(jnp.int32)
    large = jnp.minimum(large, half - 1)
    bucket = (reld > 0).astype(jnp.int32) * half + jnp.where(n < exact, n, large)
    bias = jnp.zeros((rel_bias.shape[1],) + bucket.shape, jnp.float32)
    for b in range(REL_BUCKETS):
        bias = jnp.where((bucket == b)[None], rel_bias[b].astype(jnp.float32)[:, None, None], bias)
    return jnp.where((jnp.abs(rel) <= R_ATT)[None], bias, NEG)


def _dilated_group_call(q, k, v, rel_bias, dilation):
    bsz, dil, ls, nd = q.shape
    assert dil == dilation
    assert ls % TB_ATT == 0
    tb, tq = TB_ATT, TQ_ATT
    nt = ls // tb
    hb = tb // R_ATT
    nk = tq + 2 * R_ATT
    cur = lambda b, r, i: (b, r, i, 0)
    prev = lambda b, r, i: (b, r, jnp.maximum(i * hb - 1, 0), 0)
    nxt = lambda b, r, i: (b, r, jnp.minimum((i + 1) * hb, ls // R_ATT - 1), 0)
    kv_specs = [pl.BlockSpec((1, None, tb, nd), cur), pl.BlockSpec((1, None, R_ATT, nd), prev), pl.BlockSpec((1, None, R_ATT, nd), nxt)]
    return pl.pallas_call(
        _dilated_kernel,
        grid=(bsz, dilation, nt),
        in_specs=[pl.BlockSpec((1, None, tb, nd), cur)] + kv_specs + kv_specs + [pl.BlockSpec((D_HEADS, tq, nk), lambda b, r, i: (0, 0, 0))],
        out_specs=[pl.BlockSpec((1, None, tb, nd), cur), pl.BlockSpec((1, None, tb, 128), cur)],
        out_shape=[jax.ShapeDtypeStruct((bsz, dilation, ls, nd), jnp.float32), jax.ShapeDtypeStruct((bsz, dilation, ls, 128), jnp.float32)],
        scratch_shapes=[pltpu.VMEM((tb + 2 * R_ATT, nd), jnp.bfloat16)] * 2,
        compiler_params=pltpu.CompilerParams(dimension_semantics=("parallel", "parallel", "parallel"), vmem_limit_bytes=VMEM_LIMIT),
        name=f"dilated_d{dilation}",
    )(q, k, k, k, v, v, v, _dilated_bias(rel_bias, dilation, tq))


def _inproj(x, g, scale, shift, w_bf16):
    bsz, s, d = x.shape
    n = w_bf16.shape[1]
    return pl.pallas_call(
        _inproj_kernel,
        grid=(bsz, s // TM_PROJ),
        in_specs=[
            pl.BlockSpec((1, TM_PROJ, d), lambda b, i: (b, i, 0)),
            pl.BlockSpec((1, d), lambda b, i: (0, 0)),
            pl.BlockSpec((1, 1, d), lambda b, i: (b, 0, 0)),
            pl.BlockSpec((1, 1, d), lambda b, i: (b, 0, 0)),
            pl.BlockSpec((d, n), lambda b, i: (0, 0)),
        ],
        out_specs=pl.BlockSpec((1, TM_PROJ, n), lambda b, i: (b, i, 0)),
        out_shape=jax.ShapeDtypeStruct((bsz, s, n), jnp.float32),
        compiler_params=pltpu.CompilerParams(dimension_semantics=("parallel", "parallel"), vmem_limit_bytes=VMEM_LIMIT),
        name="inproj",
    )(x, g.reshape(1, d), scale, shift, w_bf16)


def _outproj_kernel(mix_ref, z_ref, x_ref, gate_ref, w_ref, o_ref):
    z = z_ref[0]
    m = mix_ref[0] * (z * jax.nn.sigmoid(z))
    y = jnp.dot(m.astype(jnp.bfloat16), w_ref[...], preferred_element_type=jnp.float32)
    o_ref[0] = x_ref[0] + gate_ref[0] * y


def _outproj(mix, z, x, gate, w_bf16):
    bsz, s, d = x.shape
    k = mix.shape[-1]
    return pl.pallas_call(
        _outproj_kernel,
        grid=(bsz, s // TM_PROJ),
        in_specs=[
            pl.BlockSpec((1, TM_PROJ, k), lambda b, i: (b, i, 0)),
            pl.BlockSpec((1, TM_PROJ, k), lambda b, i: (b, i, 0)),
            pl.BlockSpec((1, TM_PROJ, d), lambda b, i: (b, i, 0)),
            pl.BlockSpec((1, 1, d), lambda b, i: (b, 0, 0)),
            pl.BlockSpec((k, d), lambda b, i: (0, 0)),
        ],
        out_specs=pl.BlockSpec((1, TM_PROJ, d), lambda b, i: (b, i, 0)),
        out_shape=jax.ShapeDtypeStruct((bsz, s, d), jnp.float32),
        compiler_params=pltpu.CompilerParams(dimension_semantics=("parallel", "parallel"), vmem_limit_bytes=VMEM_LIMIT),
        name="outproj",
    )(mix, z, x, gate, w_bf16)


def _final_rms_kernel(x_ref, g_ref, o_ref):
    x = x_ref[0]
    o_ref[0] = x * lax.rsqrt(jnp.mean(x * x, axis=-1, keepdims=True) + EPS) * g_ref[...]


def _final_rms(x, g):
    bsz, s, d = x.shape
    tm = 512
    return pl.pallas_call(
        _final_rms_kernel,
        grid=(bsz, s // tm),
        in_specs=[pl.BlockSpec((1, tm, d), lambda b, i: (b, i, 0)), pl.BlockSpec((1, d), lambda b, i: (0, 0))],
        out_specs=pl.BlockSpec((1, tm, d), lambda b, i: (b, i, 0)),
        out_shape=jax.ShapeDtypeStruct((bsz, s, d), jnp.float32),
        compiler_params=pltpu.CompilerParams(dimension_semantics=("parallel", "parallel")),
        name="final_rms",
    )(x, g.reshape(1, d))


L_MLSTM = 256
_HI = lax.Precision.HIGHEST


def _log_sigmoid(t):
    return jnp.minimum(t, 0.0) - jnp.log(1.0 + jnp.exp(-jnp.abs(t)))


def _mlstm_kernel(qf_ref, kf_ref, vf_ref, gf_ref, gtf_ref, qb_ref, kb_ref, vb_ref, gb_ref, gtb_ref,
                  bias_ref, biast_ref, hf_ref, hb_ref, c_ref, m_ref):
    n = pl.program_id(1)
    ln = qf_ref.shape[1]
    f32, bf16 = jnp.float32, jnp.bfloat16

    @pl.when(n == 0)
    def _():
        c_ref[...] = jnp.zeros_like(c_ref)
        m_ref[...] = jnp.zeros_like(m_ref)

    row = lax.broadcasted_iota(jnp.int32, (ln, ln), 0)
    col = lax.broadcasted_iota(jnp.int32, (ln, ln), 1)
    ones_blk = jnp.ones((ln, A_DV), bf16)
    dirs = ((0, qf_ref, kf_ref, vf_ref, gf_ref, gtf_ref, hf_ref), (1, qb_ref, kb_ref, vb_ref, gb_ref, gtb_ref, hb_ref))
    probs = []
    for d, q_ref, k_ref, v_ref, g_ref, gt_ref, h_ref in dirs:
        mask = (row >= col) if d == 0 else (row <= col)
        tri = mask.astype(f32)
        tri_t = ((row <= col) if d == 0 else (row >= col)).astype(f32)
        g = g_ref[0] + bias_ref[...]
        gt = gt_ref[0] + biast_ref[...]
        ic = g[:, 4 * d:4 * d + 4]
        it = gt[4 * d:4 * d + 4, :]
        bc = jnp.dot(tri, _log_sigmoid(g[:, 8 + 4 * d:12 + 4 * d]), precision=_HI, preferred_element_type=f32)
        bt = jnp.dot(_log_sigmoid(gt[8 + 4 * d:12 + 4 * d, :]), tri_t, precision=_HI, preferred_element_type=f32)
        for h in range(A_HEADS):
            probs.append(dict(
                r=d * A_HEADS + h, h=h, mask=mask, last=ln - 1 if d == 0 else 0, h_ref=h_ref,
                q=q_ref[0, :, h * A_DK:(h + 1) * A_DK], k=k_ref[0, :, h * A_DK:(h + 1) * A_DK],
                vaug=jnp.concatenate([v_ref[0, :, h * A_DV:(h + 1) * A_DV], ones_blk], axis=1),
                bcol=bc[:, h:h + 1], icol=ic[:, h:h + 1], brow=bt[h:h + 1, :], irow=it[h:h + 1, :]))
    for p in probs:
        p["m_old"] = m_ref[p["r"]:p["r"] + 1, 0:1]
        p["caug"] = c_ref[p["r"]]
        p["qk"] = lax.dot_general(p["q"], p["k"], (((1,), (1,)), ((), ())), preferred_element_type=f32)
    for p in probs:
        p["qc"] = jnp.dot(p["q"], p["caug"].astype(bf16), preferred_element_type=f32)
    for p in probs:
        dmat = jnp.where(p["mask"], p["bcol"] - p["brow"] + p["irow"], -jnp.inf)
        inter = p["bcol"] + p["m_old"]
        mt = jnp.maximum(inter, jnp.max(dmat, axis=-1, keepdims=True))
        p["mt"], p["w_int"] = mt, jnp.exp(inter - mt)
        p["sc"] = (jnp.exp(dmat - mt) * p["qk"]).astype(bf16)
    for p in probs:
        tot = p["w_int"] * p["qc"] + jnp.dot(p["sc"], p["vaug"], preferred_element_type=f32)
        den = jnp.maximum(jnp.abs(tot[:, A_DV:]), jnp.exp(-p["mt"]))
        p["h_ref"][0, :, p["h"] * A_DV:(p["h"] + 1) * A_DV] = tot[:, :A_DV] / den
    for p in probs:
        last, bcol, brow = p["last"], p["bcol"], p["brow"]
        btot_c = bcol[last:last + 1, :]
        btot_r = brow[:, last:last + 1]
        m_new = jnp.maximum(btot_r + p["m_old"], jnp.max(btot_r - brow + p["irow"], axis=-1, keepdims=True))
        w_old = jnp.exp(btot_r + p["m_old"] - m_new)
        kw = (p["k"].astype(f32) * jnp.exp(btot_c - bcol + p["icol"] - m_new)).astype(bf16)
        c_ref[p["r"]] = w_old * p["caug"] + lax.dot_general(kw, p["vaug"], (((0,), (0,)), ((), ())), preferred_element_type=f32)
        m_ref[p["r"]:p["r"] + 1, :] = jnp.broadcast_to(m_new, (1, m_ref.shape[1]))


def _mlstm(q, k, v, g, gt, bias):
    bsz, s, _ = q.shape
    ln = min(L_MLSTM, s)
    nc = s // ln
    hk, hv = A_HEADS * A_DK, A_HEADS * A_DV
    fwd = lambda b, n: (b, n, 0)
    bwd = lambda b, n: (b, nc - 1 - n, 0)
    fwd_t = lambda b, n: (b, 0, n)
    bwd_t = lambda b, n: (b, 0, nc - 1 - n)
    const = lambda b, n: (0, 0)
    def specs(im, im_t):
        return [pl.BlockSpec((1, ln, hk), im), pl.BlockSpec((1, ln, hk), im), pl.BlockSpec((1, ln, hv), im),
                pl.BlockSpec((1, ln, 16), im), pl.BlockSpec((1, 16, ln), im_t)]
    return pl.pallas_call(
        _mlstm_kernel,
        grid=(bsz, nc),
        in_specs=specs(fwd, fwd_t) + specs(bwd, bwd_t) + [pl.BlockSpec((1, 16), const), pl.BlockSpec((16, 1), const)],
        out_specs=[pl.BlockSpec((1, ln, hv), fwd), pl.BlockSpec((1, ln, hv), bwd)],
        out_shape=[jax.ShapeDtypeStruct((bsz, s, hv), jnp.float32)] * 2,
        scratch_shapes=[pltpu.VMEM((2 * A_HEADS, A_DK, 2 * A_DV), jnp.float32), pltpu.VMEM((2 * A_HEADS, 128), jnp.float32)],
        compiler_params=pltpu.CompilerParams(dimension_semantics=("parallel", "arbitrary"), vmem_limit_bytes=VMEM_LIMIT),
        name="mlstm",
    )(q, k, v, g, gt, q, k, v, g, gt, bias.reshape(1, 16), bias.reshape(16, 1))


T_GDN = 256
T_GDN_STEP = 256
C_GDN = 64
HALO = 8


def _softplus(t):
    return jnp.maximum(t, 0.0) + jnp.log1p(jnp.exp(-jnp.abs(t)))


def _gdn_prep_kernel(x_ref, xp_ref, xn_ref, g_ref, gt_ref, w_ref, a_ref, at_ref, dt_ref, dtt_ref,
                     q_ref, k_ref, v_ref, gc_ref, gr_ref, xe_ref):
    i = pl.program_id(1)
    nt = pl.num_programs(1)
    t = x_ref.shape[1]
    f32 = jnp.float32
    hd = B_HEADS * B_DK
    xe_ref[0:HALO, :] = jnp.where(i > 0, xp_ref[0], 0.0)
    xe_ref[HALO:HALO + t, :] = x_ref[0]
    xe_ref[HALO + t:, :] = jnp.where(i < nt - 1, xn_ref[0], 0.0)
    half = B_CONV // 2
    for part, o_ref in enumerate((q_ref, k_ref, v_ref)):
        cs = slice(part * hd, (part + 1) * hd)
        xe = xe_ref[:, cs]
        acc = None
        for j in range(B_CONV):
            off = HALO - half + j
            shifted = xe[off:off + t] if off % SUBLANES == 0 else pltpu.roll(xe, t + 2 * HALO - off, axis=0)[0:t]
            term = shifted * w_ref[j:j + 1, cs]
            acc = term if acc is None else acc + term
        y = acc * jax.nn.sigmoid(acc)
        for h in range(B_HEADS):
            yh = y[:, h * B_DK:(h + 1) * B_DK]
            if part == 0:
                yh = yh * lax.rsqrt(jnp.sum(yh * yh, axis=-1, keepdims=True) + EPS) * (B_DK ** -0.5)
            elif part == 1:
                yh = yh * lax.rsqrt(jnp.sum(yh * yh, axis=-1, keepdims=True) + EPS)
            o_ref[0, :, h * B_DK:(h + 1) * B_DK] = yh.astype(o_ref.dtype)
    row = lax.broadcasted_iota(jnp.int32, (t, t), 0)
    col = lax.broadcasted_iota(jnp.int32, (t, t), 1)
    same = (row // C_GDN) == (col // C_GDN)
    lower = (same & (row >= col)).astype(f32)
    upper = (same & (row <= col)).astype(f32)
    g = g_ref[0]
    gt = gt_ref[0]
    nh = B_HEADS
    dec = -jnp.exp(a_ref[...]) * _softplus(g[:, 2 * nh:] + dt_ref[...])
    dect = -jnp.exp(at_ref[...]) * _softplus(gt[2 * nh:, :] + dtt_ref[...])
    gc_ref[0, :, 0:2 * nh] = jax.nn.sigmoid(g[:, 0:2 * nh])
    gc_ref[0, :, 2 * nh:3 * nh] = jnp.dot(lower, dec[:, 0:nh], precision=_HI, preferred_element_type=f32)
    gc_ref[0, :, 3 * nh:] = jnp.dot(upper, dec[:, nh:], precision=_HI, preferred_element_type=f32)
    gr_ref[0, 0:2 * nh, :] = jax.nn.sigmoid(gt[0:2 * nh, :])
    gr_ref[0, 2 * nh:3 * nh, :] = jnp.dot(dect[0:nh, :], upper, precision=_HI, preferred_element_type=f32)
    gr_ref[0, 3 * nh:, :] = jnp.dot(dect[nh:, :], lower, precision=_HI, preferred_element_type=f32)


def _gdn_prep(dqkv, g, gt, conv_w, a_log, dt_bias):
    bsz, s, n3 = dqkv.shape
    t = min(T_GDN, s)
    nt = s // t
    hd = B_HEADS * B_DK
    hb = t // HALO
    cur = lambda b, i: (b, i, 0)
    const = lambda b, i: (0, 0)
    bf16 = jnp.bfloat16
    return pl.pallas_call(
        _gdn_prep_kernel,
        grid=(bsz, nt),
        in_specs=[
            pl.BlockSpec((1, t, n3), cur),
            pl.BlockSpec((1, HALO, n3), lambda b, i: (b, jnp.maximum(i * hb - 1, 0), 0)),
            pl.BlockSpec((1, HALO, n3), lambda b, i: (b, jnp.minimum((i + 1) * hb, s // HALO - 1), 0)),
            pl.BlockSpec((1, t, 16), cur),
            pl.BlockSpec((1, 16, t), lambda b, i: (b, 0, i)),
            pl.BlockSpec((B_CONV, n3), const),
            pl.BlockSpec((1, 8), const), pl.BlockSpec((8, 1), const),
            pl.BlockSpec((1, 8), const), pl.BlockSpec((8, 1), const),
        ],
        out_specs=[pl.BlockSpec((1, t, hd), cur)] * 3 + [pl.BlockSpec((1, t, 16), cur), pl.BlockSpec((1, 16, t), lambda b, i: (b, 0, i))],
        out_shape=[jax.ShapeDtypeStruct((bsz, s, hd), bf16)] * 3 + [jax.ShapeDtypeStruct((bsz, s, 16), jnp.float32), jax.ShapeDtypeStruct((bsz, 16, s), jnp.float32)],
        scratch_shapes=[pltpu.VMEM((t + 2 * HALO, n3), jnp.float32)],
        compiler_params=pltpu.CompilerParams(dimension_semantics=("parallel", "parallel"), vmem_limit_bytes=VMEM_LIMIT),
        name="gdn_prep",
    )(dqkv, dqkv, dqkv, g, gt, conv_w, a_log.reshape(1, 8), a_log.reshape(8, 1), dt_bias.reshape(1, 8), dt_bias.reshape(8, 1))


def _tri_inverse_many(a_list, masks):
    eye, m16, m32, m64 = masks
    f32, bf16 = jnp.float32, jnp.bfloat16
    mm = lambda x, y: jnp.dot(x.astype(bf16), y.astype(bf16), preferred_element_type=f32)
    ads = [jnp.where(m16, a, 0.0) for a in a_list]
    xs = [eye - ad for ad in ads]
    ps = [mm(ad, ad) for ad in ads]
    for stage in range(3):
        xs = [x + mm(x, p) for x, p in zip(xs, ps)]
        if stage < 2:
            ps = [mm(p, p) for p in ps]
    for lo, hi in ((m16, m32), (m32, m64)):
        off = hi & ~lo
        ys = [mm(jnp.where(off, a, 0.0), x) for a, x in zip(a_list, xs)]
        xs = [x - mm(x, y) for x, y in zip(xs, ys)]
    return xs


def _gdn_kernel(qf_ref, kf_ref, vf_ref, gcf_ref, grf_ref, qb_ref, kb_ref, vb_ref, gcb_ref, grb_ref, of_ref, ob_ref, s_ref):
    n = pl.program_id(1)
    t = qf_ref.shape[1]
    c = C_GDN
    f32, bf16 = jnp.float32, jnp.bfloat16

    @pl.when(n == 0)
    def _():
        s_ref[...] = jnp.zeros_like(s_ref)

    row = lax.broadcasted_iota(jnp.int32, (c, c), 0)
    col = lax.broadcasted_iota(jnp.int32, (c, c), 1)
    eye = (row == col).astype(f32)
    blk = lambda w: (row // w) == (col // w)
    masks = (eye, blk(16), blk(32), blk(64))
    nh, nchunk = B_HEADS, t // c
    dir_refs = ((qf_ref, kf_ref, vf_ref, gcf_ref, grf_ref, of_ref), (qb_ref, kb_ref, vb_ref, gcb_ref, grb_ref, ob_ref))
    probs = [(d, h, ci) for d in range(2) for h in range(nh) for ci in range(nchunk)]
    xpose = (((1,), (1,)), ((), ()))

    def load(d, h, ci):
        q_ref, k_ref, v_ref, gc_ref, gr_ref, _ = dir_refs[d]
        rs, cs = slice(ci * c, (ci + 1) * c), slice(h * B_DK, (h + 1) * B_DK)
        beta = gc_ref[0, rs, d * nh + h:d * nh + h + 1]
        gcol = gc_ref[0, rs, (2 + d) * nh + h:(2 + d) * nh + h + 1]
        grow = gr_ref[0, (2 + d) * nh + h:(2 + d) * nh + h + 1, rs]
        return q_ref[0, rs, cs], k_ref[0, rs, cs], v_ref[0, rs, cs], beta, gcol, grow

    data = [load(*p) for p in probs]
    gams = []
    for (d, _, _), (_, _, _, _, gcol, grow) in zip(probs, data):
        incl = (row >= col) if d == 0 else (row <= col)
        gams.append(jnp.exp(jnp.where(incl, gcol - grow, -jnp.inf)))
    kks = [lax.dot_general(k, k, xpose, preferred_element_type=f32) for (_, k, _, _, _, _) in data]
    qks = [lax.dot_general(q, k, xpose, preferred_element_type=f32) for (q, k, _, _, _, _) in data]
    a_list = []
    for (d, _, _), (_, _, _, beta, _, _), kk, gam in zip(probs, data, kks, gams):
        strict = (row > col) if d == 0 else (row < col)
        a_list.append(jnp.where(strict, beta * kk * gam, 0.0))
    tinvs = _tri_inverse_many(a_list, masks)
    egcs = [jnp.exp(gcol) for (_, _, _, _, gcol, _) in data]
    uws = []
    for (q, k, v, beta, gcol, _), tinv, egc in zip(data, tinvs, egcs):
        rhs = jnp.concatenate([beta * v.astype(f32), (beta * egc) * k.astype(f32)], axis=1).astype(bf16)
        uws.append(jnp.dot(tinv.astype(bf16), rhs, preferred_element_type=f32))
    attns = [(qk * gam).astype(bf16) for qk, gam in zip(qks, gams)]
    index = {p: i for i, p in enumerate(probs)}
    chains = [(d, h) for d in range(2) for h in range(nh)]
    states = [s_ref[d * nh + h] for d, h in chains]
    for step in range(nchunk):
        ids = [index[(d, h, step if d == 0 else nchunk - 1 - step)] for d, h in chains]
        wss = []
        for i, state in zip(ids, states):
            q, _, _, _, _, _ = data[i]
            wq = jnp.concatenate([uws[i][:, B_DV:], q.astype(f32) * egcs[i]], axis=0).astype(bf16)
            wss.append(jnp.dot(wq, state.astype(bf16), preferred_element_type=f32))
        v_news = [(uws[i][:, :B_DV] - ws[:c]).astype(bf16) for i, ws in zip(ids, wss)]
        for (d, h), i, ws, v_new in zip(chains, ids, wss, v_news):
            ci = probs[i][2]
            dir_refs[d][5][0, ci * c:(ci + 1) * c, h * B_DV:(h + 1) * B_DV] = ws[c:] + jnp.dot(attns[i], v_new, preferred_element_type=f32)
        new_states = []
        for (d, h), i, state, v_new in zip(chains, ids, states, v_news):
            _, k, _, _, gcol, _ = data[i]
            last = c - 1 if d == 0 else 0
            gl = gcol[last:last + 1, :]
            kd = (k.astype(f32) * jnp.exp(gl - gcol)).astype(bf16)
            new_states.append(jnp.exp(gl) * state + lax.dot_general(kd, v_new, (((0,), (0,)), ((), ())), preferred_element_type=f32))
        states = new_states
    for (d, h), state in zip(chains, states):
        s_ref[d * nh + h] = state


def _gdn(q, k, v, gc, gr):
    bsz, s, hd = q.shape
    t = min(T_GDN_STEP, s)
    nb = s // t
    fwd = lambda b, n: (b, n, 0)
    bwd = lambda b, n: (b, nb - 1 - n, 0)
    def specs(im, im_t):
        return [pl.BlockSpec((1, t, hd), im)] * 3 + [pl.BlockSpec((1, t, 16), im), pl.BlockSpec((1, 16, t), im_t)]
    return pl.pallas_call(
        _gdn_kernel,
        grid=(bsz, nb),
        in_specs=specs(fwd, lambda b, n: (b, 0, n)) + specs(bwd, lambda b, n: (b, 0, nb - 1 - n)),
        out_specs=[pl.BlockSpec((1, t, hd), fwd), pl.BlockSpec((1, t, hd), bwd)],
        out_shape=[jax.ShapeDtypeStruct((bsz, s, hd), jnp.float32)] * 2,
        scratch_shapes=[pltpu.VMEM((2 * B_HEADS, B_DK, B_DV), jnp.float32)],
        compiler_params=pltpu.CompilerParams(dimension_semantics=("parallel", "arbitrary"), vmem_limit_bytes=VMEM_LIMIT),
        name="gdn",
    )(q, k, v, gc, gr, q, k, v, gc, gr)


def _split(p, sizes):
    return jnp.split(p, np.cumsum(sizes)[:-1].tolist(), axis=-1)


def _layernorm(x, g, b):
    xc = x - jnp.mean(x, axis=-1, keepdims=True)
    y = xc * lax.rsqrt(jnp.mean(xc * xc, axis=-1, keepdims=True) + EPS)
    return y * g + b


def _head_rms(t, g):
    bsz, s, h, d = t.shape
    y = t * lax.rsqrt(jnp.mean(t * t, axis=-1, keepdims=True) + EPS)
    return y.reshape(bsz, s, h * d) * g


def _l2n(t):
    return t * lax.rsqrt(jnp.sum(t * t, axis=-1, keepdims=True) + EPS)


def _dwconv(x, w):
    return lax.conv_general_dilated(x, w[:, None, :].astype(x.dtype), window_strides=(1,), padding='SAME', dimension_numbers=('NWC', 'WIO', 'NWC'), feature_group_count=x.shape[-1])


def _flip(t):
    return jnp.flip(t, axis=1)


def _to_chunks(t):
    bsz, s, h = t.shape[:3]
    t = t.reshape((bsz, s // CHUNK, CHUNK, h) + t.shape[3:])
    return jnp.moveaxis(t, (1, 3), (0, 2))


def _from_chunks(t):
    nc, bsz, h, l = t.shape[:4]
    t = jnp.moveaxis(t, (0, 2), (1, 3))
    return t.reshape((bsz, nc * l, h) + t.shape[4:])


def _mlstm_chunkwise(q, k, v, i_pre, logf):
    q, k, v, i_pre, logf = (_to_chunks(t) for t in (q, k, v, i_pre, logf))
    nc, bsz, h = q.shape[:3]
    causal = jnp.tril(jnp.ones((CHUNK, CHUNK), dtype=bool))
    b = jnp.cumsum(logf, axis=-1)
    dmat = jnp.where(causal, b[..., :, None] - b[..., None, :] + i_pre[..., None, :], -jnp.inf)
    dmax = jnp.max(dmat, axis=-1)
    qk = jnp.einsum('nbhld,nbhsd->nbhls', q, k)
    a_end = b[..., -1:] - b + i_pre

    def step(carry, xs):
        cmat, nvec, m = carry
        qc, kc, vc, bc, dc, dmc, qkc, aec = xs
        inter = bc + m[..., None]
        mt = jnp.maximum(inter, dmc)
        w_int = jnp.exp(inter - mt)
        sc = jnp.exp(dc - mt[..., None]) * qkc
        num = w_int[..., None] * jnp.einsum('bhld,bhde->bhle', qc, cmat) + jnp.einsum('bhls,bhse->bhle', sc, vc)
        den = w_int * jnp.einsum('bhld,bhd->bhl', qc, nvec) + jnp.sum(sc, axis=-1)
        hc = num / jnp.maximum(jnp.abs(den), jnp.exp(-mt))[..., None]
        m_new = jnp.maximum(bc[..., -1] + m, jnp.max(aec, axis=-1))
        w_old = jnp.exp(bc[..., -1] + m - m_new)
        kw = kc * jnp.exp(aec - m_new[..., None])[..., None]
        cmat = w_old[..., None, None] * cmat + jnp.einsum('bhld,bhle->bhde', kw, vc)
        nvec = w_old[..., None] * nvec + jnp.sum(kw, axis=-2)
        return (cmat, nvec, m_new), hc

    init = (jnp.zeros((bsz, h, A_DK, A_DV), jnp.float32), jnp.zeros((bsz, h, A_DK), jnp.float32), jnp.zeros((bsz, h), jnp.float32))
    _, hs = lax.scan(step, init, (q, k, v, b, dmat, dmax, qk, a_end))
    return _from_chunks(hs)


def _gdn_chunked(q, k, v, beta, g):
    q, k, v, beta, g = (_to_chunks(t) for t in (q, k, v, beta, g))
    nc, bsz, h = q.shape[:3]
    tril = jnp.tril(jnp.ones((CHUNK, CHUNK), dtype=bool))
    strict = jnp.tril(jnp.ones((CHUNK, CHUNK), dtype=bool), -1)
    gc = jnp.cumsum(g, axis=-1)
    gam = jnp.exp(jnp.where(tril, gc[..., :, None] - gc[..., None, :], -jnp.inf))
    a = jnp.where(strict, beta[..., :, None] * jnp.einsum('nbhid,nbhjd->nbhij', k, k) * gam, 0.0)
    tmat = a + jnp.eye(CHUNK, dtype=a.dtype)
    u = lax.linalg.triangular_solve(tmat, beta[..., None] * v, left_side=True, lower=True, unit_diagonal=True)
    w = lax.linalg.triangular_solve(tmat, (beta * jnp.exp(gc))[..., None] * k, left_side=True, lower=True, unit_diagonal=True)
    attn = jnp.einsum('nbhid,nbhjd->nbhij', q, k) * gam

    def step(state, xs):
        qc, kc, uc, wc, gcc, ac = xs
        v_new = uc - jnp.einsum('bhld,bhde->bhle', wc, state)
        o = jnp.einsum('bhld,bhde->bhle', qc * jnp.exp(gcc)[..., None], state) + jnp.einsum('bhls,bhse->bhle', ac, v_new)
        gl = gcc[..., -1]
        state = jnp.exp(gl)[..., None, None] * state + jnp.einsum('bhld,bhle->bhde', kc * jnp.exp(gl[..., None] - gcc)[..., None], v_new)
        return state, o

    _, os_ = lax.scan(step, jnp.zeros((bsz, h, B_DK, B_DV), jnp.float32), (q, k, u, w, gc, attn))
    return _from_chunks(os_)


def _t5_bucket(rel):
    half = REL_BUCKETS // 2
    exact = half // 2
    n = jnp.abs(rel)
    large = exact + (jnp.log(jnp.maximum(n, 1).astype(jnp.float32) / exact) / math.log(REL_MAX_DIST / exact) * (half - exact)).astype(jnp.int32)
    large = jnp.minimum(large, half - 1)
    return (rel > 0).astype(jnp.int32) * half + jnp.where(n < exact, n, large)


def _dilated_group(q, k, v, dilation, radius, rel_bias):
    bsz, s, h, dh = q.shape
    ls = s // dilation
    nb = -(-ls // radius)
    lp = nb * radius

    def sub(t, lo, hi):
        t = t.reshape(bsz, ls, dilation, h, dh).transpose(0, 3, 2, 1, 4)
        return jnp.pad(t, ((0, 0), (0, 0), (0, 0), (lo, hi), (0, 0)))

    qb = sub(q, 0, lp - ls).reshape(bsz, h, dilation, nb, radius, dh)

    def band(t):
        t = sub(t, radius, lp - ls + radius).reshape(bsz, h, dilation, nb + 2, radius, dh)
        return jnp.concatenate([t[:, :, :, :-2], t[:, :, :, 1:-1], t[:, :, :, 2:]], axis=4)

    kb, vb = band(k), band(v)
    qi = jnp.arange(radius)[:, None]
    kj = jnp.arange(3 * radius)[None, :]
    rel = kj - radius - qi
    kpos = jnp.arange(nb)[:, None, None] * radius + kj - radius
    valid = (jnp.abs(rel) <= radius) & (kpos >= 0) & (kpos < ls)
    bias = jnp.transpose(rel_bias[_t5_bucket(rel * dilation)], (2, 0, 1)).astype(jnp.float32)
    sc = jnp.einsum('bhrnid,bhrnjd->bhrnij', qb, kb).astype(jnp.float32) * (dh ** -0.5) + bias[:, None, None]
    sc = jnp.where(valid, sc, NEG)
    m = jnp.max(sc, axis=-1, keepdims=True)
    p = jnp.exp(sc - m)
    den = jnp.sum(p, axis=-1)
    o = jnp.einsum('bhrnij,bhrnjd->bhrnid', p, vb.astype(jnp.float32)) / den[..., None]
    lse = m[..., 0] + jnp.log(den)
    o = o.reshape(bsz, h, dilation, lp, dh)[:, :, :, :ls].transpose(0, 3, 2, 1, 4).reshape(bsz, s, h, dh)
    lse = lse.reshape(bsz, h, dilation, lp)[:, :, :, :ls].transpose(0, 3, 2, 1).reshape(bsz, s, h)
    return o, lse


def _dilated_attention(q, k, v, rel_bias):
    outs, lses = [], []
    for window, dilation in D_GROUPS:
        o, l = _dilated_group(q, k, v, dilation, window // (2 * dilation), rel_bias)
        outs.append(o)
        lses.append(l)
    wts = jax.nn.softmax(jnp.stack(lses, axis=0), axis=0)
    return jnp.sum(wts[..., None] * jnp.stack(outs, axis=0), axis=0)


def _even_mixer_core(p, m_gate_b, dn_dt_bias, dn_a_log, dn_conv_w, m_norm_g, dn_norm_g):
    bsz, s, _ = p.shape
    f32 = jnp.float32
    mq, mk, mv, mo, mg, dqkv, dg, z = _split(p, EVEN_SPLITS)
    q = mq.reshape(bsz, s, A_HEADS, A_DK)
    k = mk.reshape(bsz, s, A_HEADS, A_DK) * (A_DK ** -0.5)
    v = mv.reshape(bsz, s, A_HEADS, A_DV)
    gt = mg.reshape(bsz, s, 4, A_HEADS) + m_gate_b
    logf = jax.nn.log_sigmoid(gt[:, :, 2:4])
    h_fwd = _mlstm_chunkwise(q, k, v, gt[:, :, 0], logf[:, :, 0])
    h_bwd = _flip(_mlstm_chunkwise(_flip(q), _flip(k), _flip(v), _flip(gt[:, :, 1]), _flip(logf[:, :, 1])))
    out_a = jax.nn.sigmoid(mo) * _head_rms(h_fwd + h_bwd, m_norm_g)
    qkv = jax.nn.silu(_dwconv(dqkv, dn_conv_w))
    bq, bk, bv = _split(qkv, (B_HEADS * B_DK, B_HEADS * B_DK, B_HEADS * B_DV))
    q = _l2n(bq.reshape(bsz, s, B_HEADS, B_DK)) * (B_DK ** -0.5)
    k = _l2n(bk.reshape(bsz, s, B_HEADS, B_DK))
    v = bv.reshape(bsz, s, B_HEADS, B_DV)
    gb = dg.reshape(bsz, s, 4, B_HEADS)
    beta = jax.nn.sigmoid(gb[:, :, 0:2])
    decay = -jnp.exp(dn_a_log) * jax.nn.softplus(gb[:, :, 2:4] + dn_dt_bias)
    o_fwd = _gdn_chunked(q, k, v, beta[:, :, 0], decay[:, :, 0])
    o_bwd = _flip(_gdn_chunked(_flip(q), _flip(k), _flip(v), _flip(beta[:, :, 1]), _flip(decay[:, :, 1])))
    out_b = _head_rms(o_fwd + o_bwd, dn_norm_g)
    return jnp.concatenate([out_a, out_b], axis=-1), z


def _odd_mixer_core(p, dw_w, dw_b, ln_g, ln_b, rel_bias):
    bsz, s, _ = p.shape
    ga, gb, aq, ak, av, z = _split(p, ODD_SPLITS)
    u = _dwconv(ga * jax.nn.sigmoid(gb), dw_w) + dw_b
    out_c = jax.nn.silu(_layernorm(u, ln_g, ln_b))
    shp = (bsz, s, D_HEADS, D_DH)
    out_d = _dilated_attention(aq.reshape(shp), ak.reshape(shp), av.reshape(shp), rel_bias).reshape(bsz, s, D_HEADS * D_DH)
    return jnp.concatenate([out_c, out_d], axis=-1), z


def kernel(x, c, norm_g, ada_w, ada_b, ev_w_in, ev_m_gate_b, ev_dn_dt_bias, ev_dn_a_log, ev_dn_conv_w, ev_m_norm_g, ev_dn_norm_g, ev_w_out, od_w_in, od_dw_w, od_dw_b, od_ln_g, od_ln_b, od_w_out, rel_bias, final_g):
    assert DEPTH == 2, "the final RMSNorm is fused into the (last) odd layer's output projection"
    assert all(window // (2 * dil) == R_ATT for window, dil in D_GROUPS)
    d = x.shape[-1]
    mod = _adaln(c, ada_w, ada_b)
    for layer in range(DEPTH):
        shift, scale, gate = (mod[layer, :, i * d:(i + 1) * d][:, None, :] for i in range(3))
        j = layer // 2
        if layer % 2 == 0:
            x = _even_layer(x, norm_g[layer], scale, shift, gate, ev_w_in[j], ev_m_gate_b[j], ev_dn_dt_bias[j], ev_dn_a_log[j],
                            ev_dn_conv_w[j], ev_m_norm_g[j], ev_dn_norm_g[j], ev_w_out[j])
        else:
            x = _odd_layer_final(x, norm_g[layer], scale, shift, gate, od_w_in[j], od_dw_w[j], od_dw_b[j], od_ln_g[j], od_ln_b[j],
                                 rel_bias, od_w_out[j], final_g)
    return x


def _even_layer(x, norm_g, scale, shift, gate, w_in, m_gate_b, dn_dt_bias, dn_a_log, dn_conv_w, m_norm_g, dn_norm_g, w_out):
    bf16 = jnp.bfloat16
    mq, mk, mv, mo, mg, dqkv, dg, z = _split(w_in, EVEN_SPLITS)
    w = jnp.concatenate([mq, mk * (A_DK ** -0.5), mv, dqkv], axis=1).astype(bf16)
    wg = jnp.concatenate([mg, dg], axis=1).astype(bf16)
    wz = jnp.concatenate([mo, z], axis=1).astype(bf16)
    pq, pk, pv, pdqkv, g_m, g_d, gt_m, gt_d = _inproj_even(x, norm_g, scale, shift, w, wg, wg.T)
    hf, hb = _mlstm(pq, pk, pv, g_m, gt_m, m_gate_b.reshape(16))
    bq, bk, bv, gc, gr = _gdn_prep(pdqkv, g_d, gt_d, dn_conv_w, dn_a_log, dn_dt_bias)
    of, ob = _gdn(bq, bk, bv, gc, gr)
    return _outproj_even(hf, hb, of, ob, x, norm_g, scale, shift, gate, m_norm_g, dn_norm_g, wz, w_out.astype(bf16))


def _odd_layer_final(x, norm_g, scale, shift, gate, w_in, dw_w, dw_b, ln_g, ln_b, rel_bias, w_out, final_g):
    bf16 = jnp.bfloat16
    ga, gb, aq, ak, av, z = _split(w_in, ODD_SPLITS)
    w = jnp.concatenate([ga, gb, aq * (D_DH ** -0.5), ak, av], axis=1).astype(bf16)
    glu, pq, pk, pv = _inproj_odd(x, norm_g, scale, shift, w)
    out_c = _conformer(glu, dw_w, dw_b, ln_g, ln_b)
    og, lg = zip(*[_dilated_group_call(qd, kd, vd, rel_bias, dil) for qd, kd, vd, dil in zip(pq, pk, pv, DILATIONS)])
    return _outproj_odd_final(out_c, og, lg, x, norm_g, scale, shift, gate, final_g, z.astype(bf16), w_out.astype(bf16))
```

```python
import math
from functools import partial

import jax
import jax.numpy as jnp
import numpy as np
from jax import lax
from jax.experimental import pallas as pl
from jax.experimental.pallas import tpu as pltpu

D_MODEL = 1024
BATCH = 4
SEQ = 8192
DEPTH = 2
A_HEADS = 4
A_DK = 64
A_DV = 128
B_HEADS = 4
B_DK = 128
B_DV = 128
B_CONV = 5
C_WIDTH = 512
C_CONV = 31
D_HEADS = 8
D_DH = 64
D_GROUPS = ((128, 1), (512, 4), (2048, 16))
REL_BUCKETS = 32
REL_MAX_DIST = 1024
CHUNK = 64
EPS = 1e-6
NEG = -1e30
MIX_EVEN = A_HEADS * A_DV + B_HEADS * B_DV
MIX_ODD = C_WIDTH + D_HEADS * D_DH
B_QKV = B_HEADS * (2 * B_DK + B_DV)
EVEN_SPLITS = (A_HEADS * A_DK, A_HEADS * A_DK, A_HEADS * A_DV, A_HEADS * A_DV, 4 * A_HEADS, B_QKV, 4 * B_HEADS, MIX_EVEN)
ODD_SPLITS = (C_WIDTH, C_WIDTH, D_HEADS * D_DH, D_HEADS * D_DH, D_HEADS * D_DH, MIX_ODD)

VMEM_LIMIT = 56 * 1024 * 1024
TM_PROJ = 512


def _adaln_kernel(c_ref, w_ref, b_ref, o_ref):
    c = c_ref[...]
    cs = (c * jax.nn.sigmoid(c)).astype(jnp.bfloat16)
    o_ref[0] = jnp.dot(cs, w_ref[0].astype(jnp.bfloat16), preferred_element_type=jnp.float32) + b_ref[0]


def _adaln(c, ada_w, ada_b):
    depth, d, n3 = ada_w.shape
    bsz = c.shape[0]
    tn = 1024
    return pl.pallas_call(
        _adaln_kernel,
        grid=(depth, n3 // tn),
        in_specs=[pl.BlockSpec((bsz, d), lambda l, j: (0, 0)), pl.BlockSpec((1, d, tn), lambda l, j: (l, 0, j)),
                  pl.BlockSpec((1, 1, tn), lambda l, j: (l, 0, j))],
        out_specs=pl.BlockSpec((1, bsz, tn), lambda l, j: (l, 0, j)),
        out_shape=jax.ShapeDtypeStruct((depth, bsz, n3), jnp.float32),
        compiler_params=pltpu.CompilerParams(dimension_semantics=("parallel", "parallel")),
        name="adaln",
    )(c, ada_w, ada_b.reshape(depth, 1, n3))


def _modulated_rms_val(x, g, scale, shift):
    y = x * lax.rsqrt(jnp.mean(x * x, axis=-1, keepdims=True) + EPS)
    return ((y * g) * (1.0 + scale) + shift).astype(jnp.bfloat16)


def _modulated_rms(x_ref, g_ref, sc_ref, sh_ref):
    return _modulated_rms_val(x_ref[0], g_ref[...], sc_ref[0], sh_ref[0])


_EV_COLS = {"mq": (0, 256), "mk": (256, 512), "mv": (512, 1024), "dqkv": (1024, 2560)}
_OD_COLS = {"ga": (0, 512), "gb": (512, 1024), "aq": (1024, 1536), "ak": (1536, 2048), "av": (2048, 2560)}


HALO_X = 16
C_GDN = 64


def _softplus(t):
    return jnp.maximum(t, 0.0) + jnp.log1p(jnp.exp(-jnp.abs(t)))


def _seg_cumsum_lanes(x, seg, reverse):
    n = x.shape[1]
    pos = lax.broadcasted_iota(jnp.int32, x.shape, 1) % seg
    k = 1
    while k < seg:
        if reverse:
            x = x + jnp.where(pos < seg - k, pltpu.roll(x, n - k, axis=1), 0.0)
        else:
            x = x + jnp.where(pos >= k, pltpu.roll(x, k, axis=1), 0.0)
        k *= 2
    return x


def _conv_taps(xe, w_ref, cs, width, t, halo):
    n = t + 2 * halo
    acc = None
    for j in range(width):
        off = halo - width // 2 + j
        shifted = xe[off:off + t] if off % SUBLANES == 0 else pltpu.roll(xe, n - off, axis=0)[0:t]
        term = shifted * w_ref[j:j + 1, cs]
        acc = term if acc is None else acc + term
    return acc


def _zero_outside(d, t, halo, first, last):
    return jnp.concatenate([jnp.where(first, 0.0, d[:halo]), d[halo:halo + t], jnp.where(last, 0.0, d[halo + t:])], axis=0)


def _inproj_even_kernel(x_ref, xp_ref, xn_ref, g_ref, sc_ref, sh_ref, w_ref, wgt_ref, cw_ref, a_ref, dt_ref,
                        mq_ref, mk_ref, mv_ref, bq_ref, bk_ref, bv_ref, mg_ref, mgt_ref, gc_ref, gr_ref):
    i = pl.program_id(1)
    nt = pl.num_programs(1)
    tm = x_ref.shape[1]
    f32 = jnp.float32
    x_ext = jnp.concatenate([xp_ref[0], x_ref[0], xn_ref[0]], axis=0)
    h_ext = _modulated_rms_val(x_ext, g_ref[...], sc_ref[0], sh_ref[0])
    h = h_ext[HALO_X:HALO_X + tm]
    for name, o_ref in (("mq", mq_ref), ("mk", mk_ref), ("mv", mv_ref)):
        lo, hi = _EV_COLS[name]
        o_ref[0] = jnp.dot(h, w_ref[:, lo:hi], preferred_element_type=f32).astype(o_ref.dtype)
    gates_t = lax.dot_general(wgt_ref[...], h, (((1,), (1,)), ((), ())), preferred_element_type=f32)
    mgt_ref[0] = gates_t[:16]
    mg_ref[0] = gates_t[:16].T
    nh = B_HEADS
    dgt = gates_t[16:]
    dec = -jnp.exp(a_ref[...]) * _softplus(dgt[2 * nh:] + dt_ref[...])
    gr = jnp.concatenate([jax.nn.sigmoid(dgt[:2 * nh]), _seg_cumsum_lanes(dec[:nh], C_GDN, False),
                          _seg_cumsum_lanes(dec[nh:], C_GDN, True)], axis=0)
    gr_ref[0] = gr
    gc_ref[0] = gr.T
    hd = B_HEADS * B_DK
    for part, o_ref in enumerate((bq_ref, bk_ref, bv_ref)):
        lo = _EV_COLS["dqkv"][0] + part * hd
        cs = slice(part * hd, (part + 1) * hd)
        d = jnp.dot(h_ext, w_ref[:, lo:lo + hd], preferred_element_type=f32)
        acc = _conv_taps(_zero_outside(d, tm, HALO_X, i == 0, i == nt - 1), cw_ref, cs, B_CONV, tm, HALO_X)
        y = acc * jax.nn.sigmoid(acc)
        for hh in range(B_HEADS):
            yh = y[:, hh * B_DK:(hh + 1) * B_DK]
            if part == 0:
                yh = yh * lax.rsqrt(jnp.sum(yh * yh, axis=-1, keepdims=True) + EPS) * (B_DK ** -0.5)
            elif part == 1:
                yh = yh * lax.rsqrt(jnp.sum(yh * yh, axis=-1, keepdims=True) + EPS)
            o_ref[0, :, hh * B_DK:(hh + 1) * B_DK] = yh.astype(o_ref.dtype)


def _halo_specs(tm, s, d):
    hb = tm // HALO_X
    return [pl.BlockSpec((1, HALO_X, d), lambda b, i: (b, jnp.maximum(i * hb - 1, 0), 0)),
            pl.BlockSpec((1, HALO_X, d), lambda b, i: (b, jnp.minimum((i + 1) * hb, s // HALO_X - 1), 0))]


def _inproj_even(x, g, scale, shift, w, wgt, conv_w, a_log, dt_bias):
    bsz, s, d = x.shape
    tm = TM_PROJ
    tok = lambda b, i: (b, i, 0)
    tok_t = lambda b, i: (b, 0, i)
    const = lambda b, i: (0, 0)
    bvec = lambda b, i: (b, 0, 0)
    bf16, f32 = jnp.bfloat16, jnp.float32
    hd = B_HEADS * B_DK
    widths = [A_HEADS * A_DK, A_HEADS * A_DK, A_HEADS * A_DV, hd, hd, hd]
    return pl.pallas_call(
        _inproj_even_kernel,
        grid=(bsz, s // tm),
        in_specs=[pl.BlockSpec((1, tm, d), tok)] + _halo_specs(tm, s, d) + [pl.BlockSpec((1, d), const), pl.BlockSpec((1, 1, d), bvec),
                  pl.BlockSpec((1, 1, d), bvec), pl.BlockSpec(w.shape, const), pl.BlockSpec(wgt.shape, const),
                  pl.BlockSpec(conv_w.shape, const), pl.BlockSpec((8, 1), const), pl.BlockSpec((8, 1), const)],
        out_specs=[pl.BlockSpec((1, tm, wd), tok) for wd in widths]
        + [pl.BlockSpec((1, tm, 16), tok), pl.BlockSpec((1, 16, tm), tok_t)] * 2,
        out_shape=[jax.ShapeDtypeStruct((bsz, s, wd), bf16) for wd in widths]
        + [jax.ShapeDtypeStruct((bsz, s, 16), f32), jax.ShapeDtypeStruct((bsz, 16, s), f32)] * 2,
        compiler_params=pltpu.CompilerParams(dimension_semantics=("parallel", "parallel"), vmem_limit_bytes=VMEM_LIMIT),
        name="inproj_even",
    )(x, x, x, g.reshape(1, d), scale, shift, w, wgt, conv_w, a_log.reshape(8, 1), dt_bias.reshape(8, 1))


DILATIONS = tuple(dil for _, dil in D_GROUPS)
LANES = 128
SUBLANES = 8


SUB_C = 64


def _inproj_odd_kernel(x_ref, xp_ref, xn_ref, g_ref, sc_ref, sh_ref, w_ref, cw_ref, cb_ref, lg_ref, lb_ref, oc_ref, *rest):
    out_refs, plane_ref = rest[:-1], rest[-1]
    i = pl.program_id(1)
    nt = pl.num_programs(1)
    f32 = jnp.float32
    tm = x_ref.shape[1]
    nd = D_HEADS * D_DH
    x_ext = jnp.concatenate([xp_ref[0], x_ref[0], xn_ref[0]], axis=0)
    h_ext = _modulated_rms_val(x_ext, g_ref[...], sc_ref[0], sh_ref[0])
    h = h_ext[HALO_X:HALO_X + tm]
    dot = lambda name: jnp.dot(h, w_ref[:, _OD_COLS[name][0]:_OD_COLS[name][1]], preferred_element_type=f32)
    dot_ext = lambda name: jnp.dot(h_ext, w_ref[:, _OD_COLS[name][0]:_OD_COLS[name][1]], preferred_element_type=f32)
    xe = _zero_outside(dot_ext("ga") * jax.nn.sigmoid(dot_ext("gb")), tm, HALO_X, i == 0, i == nt - 1)
    n = tm + 2 * HALO_X
    phases = [xe] + [pltpu.roll(xe, n - b, axis=0) for b in range(1, SUBLANES)]

    def conv_rows(r0):
        acc = None
        for j in range(C_CONV):
            a, b = divmod(HALO_X - C_CONV // 2 + j, SUBLANES)
            lo = a * SUBLANES + r0
            term = phases[b][lo:lo + SUB_C] * cw_ref[j:j + 1, :]
            acc = term if acc is None else acc + term
        u = acc + cb_ref[...]
        uc = u - jnp.mean(u, axis=-1, keepdims=True)
        y = uc * lax.rsqrt(jnp.mean(uc * uc, axis=-1, keepdims=True) + EPS) * lg_ref[...] + lb_ref[...]
        oc_ref[0, r0:r0 + SUB_C, :] = y * jax.nn.sigmoid(y)

    def attention_operand(a, name):
        r = dot(name)
        group_refs = out_refs[a * len(DILATIONS):(a + 1) * len(DILATIONS)]
        for j in range(nd // LANES):
            plane_ref[a, j] = r[:, j * LANES:(j + 1) * LANES]
        for dil, o_ref in zip(DILATIONS, group_refs):
            if dil == 1:
                o_ref[0, 0] = r.astype(o_ref.dtype)
                continue
            for res in range(dil):
                for j in range(nd // LANES):
                    o_ref[0, res, :, j * LANES:(j + 1) * LANES] = plane_ref[a, j, pl.ds(res, tm // dil, stride=dil), :].astype(o_ref.dtype)

    row_blocks = list(range(0, tm, SUB_C))
    names = ("aq", "ak", "av")
    per = -(-len(row_blocks) // (len(names) + 1))
    for a in range(len(names) + 1):
        for r0 in row_blocks[a * per:(a + 1) * per]:
            conv_rows(r0)
        if a < len(names):
            attention_operand(a, names[a])


def _inproj_odd(x, g, scale, shift, w, dw_w, dw_b, ln_g, ln_b):
    bsz, s, d = x.shape
    tm = TM_PROJ
    tok = lambda b, i: (b, i, 0)
    const = lambda b, i: (0, 0)
    bvec = lambda b, i: (b, 0, 0)
    bf16, f32 = jnp.bfloat16, jnp.float32
    nd = D_HEADS * D_DH
    cw = C_WIDTH
    att_specs = [pl.BlockSpec((1, dil, tm // dil, nd), lambda b, i: (b, 0, i, 0)) for dil in DILATIONS] * 3
    att_shapes = [jax.ShapeDtypeStruct((bsz, dil, s // dil, nd), bf16) for dil in DILATIONS] * 3
    outs = pl.pallas_call(
        _inproj_odd_kernel,
        grid=(bsz, s // tm),
        in_specs=[pl.BlockSpec((1, tm, d), tok)] + _halo_specs(tm, s, d) + [pl.BlockSpec((1, d), const), pl.BlockSpec((1, 1, d), bvec),
                  pl.BlockSpec((1, 1, d), bvec), pl.BlockSpec(w.shape, const), pl.BlockSpec((C_CONV, cw), const)]
        + [pl.BlockSpec((1, cw), const)] * 3,
        out_specs=[pl.BlockSpec((1, tm, cw), tok)] + att_specs,
        out_shape=[jax.ShapeDtypeStruct((bsz, s, cw), f32)] + att_shapes,
        scratch_shapes=[pltpu.VMEM((3, nd // LANES, tm, LANES), f32)],
        compiler_params=pltpu.CompilerParams(dimension_semantics=("parallel", "parallel"), vmem_limit_bytes=VMEM_LIMIT),
        name="inproj_odd",
    )(x, x, x, g.reshape(1, d), scale, shift, w, dw_w, dw_b.reshape(1, cw), ln_g.reshape(1, cw), ln_b.reshape(1, cw))
    ng = len(DILATIONS)
    return outs[0], outs[1:1 + ng], outs[1 + ng:1 + 2 * ng], outs[1 + 2 * ng:]


def _head_rms_cols(t, g, width):
    parts = []
    for h in range(t.shape[1] // width):
        th = t[:, h * width:(h + 1) * width]
        parts.append(th * lax.rsqrt(jnp.mean(th * th, axis=-1, keepdims=True) + EPS))
    return jnp.concatenate(parts, axis=1) * g


def _outproj_even_kernel(hf_ref, hb_ref, of_ref, ob_ref, x_ref, g_ref, sc_ref, sh_ref, gate_ref, mg_ref, dg_ref, wz_ref, w_ref, o_ref):
    f32, bf16 = jnp.float32, jnp.bfloat16
    na = A_HEADS * A_DV
    h = _modulated_rms(x_ref, g_ref, sc_ref, sh_ref)
    mo = jnp.dot(h, wz_ref[:, :na], preferred_element_type=f32)
    z = jnp.dot(h, wz_ref[:, na:], preferred_element_type=f32)
    sz = z * jax.nn.sigmoid(z)
    out_a = jax.nn.sigmoid(mo) * _head_rms_cols(hf_ref[0] + hb_ref[0], mg_ref[...], A_DV)
    out_b = _head_rms_cols(of_ref[0] + ob_ref[0], dg_ref[...], B_DV)
    y = jnp.dot((out_a * sz[:, :na]).astype(bf16), w_ref[:na, :], preferred_element_type=f32)
    y = y + jnp.dot((out_b * sz[:, na:]).astype(bf16), w_ref[na:, :], preferred_element_type=f32)
    o_ref[0] = x_ref[0] + gate_ref[0] * y


def _outproj_even(hf, hb, of, ob, x, norm_g, scale, shift, gate, m_norm_g, dn_norm_g, wz, w):
    bsz, s, d = x.shape
    tm = TM_PROJ
    tok = lambda b, i: (b, i, 0)
    const = lambda b, i: (0, 0)
    bvec = lambda b, i: (b, 0, 0)
    na, nb = A_HEADS * A_DV, B_HEADS * B_DV
    return pl.pallas_call(
        _outproj_even_kernel,
        grid=(bsz, s // tm),
        in_specs=[pl.BlockSpec((1, tm, na), tok)] * 2 + [pl.BlockSpec((1, tm, nb), tok)] * 2 + [pl.BlockSpec((1, tm, d), tok),
                  pl.BlockSpec((1, d), const), pl.BlockSpec((1, 1, d), bvec), pl.BlockSpec((1, 1, d), bvec), pl.BlockSpec((1, 1, d), bvec),
                  pl.BlockSpec((1, na), const), pl.BlockSpec((1, nb), const), pl.BlockSpec(wz.shape, const), pl.BlockSpec(w.shape, const)],
        out_specs=pl.BlockSpec((1, tm, d), tok),
        out_shape=jax.ShapeDtypeStruct((bsz, s, d), jnp.float32),
        compiler_params=pltpu.CompilerParams(dimension_semantics=("parallel", "parallel"), vmem_limit_bytes=VMEM_LIMIT),
        name="outproj_even",
    )(hf, hb, of, ob, x, norm_g.reshape(1, d), scale, shift, gate, m_norm_g.reshape(1, na), dn_norm_g.reshape(1, nb), wz, w)


def _outproj_odd_kernel(oc_ref, o1_ref, o2_ref, o3_ref, l1_ref, l2_ref, l3_ref, x_ref, g_ref, sc_ref, sh_ref, gate_ref, fg_ref,
                        wz_ref, w_ref, o_ref, nat_ref):
    f32, bf16 = jnp.float32, jnp.bfloat16
    tm = x_ref.shape[1]
    npl = D_HEADS * D_DH // LANES
    z = jnp.dot(_modulated_rms(x_ref, g_ref, sc_ref, sh_ref), wz_ref[...], preferred_element_type=f32)
    sz = z * jax.nn.sigmoid(z)
    groups = []
    for gi, (dil, og_ref, lg_ref) in enumerate(zip(DILATIONS, (o1_ref, o2_ref, o3_ref), (l1_ref, l2_ref, l3_ref))):
        if dil == 1:
            groups.append(([og_ref[0, 0, :, j * LANES:(j + 1) * LANES] for j in range(npl)], lg_ref[0, 0]))
            continue
        for res in range(dil):
            rows = pl.ds(res, tm // dil, stride=dil)
            for j in range(npl):
                nat_ref[gi, j, rows, :] = og_ref[0, res, :, j * LANES:(j + 1) * LANES]
            nat_ref[gi, npl, rows, :] = lg_ref[0, res]
        groups.append(([nat_ref[gi, j] for j in range(npl)], nat_ref[gi, npl]))
    (p1, l1), (p2, l2), (p3, l3) = groups
    lm = jnp.maximum(jnp.maximum(l1, l2), l3)
    e1, e2, e3 = jnp.exp(l1 - lm), jnp.exp(l2 - lm), jnp.exp(l3 - lm)
    inv = 1.0 / (e1 + e2 + e3)
    low = lax.broadcasted_iota(jnp.int32, (tm, LANES), 1) < D_DH
    planes = []
    for j in range(npl):
        acc = None
        for e, p in ((e1, p1), (e2, p2), (e3, p3)):
            wgt = e * inv
            term = jnp.where(low, wgt[:, 2 * j:2 * j + 1], wgt[:, 2 * j + 1:2 * j + 2]) * p[j]
            acc = term if acc is None else acc + term
        planes.append(acc)
    out_d = jnp.concatenate(planes, axis=1)
    y = jnp.dot((oc_ref[0] * sz[:, :C_WIDTH]).astype(bf16), w_ref[:C_WIDTH, :], preferred_element_type=f32)
    y = y + jnp.dot((out_d * sz[:, C_WIDTH:]).astype(bf16), w_ref[C_WIDTH:, :], preferred_element_type=f32)
    xn = x_ref[0] + gate_ref[0] * y
    o_ref[0] = xn * lax.rsqrt(jnp.mean(xn * xn, axis=-1, keepdims=True) + EPS) * fg_ref[...]


def _outproj_odd_final(oc, og, lg, x, norm_g, scale, shift, gate, final_g, wz, w):
    bsz, s, d = x.shape
    tm = TM_PROJ
    tok = lambda b, i: (b, i, 0)
    const = lambda b, i: (0, 0)
    bvec = lambda b, i: (b, 0, 0)
    nd = D_HEADS * D_DH
    res_major = lambda width: [pl.BlockSpec((1, dil, tm // dil, width), lambda b, i: (b, 0, i, 0)) for dil in DILATIONS]
    return pl.pallas_call(
        _outproj_odd_kernel,
        grid=(bsz, s // tm),
        in_specs=[pl.BlockSpec((1, tm, C_WIDTH), tok)] + res_major(nd) + res_major(LANES)
        + [pl.BlockSpec((1, tm, d), tok), pl.BlockSpec((1, d), const), pl.BlockSpec((1, 1, d), bvec), pl.BlockSpec((1, 1, d), bvec),
           pl.BlockSpec((1, 1, d), bvec), pl.BlockSpec((1, d), const), pl.BlockSpec(wz.shape, const), pl.BlockSpec(w.shape, const)],
        out_specs=pl.BlockSpec((1, tm, d), tok),
        out_shape=jax.ShapeDtypeStruct((bsz, s, d), jnp.float32),
        scratch_shapes=[pltpu.VMEM((len(DILATIONS), nd // LANES + 1, tm, LANES), jnp.float32)],
        compiler_params=pltpu.CompilerParams(dimension_semantics=("parallel", "parallel"), vmem_limit_bytes=VMEM_LIMIT),
        name="outproj_odd",
    )(oc, *og, *lg, x, norm_g.reshape(1, d), scale, shift, gate, final_g.reshape(1, d), wz, w)


T_CONV = 512
HALO_C = 16
SUB_C = 64


def _conformer_kernel(x_ref, xp_ref, xn_ref, w_ref, b_ref, lg_ref, lb_ref, o_ref, xe_ref, ph_ref):
    i = pl.program_id(1)
    nt = pl.num_programs(1)
    t = x_ref.shape[1]
    xe_ref[0:HALO_C, :] = jnp.where(i > 0, xp_ref[0], 0.0)
    xe_ref[HALO_C:HALO_C + t, :] = x_ref[0]
    xe_ref[HALO_C + t:, :] = jnp.where(i < nt - 1, xn_ref[0], 0.0)
    half = C_CONV // 2
    n = t + 2 * HALO_C
    xe = xe_ref[...]
    for b in range(1, SUBLANES):
        ph_ref[b - 1] = pltpu.roll(xe, n - b, axis=0)
    for r0 in range(0, t, SUB_C):
        acc = None
        for j in range(C_CONV):
            a, b = divmod(HALO_C - half + j, SUBLANES)
            lo = a * SUBLANES + r0
            src = xe_ref[lo:lo + SUB_C, :] if b == 0 else ph_ref[b - 1, lo:lo + SUB_C, :]
            term = src * w_ref[j:j + 1, :]
            acc = term if acc is None else acc + term
        u = acc + b_ref[...]
        uc = u - jnp.mean(u, axis=-1, keepdims=True)
        y = uc * lax.rsqrt(jnp.mean(uc * uc, axis=-1, keepdims=True) + EPS) * lg_ref[...] + lb_ref[...]
        o_ref[0, r0:r0 + SUB_C, :] = y * jax.nn.sigmoid(y)


def _conformer(glu, dw_w, dw_b, ln_g, ln_b):
    bsz, s, cw = glu.shape
    t = min(T_CONV, s)
    hb = t // HALO_C
    cur = lambda b, i: (b, i, 0)
    const = lambda b, i: (0, 0)
    return pl.pallas_call(
        _conformer_kernel,
        grid=(bsz, s // t),
        in_specs=[pl.BlockSpec((1, t, cw), cur),
                  pl.BlockSpec((1, HALO_C, cw), lambda b, i: (b, jnp.maximum(i * hb - 1, 0), 0)),
                  pl.BlockSpec((1, HALO_C, cw), lambda b, i: (b, jnp.minimum((i + 1) * hb, s // HALO_C - 1), 0)),
                  pl.BlockSpec((C_CONV, cw), const)] + [pl.BlockSpec((1, cw), const)] * 3,
        out_specs=pl.BlockSpec((1, t, cw), cur),
        out_shape=jax.ShapeDtypeStruct((bsz, s, cw), jnp.float32),
        scratch_shapes=[pltpu.VMEM((t + 2 * HALO_C, cw), jnp.float32), pltpu.VMEM((SUBLANES - 1, t + 2 * HALO_C, cw), jnp.float32)],
        compiler_params=pltpu.CompilerParams(dimension_semantics=("parallel", "parallel"), vmem_limit_bytes=VMEM_LIMIT),
        name="conformer",
    )(glu, glu, glu, dw_w, dw_b.reshape(1, cw), ln_g.reshape(1, cw), ln_b.reshape(1, cw))


TQ_ATT = 128
TB_ATT = 512
R_ATT = 64


def _dilated_kernel(q_ref, kc_ref, kp_ref, kn_ref, vc_ref, vp_ref, vn_ref, bias_ref, o_ref, lse_ref, kx_ref, vx_ref):
    i = pl.program_id(2)
    nt = pl.num_programs(2)
    tb = q_ref.shape[1]
    tq = TQ_ATT
    nk = tq + 2 * R_ATT
    nsub = tb // tq
    f32, bf16 = jnp.float32, jnp.bfloat16
    kx_ref[0:R_ATT, :] = kp_ref[0]
    kx_ref[R_ATT:R_ATT + tb, :] = kc_ref[0]
    kx_ref[R_ATT + tb:, :] = kn_ref[0]
    vx_ref[0:R_ATT, :] = vp_ref[0]
    vx_ref[R_ATT:R_ATT + tb, :] = vc_ref[0]
    vx_ref[R_ATT + tb:, :] = vn_ref[0]
    kj = lax.broadcasted_iota(jnp.int32, (tq, nk), 1)
    lane = lax.broadcasted_iota(jnp.int32, (tq, 128), 1)
    low = lane < D_DH
    heads = [(pr, hi) for pr in range(D_HEADS // 2) for hi in (False, True)]
    for sub in range(nsub):
        qs = slice(sub * tq, (sub + 1) * tq)
        ks = slice(sub * tq, sub * tq + nk)
        outside = None
        if sub == 0:
            outside = (kj < R_ATT) & (i == 0)
        if sub == nsub - 1:
            after = (kj >= R_ATT + tq) & (i == nt - 1)
            outside = after if outside is None else outside | after
        scs = []
        for pr, hi in heads:
            ps = slice(pr * 128, (pr + 1) * 128)
            qp = q_ref[0, qs, ps]
            qh = jnp.where(low != hi, qp, jnp.zeros_like(qp))
            sc = lax.dot_general(qh, kx_ref[ks, ps], (((1,), (1,)), ((), ())), preferred_element_type=f32) + bias_ref[2 * pr + int(hi)]
            scs.append(sc if outside is None else jnp.where(outside, NEG, sc))
        ms = [jnp.max(sc, axis=-1, keepdims=True) for sc in scs]
        ps_ = [jnp.exp(sc - m) for sc, m in zip(scs, ms)]
        dens = [jnp.sum(p, axis=-1, keepdims=True) for p in ps_]
        pvs = [jnp.dot(p.astype(bf16), vx_ref[ks, pr * 128:(pr + 1) * 128], preferred_element_type=f32) for (pr, _), p in zip(heads, ps_)]
        lse_all = jnp.zeros((tq, 128), f32)
        for pr in range(D_HEADS // 2):
            lo, hi = 2 * pr, 2 * pr + 1
            o_ref[0, qs, pr * 128:(pr + 1) * 128] = jnp.where(low, pvs[lo] / dens[lo], pvs[hi] / dens[hi])
            lse_all = jnp.where(lane == lo, ms[lo] + jnp.log(dens[lo]), lse_all)
            lse_all = jnp.where(lane == hi, ms[hi] + jnp.log(dens[hi]), lse_all)
        lse_ref[0, qs, :] = lse_all


def _dilated_bias(rel_bias, dilation, tq):
    half = REL_BUCKETS // 2
    exact = half // 2
    qi = jnp.arange(tq)[:, None]
    kj = jnp.arange(tq + 2 * R_ATT)[None, :]
    rel = kj - R_ATT - qi
    reld = rel * dilation
    n = jnp.abs(reld)
    large = exact + (jnp.log(jnp.maximum(n, 1).astype(jnp.float32) / exact) / math.log(REL_MAX_DIST / exact) * (half - exact)).astype(jnp.int32)
    large = jnp.minimum(large, half - 1)
    bucket = (reld > 0).astype(jnp.int32) * half + jnp.where(n < exact, n, large)
    bias = jnp.zeros((rel_bias.shape[1],) + bucket.shape, jnp.float32)
    for b in range(REL_BUCKETS):
        bias = jnp.where((bucket == b)[None], rel_bias[b].astype(jnp.float32)[:, None, None], bias)
    return jnp.where((jnp.abs(rel) <= R_ATT)[None], bias, NEG)


def _dilated_group_call(q, k, v, rel_bias, dilation):
    bsz, dil, ls, nd = q.shape
    assert dil == dilation
    assert ls % TB_ATT == 0
    tb, tq = TB_ATT, TQ_ATT
    nt = ls // tb
    hb = tb // R_ATT
    nk = tq + 2 * R_ATT
    cur = lambda b, r, i: (b, r, i, 0)
    prev = lambda b, r, i: (b, r, jnp.maximum(i * hb - 1, 0), 0)
    nxt = lambda b, r, i: (b, r, jnp.minimum((i + 1) * hb, ls // R_ATT - 1), 0)
    kv_specs = [pl.BlockSpec((1, None, tb, nd), cur), pl.BlockSpec((1, None, R_ATT, nd), prev), pl.BlockSpec((1, None, R_ATT, nd), nxt)]
    return pl.pallas_call(
        _dilated_kernel,
        grid=(bsz, dilation, nt),
        in_specs=[pl.BlockSpec((1, None, tb, nd), cur)] + kv_specs + kv_specs + [pl.BlockSpec((D_HEADS, tq, nk), lambda b, r, i: (0, 0, 0))],
        out_specs=[pl.BlockSpec((1, None, tb, nd), cur), pl.BlockSpec((1, None, tb, 128), cur)],
        out_shape=[jax.ShapeDtypeStruct((bsz, dilation, ls, nd), jnp.float32), jax.ShapeDtypeStruct((bsz, dilation, ls, 128), jnp.float32)],
        scratch_shapes=[pltpu.VMEM((tb + 2 * R_ATT, nd), jnp.bfloat16)] * 2,
        compiler_params=pltpu.CompilerParams(dimension_semantics=("parallel", "parallel", "parallel"), vmem_limit_bytes=VMEM_LIMIT),
        name=f"dilated_d{dilation}",
    )(q, k, k, k, v, v, v, _dilated_bias(rel_bias, dilation, tq))


def _inproj(x, g, scale, shift, w_bf16):
    bsz, s, d = x.shape
    n = w_bf16.shape[1]
    return pl.pallas_call(
        _inproj_kernel,
        grid=(bsz, s // TM_PROJ),
        in_specs=[
            pl.BlockSpec((1, TM_PROJ, d), lambda b, i: (b, i, 0)),
            pl.BlockSpec((1, d), lambda b, i: (0, 0)),
            pl.BlockSpec((1, 1, d), lambda b, i: (b, 0, 0)),
            pl.BlockSpec((1, 1, d), lambda b, i: (b, 0, 0)),
            pl.BlockSpec((d, n), lambda b, i: (0, 0)),
        ],
        out_specs=pl.BlockSpec((1, TM_PROJ, n), lambda b, i: (b, i, 0)),
        out_shape=jax.ShapeDtypeStruct((bsz, s, n), jnp.float32),
        compiler_params=pltpu.CompilerParams(dimension_semantics=("parallel", "parallel"), vmem_limit_bytes=VMEM_LIMIT),
        name="inproj",
    )(x, g.reshape(1, d), scale, shift, w_bf16)


def _outproj_kernel(mix_ref, z_ref, x_ref, gate_ref, w_ref, o_ref):
    z = z_ref[0]
    m = mix_ref[0] * (z * jax.nn.sigmoid(z))
    y = jnp.dot(m.astype(jnp.bfloat16), w_ref[...], preferred_element_type=jnp.float32)
    o_ref[0] = x_ref[0] + gate_ref[0] * y


def _outproj(mix, z, x, gate, w_bf16):
    bsz, s, d = x.shape
    k = mix.shape[-1]
    return pl.pallas_call(
        _outproj_kernel,
        grid=(bsz, s // TM_PROJ),
        in_specs=[
            pl.BlockSpec((1, TM_PROJ, k), lambda b, i: (b, i, 0)),
            pl.BlockSpec((1, TM_PROJ, k), lambda b, i: (b, i, 0)),
            pl.BlockSpec((1, TM_PROJ, d), lambda b, i: (b, i, 0)),
            pl.BlockSpec((1, 1, d), lambda b, i: (b, 0, 0)),
            pl.BlockSpec((k, d), lambda b, i: (0, 0)),
        ],
        out_specs=pl.BlockSpec((1, TM_PROJ, d), lambda b, i: (b, i, 0)),
        out_shape=jax.ShapeDtypeStruct((bsz, s, d), jnp.float32),
        compiler_params=pltpu.CompilerParams(dimension_semantics=("parallel", "parallel"), vmem_limit_bytes=VMEM_LIMIT),
        name="outproj",
    )(mix, z, x, gate, w_bf16)


def _final_rms_kernel(x_ref, g_ref, o_ref):
    x = x_ref[0]
    o_ref[0] = x * lax.rsqrt(jnp.mean(x * x, axis=-1, keepdims=True) + EPS) * g_ref[...]


def _final_rms(x, g):
    bsz, s, d = x.shape
    tm = 512
    return pl.pallas_call(
        _final_rms_kernel,
        grid=(bsz, s // tm),
        in_specs=[pl.BlockSpec((1, tm, d), lambda b, i: (b, i, 0)), pl.BlockSpec((1, d), lambda b, i: (0, 0))],
        out_specs=pl.BlockSpec((1, tm, d), lambda b, i: (b, i, 0)),
        out_shape=jax.ShapeDtypeStruct((bsz, s, d), jnp.float32),
        compiler_params=pltpu.CompilerParams(dimension_semantics=("parallel", "parallel")),
        name="final_rms",
    )(x, g.reshape(1, d))


L_MLSTM = 256
_HI = lax.Precision.HIGHEST


def _log_sigmoid(t):
    return jnp.minimum(t, 0.0) - jnp.log(1.0 + jnp.exp(-jnp.abs(t)))


def _mlstm_kernel(qf_ref, kf_ref, vf_ref, gf_ref, gtf_ref, qb_ref, kb_ref, vb_ref, gb_ref, gtb_ref,
                  bias_ref, biast_ref, hf_ref, hb_ref, c_ref, m_ref):
    n = pl.program_id(1)
    ln = qf_ref.shape[1]
    f32, bf16 = jnp.float32, jnp.bfloat16

    @pl.when(n == 0)
    def _():
        c_ref[...] = jnp.zeros_like(c_ref)
        m_ref[...] = jnp.zeros_like(m_ref)

    row = lax.broadcasted_iota(jnp.int32, (ln, ln), 0)
    col = lax.broadcasted_iota(jnp.int32, (ln, ln), 1)
    ones_blk = jnp.ones((ln, A_DV), bf16)
    dirs = ((0, qf_ref, kf_ref, vf_ref, gf_ref, gtf_ref, hf_ref), (1, qb_ref, kb_ref, vb_ref, gb_ref, gtb_ref, hb_ref))
    probs = []
    for d, q_ref, k_ref, v_ref, g_ref, gt_ref, h_ref in dirs:
        mask = (row >= col) if d == 0 else (row <= col)
        tri = mask.astype(f32)
        tri_t = ((row <= col) if d == 0 else (row >= col)).astype(f32)
        g = g_ref[0] + bias_ref[...]
        gt = gt_ref[0] + biast_ref[...]
        ic = g[:, 4 * d:4 * d + 4]
        it = gt[4 * d:4 * d + 4, :]
        bc = jnp.dot(tri, _log_sigmoid(g[:, 8 + 4 * d:12 + 4 * d]), precision=_HI, preferred_element_type=f32)
        bt = jnp.dot(_log_sigmoid(gt[8 + 4 * d:12 + 4 * d, :]), tri_t, precision=_HI, preferred_element_type=f32)
        for h in range(A_HEADS):
            probs.append(dict(
                r=d * A_HEADS + h, h=h, mask=mask, last=ln - 1 if d == 0 else 0, h_ref=h_ref,
                q=q_ref[0, :, h * A_DK:(h + 1) * A_DK], k=k_ref[0, :, h * A_DK:(h + 1) * A_DK],
                vaug=jnp.concatenate([v_ref[0, :, h * A_DV:(h + 1) * A_DV], ones_blk], axis=1),
                bcol=bc[:, h:h + 1], icol=ic[:, h:h + 1], brow=bt[h:h + 1, :], irow=it[h:h + 1, :]))
    for p in probs:
        p["m_old"] = m_ref[p["r"]:p["r"] + 1, 0:1]
        p["caug"] = c_ref[p["r"]]
        p["qk"] = lax.dot_general(p["q"], p["k"], (((1,), (1,)), ((), ())), preferred_element_type=f32)
    for p in probs:
        p["qc"] = jnp.dot(p["q"], p["caug"].astype(bf16), preferred_element_type=f32)
    for p in probs:
        dmat = jnp.where(p["mask"], p["bcol"] - p["brow"] + p["irow"], -jnp.inf)
        inter = p["bcol"] + p["m_old"]
        mt = jnp.maximum(inter, jnp.max(dmat, axis=-1, keepdims=True))
        p["mt"], p["w_int"] = mt, jnp.exp(inter - mt)
        p["sc"] = (jnp.exp(dmat - mt) * p["qk"]).astype(bf16)
    for p in probs:
        tot = p["w_int"] * p["qc"] + jnp.dot(p["sc"], p["vaug"], preferred_element_type=f32)
        den = jnp.maximum(jnp.abs(tot[:, A_DV:]), jnp.exp(-p["mt"]))
        p["h_ref"][0, :, p["h"] * A_DV:(p["h"] + 1) * A_DV] = tot[:, :A_DV] / den
    for p in probs:
        last, bcol, brow = p["last"], p["bcol"], p["brow"]
        btot_c = bcol[last:last + 1, :]
        btot_r = brow[:, last:last + 1]
        m_new = jnp.maximum(btot_r + p["m_old"], jnp.max(btot_r - brow + p["irow"], axis=-1, keepdims=True))
        w_old = jnp.exp(btot_r + p["m_old"] - m_new)
        kw = (p["k"].astype(f32) * jnp.exp(btot_c - bcol + p["icol"] - m_new)).astype(bf16)
        c_ref[p["r"]] = w_old * p["caug"] + lax.dot_general(kw, p["vaug"], (((0,), (0,)), ((), ())), preferred_element_type=f32)
        m_ref[p["r"]:p["r"] + 1, :] = jnp.broadcast_to(m_new, (1, m_ref.shape[1]))


def _mlstm(q, k, v, g, gt, bias):
    bsz, s, _ = q.shape
    ln = min(L_MLSTM, s)
    nc = s // ln
    hk, hv = A_HEADS * A_DK, A_HEADS * A_DV
    fwd = lambda b, n: (b, n, 0)
    bwd = lambda b, n: (b, nc - 1 - n, 0)
    fwd_t = lambda b, n: (b, 0, n)
    bwd_t = lambda b, n: (b, 0, nc - 1 - n)
    const = lambda b, n: (0, 0)
    def specs(im, im_t):
        return [pl.BlockSpec((1, ln, hk), im), pl.BlockSpec((1, ln, hk), im), pl.BlockSpec((1, ln, hv), im),
                pl.BlockSpec((1, ln, 16), im), pl.BlockSpec((1, 16, ln), im_t)]
    return pl.pallas_call(
        _mlstm_kernel,
        grid=(bsz, nc),
        in_specs=specs(fwd, fwd_t) + specs(bwd, bwd_t) + [pl.BlockSpec((1, 16), const), pl.BlockSpec((16, 1), const)],
        out_specs=[pl.BlockSpec((1, ln, hv), fwd), pl.BlockSpec((1, ln, hv), bwd)],
        out_shape=[jax.ShapeDtypeStruct((bsz, s, hv), jnp.float32)] * 2,
        scratch_shapes=[pltpu.VMEM((2 * A_HEADS, A_DK, 2 * A_DV), jnp.float32), pltpu.VMEM((2 * A_HEADS, 128), jnp.float32)],
        compiler_params=pltpu.CompilerParams(dimension_semantics=("parallel", "arbitrary"), vmem_limit_bytes=VMEM_LIMIT),
        name="mlstm",
    )(q, k, v, g, gt, q, k, v, g, gt, bias.reshape(1, 16), bias.reshape(16, 1))


T_GDN = 256
T_GDN_STEP = 256
C_GDN = 64
HALO = 8


def _softplus(t):
    return jnp.maximum(t, 0.0) + jnp.log1p(jnp.exp(-jnp.abs(t)))


def _gdn_prep_kernel(x_ref, xp_ref, xn_ref, g_ref, gt_ref, w_ref, a_ref, at_ref, dt_ref, dtt_ref,
                     q_ref, k_ref, v_ref, gc_ref, gr_ref, xe_ref):
    i = pl.program_id(1)
    nt = pl.num_programs(1)
    t = x_ref.shape[1]
    f32 = jnp.float32
    hd = B_HEADS * B_DK
    xe_ref[0:HALO, :] = jnp.where(i > 0, xp_ref[0], 0.0)
    xe_ref[HALO:HALO + t, :] = x_ref[0]
    xe_ref[HALO + t:, :] = jnp.where(i < nt - 1, xn_ref[0], 0.0)
    half = B_CONV // 2
    for part, o_ref in enumerate((q_ref, k_ref, v_ref)):
        cs = slice(part * hd, (part + 1) * hd)
        xe = xe_ref[:, cs]
        acc = None
        for j in range(B_CONV):
            off = HALO - half + j
            shifted = xe[off:off + t] if off % SUBLANES == 0 else pltpu.roll(xe, t + 2 * HALO - off, axis=0)[0:t]
            term = shifted * w_ref[j:j + 1, cs]
            acc = term if acc is None else acc + term
        y = acc * jax.nn.sigmoid(acc)
        for h in range(B_HEADS):
            yh = y[:, h * B_DK:(h + 1) * B_DK]
            if part == 0:
                yh = yh * lax.rsqrt(jnp.sum(yh * yh, axis=-1, keepdims=True) + EPS) * (B_DK ** -0.5)
            elif part == 1:
                yh = yh * lax.rsqrt(jnp.sum(yh * yh, axis=-1, keepdims=True) + EPS)
            o_ref[0, :, h * B_DK:(h + 1) * B_DK] = yh.astype(o_ref.dtype)
    row = lax.broadcasted_iota(jnp.int32, (t, t), 0)
    col = lax.broadcasted_iota(jnp.int32, (t, t), 1)
    same = (row // C_GDN) == (col // C_GDN)
    lower = (same & (row >= col)).astype(f32)
    upper = (same & (row <= col)).astype(f32)
    g = g_ref[0]
    gt = gt_ref[0]
    nh = B_HEADS
    dec = -jnp.exp(a_ref[...]) * _softplus(g[:, 2 * nh:] + dt_ref[...])
    dect = -jnp.exp(at_ref[...]) * _softplus(gt[2 * nh:, :] + dtt_ref[...])
    gc_ref[0, :, 0:2 * nh] = jax.nn.sigmoid(g[:, 0:2 * nh])
    gc_ref[0, :, 2 * nh:3 * nh] = jnp.dot(lower, dec[:, 0:nh], precision=_HI, preferred_element_type=f32)
    gc_ref[0, :, 3 * nh:] = jnp.dot(upper, dec[:, nh:], precision=_HI, preferred_element_type=f32)
    gr_ref[0, 0:2 * nh, :] = jax.nn.sigmoid(gt[0:2 * nh, :])
    gr_ref[0, 2 * nh:3 * nh, :] = jnp.dot(dect[0:nh, :], upper, precision=_HI, preferred_element_type=f32)
    gr_ref[0, 3 * nh:, :] = jnp.dot(dect[nh:, :], lower, precision=_HI, preferred_element_type=f32)


def _gdn_prep(dqkv, g, gt, conv_w, a_log, dt_bias):
    bsz, s, n3 = dqkv.shape
    t = min(T_GDN, s)
    nt = s // t
    hd = B_HEADS * B_DK
    hb = t // HALO
    cur = lambda b, i: (b, i, 0)
    const = lambda b, i: (0, 0)
    bf16 = jnp.bfloat16
    return pl.pallas_call(
        _gdn_prep_kernel,
        grid=(bsz, nt),
        in_specs=[
            pl.BlockSpec((1, t, n3), cur),
            pl.BlockSpec((1, HALO, n3), lambda b, i: (b, jnp.maximum(i * hb - 1, 0), 0)),
            pl.BlockSpec((1, HALO, n3), lambda b, i: (b, jnp.minimum((i + 1) * hb, s // HALO - 1), 0)),
            pl.BlockSpec((1, t, 16), cur),
            pl.BlockSpec((1, 16, t), lambda b, i: (b, 0, i)),
            pl.BlockSpec((B_CONV, n3), const),
            pl.BlockSpec((1, 8), const), pl.BlockSpec((8, 1), const),
            pl.BlockSpec((1, 8), const), pl.BlockSpec((8, 1), const),
        ],
        out_specs=[pl.BlockSpec((1, t, hd), cur)] * 3 + [pl.BlockSpec((1, t, 16), cur), pl.BlockSpec((1, 16, t), lambda b, i: (b, 0, i))],
        out_shape=[jax.ShapeDtypeStruct((bsz, s, hd), bf16)] * 3 + [jax.ShapeDtypeStruct((bsz, s, 16), jnp.float32), jax.ShapeDtypeStruct((bsz, 16, s), jnp.float32)],
        scratch_shapes=[pltpu.VMEM((t + 2 * HALO, n3), jnp.float32)],
        compiler_params=pltpu.CompilerParams(dimension_semantics=("parallel", "parallel"), vmem_limit_bytes=VMEM_LIMIT),
        name="gdn_prep",
    )(dqkv, dqkv, dqkv, g, gt, conv_w, a_log.reshape(1, 8), a_log.reshape(8, 1), dt_bias.reshape(1, 8), dt_bias.reshape(8, 1))


def _tri_inverse_many(a_list, masks):
    eye, m16, m32, m64 = masks
    f32, bf16 = jnp.float32, jnp.bfloat16
    mm = lambda x, y: jnp.dot(x.astype(bf16), y.astype(bf16), preferred_element_type=f32)
    ads = [jnp.where(m16, a, 0.0) for a in a_list]
    xs = [eye - ad for ad in ads]
    ps = [mm(ad, ad) for ad in ads]
    for stage in range(3):
        xs = [x + mm(x, p) for x, p in zip(xs, ps)]
        if stage < 2:
            ps = [mm(p, p) for p in ps]
    for lo, hi in ((m16, m32), (m32, m64)):
        off = hi & ~lo
        ys = [mm(jnp.where(off, a, 0.0), x) for a, x in zip(a_list, xs)]
        xs = [x - mm(x, y) for x, y in zip(xs, ys)]
    return xs


def _gdn_kernel(qf_ref, kf_ref, vf_ref, gcf_ref, grf_ref, qb_ref, kb_ref, vb_ref, gcb_ref, grb_ref, of_ref, ob_ref, s_ref):
    n = pl.program_id(1)
    t = qf_ref.shape[1]
    c = C_GDN
    f32, bf16 = jnp.float32, jnp.bfloat16

    @pl.when(n == 0)
    def _():
        s_ref[...] = jnp.zeros_like(s_ref)

    row = lax.broadcasted_iota(jnp.int32, (c, c), 0)
    col = lax.broadcasted_iota(jnp.int32, (c, c), 1)
    eye = (row == col).astype(f32)
    blk = lambda w: (row // w) == (col // w)
    masks = (eye, blk(16), blk(32), blk(64))
    nh, nchunk = B_HEADS, t // c
    dir_refs = ((qf_ref, kf_ref, vf_ref, gcf_ref, grf_ref, of_ref), (qb_ref, kb_ref, vb_ref, gcb_ref, grb_ref, ob_ref))
    probs = [(d, h, ci) for d in range(2) for h in range(nh) for ci in range(nchunk)]
    xpose = (((1,), (1,)), ((), ()))

    def load(d, h, ci):
        q_ref, k_ref, v_ref, gc_ref, gr_ref, _ = dir_refs[d]
        rs, cs = slice(ci * c, (ci + 1) * c), slice(h * B_DK, (h + 1) * B_DK)
        beta = gc_ref[0, rs, d * nh + h:d * nh + h + 1]
        gcol = gc_ref[0, rs, (2 + d) * nh + h:(2 + d) * nh + h + 1]
        grow = gr_ref[0, (2 + d) * nh + h:(2 + d) * nh + h + 1, rs]
        return q_ref[0, rs, cs], k_ref[0, rs, cs], v_ref[0, rs, cs], beta, gcol, grow

    data = [load(*p) for p in probs]
    gams = []
    for (d, _, _), (_, _, _, _, gcol, grow) in zip(probs, data):
        incl = (row >= col) if d == 0 else (row <= col)
        gams.append(jnp.exp(jnp.where(incl, gcol - grow, -jnp.inf)))
    kks = [lax.dot_general(k, k, xpose, preferred_element_type=f32) for (_, k, _, _, _, _) in data]
    qks = [lax.dot_general(q, k, xpose, preferred_element_type=f32) for (q, k, _, _, _, _) in data]
    a_list = []
    for (d, _, _), (_, _, _, beta, _, _), kk, gam in zip(probs, data, kks, gams):
        strict = (row > col) if d == 0 else (row < col)
        a_list.append(jnp.where(strict, beta * kk * gam, 0.0))
    tinvs = _tri_inverse_many(a_list, masks)
    egcs = [jnp.exp(gcol) for (_, _, _, _, gcol, _) in data]
    uws = []
    for (q, k, v, beta, gcol, _), tinv, egc in zip(data, tinvs, egcs):
        rhs = jnp.concatenate([beta * v.astype(f32), (beta * egc) * k.astype(f32)], axis=1).astype(bf16)
        uws.append(jnp.dot(tinv.astype(bf16), rhs, preferred_element_type=f32))
    attns = [(qk * gam).astype(bf16) for qk, gam in zip(qks, gams)]
    index = {p: i for i, p in enumerate(probs)}
    chains = [(d, h) for d in range(2) for h in range(nh)]
    states = [s_ref[d * nh + h] for d, h in chains]
    for step in range(nchunk):
        ids = [index[(d, h, step if d == 0 else nchunk - 1 - step)] for d, h in chains]
        wss = []
        for i, state in zip(ids, states):
            q, _, _, _, _, _ = data[i]
            wq = jnp.concatenate([uws[i][:, B_DV:], q.astype(f32) * egcs[i]], axis=0).astype(bf16)
            wss.append(jnp.dot(wq, state.astype(bf16), preferred_element_type=f32))
        v_news = [(uws[i][:, :B_DV] - ws[:c]).astype(bf16) for i, ws in zip(ids, wss)]
        for (d, h), i, ws, v_new in zip(chains, ids, wss, v_news):
            ci = probs[i][2]
            dir_refs[d][5][0, ci * c:(ci + 1) * c, h * B_DV:(h + 1) * B_DV] = ws[c:] + jnp.dot(attns[i], v_new, preferred_element_type=f32)
        new_states = []
        for (d, h), i, state, v_new in zip(chains, ids, states, v_news):
            _, k, _, _, gcol, _ = data[i]
            last = c - 1 if d == 0 else 0
            gl = gcol[last:last + 1, :]
            kd = (k.astype(f32) * jnp.exp(gl - gcol)).astype(bf16)
            new_states.append(jnp.exp(gl) * state + lax.dot_general(kd, v_new, (((0,), (0,)), ((), ())), preferred_element_type=f32))
        states = new_states
    for (d, h), state in zip(chains, states):
        s_ref[d * nh + h] = state


def _gdn(q, k, v, gc, gr):
    bsz, s, hd = q.shape
    t = min(T_GDN_STEP, s)
    nb = s // t
    fwd = lambda b, n: (b, n, 0)
    bwd = lambda b, n: (b, nb - 1 - n, 0)
    def specs(im, im_t):
        return [pl.BlockSpec((1, t, hd), im)] * 3 + [pl.BlockSpec((1, t, 16), im), pl.BlockSpec((1, 16, t), im_t)]
    return pl.pallas_call(
        _gdn_kernel,
        grid=(bsz, nb),
        in_specs=specs(fwd, lambda b, n: (b, 0, n)) + specs(bwd, lambda b, n: (b, 0, nb - 1 - n)),
        out_specs=[pl.BlockSpec((1, t, hd), fwd), pl.BlockSpec((1, t, hd), bwd)],
        out_shape=[jax.ShapeDtypeStruct((bsz, s, hd), jnp.float32)] * 2,
        scratch_shapes=[pltpu.VMEM((2 * B_HEADS, B_DK, B_DV), jnp.float32)],
        compiler_params=pltpu.CompilerParams(dimension_semantics=("parallel", "arbitrary"), vmem_limit_bytes=VMEM_LIMIT),
        name="gdn",
    )(q, k, v, gc, gr, q, k, v, gc, gr)


def _split(p, sizes):
    return jnp.split(p, np.cumsum(sizes)[:-1].tolist(), axis=-1)


def _layernorm(x, g, b):
    xc = x - jnp.mean(x, axis=-1, keepdims=True)
    y = xc * lax.rsqrt(jnp.mean(xc * xc, axis=-1, keepdims=True) + EPS)
    return y * g + b


def _head_rms(t, g):
    bsz, s, h, d = t.shape
    y = t * lax.rsqrt(jnp.mean(t * t, axis=-1, keepdims=True) + EPS)
    return y.reshape(bsz, s, h * d) * g


def _l2n(t):
    return t * lax.rsqrt(jnp.sum(t * t, axis=-1, keepdims=True) + EPS)


def _dwconv(x, w):
    return lax.conv_general_dilated(x, w[:, None, :].astype(x.dtype), window_strides=(1,), padding='SAME', dimension_numbers=('NWC', 'WIO', 'NWC'), feature_group_count=x.shape[-1])


def _flip(t):
    return jnp.flip(t, axis=1)


def _to_chunks(t):
    bsz, s, h = t.shape[:3]
    t = t.reshape((bsz, s // CHUNK, CHUNK, h) + t.shape[3:])
    return jnp.moveaxis(t, (1, 3), (0, 2))


def _from_chunks(t):
    nc, bsz, h, l = t.shape[:4]
    t = jnp.moveaxis(t, (0, 2), (1, 3))
    return t.reshape((bsz, nc * l, h) + t.shape[4:])


def _mlstm_chunkwise(q, k, v, i_pre, logf):
    q, k, v, i_pre, logf = (_to_chunks(t) for t in (q, k, v, i_pre, logf))
    nc, bsz, h = q.shape[:3]
    causal = jnp.tril(jnp.ones((CHUNK, CHUNK), dtype=bool))
    b = jnp.cumsum(logf, axis=-1)
    dmat = jnp.where(causal, b[..., :, None] - b[..., None, :] + i_pre[..., None, :], -jnp.inf)
    dmax = jnp.max(dmat, axis=-1)
    qk = jnp.einsum('nbhld,nbhsd->nbhls', q, k)
    a_end = b[..., -1:] - b + i_pre

    def step(carry, xs):
        cmat, nvec, m = carry
        qc, kc, vc, bc, dc, dmc, qkc, aec = xs
        inter = bc + m[..., None]
        mt = jnp.maximum(inter, dmc)
        w_int = jnp.exp(inter - mt)
        sc = jnp.exp(dc - mt[..., None]) * qkc
        num = w_int[..., None] * jnp.einsum('bhld,bhde->bhle', qc, cmat) + jnp.einsum('bhls,bhse->bhle', sc, vc)
        den = w_int * jnp.einsum('bhld,bhd->bhl', qc, nvec) + jnp.sum(sc, axis=-1)
        hc = num / jnp.maximum(jnp.abs(den), jnp.exp(-mt))[..., None]
        m_new = jnp.maximum(bc[..., -1] + m, jnp.max(aec, axis=-1))
        w_old = jnp.exp(bc[..., -1] + m - m_new)
        kw = kc * jnp.exp(aec - m_new[..., None])[..., None]
        cmat = w_old[..., None, None] * cmat + jnp.einsum('bhld,bhle->bhde', kw, vc)
        nvec = w_old[..., None] * nvec + jnp.sum(kw, axis=-2)
        return (cmat, nvec, m_new), hc

    init = (jnp.zeros((bsz, h, A_DK, A_DV), jnp.float32), jnp.zeros((bsz, h, A_DK), jnp.float32), jnp.zeros((bsz, h), jnp.float32))
    _, hs = lax.scan(step, init, (q, k, v, b, dmat, dmax, qk, a_end))
    return _from_chunks(hs)


def _gdn_chunked(q, k, v, beta, g):
    q, k, v, beta, g = (_to_chunks(t) for t in (q, k, v, beta, g))
    nc, bsz, h = q.shape[:3]
    tril = jnp.tril(jnp.ones((CHUNK, CHUNK), dtype=bool))
    strict = jnp.tril(jnp.ones((CHUNK, CHUNK), dtype=bool), -1)
    gc = jnp.cumsum(g, axis=-1)
    gam = jnp.exp(jnp.where(tril, gc[..., :, None] - gc[..., None, :], -jnp.inf))
    a = jnp.where(strict, beta[..., :, None] * jnp.einsum('nbhid,nbhjd->nbhij', k, k) * gam, 0.0)
    tmat = a + jnp.eye(CHUNK, dtype=a.dtype)
    u = lax.linalg.triangular_solve(tmat, beta[..., None] * v, left_side=True, lower=True, unit_diagonal=True)
    w = lax.linalg.triangular_solve(tmat, (beta * jnp.exp(gc))[..., None] * k, left_side=True, lower=True, unit_diagonal=True)
    attn = jnp.einsum('nbhid,nbhjd->nbhij', q, k) * gam

    def step(state, xs):
        qc, kc, uc, wc, gcc, ac = xs
        v_new = uc - jnp.einsum('bhld,bhde->bhle', wc, state)
        o = jnp.einsum('bhld,bhde->bhle', qc * jnp.exp(gcc)[..., None], state) + jnp.einsum('bhls,bhse->bhle', ac, v_new)
        gl = gcc[..., -1]
        state = jnp.exp(gl)[..., None, None] * state + jnp.einsum('bhld,bhle->bhde', kc * jnp.exp(gl[..., None] - gcc)[..., None], v_new)
        return state, o

    _, os_ = lax.scan(step, jnp.zeros((bsz, h, B_DK, B_DV), jnp.float32), (q, k, u, w, gc, attn))
    return _from_chunks(os_)


def _t5_bucket(rel):
    half = REL_BUCKETS // 2
    exact = half // 2
    n = jnp.abs(rel)
    large = exact + (jnp.log(jnp.maximum(n, 1).astype(jnp.float32) / exact) / math.log(REL_MAX_DIST / exact) * (half - exact)).astype(jnp.int32)
    large = jnp.minimum(large, half - 1)
    return (rel > 0).astype(jnp.int32) * half + jnp.where(n < exact, n, large)


def _dilated_group(q, k, v, dilation, radius, rel_bias):
    bsz, s, h, dh = q.shape
    ls = s // dilation
    nb = -(-ls // radius)
    lp = nb * radius

    def sub(t, lo, hi):
        t = t.reshape(bsz, ls, dilation, h, dh).transpose(0, 3, 2, 1, 4)
        return jnp.pad(t, ((0, 0), (0, 0), (0, 0), (lo, hi), (0, 0)))

    qb = sub(q, 0, lp - ls).reshape(bsz, h, dilation, nb, radius, dh)

    def band(t):
        t = sub(t, radius, lp - ls + radius).reshape(bsz, h, dilation, nb + 2, radius, dh)
        return jnp.concatenate([t[:, :, :, :-2], t[:, :, :, 1:-1], t[:, :, :, 2:]], axis=4)

    kb, vb = band(k), band(v)
    qi = jnp.arange(radius)[:, None]
    kj = jnp.arange(3 * radius)[None, :]
    rel = kj - radius - qi
    kpos = jnp.arange(nb)[:, None, None] * radius + kj - radius
    valid = (jnp.abs(rel) <= radius) & (kpos >= 0) & (kpos < ls)
    bias = jnp.transpose(rel_bias[_t5_bucket(rel * dilation)], (2, 0, 1)).astype(jnp.float32)
    sc = jnp.einsum('bhrnid,bhrnjd->bhrnij', qb, kb).astype(jnp.float32) * (dh ** -0.5) + bias[:, None, None]
    sc = jnp.where(valid, sc, NEG)
    m = jnp.max(sc, axis=-1, keepdims=True)
    p = jnp.exp(sc - m)
    den = jnp.sum(p, axis=-1)
    o = jnp.einsum('bhrnij,bhrnjd->bhrnid', p, vb.astype(jnp.float32)) / den[..., None]
    lse = m[..., 0] + jnp.log(den)
    o = o.reshape(bsz, h, dilation, lp, dh)[:, :, :, :ls].transpose(0, 3, 2, 1, 4).reshape(bsz, s, h, dh)
    lse = lse.reshape(bsz, h, dilation, lp)[:, :, :, :ls].transpose(0, 3, 2, 1).reshape(bsz, s, h)
    return o, lse


def _dilated_attention(q, k, v, rel_bias):
    outs, lses = [], []
    for window, dilation in D_GROUPS:
        o, l = _dilated_group(q, k, v, dilation, window // (2 * dilation), rel_bias)
        outs.append(o)
        lses.append(l)
    wts = jax.nn.softmax(jnp.stack(lses, axis=0), axis=0)
    return jnp.sum(wts[..., None] * jnp.stack(outs, axis=0), axis=0)


def _even_mixer_core(p, m_gate_b, dn_dt_bias, dn_a_log, dn_conv_w, m_norm_g, dn_norm_g):
    bsz, s, _ = p.shape
    f32 = jnp.float32
    mq, mk, mv, mo, mg, dqkv, dg, z = _split(p, EVEN_SPLITS)
    q = mq.reshape(bsz, s, A_HEADS, A_DK)
    k = mk.reshape(bsz, s, A_HEADS, A_DK) * (A_DK ** -0.5)
    v = mv.reshape(bsz, s, A_HEADS, A_DV)
    gt = mg.reshape(bsz, s, 4, A_HEADS) + m_gate_b
    logf = jax.nn.log_sigmoid(gt[:, :, 2:4])
    h_fwd = _mlstm_chunkwise(q, k, v, gt[:, :, 0], logf[:, :, 0])
    h_bwd = _flip(_mlstm_chunkwise(_flip(q), _flip(k), _flip(v), _flip(gt[:, :, 1]), _flip(logf[:, :, 1])))
    out_a = jax.nn.sigmoid(mo) * _head_rms(h_fwd + h_bwd, m_norm_g)
    qkv = jax.nn.silu(_dwconv(dqkv, dn_conv_w))
    bq, bk, bv = _split(qkv, (B_HEADS * B_DK, B_HEADS * B_DK, B_HEADS * B_DV))
    q = _l2n(bq.reshape(bsz, s, B_HEADS, B_DK)) * (B_DK ** -0.5)
    k = _l2n(bk.reshape(bsz, s, B_HEADS, B_DK))
    v = bv.reshape(bsz, s, B_HEADS, B_DV)
    gb = dg.reshape(bsz, s, 4, B_HEADS)
    beta = jax.nn.sigmoid(gb[:, :, 0:2])
    decay = -jnp.exp(dn_a_log) * jax.nn.softplus(gb[:, :, 2:4] + dn_dt_bias)
    o_fwd = _gdn_chunked(q, k, v, beta[:, :, 0], decay[:, :, 0])
    o_bwd = _flip(_gdn_chunked(_flip(q), _flip(k), _flip(v), _flip(beta[:, :, 1]), _flip(decay[:, :, 1])))
    out_b = _head_rms(o_fwd + o_bwd, dn_norm_g)
    return jnp.concatenate([out_a, out_b], axis=-1), z


def _odd_mixer_core(p, dw_w, dw_b, ln_g, ln_b, rel_bias):
    bsz, s, _ = p.shape
    ga, gb, aq, ak, av, z = _split(p, ODD_SPLITS)
    u = _dwconv(ga * jax.nn.sigmoid(gb), dw_w) + dw_b
    out_c = jax.nn.silu(_layernorm(u, ln_g, ln_b))
    shp = (bsz, s, D_HEADS, D_DH)
    out_d = _dilated_attention(aq.reshape(shp), ak.reshape(shp), av.reshape(shp), rel_bias).reshape(bsz, s, D_HEADS * D_DH)
    return jnp.concatenate([out_c, out_d], axis=-1), z


def kernel(x, c, norm_g, ada_w, ada_b, ev_w_in, ev_m_gate_b, ev_dn_dt_bias, ev_dn_a_log, ev_dn_conv_w, ev_m_norm_g, ev_dn_norm_g, ev_w_out, od_w_in, od_dw_w, od_dw_b, od_ln_g, od_ln_b, od_w_out, rel_bias, final_g):
    assert DEPTH == 2, "the final RMSNorm is fused into the (last) odd layer's output projection"
    assert all(window // (2 * dil) == R_ATT for window, dil in D_GROUPS)
    d = x.shape[-1]
    mod = _adaln(c, ada_w, ada_b)
    for layer in range(DEPTH):
        shift, scale, gate = (mod[layer, :, i * d:(i + 1) * d][:, None, :] for i in range(3))
        j = layer // 2
        if layer % 2 == 0:
            x = _even_layer(x, norm_g[layer], scale, shift, gate, ev_w_in[j], ev_m_gate_b[j], ev_dn_dt_bias[j], ev_dn_a_log[j],
                            ev_dn_conv_w[j], ev_m_norm_g[j], ev_dn_norm_g[j], ev_w_out[j])
        else:
            x = _odd_layer_final(x, norm_g[layer], scale, shift, gate, od_w_in[j], od_dw_w[j], od_dw_b[j], od_ln_g[j], od_ln_b[j],
                                 rel_bias, od_w_out[j], final_g)
    return x


def _even_layer(x, norm_g, scale, shift, gate, w_in, m_gate_b, dn_dt_bias, dn_a_log, dn_conv_w, m_norm_g, dn_norm_g, w_out):
    bf16 = jnp.bfloat16
    mq, mk, mv, mo, mg, dqkv, dg, z = _split(w_in, EVEN_SPLITS)
    w = jnp.concatenate([mq, mk * (A_DK ** -0.5), mv, dqkv], axis=1).astype(bf16)
    wg = jnp.concatenate([mg, dg], axis=1).astype(bf16)
    wz = jnp.concatenate([mo, z], axis=1).astype(bf16)
    pq, pk, pv, bq, bk, bv, g_m, gt_m, gc, gr = _inproj_even(x, norm_g, scale, shift, w, wg.T, dn_conv_w, dn_a_log, dn_dt_bias)
    hf, hb = _mlstm(pq, pk, pv, g_m, gt_m, m_gate_b.reshape(16))
    of, ob = _gdn(bq, bk, bv, gc, gr)
    return _outproj_even(hf, hb, of, ob, x, norm_g, scale, shift, gate, m_norm_g, dn_norm_g, wz, w_out.astype(bf16))


def _odd_layer_final(x, norm_g, scale, shift, gate, w_in, dw_w, dw_b, ln_g, ln_b, rel_bias, w_out, final_g):
    bf16 = jnp.bfloat16
    ga, gb, aq, ak, av, z = _split(w_in, ODD_SPLITS)
    w = jnp.concatenate([ga, gb, aq * (D_DH ** -0.5), ak, av], axis=1).astype(bf16)
    out_c, pq, pk, pv = _inproj_odd(x, norm_g, scale, shift, w, dw_w, dw_b, ln_g, ln_b)
    og, lg = zip(*[_dilated_group_call(qd, kd, vd, rel_bias, dil) for qd, kd, vd, dil in zip(pq, pk, pv, DILATIONS)])
    return _outproj_odd_final(out_c, og, lg, x, norm_g, scale, shift, gate, final_g, z.astype(bf16), w_out.astype(bf16))
```

```python
import math
from functools import partial

import jax
import jax.numpy as jnp
import numpy as np
from jax import lax
from jax.experimental import pallas as pl
from jax.experimental.pallas import tpu as pltpu

D_MODEL = 1024
BATCH = 4
SEQ = 8192
DEPTH = 2
A_HEADS = 4
A_DK = 64
A_DV = 128
B_HEADS = 4
B_DK = 128
B_DV = 128
B_CONV = 5
C_WIDTH = 512
C_CONV = 31
D_HEADS = 8
D_DH = 64
D_GROUPS = ((128, 1), (512, 4), (2048, 16))
REL_BUCKETS = 32
REL_MAX_DIST = 1024
CHUNK = 64
EPS = 1e-6
NEG = -1e30
MIX_EVEN = A_HEADS * A_DV + B_HEADS * B_DV
MIX_ODD = C_WIDTH + D_HEADS * D_DH
B_QKV = B_HEADS * (2 * B_DK + B_DV)
EVEN_SPLITS = (A_HEADS * A_DK, A_HEADS * A_DK, A_HEADS * A_DV, A_HEADS * A_DV, 4 * A_HEADS, B_QKV, 4 * B_HEADS, MIX_EVEN)
ODD_SPLITS = (C_WIDTH, C_WIDTH, D_HEADS * D_DH, D_HEADS * D_DH, D_HEADS * D_DH, MIX_ODD)

VMEM_LIMIT = 56 * 1024 * 1024
TM_PROJ = 512


def _adaln_kernel(c_ref, w_ref, b_ref, o_ref):
    c = c_ref[...]
    cs = (c * jax.nn.sigmoid(c)).astype(jnp.bfloat16)
    o_ref[0] = jnp.dot(cs, w_ref[0].astype(jnp.bfloat16), preferred_element_type=jnp.float32) + b_ref[0]


def _adaln(c, ada_w, ada_b):
    depth, d, n3 = ada_w.shape
    bsz = c.shape[0]
    tn = 1024
    return pl.pallas_call(
        _adaln_kernel,
        grid=(depth, n3 // tn),
        in_specs=[pl.BlockSpec((bsz, d), lambda l, j: (0, 0)), pl.BlockSpec((1, d, tn), lambda l, j: (l, 0, j)),
                  pl.BlockSpec((1, 1, tn), lambda l, j: (l, 0, j))],
        out_specs=pl.BlockSpec((1, bsz, tn), lambda l, j: (l, 0, j)),
        out_shape=jax.ShapeDtypeStruct((depth, bsz, n3), jnp.float32),
        compiler_params=pltpu.CompilerParams(dimension_semantics=("parallel", "parallel")),
        name="adaln",
    )(c, ada_w, ada_b.reshape(depth, 1, n3))


def _modulated_rms_val(x, g, scale, shift):
    y = x * lax.rsqrt(jnp.mean(x * x, axis=-1, keepdims=True) + EPS)
    return ((y * g) * (1.0 + scale) + shift).astype(jnp.bfloat16)


def _modulated_rms(x_ref, g_ref, sc_ref, sh_ref):
    return _modulated_rms_val(x_ref[0], g_ref[...], sc_ref[0], sh_ref[0])


_EV_COLS = {"mq": (0, 256), "mv": (256, 768), "dqkv": (768, 2304)}
_OD_COLS = {"ga": (0, 512), "gb": (512, 1024), "aq": (1024, 1536), "ak": (1536, 2048), "av": (2048, 2560)}


HALO_X = 16
C_GDN = 64


def _softplus(t):
    return jnp.maximum(t, 0.0) + jnp.log1p(jnp.exp(-jnp.abs(t)))


def _seg_scan_lanes(x, seg, reverse, op, fill):
    n = x.shape[1]
    pos = lax.broadcasted_iota(jnp.int32, x.shape, 1) % seg
    k = 1
    while k < seg:
        if reverse:
            x = op(x, jnp.where(pos < seg - k, pltpu.roll(x, n - k, axis=1), fill))
        else:
            x = op(x, jnp.where(pos >= k, pltpu.roll(x, k, axis=1), fill))
        k *= 2
    return x


def _seg_cumsum_lanes(x, seg, reverse):
    return _seg_scan_lanes(x, seg, reverse, jnp.add, 0.0)


def _seg_cummax_lanes(x, seg, reverse):
    return _seg_scan_lanes(x, seg, reverse, jnp.maximum, -jnp.inf)


def _conv_taps(xe, w_ref, cs, width, t, halo):
    n = t + 2 * halo
    acc = None
    for j in range(width):
        off = halo - width // 2 + j
        shifted = xe[off:off + t] if off % SUBLANES == 0 else pltpu.roll(xe, n - off, axis=0)[0:t]
        term = shifted * w_ref[j:j + 1, cs]
        acc = term if acc is None else acc + term
    return acc


def _zero_outside(d, t, halo, first, last):
    return jnp.concatenate([jnp.where(first, 0.0, d[:halo]), d[halo:halo + t], jnp.where(last, 0.0, d[halo + t:])], axis=0)


def _inproj_even_kernel(x_ref, xp_ref, xn_ref, g_ref, sc_ref, sh_ref, w_ref, wkt_ref, wgt_ref, cw_ref, mb_ref, a_ref, dt_ref,
                        mq_ref, mkt_ref, mv_ref, bq_ref, bk_ref, bv_ref, mc_ref, mr_ref, gc_ref, gr_ref):
    i = pl.program_id(1)
    nt = pl.num_programs(1)
    tm = x_ref.shape[1]
    f32 = jnp.float32
    x_ext = jnp.concatenate([xp_ref[0], x_ref[0], xn_ref[0]], axis=0)
    h_ext = _modulated_rms_val(x_ext, g_ref[...], sc_ref[0], sh_ref[0])
    h = h_ext[HALO_X:HALO_X + tm]
    xpose = (((1,), (1,)), ((), ()))
    for name, o_ref in (("mq", mq_ref), ("mv", mv_ref)):
        lo, hi = _EV_COLS[name]
        o_ref[0] = jnp.dot(h, w_ref[:, lo:hi], preferred_element_type=f32).astype(o_ref.dtype)
    mkt_ref[0] = lax.dot_general(wkt_ref[...], h, xpose, preferred_element_type=f32).astype(mkt_ref.dtype)
    gates_t = lax.dot_general(wgt_ref[...], h, xpose, preferred_element_type=f32)
    na = A_HEADS
    gm = gates_t[:16] + mb_ref[...]
    logf = _log_sigmoid(gm[2 * na:])
    b_f, b_b = _seg_cumsum_lanes(logf[:na], L_MLSTM, False), _seg_cumsum_lanes(logf[na:], L_MLSTM, True)
    pm_f = _seg_cummax_lanes(gm[:na] - b_f, L_MLSTM, False)
    pm_b = _seg_cummax_lanes(gm[na:2 * na] - b_b, L_MLSTM, True)
    mr_ref[0] = jnp.concatenate([gm[:2 * na], b_f, b_b], axis=0)
    mc_ref[0] = jnp.concatenate([b_f, b_b, pm_f, pm_b], axis=0).T
    nh = B_HEADS
    dgt = gates_t[16:]
    dec = -jnp.exp(a_ref[...]) * _softplus(dgt[2 * nh:] + dt_ref[...])
    gr = jnp.concatenate([jax.nn.sigmoid(dgt[:2 * nh]), _seg_cumsum_lanes(dec[:nh], C_GDN, False),
                          _seg_cumsum_lanes(dec[nh:], C_GDN, True)], axis=0)
    gr_ref[0] = gr
    gc_ref[0] = gr.T
    hd = B_HEADS * B_DK
    for part, o_ref in enumerate((bq_ref, bk_ref, bv_ref)):
        lo = _EV_COLS["dqkv"][0] + part * hd
        cs = slice(part * hd, (part + 1) * hd)
        d = jnp.dot(h_ext, w_ref[:, lo:lo + hd], preferred_element_type=f32)
        acc = _conv_taps(_zero_outside(d, tm, HALO_X, i == 0, i == nt - 1), cw_ref, cs, B_CONV, tm, HALO_X)
        y = acc * jax.nn.sigmoid(acc)
        for hh in range(B_HEADS):
            yh = y[:, hh * B_DK:(hh + 1) * B_DK]
            if part == 0:
                yh = yh * lax.rsqrt(jnp.sum(yh * yh, axis=-1, keepdims=True) + EPS) * (B_DK ** -0.5)
            elif part == 1:
                yh = yh * lax.rsqrt(jnp.sum(yh * yh, axis=-1, keepdims=True) + EPS)
            o_ref[0, :, hh * B_DK:(hh + 1) * B_DK] = yh.astype(o_ref.dtype)


def _halo_specs(tm, s, d):
    hb = tm // HALO_X
    return [pl.BlockSpec((1, HALO_X, d), lambda b, i: (b, jnp.maximum(i * hb - 1, 0), 0)),
            pl.BlockSpec((1, HALO_X, d), lambda b, i: (b, jnp.minimum((i + 1) * hb, s // HALO_X - 1), 0))]


def _inproj_even(x, g, scale, shift, w, wkt, wgt, conv_w, m_gate_b, a_log, dt_bias):
    bsz, s, d = x.shape
    tm = TM_PROJ
    assert tm % L_MLSTM == 0 and tm % C_GDN == 0
    tok = lambda b, i: (b, i, 0)
    tok_t = lambda b, i: (b, 0, i)
    const = lambda b, i: (0, 0)
    bvec = lambda b, i: (b, 0, 0)
    bf16, f32 = jnp.bfloat16, jnp.float32
    hk, hv, hd = A_HEADS * A_DK, A_HEADS * A_DV, B_HEADS * B_DK
    tok_specs = lambda wd: pl.BlockSpec((1, tm, wd), tok)
    gate_specs = [pl.BlockSpec((1, tm, 16), tok), pl.BlockSpec((1, 16, tm), tok_t)]
    gate_shapes = [jax.ShapeDtypeStruct((bsz, s, 16), f32), jax.ShapeDtypeStruct((bsz, 16, s), f32)]
    return pl.pallas_call(
        _inproj_even_kernel,
        grid=(bsz, s // tm),
        in_specs=[pl.BlockSpec((1, tm, d), tok)] + _halo_specs(tm, s, d) + [pl.BlockSpec((1, d), const), pl.BlockSpec((1, 1, d), bvec),
                  pl.BlockSpec((1, 1, d), bvec), pl.BlockSpec(w.shape, const), pl.BlockSpec(wkt.shape, const), pl.BlockSpec(wgt.shape, const),
                  pl.BlockSpec(conv_w.shape, const), pl.BlockSpec((16, 1), const), pl.BlockSpec((8, 1), const), pl.BlockSpec((8, 1), const)],
        out_specs=[tok_specs(hk), pl.BlockSpec((1, hk, tm), tok_t), tok_specs(hv), tok_specs(hd), tok_specs(hd), tok_specs(hd)]
        + gate_specs + gate_specs,
        out_shape=[jax.ShapeDtypeStruct((bsz, s, hk), bf16), jax.ShapeDtypeStruct((bsz, hk, s), bf16), jax.ShapeDtypeStruct((bsz, s, hv), bf16)]
        + [jax.ShapeDtypeStruct((bsz, s, hd), bf16)] * 3 + gate_shapes + gate_shapes,
        compiler_params=pltpu.CompilerParams(dimension_semantics=("parallel", "parallel"), vmem_limit_bytes=VMEM_LIMIT),
        name="inproj_even",
    )(x, x, x, g.reshape(1, d), scale, shift, w, wkt, wgt, conv_w, m_gate_b.reshape(16, 1), a_log.reshape(8, 1), dt_bias.reshape(8, 1))


DILATIONS = tuple(dil for _, dil in D_GROUPS)
LANES = 128
SUBLANES = 8


SUB_C = 64


def _inproj_odd_kernel(x_ref, xp_ref, xn_ref, g_ref, sc_ref, sh_ref, w_ref, cw_ref, cb_ref, lg_ref, lb_ref, oc_ref, *rest):
    out_refs, plane_ref = rest[:-1], rest[-1]
    i = pl.program_id(1)
    nt = pl.num_programs(1)
    f32 = jnp.float32
    tm = x_ref.shape[1]
    nd = D_HEADS * D_DH
    x_ext = jnp.concatenate([xp_ref[0], x_ref[0], xn_ref[0]], axis=0)
    h_ext = _modulated_rms_val(x_ext, g_ref[...], sc_ref[0], sh_ref[0])
    h = h_ext[HALO_X:HALO_X + tm]
    dot = lambda name: jnp.dot(h, w_ref[:, _OD_COLS[name][0]:_OD_COLS[name][1]], preferred_element_type=f32)
    dot_ext = lambda name: jnp.dot(h_ext, w_ref[:, _OD_COLS[name][0]:_OD_COLS[name][1]], preferred_element_type=f32)
    xe = _zero_outside(dot_ext("ga") * jax.nn.sigmoid(dot_ext("gb")), tm, HALO_X, i == 0, i == nt - 1)
    n = tm + 2 * HALO_X
    phases = [xe] + [pltpu.roll(xe, n - b, axis=0) for b in range(1, SUBLANES)]

    def conv_rows(r0):
        acc = None
        for j in range(C_CONV):
            a, b = divmod(HALO_X - C_CONV // 2 + j, SUBLANES)
            lo = a * SUBLANES + r0
            term = phases[b][lo:lo + SUB_C] * cw_ref[j:j + 1, :]
            acc = term if acc is None else acc + term
        u = acc + cb_ref[...]
        uc = u - jnp.mean(u, axis=-1, keepdims=True)
        y = uc * lax.rsqrt(jnp.mean(uc * uc, axis=-1, keepdims=True) + EPS) * lg_ref[...] + lb_ref[...]
        oc_ref[0, r0:r0 + SUB_C, :] = y * jax.nn.sigmoid(y)

    def attention_operand(a, name):
        r = dot(name)
        group_refs = out_refs[a * len(DILATIONS):(a + 1) * len(DILATIONS)]
        for j in range(nd // LANES):
            plane_ref[a, j] = r[:, j * LANES:(j + 1) * LANES]
        for dil, o_ref in zip(DILATIONS, group_refs):
            if dil == 1:
                o_ref[0, 0] = r.astype(o_ref.dtype)
                continue
            for res in range(dil):
                for j in range(nd // LANES):
                    o_ref[0, res, :, j * LANES:(j + 1) * LANES] = plane_ref[a, j, pl.ds(res, tm // dil, stride=dil), :].astype(o_ref.dtype)

    row_blocks = list(range(0, tm, SUB_C))
    names = ("aq", "ak", "av")
    per = -(-len(row_blocks) // (len(names) + 1))
    for a in range(len(names) + 1):
        for r0 in row_blocks[a * per:(a + 1) * per]:
            conv_rows(r0)
        if a < len(names):
            attention_operand(a, names[a])


def _inproj_odd(x, g, scale, shift, w, dw_w, dw_b, ln_g, ln_b):
    bsz, s, d = x.shape
    tm = TM_PROJ
    tok = lambda b, i: (b, i, 0)
    const = lambda b, i: (0, 0)
    bvec = lambda b, i: (b, 0, 0)
    bf16, f32 = jnp.bfloat16, jnp.float32
    nd = D_HEADS * D_DH
    cw = C_WIDTH
    att_specs = [pl.BlockSpec((1, dil, tm // dil, nd), lambda b, i: (b, 0, i, 0)) for dil in DILATIONS] * 3
    att_shapes = [jax.ShapeDtypeStruct((bsz, dil, s // dil, nd), bf16) for dil in DILATIONS] * 3
    outs = pl.pallas_call(
        _inproj_odd_kernel,
        grid=(bsz, s // tm),
        in_specs=[pl.BlockSpec((1, tm, d), tok)] + _halo_specs(tm, s, d) + [pl.BlockSpec((1, d), const), pl.BlockSpec((1, 1, d), bvec),
                  pl.BlockSpec((1, 1, d), bvec), pl.BlockSpec(w.shape, const), pl.BlockSpec((C_CONV, cw), const)]
        + [pl.BlockSpec((1, cw), const)] * 3,
        out_specs=[pl.BlockSpec((1, tm, cw), tok)] + att_specs,
        out_shape=[jax.ShapeDtypeStruct((bsz, s, cw), f32)] + att_shapes,
        scratch_shapes=[pltpu.VMEM((3, nd // LANES, tm, LANES), f32)],
        compiler_params=pltpu.CompilerParams(dimension_semantics=("parallel", "parallel"), vmem_limit_bytes=VMEM_LIMIT),
        name="inproj_odd",
    )(x, x, x, g.reshape(1, d), scale, shift, w, dw_w, dw_b.reshape(1, cw), ln_g.reshape(1, cw), ln_b.reshape(1, cw))
    ng = len(DILATIONS)
    return outs[0], outs[1:1 + ng], outs[1 + ng:1 + 2 * ng], outs[1 + 2 * ng:]


def _head_rms_cols(t, g, width):
    parts = []
    for h in range(t.shape[1] // width):
        th = t[:, h * width:(h + 1) * width]
        parts.append(th * lax.rsqrt(jnp.mean(th * th, axis=-1, keepdims=True) + EPS))
    return jnp.concatenate(parts, axis=1) * g


def _outproj_even_kernel(hf_ref, hb_ref, of_ref, ob_ref, x_ref, g_ref, sc_ref, sh_ref, gate_ref, mg_ref, dg_ref, wz_ref, w_ref, o_ref):
    f32, bf16 = jnp.float32, jnp.bfloat16
    na = A_HEADS * A_DV
    h = _modulated_rms(x_ref, g_ref, sc_ref, sh_ref)
    mo = jnp.dot(h, wz_ref[:, :na], preferred_element_type=f32)
    z = jnp.dot(h, wz_ref[:, na:], preferred_element_type=f32)
    sz = z * jax.nn.sigmoid(z)
    out_a = jax.nn.sigmoid(mo) * _head_rms_cols(hf_ref[0] + hb_ref[0], mg_ref[...], A_DV)
    out_b = _head_rms_cols(of_ref[0] + ob_ref[0], dg_ref[...], B_DV)
    y = jnp.dot((out_a * sz[:, :na]).astype(bf16), w_ref[:na, :], preferred_element_type=f32)
    y = y + jnp.dot((out_b * sz[:, na:]).astype(bf16), w_ref[na:, :], preferred_element_type=f32)
    o_ref[0] = x_ref[0] + gate_ref[0] * y


def _outproj_even(hf, hb, of, ob, x, norm_g, scale, shift, gate, m_norm_g, dn_norm_g, wz, w):
    bsz, s, d = x.shape
    tm = TM_PROJ
    tok = lambda b, i: (b, i, 0)
    const = lambda b, i: (0, 0)
    bvec = lambda b, i: (b, 0, 0)
    na, nb = A_HEADS * A_DV, B_HEADS * B_DV
    return pl.pallas_call(
        _outproj_even_kernel,
        grid=(bsz, s // tm),
        in_specs=[pl.BlockSpec((1, tm, na), tok)] * 2 + [pl.BlockSpec((1, tm, nb), tok)] * 2 + [pl.BlockSpec((1, tm, d), tok),
                  pl.BlockSpec((1, d), const), pl.BlockSpec((1, 1, d), bvec), pl.BlockSpec((1, 1, d), bvec), pl.BlockSpec((1, 1, d), bvec),
                  pl.BlockSpec((1, na), const), pl.BlockSpec((1, nb), const), pl.BlockSpec(wz.shape, const), pl.BlockSpec(w.shape, const)],
        out_specs=pl.BlockSpec((1, tm, d), tok),
        out_shape=jax.ShapeDtypeStruct((bsz, s, d), jnp.float32),
        compiler_params=pltpu.CompilerParams(dimension_semantics=("parallel", "parallel"), vmem_limit_bytes=VMEM_LIMIT),
        name="outproj_even",
    )(hf, hb, of, ob, x, norm_g.reshape(1, d), scale, shift, gate, m_norm_g.reshape(1, na), dn_norm_g.reshape(1, nb), wz, w)


def _outproj_odd_kernel(oc_ref, o1_ref, o2_ref, o3_ref, l1_ref, l2_ref, l3_ref, x_ref, g_ref, sc_ref, sh_ref, gate_ref, fg_ref,
                        wz_ref, w_ref, o_ref, nat_ref):
    f32, bf16 = jnp.float32, jnp.bfloat16
    tm = x_ref.shape[1]
    npl = D_HEADS * D_DH // LANES
    z = jnp.dot(_modulated_rms(x_ref, g_ref, sc_ref, sh_ref), wz_ref[...], preferred_element_type=f32)
    sz = z * jax.nn.sigmoid(z)
    groups = []
    for gi, (dil, og_ref, lg_ref) in enumerate(zip(DILATIONS, (o1_ref, o2_ref, o3_ref), (l1_ref, l2_ref, l3_ref))):
        if dil == 1:
            groups.append(([og_ref[0, 0, :, j * LANES:(j + 1) * LANES] for j in range(npl)], lg_ref[0, 0]))
            continue
        for res in range(dil):
            rows = pl.ds(res, tm // dil, stride=dil)
            for j in range(npl):
                nat_ref[gi, j, rows, :] = og_ref[0, res, :, j * LANES:(j + 1) * LANES]
            nat_ref[gi, npl, rows, :] = lg_ref[0, res]
        groups.append(([nat_ref[gi, j] for j in range(npl)], nat_ref[gi, npl]))
    (p1, l1), (p2, l2), (p3, l3) = groups
    lm = jnp.maximum(jnp.maximum(l1, l2), l3)
    e1, e2, e3 = jnp.exp(l1 - lm), jnp.exp(l2 - lm), jnp.exp(l3 - lm)
    inv = 1.0 / (e1 + e2 + e3)
    low = lax.broadcasted_iota(jnp.int32, (tm, LANES), 1) < D_DH
    planes = []
    for j in range(npl):
        acc = None
        for e, p in ((e1, p1), (e2, p2), (e3, p3)):
            wgt = e * inv
            term = jnp.where(low, wgt[:, 2 * j:2 * j + 1], wgt[:, 2 * j + 1:2 * j + 2]) * p[j]
            acc = term if acc is None else acc + term
        planes.append(acc)
    out_d = jnp.concatenate(planes, axis=1)
    y = jnp.dot((oc_ref[0] * sz[:, :C_WIDTH]).astype(bf16), w_ref[:C_WIDTH, :], preferred_element_type=f32)
    y = y + jnp.dot((out_d * sz[:, C_WIDTH:]).astype(bf16), w_ref[C_WIDTH:, :], preferred_element_type=f32)
    xn = x_ref[0] + gate_ref[0] * y
    o_ref[0] = xn * lax.rsqrt(jnp.mean(xn * xn, axis=-1, keepdims=True) + EPS) * fg_ref[...]


def _outproj_odd_final(oc, og, lg, x, norm_g, scale, shift, gate, final_g, wz, w):
    bsz, s, d = x.shape
    tm = TM_PROJ
    tok = lambda b, i: (b, i, 0)
    const = lambda b, i: (0, 0)
    bvec = lambda b, i: (b, 0, 0)
    nd = D_HEADS * D_DH
    res_major = lambda width: [pl.BlockSpec((1, dil, tm // dil, width), lambda b, i: (b, 0, i, 0)) for dil in DILATIONS]
    return pl.pallas_call(
        _outproj_odd_kernel,
        grid=(bsz, s // tm),
        in_specs=[pl.BlockSpec((1, tm, C_WIDTH), tok)] + res_major(nd) + res_major(LANES)
        + [pl.BlockSpec((1, tm, d), tok), pl.BlockSpec((1, d), const), pl.BlockSpec((1, 1, d), bvec), pl.BlockSpec((1, 1, d), bvec),
           pl.BlockSpec((1, 1, d), bvec), pl.BlockSpec((1, d), const), pl.BlockSpec(wz.shape, const), pl.BlockSpec(w.shape, const)],
        out_specs=pl.BlockSpec((1, tm, d), tok),
        out_shape=jax.ShapeDtypeStruct((bsz, s, d), jnp.float32),
        scratch_shapes=[pltpu.VMEM((len(DILATIONS), nd // LANES + 1, tm, LANES), jnp.float32)],
        compiler_params=pltpu.CompilerParams(dimension_semantics=("parallel", "parallel"), vmem_limit_bytes=VMEM_LIMIT),
        name="outproj_odd",
    )(oc, *og, *lg, x, norm_g.reshape(1, d), scale, shift, gate, final_g.reshape(1, d), wz, w)


T_CONV = 512
HALO_C = 16
SUB_C = 64


def _conformer_kernel(x_ref, xp_ref, xn_ref, w_ref, b_ref, lg_ref, lb_ref, o_ref, xe_ref, ph_ref):
    i = pl.program_id(1)
    nt = pl.num_programs(1)
    t = x_ref.shape[1]
    xe_ref[0:HALO_C, :] = jnp.where(i > 0, xp_ref[0], 0.0)
    xe_ref[HALO_C:HALO_C + t, :] = x_ref[0]
    xe_ref[HALO_C + t:, :] = jnp.where(i < nt - 1, xn_ref[0], 0.0)
    half = C_CONV // 2
    n = t + 2 * HALO_C
    xe = xe_ref[...]
    for b in range(1, SUBLANES):
        ph_ref[b - 1] = pltpu.roll(xe, n - b, axis=0)
    for r0 in range(0, t, SUB_C):
        acc = None
        for j in range(C_CONV):
            a, b = divmod(HALO_C - half + j, SUBLANES)
            lo = a * SUBLANES + r0
            src = xe_ref[lo:lo + SUB_C, :] if b == 0 else ph_ref[b - 1, lo:lo + SUB_C, :]
            term = src * w_ref[j:j + 1, :]
            acc = term if acc is None else acc + term
        u = acc + b_ref[...]
        uc = u - jnp.mean(u, axis=-1, keepdims=True)
        y = uc * lax.rsqrt(jnp.mean(uc * uc, axis=-1, keepdims=True) + EPS) * lg_ref[...] + lb_ref[...]
        o_ref[0, r0:r0 + SUB_C, :] = y * jax.nn.sigmoid(y)


def _conformer(glu, dw_w, dw_b, ln_g, ln_b):
    bsz, s, cw = glu.shape
    t = min(T_CONV, s)
    hb = t // HALO_C
    cur = lambda b, i: (b, i, 0)
    const = lambda b, i: (0, 0)
    return pl.pallas_call(
        _conformer_kernel,
        grid=(bsz, s // t),
        in_specs=[pl.BlockSpec((1, t, cw), cur),
                  pl.BlockSpec((1, HALO_C, cw), lambda b, i: (b, jnp.maximum(i * hb - 1, 0), 0)),
                  pl.BlockSpec((1, HALO_C, cw), lambda b, i: (b, jnp.minimum((i + 1) * hb, s // HALO_C - 1), 0)),
                  pl.BlockSpec((C_CONV, cw), const)] + [pl.BlockSpec((1, cw), const)] * 3,
        out_specs=pl.BlockSpec((1, t, cw), cur),
        out_shape=jax.ShapeDtypeStruct((bsz, s, cw), jnp.float32),
        scratch_shapes=[pltpu.VMEM((t + 2 * HALO_C, cw), jnp.float32), pltpu.VMEM((SUBLANES - 1, t + 2 * HALO_C, cw), jnp.float32)],
        compiler_params=pltpu.CompilerParams(dimension_semantics=("parallel", "parallel"), vmem_limit_bytes=VMEM_LIMIT),
        name="conformer",
    )(glu, glu, glu, dw_w, dw_b.reshape(1, cw), ln_g.reshape(1, cw), ln_b.reshape(1, cw))


TQ_ATT = 128
TB_ATT = 512
R_ATT = 64


def _dilated_kernel(q_ref, kc_ref, kp_ref, kn_ref, vc_ref, vp_ref, vn_ref, bias_ref, o_ref, lse_ref, kx_ref, vx_ref):
    i = pl.program_id(2)
    nt = pl.num_programs(2)
    tb = q_ref.shape[1]
    tq = TQ_ATT
    nk = tq + 2 * R_ATT
    nsub = tb // tq
    f32, bf16 = jnp.float32, jnp.bfloat16
    kx_ref[0:R_ATT, :] = kp_ref[0]
    kx_ref[R_ATT:R_ATT + tb, :] = kc_ref[0]
    kx_ref[R_ATT + tb:, :] = kn_ref[0]
    vx_ref[0:R_ATT, :] = vp_ref[0]
    vx_ref[R_ATT:R_ATT + tb, :] = vc_ref[0]
    vx_ref[R_ATT + tb:, :] = vn_ref[0]
    kj = lax.broadcasted_iota(jnp.int32, (tq, nk), 1)
    lane = lax.broadcasted_iota(jnp.int32, (tq, 128), 1)
    low = lane < D_DH
    heads = [(pr, hi) for pr in range(D_HEADS // 2) for hi in (False, True)]
    for sub in range(nsub):
        qs = slice(sub * tq, (sub + 1) * tq)
        ks = slice(sub * tq, sub * tq + nk)
        outside = None
        if sub == 0:
            outside = (kj < R_ATT) & (i == 0)
        if sub == nsub - 1:
            after = (kj >= R_ATT + tq) & (i == nt - 1)
            outside = after if outside is None else outside | after
        scs = []
        for pr, hi in heads:
            ps = slice(pr * 128, (pr + 1) * 128)
            qp = q_ref[0, qs, ps]
            qh = jnp.where(low != hi, qp, jnp.zeros_like(qp))
            sc = lax.dot_general(qh, kx_ref[ks, ps], (((1,), (1,)), ((), ())), preferred_element_type=f32) + bias_ref[2 * pr + int(hi)]
            scs.append(sc if outside is None else jnp.where(outside, NEG, sc))
        ms = [jnp.max(sc, axis=-1, keepdims=True) for sc in scs]
        ps_ = [jnp.exp(sc - m) for sc, m in zip(scs, ms)]
        dens = [jnp.sum(p, axis=-1, keepdims=True) for p in ps_]
        pvs = [jnp.dot(p.astype(bf16), vx_ref[ks, pr * 128:(pr + 1) * 128], preferred_element_type=f32) for (pr, _), p in zip(heads, ps_)]
        lse_all = jnp.zeros((tq, 128), f32)
        for pr in range(D_HEADS // 2):
            lo, hi = 2 * pr, 2 * pr + 1
            o_ref[0, qs, pr * 128:(pr + 1) * 128] = jnp.where(low, pvs[lo] / dens[lo], pvs[hi] / dens[hi])
            lse_all = jnp.where(lane == lo, ms[lo] + jnp.log(dens[lo]), lse_all)
            lse_all = jnp.where(lane == hi, ms[hi] + jnp.log(dens[hi]), lse_all)
        lse_ref[0, qs, :] = lse_all


def _dilated_bias(rel_bias, dilation, tq):
    half = REL_BUCKETS // 2
    exact = half // 2
    qi = jnp.arange(tq)[:, None]
    kj = jnp.arange(tq + 2 * R_ATT)[None, :]
    rel = kj - R_ATT - qi
    reld = rel * dilation
    n = jnp.abs(reld)
    large = exact + (jnp.log(jnp.maximum(n, 1).astype(jnp.float32) / exact) / math.log(REL_MAX_DIST / exact) * (half - exact)).astype(jnp.int32)
    large = jnp.minimum(large, half - 1)
    bucket = (reld > 0).astype(jnp.int32) * half + jnp.where(n < exact, n, large)
    bias = jnp.zeros((rel_bias.shape[1],) + bucket.shape, jnp.float32)
    for b in range(REL_BUCKETS):
        bias = jnp.where((bucket == b)[None], rel_bias[b].astype(jnp.float32)[:, None, None], bias)
    return jnp.where((jnp.abs(rel) <= R_ATT)[None], bias, NEG)


def _dilated_group_call(q, k, v, rel_bias, dilation):
    bsz, dil, ls, nd = q.shape
    assert dil == dilation
    assert ls % TB_ATT == 0
    tb, tq = TB_ATT, TQ_ATT
    nt = ls // tb
    hb = tb // R_ATT
    nk = tq + 2 * R_ATT
    cur = lambda b, r, i: (b, r, i, 0)
    prev = lambda b, r, i: (b, r, jnp.maximum(i * hb - 1, 0), 0)
    nxt = lambda b, r, i: (b, r, jnp.minimum((i + 1) * hb, ls // R_ATT - 1), 0)
    kv_specs = [pl.BlockSpec((1, None, tb, nd), cur), pl.BlockSpec((1, None, R_ATT, nd), prev), pl.BlockSpec((1, None, R_ATT, nd), nxt)]
    return pl.pallas_call(
        _dilated_kernel,
        grid=(bsz, dilation, nt),
        in_specs=[pl.BlockSpec((1, None, tb, nd), cur)] + kv_specs + kv_specs + [pl.BlockSpec((D_HEADS, tq, nk), lambda b, r, i: (0, 0, 0))],
        out_specs=[pl.BlockSpec((1, None, tb, nd), cur), pl.BlockSpec((1, None, tb, 128), cur)],
        out_shape=[jax.ShapeDtypeStruct((bsz, dilation, ls, nd), jnp.float32), jax.ShapeDtypeStruct((bsz, dilation, ls, 128), jnp.float32)],
        scratch_shapes=[pltpu.VMEM((tb + 2 * R_ATT, nd), jnp.bfloat16)] * 2,
        compiler_params=pltpu.CompilerParams(dimension_semantics=("parallel", "parallel", "parallel"), vmem_limit_bytes=VMEM_LIMIT),
        name=f"dilated_d{dilation}",
    )(q, k, k, k, v, v, v, _dilated_bias(rel_bias, dilation, tq))


def _inproj(x, g, scale, shift, w_bf16):
    bsz, s, d = x.shape
    n = w_bf16.shape[1]
    return pl.pallas_call(
        _inproj_kernel,
        grid=(bsz, s // TM_PROJ),
        in_specs=[
            pl.BlockSpec((1, TM_PROJ, d), lambda b, i: (b, i, 0)),
            pl.BlockSpec((1, d), lambda b, i: (0, 0)),
            pl.BlockSpec((1, 1, d), lambda b, i: (b, 0, 0)),
            pl.BlockSpec((1, 1, d), lambda b, i: (b, 0, 0)),
            pl.BlockSpec((d, n), lambda b, i: (0, 0)),
        ],
        out_specs=pl.BlockSpec((1, TM_PROJ, n), lambda b, i: (b, i, 0)),
        out_shape=jax.ShapeDtypeStruct((bsz, s, n), jnp.float32),
        compiler_params=pltpu.CompilerParams(dimension_semantics=("parallel", "parallel"), vmem_limit_bytes=VMEM_LIMIT),
        name="inproj",
    )(x, g.reshape(1, d), scale, shift, w_bf16)


def _outproj_kernel(mix_ref, z_ref, x_ref, gate_ref, w_ref, o_ref):
    z = z_ref[0]
    m = mix_ref[0] * (z * jax.nn.sigmoid(z))
    y = jnp.dot(m.astype(jnp.bfloat16), w_ref[...], preferred_element_type=jnp.float32)
    o_ref[0] = x_ref[0] + gate_ref[0] * y


def _outproj(mix, z, x, gate, w_bf16):
    bsz, s, d = x.shape
    k = mix.shape[-1]
    return pl.pallas_call(
        _outproj_kernel,
        grid=(bsz, s // TM_PROJ),
        in_specs=[
            pl.BlockSpec((1, TM_PROJ, k), lambda b, i: (b, i, 0)),
            pl.BlockSpec((1, TM_PROJ, k), lambda b, i: (b, i, 0)),
            pl.BlockSpec((1, TM_PROJ, d), lambda b, i: (b, i, 0)),
            pl.BlockSpec((1, 1, d), lambda b, i: (b, 0, 0)),
            pl.BlockSpec((k, d), lambda b, i: (0, 0)),
        ],
        out_specs=pl.BlockSpec((1, TM_PROJ, d), lambda b, i: (b, i, 0)),
        out_shape=jax.ShapeDtypeStruct((bsz, s, d), jnp.float32),
        compiler_params=pltpu.CompilerParams(dimension_semantics=("parallel", "parallel"), vmem_limit_bytes=VMEM_LIMIT),
        name="outproj",
    )(mix, z, x, gate, w_bf16)


def _final_rms_kernel(x_ref, g_ref, o_ref):
    x = x_ref[0]
    o_ref[0] = x * lax.rsqrt(jnp.mean(x * x, axis=-1, keepdims=True) + EPS) * g_ref[...]


def _final_rms(x, g):
    bsz, s, d = x.shape
    tm = 512
    return pl.pallas_call(
        _final_rms_kernel,
        grid=(bsz, s // tm),
        in_specs=[pl.BlockSpec((1, tm, d), lambda b, i: (b, i, 0)), pl.BlockSpec((1, d), lambda b, i: (0, 0))],
        out_specs=pl.BlockSpec((1, tm, d), lambda b, i: (b, i, 0)),
        out_shape=jax.ShapeDtypeStruct((bsz, s, d), jnp.float32),
        compiler_params=pltpu.CompilerParams(dimension_semantics=("parallel", "parallel")),
        name="final_rms",
    )(x, g.reshape(1, d))


L_MLSTM = 256
_HI = lax.Precision.HIGHEST


def _log_sigmoid(t):
    return jnp.minimum(t, 0.0) - jnp.log(1.0 + jnp.exp(-jnp.abs(t)))


def _mlstm_kernel(qf_ref, kf_ref, vf_ref, gf_ref, gtf_ref, qb_ref, kb_ref, vb_ref, gb_ref, gtb_ref,
                  hf_ref, hb_ref, c_ref, m_ref):
    n = pl.program_id(1)
    ln = qf_ref.shape[1]
    f32, bf16 = jnp.float32, jnp.bfloat16

    @pl.when(n == 0)
    def _():
        c_ref[...] = jnp.zeros_like(c_ref)
        m_ref[...] = jnp.zeros_like(m_ref)

    row = lax.broadcasted_iota(jnp.int32, (ln, ln), 0)
    col = lax.broadcasted_iota(jnp.int32, (ln, ln), 1)
    ones_blk = jnp.ones((ln, A_DV), bf16)
    dirs = ((0, qf_ref, kf_ref, vf_ref, gf_ref, gtf_ref, hf_ref), (1, qb_ref, kb_ref, vb_ref, gb_ref, gtb_ref, hb_ref))
    probs = []
    na = A_HEADS
    for d, q_ref, kt_ref, v_ref, gc_ref, gr_ref, h_ref in dirs:
        mask = (row >= col) if d == 0 else (row <= col)
        for h in range(na):
            r = d * na + h
            probs.append(dict(
                r=r, h=h, mask=mask, last=ln - 1 if d == 0 else 0, h_ref=h_ref,
                q=q_ref[0, :, h * A_DK:(h + 1) * A_DK],
                kt=kt_ref[0, h * A_DK:(h + 1) * A_DK, :],
                vaug=jnp.concatenate([v_ref[0, :, h * A_DV:(h + 1) * A_DV], ones_blk], axis=1),
                bcol=gc_ref[0, :, r:r + 1], pmcol=gc_ref[0, :, 2 * na + r:2 * na + r + 1],
                brow=gr_ref[0, 2 * na + r:2 * na + r + 1, :],
                vrow=gr_ref[0, r:r + 1, :] - gr_ref[0, 2 * na + r:2 * na + r + 1, :]))
    for p in probs:
        p["m_old"] = m_ref[p["r"]:p["r"] + 1, 0:1]
        p["caug"] = c_ref[p["r"]]
        p["qk"] = jnp.dot(p["q"], p["kt"], preferred_element_type=f32)
    for p in probs:
        p["qc"] = jnp.dot(p["q"], p["caug"].astype(bf16), preferred_element_type=f32)
    for p in probs:
        mstab = jnp.maximum(p["m_old"], p["pmcol"])
        p["w_int"] = jnp.exp(p["m_old"] - mstab)
        p["emt"] = jnp.exp(-(p["bcol"] + mstab))
        p["sc"] = (jnp.exp(jnp.where(p["mask"], p["vrow"] - mstab, -jnp.inf)) * p["qk"]).astype(bf16)
    for p in probs:
        tot = p["w_int"] * p["qc"] + jnp.dot(p["sc"], p["vaug"], preferred_element_type=f32)
        den = jnp.maximum(jnp.abs(tot[:, A_DV:]), p["emt"])
        p["h_ref"][0, :, p["h"] * A_DV:(p["h"] + 1) * A_DV] = tot[:, :A_DV] / den
    for p in probs:
        last, brow, vrow = p["last"], p["brow"], p["vrow"]
        btot = brow[:, last:last + 1]
        m_new = btot + jnp.maximum(p["m_old"], jnp.max(vrow, axis=-1, keepdims=True))
        w_old = jnp.exp(btot + p["m_old"] - m_new)
        kwt = (p["kt"].astype(f32) * jnp.exp(btot + vrow - m_new)).astype(bf16)
        c_ref[p["r"]] = w_old * p["caug"] + jnp.dot(kwt, p["vaug"], preferred_element_type=f32)
        m_ref[p["r"]:p["r"] + 1, :] = jnp.broadcast_to(m_new, (1, m_ref.shape[1]))


def _mlstm(q, kt, v, gc, gr):
    bsz, s, _ = q.shape
    ln = min(L_MLSTM, s)
    nc = s // ln
    hk, hv = A_HEADS * A_DK, A_HEADS * A_DV
    fwd = lambda b, n: (b, n, 0)
    bwd = lambda b, n: (b, nc - 1 - n, 0)
    fwd_t = lambda b, n: (b, 0, n)
    bwd_t = lambda b, n: (b, 0, nc - 1 - n)
    def specs(im, im_t):
        return [pl.BlockSpec((1, ln, hk), im), pl.BlockSpec((1, hk, ln), im_t), pl.BlockSpec((1, ln, hv), im),
                pl.BlockSpec((1, ln, 16), im), pl.BlockSpec((1, 16, ln), im_t)]
    return pl.pallas_call(
        _mlstm_kernel,
        grid=(bsz, nc),
        in_specs=specs(fwd, fwd_t) + specs(bwd, bwd_t),
        out_specs=[pl.BlockSpec((1, ln, hv), fwd), pl.BlockSpec((1, ln, hv), bwd)],
        out_shape=[jax.ShapeDtypeStruct((bsz, s, hv), jnp.float32)] * 2,
        scratch_shapes=[pltpu.VMEM((2 * A_HEADS, A_DK, 2 * A_DV), jnp.float32), pltpu.VMEM((2 * A_HEADS, 128), jnp.float32)],
        compiler_params=pltpu.CompilerParams(dimension_semantics=("parallel", "arbitrary"), vmem_limit_bytes=VMEM_LIMIT),
        name="mlstm",
    )(q, kt, v, gc, gr, q, kt, v, gc, gr)


T_GDN = 256
T_GDN_STEP = 256
C_GDN = 64
HALO = 8


def _softplus(t):
    return jnp.maximum(t, 0.0) + jnp.log1p(jnp.exp(-jnp.abs(t)))


def _gdn_prep_kernel(x_ref, xp_ref, xn_ref, g_ref, gt_ref, w_ref, a_ref, at_ref, dt_ref, dtt_ref,
                     q_ref, k_ref, v_ref, gc_ref, gr_ref, xe_ref):
    i = pl.program_id(1)
    nt = pl.num_programs(1)
    t = x_ref.shape[1]
    f32 = jnp.float32
    hd = B_HEADS * B_DK
    xe_ref[0:HALO, :] = jnp.where(i > 0, xp_ref[0], 0.0)
    xe_ref[HALO:HALO + t, :] = x_ref[0]
    xe_ref[HALO + t:, :] = jnp.where(i < nt - 1, xn_ref[0], 0.0)
    half = B_CONV // 2
    for part, o_ref in enumerate((q_ref, k_ref, v_ref)):
        cs = slice(part * hd, (part + 1) * hd)
        xe = xe_ref[:, cs]
        acc = None
        for j in range(B_CONV):
            off = HALO - half + j
            shifted = xe[off:off + t] if off % SUBLANES == 0 else pltpu.roll(xe, t + 2 * HALO - off, axis=0)[0:t]
            term = shifted * w_ref[j:j + 1, cs]
            acc = term if acc is None else acc + term
        y = acc * jax.nn.sigmoid(acc)
        for h in range(B_HEADS):
            yh = y[:, h * B_DK:(h + 1) * B_DK]
            if part == 0:
                yh = yh * lax.rsqrt(jnp.sum(yh * yh, axis=-1, keepdims=True) + EPS) * (B_DK ** -0.5)
            elif part == 1:
                yh = yh * lax.rsqrt(jnp.sum(yh * yh, axis=-1, keepdims=True) + EPS)
            o_ref[0, :, h * B_DK:(h + 1) * B_DK] = yh.astype(o_ref.dtype)
    row = lax.broadcasted_iota(jnp.int32, (t, t), 0)
    col = lax.broadcasted_iota(jnp.int32, (t, t), 1)
    same = (row // C_GDN) == (col // C_GDN)
    lower = (same & (row >= col)).astype(f32)
    upper = (same & (row <= col)).astype(f32)
    g = g_ref[0]
    gt = gt_ref[0]
    nh = B_HEADS
    dec = -jnp.exp(a_ref[...]) * _softplus(g[:, 2 * nh:] + dt_ref[...])
    dect = -jnp.exp(at_ref[...]) * _softplus(gt[2 * nh:, :] + dtt_ref[...])
    gc_ref[0, :, 0:2 * nh] = jax.nn.sigmoid(g[:, 0:2 * nh])
    gc_ref[0, :, 2 * nh:3 * nh] = jnp.dot(lower, dec[:, 0:nh], precision=_HI, preferred_element_type=f32)
    gc_ref[0, :, 3 * nh:] = jnp.dot(upper, dec[:, nh:], precision=_HI, preferred_element_type=f32)
    gr_ref[0, 0:2 * nh, :] = jax.nn.sigmoid(gt[0:2 * nh, :])
    gr_ref[0, 2 * nh:3 * nh, :] = jnp.dot(dect[0:nh, :], upper, precision=_HI, preferred_element_type=f32)
    gr_ref[0, 3 * nh:, :] = jnp.dot(dect[nh:, :], lower, precision=_HI, preferred_element_type=f32)


def _gdn_prep(dqkv, g, gt, conv_w, a_log, dt_bias):
    bsz, s, n3 = dqkv.shape
    t = min(T_GDN, s)
    nt = s // t
    hd = B_HEADS * B_DK
    hb = t // HALO
    cur = lambda b, i: (b, i, 0)
    const = lambda b, i: (0, 0)
    bf16 = jnp.bfloat16
    return pl.pallas_call(
        _gdn_prep_kernel,
        grid=(bsz, nt),
        in_specs=[
            pl.BlockSpec((1, t, n3), cur),
            pl.BlockSpec((1, HALO, n3), lambda b, i: (b, jnp.maximum(i * hb - 1, 0), 0)),
            pl.BlockSpec((1, HALO, n3), lambda b, i: (b, jnp.minimum((i + 1) * hb, s // HALO - 1), 0)),
            pl.BlockSpec((1, t, 16), cur),
            pl.BlockSpec((1, 16, t), lambda b, i: (b, 0, i)),
            pl.BlockSpec((B_CONV, n3), const),
            pl.BlockSpec((1, 8), const), pl.BlockSpec((8, 1), const),
            pl.BlockSpec((1, 8), const), pl.BlockSpec((8, 1), const),
        ],
        out_specs=[pl.BlockSpec((1, t, hd), cur)] * 3 + [pl.BlockSpec((1, t, 16), cur), pl.BlockSpec((1, 16, t), lambda b, i: (b, 0, i))],
        out_shape=[jax.ShapeDtypeStruct((bsz, s, hd), bf16)] * 3 + [jax.ShapeDtypeStruct((bsz, s, 16), jnp.float32), jax.ShapeDtypeStruct((bsz, 16, s), jnp.float32)],
        scratch_shapes=[pltpu.VMEM((t + 2 * HALO, n3), jnp.float32)],
        compiler_params=pltpu.CompilerParams(dimension_semantics=("parallel", "parallel"), vmem_limit_bytes=VMEM_LIMIT),
        name="gdn_prep",
    )(dqkv, dqkv, dqkv, g, gt, conv_w, a_log.reshape(1, 8), a_log.reshape(8, 1), dt_bias.reshape(1, 8), dt_bias.reshape(8, 1))


def _tri_inverse_many(a_list, masks):
    eye, m16, m32, m64 = masks
    f32, bf16 = jnp.float32, jnp.bfloat16
    mm = lambda x, y: jnp.dot(x.astype(bf16), y.astype(bf16), preferred_element_type=f32)
    ads = [jnp.where(m16, a, 0.0) for a in a_list]
    xs = [eye - ad for ad in ads]
    ps = [mm(ad, ad) for ad in ads]
    for stage in range(3):
        xs = [x + mm(x, p) for x, p in zip(xs, ps)]
        if stage < 2:
            ps = [mm(p, p) for p in ps]
    for lo, hi in ((m16, m32), (m32, m64)):
        off = hi & ~lo
        ys = [mm(jnp.where(off, a, 0.0), x) for a, x in zip(a_list, xs)]
        xs = [x - mm(x, y) for x, y in zip(xs, ys)]
    return xs


def _gdn_kernel(qf_ref, kf_ref, vf_ref, gcf_ref, grf_ref, qb_ref, kb_ref, vb_ref, gcb_ref, grb_ref, of_ref, ob_ref, s_ref):
    n = pl.program_id(1)
    t = qf_ref.shape[1]
    c = C_GDN
    f32, bf16 = jnp.float32, jnp.bfloat16

    @pl.when(n == 0)
    def _():
        s_ref[...] = jnp.zeros_like(s_ref)

    row = lax.broadcasted_iota(jnp.int32, (c, c), 0)
    col = lax.broadcasted_iota(jnp.int32, (c, c), 1)
    eye = (row == col).astype(f32)
    blk = lambda w: (row // w) == (col // w)
    masks = (eye, blk(16), blk(32), blk(64))
    nh, nchunk = B_HEADS, t // c
    dir_refs = ((qf_ref, kf_ref, vf_ref, gcf_ref, grf_ref, of_ref), (qb_ref, kb_ref, vb_ref, gcb_ref, grb_ref, ob_ref))
    probs = [(d, h, ci) for d in range(2) for h in range(nh) for ci in range(nchunk)]
    xpose = (((1,), (1,)), ((), ()))

    def load(d, h, ci):
        q_ref, k_ref, v_ref, gc_ref, gr_ref, _ = dir_refs[d]
        rs, cs = slice(ci * c, (ci + 1) * c), slice(h * B_DK, (h + 1) * B_DK)
        beta = gc_ref[0, rs, d * nh + h:d * nh + h + 1]
        gcol = gc_ref[0, rs, (2 + d) * nh + h:(2 + d) * nh + h + 1]
        grow = gr_ref[0, (2 + d) * nh + h:(2 + d) * nh + h + 1, rs]
        return q_ref[0, rs, cs], k_ref[0, rs, cs], v_ref[0, rs, cs], beta, gcol, grow

    data = [load(*p) for p in probs]
    gams = []
    for (d, _, _), (_, _, _, _, gcol, grow) in zip(probs, data):
        incl = (row >= col) if d == 0 else (row <= col)
        gams.append(jnp.exp(jnp.where(incl, gcol - grow, -jnp.inf)))
    kks = [lax.dot_general(k, k, xpose, preferred_element_type=f32) for (_, k, _, _, _, _) in data]
    qks = [lax.dot_general(q, k, xpose, preferred_element_type=f32) for (q, k, _, _, _, _) in data]
    a_list = []
    for (d, _, _), (_, _, _, beta, _, _), kk, gam in zip(probs, data, kks, gams):
        strict = (row > col) if d == 0 else (row < col)
        a_list.append(jnp.where(strict, beta * kk * gam, 0.0))
    tinvs = _tri_inverse_many(a_list, masks)
    egcs = [jnp.exp(gcol) for (_, _, _, _, gcol, _) in data]
    uws = []
    for (q, k, v, beta, gcol, _), tinv, egc in zip(data, tinvs, egcs):
        rhs = jnp.concatenate([beta * v.astype(f32), (beta * egc) * k.astype(f32)], axis=1).astype(bf16)
        uws.append(jnp.dot(tinv.astype(bf16), rhs, preferred_element_type=f32))
    attns = [(qk * gam).astype(bf16) for qk, gam in zip(qks, gams)]
    index = {p: i for i, p in enumerate(probs)}
    chains = [(d, h) for d in range(2) for h in range(nh)]
    states = [s_ref[d * nh + h] for d, h in chains]
    for step in range(nchunk):
        ids = [index[(d, h, step if d == 0 else nchunk - 1 - step)] for d, h in chains]
        wss = []
        for i, state in zip(ids, states):
            q, _, _, _, _, _ = data[i]
            wq = jnp.concatenate([uws[i][:, B_DV:], q.astype(f32) * egcs[i]], axis=0).astype(bf16)
            wss.append(jnp.dot(wq, state.astype(bf16), preferred_element_type=f32))
        v_news = [(uws[i][:, :B_DV] - ws[:c]).astype(bf16) for i, ws in zip(ids, wss)]
        for (d, h), i, ws, v_new in zip(chains, ids, wss, v_news):
            ci = probs[i][2]
            dir_refs[d][5][0, ci * c:(ci + 1) * c, h * B_DV:(h + 1) * B_DV] = ws[c:] + jnp.dot(attns[i], v_new, preferred_element_type=f32)
        new_states = []
        for (d, h), i, state, v_new in zip(chains, ids, states, v_news):
            _, k, _, _, gcol, _ = data[i]
            last = c - 1 if d == 0 else 0
            gl = gcol[last:last + 1, :]
            kd = (k.astype(f32) * jnp.exp(gl - gcol)).astype(bf16)
            new_states.append(jnp.exp(gl) * state + lax.dot_general(kd, v_new, (((0,), (0,)), ((), ())), preferred_element_type=f32))
        states = new_states
    for (d, h), state in zip(chains, states):
        s_ref[d * nh + h] = state


def _gdn(q, k, v, gc, gr):
    bsz, s, hd = q.shape
    t = min(T_GDN_STEP, s)
    nb = s // t
    fwd = lambda b, n: (b, n, 0)
    bwd = lambda b, n: (b, nb - 1 - n, 0)
    def specs(im, im_t):
        return [pl.BlockSpec((1, t, hd), im)] * 3 + [pl.BlockSpec((1, t, 16), im), pl.BlockSpec((1, 16, t), im_t)]
    return pl.pallas_call(
        _gdn_kernel,
        grid=(bsz, nb),
        in_specs=specs(fwd, lambda b, n: (b, 0, n)) + specs(bwd, lambda b, n: (b, 0, nb - 1 - n)),
        out_specs=[pl.BlockSpec((1, t, hd), fwd), pl.BlockSpec((1, t, hd), bwd)],
        out_shape=[jax.ShapeDtypeStruct((bsz, s, hd), jnp.float32)] * 2,
        scratch_shapes=[pltpu.VMEM((2 * B_HEADS, B_DK, B_DV), jnp.float32)],
        compiler_params=pltpu.CompilerParams(dimension_semantics=("parallel", "arbitrary"), vmem_limit_bytes=VMEM_LIMIT),
        name="gdn",
    )(q, k, v, gc, gr, q, k, v, gc, gr)


def _split(p, sizes):
    return jnp.split(p, np.cumsum(sizes)[:-1].tolist(), axis=-1)


def _layernorm(x, g, b):
    xc = x - jnp.mean(x, axis=-1, keepdims=True)
    y = xc * lax.rsqrt(jnp.mean(xc * xc, axis=-1, keepdims=True) + EPS)
    return y * g + b


def _head_rms(t, g):
    bsz, s, h, d = t.shape
    y = t * lax.rsqrt(jnp.mean(t * t, axis=-1, keepdims=True) + EPS)
    return y.reshape(bsz, s, h * d) * g


def _l2n(t):
    return t * lax.rsqrt(jnp.sum(t * t, axis=-1, keepdims=True) + EPS)


def _dwconv(x, w):
    return lax.conv_general_dilated(x, w[:, None, :].astype(x.dtype), window_strides=(1,), padding='SAME', dimension_numbers=('NWC', 'WIO', 'NWC'), feature_group_count=x.shape[-1])


def _flip(t):
    return jnp.flip(t, axis=1)


def _to_chunks(t):
    bsz, s, h = t.shape[:3]
    t = t.reshape((bsz, s // CHUNK, CHUNK, h) + t.shape[3:])
    return jnp.moveaxis(t, (1, 3), (0, 2))


def _from_chunks(t):
    nc, bsz, h, l = t.shape[:4]
    t = jnp.moveaxis(t, (0, 2), (1, 3))
    return t.reshape((bsz, nc * l, h) + t.shape[4:])


def _mlstm_chunkwise(q, k, v, i_pre, logf):
    q, k, v, i_pre, logf = (_to_chunks(t) for t in (q, k, v, i_pre, logf))
    nc, bsz, h = q.shape[:3]
    causal = jnp.tril(jnp.ones((CHUNK, CHUNK), dtype=bool))
    b = jnp.cumsum(logf, axis=-1)
    dmat = jnp.where(causal, b[..., :, None] - b[..., None, :] + i_pre[..., None, :], -jnp.inf)
    dmax = jnp.max(dmat, axis=-1)
    qk = jnp.einsum('nbhld,nbhsd->nbhls', q, k)
    a_end = b[..., -1:] - b + i_pre

    def step(carry, xs):
        cmat, nvec, m = carry
        qc, kc, vc, bc, dc, dmc, qkc, aec = xs
        inter = bc + m[..., None]
        mt = jnp.maximum(inter, dmc)
        w_int = jnp.exp(inter - mt)
        sc = jnp.exp(dc - mt[..., None]) * qkc
        num = w_int[..., None] * jnp.einsum('bhld,bhde->bhle', qc, cmat) + jnp.einsum('bhls,bhse->bhle', sc, vc)
        den = w_int * jnp.einsum('bhld,bhd->bhl', qc, nvec) + jnp.sum(sc, axis=-1)
        hc = num / jnp.maximum(jnp.abs(den), jnp.exp(-mt))[..., None]
        m_new = jnp.maximum(bc[..., -1] + m, jnp.max(aec, axis=-1))
        w_old = jnp.exp(bc[..., -1] + m - m_new)
        kw = kc * jnp.exp(aec - m_new[..., None])[..., None]
        cmat = w_old[..., None, None] * cmat + jnp.einsum('bhld,bhle->bhde', kw, vc)
        nvec = w_old[..., None] * nvec + jnp.sum(kw, axis=-2)
        return (cmat, nvec, m_new), hc

    init = (jnp.zeros((bsz, h, A_DK, A_DV), jnp.float32), jnp.zeros((bsz, h, A_DK), jnp.float32), jnp.zeros((bsz, h), jnp.float32))
    _, hs = lax.scan(step, init, (q, k, v, b, dmat, dmax, qk, a_end))
    return _from_chunks(hs)


def _gdn_chunked(q, k, v, beta, g):
    q, k, v, beta, g = (_to_chunks(t) for t in (q, k, v, beta, g))
    nc, bsz, h = q.shape[:3]
    tril = jnp.tril(jnp.ones((CHUNK, CHUNK), dtype=bool))
    strict = jnp.tril(jnp.ones((CHUNK, CHUNK), dtype=bool), -1)
    gc = jnp.cumsum(g, axis=-1)
    gam = jnp.exp(jnp.where(tril, gc[..., :, None] - gc[..., None, :], -jnp.inf))
    a = jnp.where(strict, beta[..., :, None] * jnp.einsum('nbhid,nbhjd->nbhij', k, k) * gam, 0.0)
    tmat = a + jnp.eye(CHUNK, dtype=a.dtype)
    u = lax.linalg.triangular_solve(tmat, beta[..., None] * v, left_side=True, lower=True, unit_diagonal=True)
    w = lax.linalg.triangular_solve(tmat, (beta * jnp.exp(gc))[..., None] * k, left_side=True, lower=True, unit_diagonal=True)
    attn = jnp.einsum('nbhid,nbhjd->nbhij', q, k) * gam

    def step(state, xs):
        qc, kc, uc, wc, gcc, ac = xs
        v_new = uc - jnp.einsum('bhld,bhde->bhle', wc, state)
        o = jnp.einsum('bhld,bhde->bhle', qc * jnp.exp(gcc)[..., None], state) + jnp.einsum('bhls,bhse->bhle', ac, v_new)
        gl = gcc[..., -1]
        state = jnp.exp(gl)[..., None, None] * state + jnp.einsum('bhld,bhle->bhde', kc * jnp.exp(gl[..., None] - gcc)[..., None], v_new)
        return state, o

    _, os_ = lax.scan(step, jnp.zeros((bsz, h, B_DK, B_DV), jnp.float32), (q, k, u, w, gc, attn))
    return _from_chunks(os_)


def _t5_bucket(rel):
    half = REL_BUCKETS // 2
    exact = half // 2
    n = jnp.abs(rel)
    large = exact + (jnp.log(jnp.maximum(n, 1).astype(jnp.float32) / exact) / math.log(REL_MAX_DIST / exact) * (half - exact)).astype(jnp.int32)
    large = jnp.minimum(large, half - 1)
    return (rel > 0).astype(jnp.int32) * half + jnp.where(n < exact, n, large)


def _dilated_group(q, k, v, dilation, radius, rel_bias):
    bsz, s, h, dh = q.shape
    ls = s // dilation
    nb = -(-ls // radius)
    lp = nb * radius

    def sub(t, lo, hi):
        t = t.reshape(bsz, ls, dilation, h, dh).transpose(0, 3, 2, 1, 4)
        return jnp.pad(t, ((0, 0), (0, 0), (0, 0), (lo, hi), (0, 0)))

    qb = sub(q, 0, lp - ls).reshape(bsz, h, dilation, nb, radius, dh)

    def band(t):
        t = sub(t, radius, lp - ls + radius).reshape(bsz, h, dilation, nb + 2, radius, dh)
        return jnp.concatenate([t[:, :, :, :-2], t[:, :, :, 1:-1], t[:, :, :, 2:]], axis=4)

    kb, vb = band(k), band(v)
    qi = jnp.arange(radius)[:, None]
    kj = jnp.arange(3 * radius)[None, :]
    rel = kj - radius - qi
    kpos = jnp.arange(nb)[:, None, None] * radius + kj - radius
    valid = (jnp.abs(rel) <= radius) & (kpos >= 0) & (kpos < ls)
    bias = jnp.transpose(rel_bias[_t5_bucket(rel * dilation)], (2, 0, 1)).astype(jnp.float32)
    sc = jnp.einsum('bhrnid,bhrnjd->bhrnij', qb, kb).astype(jnp.float32) * (dh ** -0.5) + bias[:, None, None]
    sc = jnp.where(valid, sc, NEG)
    m = jnp.max(sc, axis=-1, keepdims=True)
    p = jnp.exp(sc - m)
    den = jnp.sum(p, axis=-1)
    o = jnp.einsum('bhrnij,bhrnjd->bhrnid', p, vb.astype(jnp.float32)) / den[..., None]
    lse = m[..., 0] + jnp.log(den)
    o = o.reshape(bsz, h, dilation, lp, dh)[:, :, :, :ls].transpose(0, 3, 2, 1, 4).reshape(bsz, s, h, dh)
    lse = lse.reshape(bsz, h, dilation, lp)[:, :, :, :ls].transpose(0, 3, 2, 1).reshape(bsz, s, h)
    return o, lse


def _dilated_attention(q, k, v, rel_bias):
    outs, lses = [], []
    for window, dilation in D_GROUPS:
        o, l = _dilated_group(q, k, v, dilation, window // (2 * dilation), rel_bias)
        outs.append(o)
        lses.append(l)
    wts = jax.nn.softmax(jnp.stack(lses, axis=0), axis=0)
    return jnp.sum(wts[..., None] * jnp.stack(outs, axis=0), axis=0)


def _even_mixer_core(p, m_gate_b, dn_dt_bias, dn_a_log, dn_conv_w, m_norm_g, dn_norm_g):
    bsz, s, _ = p.shape
    f32 = jnp.float32
    mq, mk, mv, mo, mg, dqkv, dg, z = _split(p, EVEN_SPLITS)
    q = mq.reshape(bsz, s, A_HEADS, A_DK)
    k = mk.reshape(bsz, s, A_HEADS, A_DK) * (A_DK ** -0.5)
    v = mv.reshape(bsz, s, A_HEADS, A_DV)
    gt = mg.reshape(bsz, s, 4, A_HEADS) + m_gate_b
    logf = jax.nn.log_sigmoid(gt[:, :, 2:4])
    h_fwd = _mlstm_chunkwise(q, k, v, gt[:, :, 0], logf[:, :, 0])
    h_bwd = _flip(_mlstm_chunkwise(_flip(q), _flip(k), _flip(v), _flip(gt[:, :, 1]), _flip(logf[:, :, 1])))
    out_a = jax.nn.sigmoid(mo) * _head_rms(h_fwd + h_bwd, m_norm_g)
    qkv = jax.nn.silu(_dwconv(dqkv, dn_conv_w))
    bq, bk, bv = _split(qkv, (B_HEADS * B_DK, B_HEADS * B_DK, B_HEADS * B_DV))
    q = _l2n(bq.reshape(bsz, s, B_HEADS, B_DK)) * (B_DK ** -0.5)
    k = _l2n(bk.reshape(bsz, s, B_HEADS, B_DK))
    v = bv.reshape(bsz, s, B_HEADS, B_DV)
    gb = dg.reshape(bsz, s, 4, B_HEADS)
    beta = jax.nn.sigmoid(gb[:, :, 0:2])
    decay = -jnp.exp(dn_a_log) * jax.nn.softplus(gb[:, :, 2:4] + dn_dt_bias)
    o_fwd = _gdn_chunked(q, k, v, beta[:, :, 0], decay[:, :, 0])
    o_bwd = _flip(_gdn_chunked(_flip(q), _flip(k), _flip(v), _flip(beta[:, :, 1]), _flip(decay[:, :, 1])))
    out_b = _head_rms(o_fwd + o_bwd, dn_norm_g)
    return jnp.concatenate([out_a, out_b], axis=-1), z


def _odd_mixer_core(p, dw_w, dw_b, ln_g, ln_b, rel_bias):
    bsz, s, _ = p.shape
    ga, gb, aq, ak, av, z = _split(p, ODD_SPLITS)
    u = _dwconv(ga * jax.nn.sigmoid(gb), dw_w) + dw_b
    out_c = jax.nn.silu(_layernorm(u, ln_g, ln_b))
    shp = (bsz, s, D_HEADS, D_DH)
    out_d = _dilated_attention(aq.reshape(shp), ak.reshape(shp), av.reshape(shp), rel_bias).reshape(bsz, s, D_HEADS * D_DH)
    return jnp.concatenate([out_c, out_d], axis=-1), z


def kernel(x, c, norm_g, ada_w, ada_b, ev_w_in, ev_m_gate_b, ev_dn_dt_bias, ev_dn_a_log, ev_dn_conv_w, ev_m_norm_g, ev_dn_norm_g, ev_w_out, od_w_in, od_dw_w, od_dw_b, od_ln_g, od_ln_b, od_w_out, rel_bias, final_g):
    assert DEPTH == 2, "the final RMSNorm is fused into the (last) odd layer's output projection"
    assert all(window // (2 * dil) == R_ATT for window, dil in D_GROUPS)
    d = x.shape[-1]
    mod = _adaln(c, ada_w, ada_b)
    for layer in range(DEPTH):
        shift, scale, gate = (mod[layer, :, i * d:(i + 1) * d][:, None, :] for i in range(3))
        j = layer // 2
        if layer % 2 == 0:
            x = _even_layer(x, norm_g[layer], scale, shift, gate, ev_w_in[j], ev_m_gate_b[j], ev_dn_dt_bias[j], ev_dn_a_log[j],
                            ev_dn_conv_w[j], ev_m_norm_g[j], ev_dn_norm_g[j], ev_w_out[j])
        else:
            x = _odd_layer_final(x, norm_g[layer], scale, shift, gate, od_w_in[j], od_dw_w[j], od_dw_b[j], od_ln_g[j], od_ln_b[j],
                                 rel_bias, od_w_out[j], final_g)
    return x


def _even_layer(x, norm_g, scale, shift, gate, w_in, m_gate_b, dn_dt_bias, dn_a_log, dn_conv_w, m_norm_g, dn_norm_g, w_out):
    bf16 = jnp.bfloat16
    mq, mk, mv, mo, mg, dqkv, dg, z = _split(w_in, EVEN_SPLITS)
    w = jnp.concatenate([mq, mv, dqkv], axis=1).astype(bf16)
    wkt = (mk * (A_DK ** -0.5)).T.astype(bf16)
    wg = jnp.concatenate([mg, dg], axis=1).astype(bf16)
    wz = jnp.concatenate([mo, z], axis=1).astype(bf16)
    pq, pkt, pv, bq, bk, bv, mc, mr, gc, gr = _inproj_even(x, norm_g, scale, shift, w, wkt, wg.T, dn_conv_w, m_gate_b, dn_a_log, dn_dt_bias)
    hf, hb = _mlstm(pq, pkt, pv, mc, mr)
    of, ob = _gdn(bq, bk, bv, gc, gr)
    return _outproj_even(hf, hb, of, ob, x, norm_g, scale, shift, gate, m_norm_g, dn_norm_g, wz, w_out.astype(bf16))


def _odd_layer_final(x, norm_g, scale, shift, gate, w_in, dw_w, dw_b, ln_g, ln_b, rel_bias, w_out, final_g):
    bf16 = jnp.bfloat16
    ga, gb, aq, ak, av, z = _split(w_in, ODD_SPLITS)
    w = jnp.concatenate([ga, gb, aq * (D_DH ** -0.5), ak, av], axis=1).astype(bf16)
    out_c, pq, pk, pv = _inproj_odd(x, norm_g, scale, shift, w, dw_w, dw_b, ln_g, ln_b)
    og, lg = zip(*[_dilated_group_call(qd, kd, vd, rel_bias, dil) for qd, kd, vd, dil in zip(pq, pk, pv, DILATIONS)])
    return _outproj_odd_final(out_c, og, lg, x, norm_g, scale, shift, gate, final_g, z.astype(bf16), w_out.astype(bf16))
```

```python
import math
from functools import partial

import jax
import jax.numpy as jnp
import numpy as np
from jax import lax
from jax.experimental import pallas as pl
from jax.experimental.pallas import tpu as pltpu

D_MODEL = 1024
BATCH = 4
SEQ = 8192
DEPTH = 2
A_HEADS = 4
A_DK = 64
A_DV = 128
B_HEADS = 4
B_DK = 128
B_DV = 128
B_CONV = 5
C_WIDTH = 512
C_CONV = 31
D_HEADS = 8
D_DH = 64
D_GROUPS = ((128, 1), (512, 4), (2048, 16))
REL_BUCKETS = 32
REL_MAX_DIST = 1024
CHUNK = 64
EPS = 1e-6
NEG = -1e30
MIX_EVEN = A_HEADS * A_DV + B_HEADS * B_DV
MIX_ODD = C_WIDTH + D_HEADS * D_DH
B_QKV = B_HEADS * (2 * B_DK + B_DV)
EVEN_SPLITS = (A_HEADS * A_DK, A_HEADS * A_DK, A_HEADS * A_DV, A_HEADS * A_DV, 4 * A_HEADS, B_QKV, 4 * B_HEADS, MIX_EVEN)
ODD_SPLITS = (C_WIDTH, C_WIDTH, D_HEADS * D_DH, D_HEADS * D_DH, D_HEADS * D_DH, MIX_ODD)

VMEM_LIMIT = 56 * 1024 * 1024
TM_PROJ = 512


def _adaln_kernel(c_ref, w_ref, b_ref, o_ref):
    c = c_ref[...]
    cs = (c * jax.nn.sigmoid(c)).astype(jnp.bfloat16)
    o_ref[0] = jnp.dot(cs, w_ref[0].astype(jnp.bfloat16), preferred_element_type=jnp.float32) + b_ref[0]


def _adaln(c, ada_w, ada_b):
    depth, d, n3 = ada_w.shape
    bsz = c.shape[0]
    tn = 1024
    return pl.pallas_call(
        _adaln_kernel,
        grid=(depth, n3 // tn),
        in_specs=[pl.BlockSpec((bsz, d), lambda l, j: (0, 0)), pl.BlockSpec((1, d, tn), lambda l, j: (l, 0, j)),
                  pl.BlockSpec((1, 1, tn), lambda l, j: (l, 0, j))],
        out_specs=pl.BlockSpec((1, bsz, tn), lambda l, j: (l, 0, j)),
        out_shape=jax.ShapeDtypeStruct((depth, bsz, n3), jnp.float32),
        compiler_params=pltpu.CompilerParams(dimension_semantics=("parallel", "parallel")),
        name="adaln",
    )(c, ada_w, ada_b.reshape(depth, 1, n3))


def _modulated_rms_val(x, g, scale, shift):
    y = x * lax.rsqrt(jnp.mean(x * x, axis=-1, keepdims=True) + EPS)
    return ((y * g) * (1.0 + scale) + shift).astype(jnp.bfloat16)


def _modulated_rms(x_ref, g_ref, sc_ref, sh_ref):
    return _modulated_rms_val(x_ref[0], g_ref[...], sc_ref[0], sh_ref[0])


_EV_COLS = {"mq": (0, 256), "mv": (256, 768), "dqkv": (768, 2304)}
_OD_COLS = {"ga": (0, 512), "gb": (512, 1024), "aq": (1024, 1536), "ak": (1536, 2048), "av": (2048, 2560)}


HALO_X = 16
C_GDN = 64


def _softplus(t):
    return jnp.maximum(t, 0.0) + jnp.log1p(jnp.exp(-jnp.abs(t)))


def _seg_scan_lanes(x, seg, reverse, op, fill):
    n = x.shape[1]
    pos = lax.broadcasted_iota(jnp.int32, x.shape, 1) % seg
    k = 1
    while k < seg:
        if reverse:
            x = op(x, jnp.where(pos < seg - k, pltpu.roll(x, n - k, axis=1), fill))
        else:
            x = op(x, jnp.where(pos >= k, pltpu.roll(x, k, axis=1), fill))
        k *= 2
    return x


def _seg_cumsum_lanes(x, seg, reverse):
    return _seg_scan_lanes(x, seg, reverse, jnp.add, 0.0)


def _seg_cummax_lanes(x, seg, reverse):
    return _seg_scan_lanes(x, seg, reverse, jnp.maximum, -jnp.inf)


def _conv_taps(xe, w_ref, cs, width, t, halo):
    n = t + 2 * halo
    acc = None
    for j in range(width):
        off = halo - width // 2 + j
        shifted = xe[off:off + t] if off % SUBLANES == 0 else pltpu.roll(xe, n - off, axis=0)[0:t]
        term = shifted * w_ref[j:j + 1, cs]
        acc = term if acc is None else acc + term
    return acc


def _zero_outside(d, t, halo, first, last):
    return jnp.concatenate([jnp.where(first, 0.0, d[:halo]), d[halo:halo + t], jnp.where(last, 0.0, d[halo + t:])], axis=0)


def _inproj_even_kernel(x_ref, xp_ref, xn_ref, g_ref, sc_ref, sh_ref, w_ref, wkt_ref, wgt_ref, cw_ref, mb_ref, a_ref, dt_ref,
                        mq_ref, mkt_ref, mv_ref, bq_ref, bk_ref, bv_ref, bkt_ref, mc_ref, mr_ref, gc_ref, gr_ref):
    i = pl.program_id(1)
    nt = pl.num_programs(1)
    tm = x_ref.shape[1]
    f32 = jnp.float32
    x_ext = jnp.concatenate([xp_ref[0], x_ref[0], xn_ref[0]], axis=0)
    h_ext = _modulated_rms_val(x_ext, g_ref[...], sc_ref[0], sh_ref[0])
    h = h_ext[HALO_X:HALO_X + tm]
    xpose = (((1,), (1,)), ((), ()))
    for name, o_ref in (("mq", mq_ref), ("mv", mv_ref)):
        lo, hi = _EV_COLS[name]
        o_ref[0] = jnp.dot(h, w_ref[:, lo:hi], preferred_element_type=f32).astype(o_ref.dtype)
    mkt_ref[0] = lax.dot_general(wkt_ref[...], h, xpose, preferred_element_type=f32).astype(mkt_ref.dtype)
    gates_t = lax.dot_general(wgt_ref[...], h, xpose, preferred_element_type=f32)
    na = A_HEADS
    gm = gates_t[:16] + mb_ref[...]
    logf = _log_sigmoid(gm[2 * na:])
    b_f, b_b = _seg_cumsum_lanes(logf[:na], L_MLSTM, False), _seg_cumsum_lanes(logf[na:], L_MLSTM, True)
    pm_f = _seg_cummax_lanes(gm[:na] - b_f, L_MLSTM, False)
    pm_b = _seg_cummax_lanes(gm[na:2 * na] - b_b, L_MLSTM, True)
    mr_ref[0] = jnp.concatenate([gm[:2 * na], b_f, b_b], axis=0)
    mc_ref[0] = jnp.concatenate([b_f, b_b, pm_f, pm_b], axis=0).T
    nh = B_HEADS
    dgt = gates_t[16:]
    dec = -jnp.exp(a_ref[...]) * _softplus(dgt[2 * nh:] + dt_ref[...])
    gr = jnp.concatenate([jax.nn.sigmoid(dgt[:2 * nh]), _seg_cumsum_lanes(dec[:nh], C_GDN, False),
                          _seg_cumsum_lanes(dec[nh:], C_GDN, True)], axis=0)
    gr_ref[0] = gr
    gc_ref[0] = gr.T
    hd = B_HEADS * B_DK
    for part, o_ref in enumerate((bq_ref, bk_ref, bv_ref)):
        lo = _EV_COLS["dqkv"][0] + part * hd
        cs = slice(part * hd, (part + 1) * hd)
        d = jnp.dot(h_ext, w_ref[:, lo:lo + hd], preferred_element_type=f32)
        acc = _conv_taps(_zero_outside(d, tm, HALO_X, i == 0, i == nt - 1), cw_ref, cs, B_CONV, tm, HALO_X)
        y = acc * jax.nn.sigmoid(acc)
        for hh in range(B_HEADS):
            yh = y[:, hh * B_DK:(hh + 1) * B_DK]
            if part == 0:
                yh = yh * lax.rsqrt(jnp.sum(yh * yh, axis=-1, keepdims=True) + EPS) * (B_DK ** -0.5)
            elif part == 1:
                yh = yh * lax.rsqrt(jnp.sum(yh * yh, axis=-1, keepdims=True) + EPS)
            o_ref[0, :, hh * B_DK:(hh + 1) * B_DK] = yh.astype(o_ref.dtype)
            if part == 1:
                bkt_ref[0, hh * B_DK:(hh + 1) * B_DK, :] = yh.T.astype(bkt_ref.dtype)


def _halo_specs(tm, s, d):
    hb = tm // HALO_X
    return [pl.BlockSpec((1, HALO_X, d), lambda b, i: (b, jnp.maximum(i * hb - 1, 0), 0)),
            pl.BlockSpec((1, HALO_X, d), lambda b, i: (b, jnp.minimum((i + 1) * hb, s // HALO_X - 1), 0))]


def _inproj_even(x, g, scale, shift, w, wkt, wgt, conv_w, m_gate_b, a_log, dt_bias):
    bsz, s, d = x.shape
    tm = TM_PROJ
    assert tm % L_MLSTM == 0 and tm % C_GDN == 0
    tok = lambda b, i: (b, i, 0)
    tok_t = lambda b, i: (b, 0, i)
    const = lambda b, i: (0, 0)
    bvec = lambda b, i: (b, 0, 0)
    bf16, f32 = jnp.bfloat16, jnp.float32
    hk, hv, hd = A_HEADS * A_DK, A_HEADS * A_DV, B_HEADS * B_DK
    tok_specs = lambda wd: pl.BlockSpec((1, tm, wd), tok)
    gate_specs = [pl.BlockSpec((1, tm, 16), tok), pl.BlockSpec((1, 16, tm), tok_t)]
    gate_shapes = [jax.ShapeDtypeStruct((bsz, s, 16), f32), jax.ShapeDtypeStruct((bsz, 16, s), f32)]
    return pl.pallas_call(
        _inproj_even_kernel,
        grid=(bsz, s // tm),
        in_specs=[pl.BlockSpec((1, tm, d), tok)] + _halo_specs(tm, s, d) + [pl.BlockSpec((1, d), const), pl.BlockSpec((1, 1, d), bvec),
                  pl.BlockSpec((1, 1, d), bvec), pl.BlockSpec(w.shape, const), pl.BlockSpec(wkt.shape, const), pl.BlockSpec(wgt.shape, const),
                  pl.BlockSpec(conv_w.shape, const), pl.BlockSpec((16, 1), const), pl.BlockSpec((8, 1), const), pl.BlockSpec((8, 1), const)],
        out_specs=[tok_specs(hk), pl.BlockSpec((1, hk, tm), tok_t), tok_specs(hv), tok_specs(hd), tok_specs(hd), tok_specs(hd),
                   pl.BlockSpec((1, hd, tm), tok_t)] + gate_specs + gate_specs,
        out_shape=[jax.ShapeDtypeStruct((bsz, s, hk), bf16), jax.ShapeDtypeStruct((bsz, hk, s), bf16), jax.ShapeDtypeStruct((bsz, s, hv), bf16)]
        + [jax.ShapeDtypeStruct((bsz, s, hd), bf16)] * 3 + [jax.ShapeDtypeStruct((bsz, hd, s), bf16)] + gate_shapes + gate_shapes,
        compiler_params=pltpu.CompilerParams(dimension_semantics=("parallel", "parallel"), vmem_limit_bytes=VMEM_LIMIT),
        name="inproj_even",
    )(x, x, x, g.reshape(1, d), scale, shift, w, wkt, wgt, conv_w, m_gate_b.reshape(16, 1), a_log.reshape(8, 1), dt_bias.reshape(8, 1))


DILATIONS = tuple(dil for _, dil in D_GROUPS)
LANES = 128
SUBLANES = 8


SUB_C = 64


def _inproj_odd_kernel(x_ref, xp_ref, xn_ref, g_ref, sc_ref, sh_ref, w_ref, cw_ref, cb_ref, lg_ref, lb_ref, oc_ref, *rest):
    out_refs, plane_ref = rest[:-1], rest[-1]
    i = pl.program_id(1)
    nt = pl.num_programs(1)
    f32 = jnp.float32
    tm = x_ref.shape[1]
    nd = D_HEADS * D_DH
    x_ext = jnp.concatenate([xp_ref[0], x_ref[0], xn_ref[0]], axis=0)
    h_ext = _modulated_rms_val(x_ext, g_ref[...], sc_ref[0], sh_ref[0])
    h = h_ext[HALO_X:HALO_X + tm]
    dot = lambda name: jnp.dot(h, w_ref[:, _OD_COLS[name][0]:_OD_COLS[name][1]], preferred_element_type=f32)
    dot_ext = lambda name: jnp.dot(h_ext, w_ref[:, _OD_COLS[name][0]:_OD_COLS[name][1]], preferred_element_type=f32)
    xe = _zero_outside(dot_ext("ga") * jax.nn.sigmoid(dot_ext("gb")), tm, HALO_X, i == 0, i == nt - 1)
    n = tm + 2 * HALO_X
    phases = [xe] + [pltpu.roll(xe, n - b, axis=0) for b in range(1, SUBLANES)]

    def conv_rows(r0):
        acc = None
        for j in range(C_CONV):
            a, b = divmod(HALO_X - C_CONV // 2 + j, SUBLANES)
            lo = a * SUBLANES + r0
            term = phases[b][lo:lo + SUB_C] * cw_ref[j:j + 1, :]
            acc = term if acc is None else acc + term
        u = acc + cb_ref[...]
        uc = u - jnp.mean(u, axis=-1, keepdims=True)
        y = uc * lax.rsqrt(jnp.mean(uc * uc, axis=-1, keepdims=True) + EPS) * lg_ref[...] + lb_ref[...]
        oc_ref[0, r0:r0 + SUB_C, :] = y * jax.nn.sigmoid(y)

    def attention_operand(a, name):
        r = dot(name)
        group_refs = out_refs[a * len(DILATIONS):(a + 1) * len(DILATIONS)]
        for j in range(nd // LANES):
            plane_ref[a, j] = r[:, j * LANES:(j + 1) * LANES]
        for dil, o_ref in zip(DILATIONS, group_refs):
            if dil == 1:
                o_ref[0, 0] = r.astype(o_ref.dtype)
                continue
            for res in range(dil):
                for j in range(nd // LANES):
                    o_ref[0, res, :, j * LANES:(j + 1) * LANES] = plane_ref[a, j, pl.ds(res, tm // dil, stride=dil), :].astype(o_ref.dtype)

    row_blocks = list(range(0, tm, SUB_C))
    names = ("aq", "ak", "av")
    per = -(-len(row_blocks) // (len(names) + 1))
    for a in range(len(names) + 1):
        for r0 in row_blocks[a * per:(a + 1) * per]:
            conv_rows(r0)
        if a < len(names):
            attention_operand(a, names[a])


def _inproj_odd(x, g, scale, shift, w, dw_w, dw_b, ln_g, ln_b):
    bsz, s, d = x.shape
    tm = TM_PROJ
    tok = lambda b, i: (b, i, 0)
    const = lambda b, i: (0, 0)
    bvec = lambda b, i: (b, 0, 0)
    bf16, f32 = jnp.bfloat16, jnp.float32
    nd = D_HEADS * D_DH
    cw = C_WIDTH
    att_specs = [pl.BlockSpec((1, dil, tm // dil, nd), lambda b, i: (b, 0, i, 0)) for dil in DILATIONS] * 3
    att_shapes = [jax.ShapeDtypeStruct((bsz, dil, s // dil, nd), bf16) for dil in DILATIONS] * 3
    outs = pl.pallas_call(
        _inproj_odd_kernel,
        grid=(bsz, s // tm),
        in_specs=[pl.BlockSpec((1, tm, d), tok)] + _halo_specs(tm, s, d) + [pl.BlockSpec((1, d), const), pl.BlockSpec((1, 1, d), bvec),
                  pl.BlockSpec((1, 1, d), bvec), pl.BlockSpec(w.shape, const), pl.BlockSpec((C_CONV, cw), const)]
        + [pl.BlockSpec((1, cw), const)] * 3,
        out_specs=[pl.BlockSpec((1, tm, cw), tok)] + att_specs,
        out_shape=[jax.ShapeDtypeStruct((bsz, s, cw), f32)] + att_shapes,
        scratch_shapes=[pltpu.VMEM((3, nd // LANES, tm, LANES), f32)],
        compiler_params=pltpu.CompilerParams(dimension_semantics=("parallel", "parallel"), vmem_limit_bytes=VMEM_LIMIT),
        name="inproj_odd",
    )(x, x, x, g.reshape(1, d), scale, shift, w, dw_w, dw_b.reshape(1, cw), ln_g.reshape(1, cw), ln_b.reshape(1, cw))
    ng = len(DILATIONS)
    return outs[0], outs[1:1 + ng], outs[1 + ng:1 + 2 * ng], outs[1 + 2 * ng:]


def _head_rms_cols(t, g, width):
    parts = []
    for h in range(t.shape[1] // width):
        th = t[:, h * width:(h + 1) * width]
        parts.append(th * lax.rsqrt(jnp.mean(th * th, axis=-1, keepdims=True) + EPS))
    return jnp.concatenate(parts, axis=1) * g


def _outproj_even_kernel(hf_ref, hb_ref, of_ref, ob_ref, x_ref, g_ref, sc_ref, sh_ref, gate_ref, mg_ref, dg_ref, wz_ref, w_ref, o_ref):
    f32, bf16 = jnp.float32, jnp.bfloat16
    na = A_HEADS * A_DV
    h = _modulated_rms(x_ref, g_ref, sc_ref, sh_ref)
    mo = jnp.dot(h, wz_ref[:, :na], preferred_element_type=f32)
    z = jnp.dot(h, wz_ref[:, na:], preferred_element_type=f32)
    sz = z * jax.nn.sigmoid(z)
    out_a = jax.nn.sigmoid(mo) * _head_rms_cols(hf_ref[0] + hb_ref[0], mg_ref[...], A_DV)
    out_b = _head_rms_cols(of_ref[0] + ob_ref[0], dg_ref[...], B_DV)
    y = jnp.dot((out_a * sz[:, :na]).astype(bf16), w_ref[:na, :], preferred_element_type=f32)
    y = y + jnp.dot((out_b * sz[:, na:]).astype(bf16), w_ref[na:, :], preferred_element_type=f32)
    o_ref[0] = x_ref[0] + gate_ref[0] * y


def _outproj_even(hf, hb, of, ob, x, norm_g, scale, shift, gate, m_norm_g, dn_norm_g, wz, w):
    bsz, s, d = x.shape
    tm = TM_PROJ
    tok = lambda b, i: (b, i, 0)
    const = lambda b, i: (0, 0)
    bvec = lambda b, i: (b, 0, 0)
    na, nb = A_HEADS * A_DV, B_HEADS * B_DV
    return pl.pallas_call(
        _outproj_even_kernel,
        grid=(bsz, s // tm),
        in_specs=[pl.BlockSpec((1, tm, na), tok)] * 2 + [pl.BlockSpec((1, tm, nb), tok)] * 2 + [pl.BlockSpec((1, tm, d), tok),
                  pl.BlockSpec((1, d), const), pl.BlockSpec((1, 1, d), bvec), pl.BlockSpec((1, 1, d), bvec), pl.BlockSpec((1, 1, d), bvec),
                  pl.BlockSpec((1, na), const), pl.BlockSpec((1, nb), const), pl.BlockSpec(wz.shape, const), pl.BlockSpec(w.shape, const)],
        out_specs=pl.BlockSpec((1, tm, d), tok),
        out_shape=jax.ShapeDtypeStruct((bsz, s, d), jnp.float32),
        compiler_params=pltpu.CompilerParams(dimension_semantics=("parallel", "parallel"), vmem_limit_bytes=VMEM_LIMIT),
        name="outproj_even",
    )(hf, hb, of, ob, x, norm_g.reshape(1, d), scale, shift, gate, m_norm_g.reshape(1, na), dn_norm_g.reshape(1, nb), wz, w)


def _outproj_odd_kernel(oc_ref, o1_ref, o2_ref, o3_ref, l1_ref, l2_ref, l3_ref, x_ref, g_ref, sc_ref, sh_ref, gate_ref, fg_ref,
                        wz_ref, w_ref, o_ref, nat_ref):
    f32, bf16 = jnp.float32, jnp.bfloat16
    tm = x_ref.shape[1]
    npl = D_HEADS * D_DH // LANES
    z = jnp.dot(_modulated_rms(x_ref, g_ref, sc_ref, sh_ref), wz_ref[...], preferred_element_type=f32)
    sz = z * jax.nn.sigmoid(z)
    groups = []
    for gi, (dil, og_ref, lg_ref) in enumerate(zip(DILATIONS, (o1_ref, o2_ref, o3_ref), (l1_ref, l2_ref, l3_ref))):
        if dil == 1:
            groups.append(([og_ref[0, 0, :, j * LANES:(j + 1) * LANES] for j in range(npl)], lg_ref[0, 0]))
            continue
        for res in range(dil):
            rows = pl.ds(res, tm // dil, stride=dil)
            for j in range(npl):
                nat_ref[gi, j, rows, :] = og_ref[0, res, :, j * LANES:(j + 1) * LANES]
            nat_ref[gi, npl, rows, :] = lg_ref[0, res]
        groups.append(([nat_ref[gi, j] for j in range(npl)], nat_ref[gi, npl]))
    (p1, l1), (p2, l2), (p3, l3) = groups
    lm = jnp.maximum(jnp.maximum(l1, l2), l3)
    e1, e2, e3 = jnp.exp(l1 - lm), jnp.exp(l2 - lm), jnp.exp(l3 - lm)
    inv = 1.0 / (e1 + e2 + e3)
    low = lax.broadcasted_iota(jnp.int32, (tm, LANES), 1) < D_DH
    planes = []
    for j in range(npl):
        acc = None
        for e, p in ((e1, p1), (e2, p2), (e3, p3)):
            wgt = e * inv
            term = jnp.where(low, wgt[:, 2 * j:2 * j + 1], wgt[:, 2 * j + 1:2 * j + 2]) * p[j]
            acc = term if acc is None else acc + term
        planes.append(acc)
    out_d = jnp.concatenate(planes, axis=1)
    y = jnp.dot((oc_ref[0] * sz[:, :C_WIDTH]).astype(bf16), w_ref[:C_WIDTH, :], preferred_element_type=f32)
    y = y + jnp.dot((out_d * sz[:, C_WIDTH:]).astype(bf16), w_ref[C_WIDTH:, :], preferred_element_type=f32)
    xn = x_ref[0] + gate_ref[0] * y
    o_ref[0] = xn * lax.rsqrt(jnp.mean(xn * xn, axis=-1, keepdims=True) + EPS) * fg_ref[...]


def _outproj_odd_final(oc, og, lg, x, norm_g, scale, shift, gate, final_g, wz, w):
    bsz, s, d = x.shape
    tm = TM_PROJ
    tok = lambda b, i: (b, i, 0)
    const = lambda b, i: (0, 0)
    bvec = lambda b, i: (b, 0, 0)
    nd = D_HEADS * D_DH
    res_major = lambda width: [pl.BlockSpec((1, dil, tm // dil, width), lambda b, i: (b, 0, i, 0)) for dil in DILATIONS]
    return pl.pallas_call(
        _outproj_odd_kernel,
        grid=(bsz, s // tm),
        in_specs=[pl.BlockSpec((1, tm, C_WIDTH), tok)] + res_major(nd) + res_major(LANES)
        + [pl.BlockSpec((1, tm, d), tok), pl.BlockSpec((1, d), const), pl.BlockSpec((1, 1, d), bvec), pl.BlockSpec((1, 1, d), bvec),
           pl.BlockSpec((1, 1, d), bvec), pl.BlockSpec((1, d), const), pl.BlockSpec(wz.shape, const), pl.BlockSpec(w.shape, const)],
        out_specs=pl.BlockSpec((1, tm, d), tok),
        out_shape=jax.ShapeDtypeStruct((bsz, s, d), jnp.float32),
        scratch_shapes=[pltpu.VMEM((len(DILATIONS), nd // LANES + 1, tm, LANES), jnp.float32)],
        compiler_params=pltpu.CompilerParams(dimension_semantics=("parallel", "parallel"), vmem_limit_bytes=VMEM_LIMIT),
        name="outproj_odd",
    )(oc, *og, *lg, x, norm_g.reshape(1, d), scale, shift, gate, final_g.reshape(1, d), wz, w)


T_CONV = 512
HALO_C = 16
SUB_C = 64


def _conformer_kernel(x_ref, xp_ref, xn_ref, w_ref, b_ref, lg_ref, lb_ref, o_ref, xe_ref, ph_ref):
    i = pl.program_id(1)
    nt = pl.num_programs(1)
    t = x_ref.shape[1]
    xe_ref[0:HALO_C, :] = jnp.where(i > 0, xp_ref[0], 0.0)
    xe_ref[HALO_C:HALO_C + t, :] = x_ref[0]
    xe_ref[HALO_C + t:, :] = jnp.where(i < nt - 1, xn_ref[0], 0.0)
    half = C_CONV // 2
    n = t + 2 * HALO_C
    xe = xe_ref[...]
    for b in range(1, SUBLANES):
        ph_ref[b - 1] = pltpu.roll(xe, n - b, axis=0)
    for r0 in range(0, t, SUB_C):
        acc = None
        for j in range(C_CONV):
            a, b = divmod(HALO_C - half + j, SUBLANES)
            lo = a * SUBLANES + r0
            src = xe_ref[lo:lo + SUB_C, :] if b == 0 else ph_ref[b - 1, lo:lo + SUB_C, :]
            term = src * w_ref[j:j + 1, :]
            acc = term if acc is None else acc + term
        u = acc + b_ref[...]
        uc = u - jnp.mean(u, axis=-1, keepdims=True)
        y = uc * lax.rsqrt(jnp.mean(uc * uc, axis=-1, keepdims=True) + EPS) * lg_ref[...] + lb_ref[...]
        o_ref[0, r0:r0 + SUB_C, :] = y * jax.nn.sigmoid(y)


def _conformer(glu, dw_w, dw_b, ln_g, ln_b):
    bsz, s, cw = glu.shape
    t = min(T_CONV, s)
    hb = t // HALO_C
    cur = lambda b, i: (b, i, 0)
    const = lambda b, i: (0, 0)
    return pl.pallas_call(
        _conformer_kernel,
        grid=(bsz, s // t),
        in_specs=[pl.BlockSpec((1, t, cw), cur),
                  pl.BlockSpec((1, HALO_C, cw), lambda b, i: (b, jnp.maximum(i * hb - 1, 0), 0)),
                  pl.BlockSpec((1, HALO_C, cw), lambda b, i: (b, jnp.minimum((i + 1) * hb, s // HALO_C - 1), 0)),
                  pl.BlockSpec((C_CONV, cw), const)] + [pl.BlockSpec((1, cw), const)] * 3,
        out_specs=pl.BlockSpec((1, t, cw), cur),
        out_shape=jax.ShapeDtypeStruct((bsz, s, cw), jnp.float32),
        scratch_shapes=[pltpu.VMEM((t + 2 * HALO_C, cw), jnp.float32), pltpu.VMEM((SUBLANES - 1, t + 2 * HALO_C, cw), jnp.float32)],
        compiler_params=pltpu.CompilerParams(dimension_semantics=("parallel", "parallel"), vmem_limit_bytes=VMEM_LIMIT),
        name="conformer",
    )(glu, glu, glu, dw_w, dw_b.reshape(1, cw), ln_g.reshape(1, cw), ln_b.reshape(1, cw))


TQ_ATT = 128
TB_ATT = 1024
R_ATT = 64


def _dilated_kernel(q_ref, kc_ref, kp_ref, kn_ref, vc_ref, vp_ref, vn_ref, bias_ref, o_ref, lse_ref, kx_ref, vx_ref):
    i = pl.program_id(2)
    nt = pl.num_programs(2)
    tb = q_ref.shape[1]
    tq = TQ_ATT
    nk = tq + 2 * R_ATT
    nsub = tb // tq
    f32, bf16 = jnp.float32, jnp.bfloat16
    kx_ref[0:R_ATT, :] = kp_ref[0]
    kx_ref[R_ATT:R_ATT + tb, :] = kc_ref[0]
    kx_ref[R_ATT + tb:, :] = kn_ref[0]
    vx_ref[0:R_ATT, :] = vp_ref[0]
    vx_ref[R_ATT:R_ATT + tb, :] = vc_ref[0]
    vx_ref[R_ATT + tb:, :] = vn_ref[0]
    kj = lax.broadcasted_iota(jnp.int32, (tq, nk), 1)
    lane = lax.broadcasted_iota(jnp.int32, (tq, 128), 1)
    low = lane < D_DH
    heads = [(pr, hi) for pr in range(D_HEADS // 2) for hi in (False, True)]
    for sub in range(nsub):
        qs = slice(sub * tq, (sub + 1) * tq)
        ks = slice(sub * tq, sub * tq + nk)
        outside = None
        if sub == 0:
            outside = (kj < R_ATT) & (i == 0)
        if sub == nsub - 1:
            after = (kj >= R_ATT + tq) & (i == nt - 1)
            outside = after if outside is None else outside | after
        scs = []
        for pr, hi in heads:
            ps = slice(pr * 128, (pr + 1) * 128)
            qp = q_ref[0, qs, ps]
            qh = jnp.where(low != hi, qp, jnp.zeros_like(qp))
            sc = lax.dot_general(qh, kx_ref[ks, ps], (((1,), (1,)), ((), ())), preferred_element_type=f32) + bias_ref[2 * pr + int(hi)]
            scs.append(sc if outside is None else jnp.where(outside, NEG, sc))
        ms = [jnp.max(sc, axis=-1, keepdims=True) for sc in scs]
        ps_ = [jnp.exp(sc - m) for sc, m in zip(scs, ms)]
        dens = [jnp.sum(p, axis=-1, keepdims=True) for p in ps_]
        pvs = [jnp.dot(p.astype(bf16), vx_ref[ks, pr * 128:(pr + 1) * 128], preferred_element_type=f32) for (pr, _), p in zip(heads, ps_)]
        lse_all = jnp.zeros((tq, 128), f32)
        for pr in range(D_HEADS // 2):
            lo, hi = 2 * pr, 2 * pr + 1
            o_ref[0, qs, pr * 128:(pr + 1) * 128] = jnp.where(low, pvs[lo] / dens[lo], pvs[hi] / dens[hi])
            lse_all = jnp.where(lane == lo, ms[lo] + jnp.log(dens[lo]), lse_all)
            lse_all = jnp.where(lane == hi, ms[hi] + jnp.log(dens[hi]), lse_all)
        lse_ref[0, qs, :] = lse_all


def _dilated_bias(rel_bias, dilation, tq):
    half = REL_BUCKETS // 2
    exact = half // 2
    qi = jnp.arange(tq)[:, None]
    kj = jnp.arange(tq + 2 * R_ATT)[None, :]
    rel = kj - R_ATT - qi
    reld = rel * dilation
    n = jnp.abs(reld)
    large = exact + (jnp.log(jnp.maximum(n, 1).astype(jnp.float32) / exact) / math.log(REL_MAX_DIST / exact) * (half - exact)).astype(jnp.int32)
    large = jnp.minimum(large, half - 1)
    bucket = (reld > 0).astype(jnp.int32) * half + jnp.where(n < exact, n, large)
    bias = jnp.zeros((rel_bias.shape[1],) + bucket.shape, jnp.float32)
    for b in range(REL_BUCKETS):
        bias = jnp.where((bucket == b)[None], rel_bias[b].astype(jnp.float32)[:, None, None], bias)
    return jnp.where((jnp.abs(rel) <= R_ATT)[None], bias, NEG)


def _dilated_group_call(q, k, v, rel_bias, dilation):
    bsz, dil, ls, nd = q.shape
    assert dil == dilation
    tb, tq = min(TB_ATT, ls), TQ_ATT
    assert ls % tb == 0 and tb % tq == 0
    nt = ls // tb
    hb = tb // R_ATT
    nk = tq + 2 * R_ATT
    cur = lambda b, r, i: (b, r, i, 0)
    prev = lambda b, r, i: (b, r, jnp.maximum(i * hb - 1, 0), 0)
    nxt = lambda b, r, i: (b, r, jnp.minimum((i + 1) * hb, ls // R_ATT - 1), 0)
    kv_specs = [pl.BlockSpec((1, None, tb, nd), cur), pl.BlockSpec((1, None, R_ATT, nd), prev), pl.BlockSpec((1, None, R_ATT, nd), nxt)]
    return pl.pallas_call(
        _dilated_kernel,
        grid=(bsz, dilation, nt),
        in_specs=[pl.BlockSpec((1, None, tb, nd), cur)] + kv_specs + kv_specs + [pl.BlockSpec((D_HEADS, tq, nk), lambda b, r, i: (0, 0, 0))],
        out_specs=[pl.BlockSpec((1, None, tb, nd), cur), pl.BlockSpec((1, None, tb, 128), cur)],
        out_shape=[jax.ShapeDtypeStruct((bsz, dilation, ls, nd), jnp.float32), jax.ShapeDtypeStruct((bsz, dilation, ls, 128), jnp.float32)],
        scratch_shapes=[pltpu.VMEM((tb + 2 * R_ATT, nd), jnp.bfloat16)] * 2,
        compiler_params=pltpu.CompilerParams(dimension_semantics=("parallel", "parallel", "parallel"), vmem_limit_bytes=VMEM_LIMIT),
        name=f"dilated_d{dilation}",
    )(q, k, k, k, v, v, v, _dilated_bias(rel_bias, dilation, tq))


def _inproj(x, g, scale, shift, w_bf16):
    bsz, s, d = x.shape
    n = w_bf16.shape[1]
    return pl.pallas_call(
        _inproj_kernel,
        grid=(bsz, s // TM_PROJ),
        in_specs=[
            pl.BlockSpec((1, TM_PROJ, d), lambda b, i: (b, i, 0)),
            pl.BlockSpec((1, d), lambda b, i: (0, 0)),
            pl.BlockSpec((1, 1, d), lambda b, i: (b, 0, 0)),
            pl.BlockSpec((1, 1, d), lambda b, i: (b, 0, 0)),
            pl.BlockSpec((d, n), lambda b, i: (0, 0)),
        ],
        out_specs=pl.BlockSpec((1, TM_PROJ, n), lambda b, i: (b, i, 0)),
        out_shape=jax.ShapeDtypeStruct((bsz, s, n), jnp.float32),
        compiler_params=pltpu.CompilerParams(dimension_semantics=("parallel", "parallel"), vmem_limit_bytes=VMEM_LIMIT),
        name="inproj",
    )(x, g.reshape(1, d), scale, shift, w_bf16)


def _outproj_kernel(mix_ref, z_ref, x_ref, gate_ref, w_ref, o_ref):
    z = z_ref[0]
    m = mix_ref[0] * (z * jax.nn.sigmoid(z))
    y = jnp.dot(m.astype(jnp.bfloat16), w_ref[...], preferred_element_type=jnp.float32)
    o_ref[0] = x_ref[0] + gate_ref[0] * y


def _outproj(mix, z, x, gate, w_bf16):
    bsz, s, d = x.shape
    k = mix.shape[-1]
    return pl.pallas_call(
        _outproj_kernel,
        grid=(bsz, s // TM_PROJ),
        in_specs=[
            pl.BlockSpec((1, TM_PROJ, k), lambda b, i: (b, i, 0)),
            pl.BlockSpec((1, TM_PROJ, k), lambda b, i: (b, i, 0)),
            pl.BlockSpec((1, TM_PROJ, d), lambda b, i: (b, i, 0)),
            pl.BlockSpec((1, 1, d), lambda b, i: (b, 0, 0)),
            pl.BlockSpec((k, d), lambda b, i: (0, 0)),
        ],
        out_specs=pl.BlockSpec((1, TM_PROJ, d), lambda b, i: (b, i, 0)),
        out_shape=jax.ShapeDtypeStruct((bsz, s, d), jnp.float32),
        compiler_params=pltpu.CompilerParams(dimension_semantics=("parallel", "parallel"), vmem_limit_bytes=VMEM_LIMIT),
        name="outproj",
    )(mix, z, x, gate, w_bf16)


def _final_rms_kernel(x_ref, g_ref, o_ref):
    x = x_ref[0]
    o_ref[0] = x * lax.rsqrt(jnp.mean(x * x, axis=-1, keepdims=True) + EPS) * g_ref[...]


def _final_rms(x, g):
    bsz, s, d = x.shape
    tm = 512
    return pl.pallas_call(
        _final_rms_kernel,
        grid=(bsz, s // tm),
        in_specs=[pl.BlockSpec((1, tm, d), lambda b, i: (b, i, 0)), pl.BlockSpec((1, d), lambda b, i: (0, 0))],
        out_specs=pl.BlockSpec((1, tm, d), lambda b, i: (b, i, 0)),
        out_shape=jax.ShapeDtypeStruct((bsz, s, d), jnp.float32),
        compiler_params=pltpu.CompilerParams(dimension_semantics=("parallel", "parallel")),
        name="final_rms",
    )(x, g.reshape(1, d))


L_MLSTM = 256
_HI = lax.Precision.HIGHEST


def _log_sigmoid(t):
    return jnp.minimum(t, 0.0) - jnp.log(1.0 + jnp.exp(-jnp.abs(t)))


def _mlstm_kernel(qf_ref, kf_ref, vf_ref, gf_ref, gtf_ref, qb_ref, kb_ref, vb_ref, gb_ref, gtb_ref,
                  hf_ref, hb_ref, c_ref, m_ref):
    n = pl.program_id(1)
    ln = qf_ref.shape[1]
    f32, bf16 = jnp.float32, jnp.bfloat16

    @pl.when(n == 0)
    def _():
        c_ref[...] = jnp.zeros_like(c_ref)
        m_ref[...] = jnp.zeros_like(m_ref)

    row = lax.broadcasted_iota(jnp.int32, (ln, ln), 0)
    col = lax.broadcasted_iota(jnp.int32, (ln, ln), 1)
    ones_blk = jnp.ones((ln, A_DV), bf16)
    dirs = ((0, qf_ref, kf_ref, vf_ref, gf_ref, gtf_ref, hf_ref), (1, qb_ref, kb_ref, vb_ref, gb_ref, gtb_ref, hb_ref))
    probs = []
    na = A_HEADS
    for d, q_ref, kt_ref, v_ref, gc_ref, gr_ref, h_ref in dirs:
        mask = (row >= col) if d == 0 else (row <= col)
        for h in range(na):
            r = d * na + h
            probs.append(dict(
                r=r, h=h, mask=mask, last=ln - 1 if d == 0 else 0, h_ref=h_ref,
                q=q_ref[0, :, h * A_DK:(h + 1) * A_DK],
                kt=kt_ref[0, h * A_DK:(h + 1) * A_DK, :],
                vaug=jnp.concatenate([v_ref[0, :, h * A_DV:(h + 1) * A_DV], ones_blk], axis=1),
                bcol=gc_ref[0, :, r:r + 1], pmcol=gc_ref[0, :, 2 * na + r:2 * na + r + 1],
                brow=gr_ref[0, 2 * na + r:2 * na + r + 1, :],
                vrow=gr_ref[0, r:r + 1, :] - gr_ref[0, 2 * na + r:2 * na + r + 1, :]))
    for p in probs:
        p["m_old"] = m_ref[p["r"]:p["r"] + 1, 0:1]
        p["caug"] = c_ref[p["r"]]
        p["qk"] = jnp.dot(p["q"], p["kt"], preferred_element_type=f32)
    for p in probs:
        p["qc"] = jnp.dot(p["q"], p["caug"].astype(bf16), preferred_element_type=f32)
    for p in probs:
        mstab = jnp.maximum(p["m_old"], p["pmcol"])
        p["w_int"] = jnp.exp(p["m_old"] - mstab)
        p["emt"] = jnp.exp(-(p["bcol"] + mstab))
        p["sc"] = (jnp.exp(jnp.where(p["mask"], p["vrow"] - mstab, -jnp.inf)) * p["qk"]).astype(bf16)
    for p in probs:
        tot = p["w_int"] * p["qc"] + jnp.dot(p["sc"], p["vaug"], preferred_element_type=f32)
        den = jnp.maximum(jnp.abs(tot[:, A_DV:]), p["emt"])
        p["h_ref"][0, :, p["h"] * A_DV:(p["h"] + 1) * A_DV] = tot[:, :A_DV] / den
    for p in probs:
        last, brow, vrow = p["last"], p["brow"], p["vrow"]
        btot = brow[:, last:last + 1]
        m_new = btot + jnp.maximum(p["m_old"], jnp.max(vrow, axis=-1, keepdims=True))
        w_old = jnp.exp(btot + p["m_old"] - m_new)
        kwt = (p["kt"].astype(f32) * jnp.exp(btot + vrow - m_new)).astype(bf16)
        c_ref[p["r"]] = w_old * p["caug"] + jnp.dot(kwt, p["vaug"], preferred_element_type=f32)
        m_ref[p["r"]:p["r"] + 1, :] = jnp.broadcast_to(m_new, (1, m_ref.shape[1]))


def _mlstm(q, kt, v, gc, gr):
    bsz, s, _ = q.shape
    ln = min(L_MLSTM, s)
    nc = s // ln
    hk, hv = A_HEADS * A_DK, A_HEADS * A_DV
    fwd = lambda b, n: (b, n, 0)
    bwd = lambda b, n: (b, nc - 1 - n, 0)
    fwd_t = lambda b, n: (b, 0, n)
    bwd_t = lambda b, n: (b, 0, nc - 1 - n)
    def specs(im, im_t):
        return [pl.BlockSpec((1, ln, hk), im), pl.BlockSpec((1, hk, ln), im_t), pl.BlockSpec((1, ln, hv), im),
                pl.BlockSpec((1, ln, 16), im), pl.BlockSpec((1, 16, ln), im_t)]
    return pl.pallas_call(
        _mlstm_kernel,
        grid=(bsz, nc),
        in_specs=specs(fwd, fwd_t) + specs(bwd, bwd_t),
        out_specs=[pl.BlockSpec((1, ln, hv), fwd), pl.BlockSpec((1, ln, hv), bwd)],
        out_shape=[jax.ShapeDtypeStruct((bsz, s, hv), jnp.float32)] * 2,
        scratch_shapes=[pltpu.VMEM((2 * A_HEADS, A_DK, 2 * A_DV), jnp.float32), pltpu.VMEM((2 * A_HEADS, 128), jnp.float32)],
        compiler_params=pltpu.CompilerParams(dimension_semantics=("parallel", "arbitrary"), vmem_limit_bytes=VMEM_LIMIT),
        name="mlstm",
    )(q, kt, v, gc, gr, q, kt, v, gc, gr)


T_GDN = 256
T_GDN_STEP = 256
B_GDN_STEP = 2
C_GDN = 64
HALO = 8


def _softplus(t):
    return jnp.maximum(t, 0.0) + jnp.log1p(jnp.exp(-jnp.abs(t)))


def _gdn_prep_kernel(x_ref, xp_ref, xn_ref, g_ref, gt_ref, w_ref, a_ref, at_ref, dt_ref, dtt_ref,
                     q_ref, k_ref, v_ref, gc_ref, gr_ref, xe_ref):
    i = pl.program_id(1)
    nt = pl.num_programs(1)
    t = x_ref.shape[1]
    f32 = jnp.float32
    hd = B_HEADS * B_DK
    xe_ref[0:HALO, :] = jnp.where(i > 0, xp_ref[0], 0.0)
    xe_ref[HALO:HALO + t, :] = x_ref[0]
    xe_ref[HALO + t:, :] = jnp.where(i < nt - 1, xn_ref[0], 0.0)
    half = B_CONV // 2
    for part, o_ref in enumerate((q_ref, k_ref, v_ref)):
        cs = slice(part * hd, (part + 1) * hd)
        xe = xe_ref[:, cs]
        acc = None
        for j in range(B_CONV):
            off = HALO - half + j
            shifted = xe[off:off + t] if off % SUBLANES == 0 else pltpu.roll(xe, t + 2 * HALO - off, axis=0)[0:t]
            term = shifted * w_ref[j:j + 1, cs]
            acc = term if acc is None else acc + term
        y = acc * jax.nn.sigmoid(acc)
        for h in range(B_HEADS):
            yh = y[:, h * B_DK:(h + 1) * B_DK]
            if part == 0:
                yh = yh * lax.rsqrt(jnp.sum(yh * yh, axis=-1, keepdims=True) + EPS) * (B_DK ** -0.5)
            elif part == 1:
                yh = yh * lax.rsqrt(jnp.sum(yh * yh, axis=-1, keepdims=True) + EPS)
            o_ref[0, :, h * B_DK:(h + 1) * B_DK] = yh.astype(o_ref.dtype)
    row = lax.broadcasted_iota(jnp.int32, (t, t), 0)
    col = lax.broadcasted_iota(jnp.int32, (t, t), 1)
    same = (row // C_GDN) == (col // C_GDN)
    lower = (same & (row >= col)).astype(f32)
    upper = (same & (row <= col)).astype(f32)
    g = g_ref[0]
    gt = gt_ref[0]
    nh = B_HEADS
    dec = -jnp.exp(a_ref[...]) * _softplus(g[:, 2 * nh:] + dt_ref[...])
    dect = -jnp.exp(at_ref[...]) * _softplus(gt[2 * nh:, :] + dtt_ref[...])
    gc_ref[0, :, 0:2 * nh] = jax.nn.sigmoid(g[:, 0:2 * nh])
    gc_ref[0, :, 2 * nh:3 * nh] = jnp.dot(lower, dec[:, 0:nh], precision=_HI, preferred_element_type=f32)
    gc_ref[0, :, 3 * nh:] = jnp.dot(upper, dec[:, nh:], precision=_HI, preferred_element_type=f32)
    gr_ref[0, 0:2 * nh, :] = jax.nn.sigmoid(gt[0:2 * nh, :])
    gr_ref[0, 2 * nh:3 * nh, :] = jnp.dot(dect[0:nh, :], upper, precision=_HI, preferred_element_type=f32)
    gr_ref[0, 3 * nh:, :] = jnp.dot(dect[nh:, :], lower, precision=_HI, preferred_element_type=f32)


def _gdn_prep(dqkv, g, gt, conv_w, a_log, dt_bias):
    bsz, s, n3 = dqkv.shape
    t = min(T_GDN, s)
    nt = s // t
    hd = B_HEADS * B_DK
    hb = t // HALO
    cur = lambda b, i: (b, i, 0)
    const = lambda b, i: (0, 0)
    bf16 = jnp.bfloat16
    return pl.pallas_call(
        _gdn_prep_kernel,
        grid=(bsz, nt),
        in_specs=[
            pl.BlockSpec((1, t, n3), cur),
            pl.BlockSpec((1, HALO, n3), lambda b, i: (b, jnp.maximum(i * hb - 1, 0), 0)),
            pl.BlockSpec((1, HALO, n3), lambda b, i: (b, jnp.minimum((i + 1) * hb, s // HALO - 1), 0)),
            pl.BlockSpec((1, t, 16), cur),
            pl.BlockSpec((1, 16, t), lambda b, i: (b, 0, i)),
            pl.BlockSpec((B_CONV, n3), const),
            pl.BlockSpec((1, 8), const), pl.BlockSpec((8, 1), const),
            pl.BlockSpec((1, 8), const), pl.BlockSpec((8, 1), const),
        ],
        out_specs=[pl.BlockSpec((1, t, hd), cur)] * 3 + [pl.BlockSpec((1, t, 16), cur), pl.BlockSpec((1, 16, t), lambda b, i: (b, 0, i))],
        out_shape=[jax.ShapeDtypeStruct((bsz, s, hd), bf16)] * 3 + [jax.ShapeDtypeStruct((bsz, s, 16), jnp.float32), jax.ShapeDtypeStruct((bsz, 16, s), jnp.float32)],
        scratch_shapes=[pltpu.VMEM((t + 2 * HALO, n3), jnp.float32)],
        compiler_params=pltpu.CompilerParams(dimension_semantics=("parallel", "parallel"), vmem_limit_bytes=VMEM_LIMIT),
        name="gdn_prep",
    )(dqkv, dqkv, dqkv, g, gt, conv_w, a_log.reshape(1, 8), a_log.reshape(8, 1), dt_bias.reshape(1, 8), dt_bias.reshape(8, 1))


def _tri_inverse_many(a_list, masks):
    eye, m16, m32, m64 = masks
    f32, bf16 = jnp.float32, jnp.bfloat16
    mm = lambda x, y: jnp.dot(x.astype(bf16), y.astype(bf16), preferred_element_type=f32)
    ads = [jnp.where(m16, a, 0.0) for a in a_list]
    xs = [eye - ad for ad in ads]
    ps = [mm(ad, ad) for ad in ads]
    for stage in range(3):
        xs = [x + mm(x, p) for x, p in zip(xs, ps)]
        if stage < 2:
            ps = [mm(p, p) for p in ps]
    for lo, hi in ((m16, m32), (m32, m64)):
        off = hi & ~lo
        ys = [mm(jnp.where(off, a, 0.0), x) for a, x in zip(a_list, xs)]
        xs = [x - mm(x, y) for x, y in zip(xs, ys)]
    return xs


def _gdn_kernel(qf_ref, kf_ref, ktf_ref, vf_ref, gcf_ref, grf_ref, qb_ref, kb_ref, ktb_ref, vb_ref, gcb_ref, grb_ref,
                of_ref, ob_ref, s_ref):
    n = pl.program_id(1)
    nbat, t = qf_ref.shape[0], qf_ref.shape[1]
    c = C_GDN
    f32, bf16 = jnp.float32, jnp.bfloat16

    @pl.when(n == 0)
    def _():
        s_ref[...] = jnp.zeros_like(s_ref)

    row = lax.broadcasted_iota(jnp.int32, (c, c), 0)
    col = lax.broadcasted_iota(jnp.int32, (c, c), 1)
    eye = (row == col).astype(f32)
    blk = lambda w: (row // w) == (col // w)
    masks = (eye, blk(16), blk(32), blk(64))
    nh, nchunk = B_HEADS, t // c
    dir_refs = ((qf_ref, kf_ref, ktf_ref, vf_ref, gcf_ref, grf_ref, of_ref), (qb_ref, kb_ref, ktb_ref, vb_ref, gcb_ref, grb_ref, ob_ref))
    probs = [(bi, d, h, ci) for bi in range(nbat) for d in range(2) for h in range(nh) for ci in range(nchunk)]
    xpose = (((1,), (1,)), ((), ()))

    def load(bi, d, h, ci):
        q_ref, k_ref, kt_ref, v_ref, gc_ref, gr_ref, _ = dir_refs[d]
        rs, cs = slice(ci * c, (ci + 1) * c), slice(h * B_DK, (h + 1) * B_DK)
        return dict(
            q=q_ref[bi, rs, cs], k=k_ref[bi, rs, cs], kt=kt_ref[bi, cs, rs], v=v_ref[bi, rs, cs],
            beta=gc_ref[bi, rs, d * nh + h:d * nh + h + 1],
            gcol=gc_ref[bi, rs, (2 + d) * nh + h:(2 + d) * nh + h + 1],
            grow=gr_ref[bi, (2 + d) * nh + h:(2 + d) * nh + h + 1, rs])

    data = [load(*p) for p in probs]
    for (_, d, _, _), p in zip(probs, data):
        incl = (row >= col) if d == 0 else (row <= col)
        p["gam"] = jnp.exp(jnp.where(incl, p["gcol"] - p["grow"], -jnp.inf))
    for p in data:
        p["kk"] = lax.dot_general(p["k"], p["k"], xpose, preferred_element_type=f32)
    for p in data:
        p["qk"] = lax.dot_general(p["q"], p["k"], xpose, preferred_element_type=f32)
    a_list = []
    for (_, d, _, _), p in zip(probs, data):
        strict = (row > col) if d == 0 else (row < col)
        a_list.append(jnp.where(strict, p["beta"] * p["kk"] * p["gam"], 0.0))
    tinvs = _tri_inverse_many(a_list, masks)
    for p, tinv in zip(data, tinvs):
        p["egc"] = jnp.exp(p["gcol"])
        rhs = jnp.concatenate([p["beta"] * p["v"].astype(f32), (p["beta"] * p["egc"]) * p["k"].astype(f32)], axis=1).astype(bf16)
        p["uw"] = jnp.dot(tinv.astype(bf16), rhs, preferred_element_type=f32)
        p["attn"] = (p["qk"] * p["gam"]).astype(bf16)
    index = {p: i for i, p in enumerate(probs)}
    chains = [(bi, d, h) for bi in range(nbat) for d in range(2) for h in range(nh)]
    slot = lambda bi, d, h: (bi * 2 + d) * nh + h
    states = [s_ref[slot(*ch)] for ch in chains]
    for step in range(nchunk):
        cur = [data[index[(bi, d, h, step if d == 0 else nchunk - 1 - step)]] for bi, d, h in chains]
        wss = []
        for p, state in zip(cur, states):
            wq = jnp.concatenate([p["uw"][:, B_DV:], p["q"].astype(f32) * p["egc"]], axis=0).astype(bf16)
            wss.append(jnp.dot(wq, state.astype(bf16), preferred_element_type=f32))
        v_news = [(p["uw"][:, :B_DV] - ws[:c]).astype(bf16) for p, ws in zip(cur, wss)]
        for (bi, d, h), p, ws, v_new in zip(chains, cur, wss, v_news):
            ci = step if d == 0 else nchunk - 1 - step
            dir_refs[d][6][bi, ci * c:(ci + 1) * c, h * B_DV:(h + 1) * B_DV] = ws[c:] + jnp.dot(p["attn"], v_new, preferred_element_type=f32)
        new_states = []
        for (bi, d, h), p, state, v_new in zip(chains, cur, states, v_news):
            last = c - 1 if d == 0 else 0
            gl = p["grow"][:, last:last + 1]
            kdt = (p["kt"].astype(f32) * jnp.exp(gl - p["grow"])).astype(bf16)
            new_states.append(jnp.exp(gl) * state + jnp.dot(kdt, v_new, preferred_element_type=f32))
        states = new_states
    for ch, state in zip(chains, states):
        s_ref[slot(*ch)] = state


def _gdn(q, k, kt, v, gc, gr):
    bsz, s, hd = q.shape
    t = min(T_GDN_STEP, s)
    nb = s // t
    bb = B_GDN_STEP if bsz % B_GDN_STEP == 0 else 1
    fwd = lambda b, n: (b, n, 0)
    bwd = lambda b, n: (b, nb - 1 - n, 0)
    def specs(im, im_t):
        return [pl.BlockSpec((bb, t, hd), im)] * 2 + [pl.BlockSpec((bb, hd, t), im_t), pl.BlockSpec((bb, t, hd), im),
                                                      pl.BlockSpec((bb, t, 16), im), pl.BlockSpec((bb, 16, t), im_t)]
    return pl.pallas_call(
        _gdn_kernel,
        grid=(bsz // bb, nb),
        in_specs=specs(fwd, lambda b, n: (b, 0, n)) + specs(bwd, lambda b, n: (b, 0, nb - 1 - n)),
        out_specs=[pl.BlockSpec((bb, t, hd), fwd), pl.BlockSpec((bb, t, hd), bwd)],
        out_shape=[jax.ShapeDtypeStruct((bsz, s, hd), jnp.float32)] * 2,
        scratch_shapes=[pltpu.VMEM((bb * 2 * B_HEADS, B_DK, B_DV), jnp.float32)],
        compiler_params=pltpu.CompilerParams(dimension_semantics=("parallel", "arbitrary"), vmem_limit_bytes=VMEM_LIMIT),
        name="gdn",
    )(q, k, kt, v, gc, gr, q, k, kt, v, gc, gr)


def _split(p, sizes):
    return jnp.split(p, np.cumsum(sizes)[:-1].tolist(), axis=-1)


def _layernorm(x, g, b):
    xc = x - jnp.mean(x, axis=-1, keepdims=True)
    y = xc * lax.rsqrt(jnp.mean(xc * xc, axis=-1, keepdims=True) + EPS)
    return y * g + b


def _head_rms(t, g):
    bsz, s, h, d = t.shape
    y = t * lax.rsqrt(jnp.mean(t * t, axis=-1, keepdims=True) + EPS)
    return y.reshape(bsz, s, h * d) * g


def _l2n(t):
    return t * lax.rsqrt(jnp.sum(t * t, axis=-1, keepdims=True) + EPS)


def _dwconv(x, w):
    return lax.conv_general_dilated(x, w[:, None, :].astype(x.dtype), window_strides=(1,), padding='SAME', dimension_numbers=('NWC', 'WIO', 'NWC'), feature_group_count=x.shape[-1])


def _flip(t):
    return jnp.flip(t, axis=1)


def _to_chunks(t):
    bsz, s, h = t.shape[:3]
    t = t.reshape((bsz, s // CHUNK, CHUNK, h) + t.shape[3:])
    return jnp.moveaxis(t, (1, 3), (0, 2))


def _from_chunks(t):
    nc, bsz, h, l = t.shape[:4]
    t = jnp.moveaxis(t, (0, 2), (1, 3))
    return t.reshape((bsz, nc * l, h) + t.shape[4:])


def _mlstm_chunkwise(q, k, v, i_pre, logf):
    q, k, v, i_pre, logf = (_to_chunks(t) for t in (q, k, v, i_pre, logf))
    nc, bsz, h = q.shape[:3]
    causal = jnp.tril(jnp.ones((CHUNK, CHUNK), dtype=bool))
    b = jnp.cumsum(logf, axis=-1)
    dmat = jnp.where(causal, b[..., :, None] - b[..., None, :] + i_pre[..., None, :], -jnp.inf)
    dmax = jnp.max(dmat, axis=-1)
    qk = jnp.einsum('nbhld,nbhsd->nbhls', q, k)
    a_end = b[..., -1:] - b + i_pre

    def step(carry, xs):
        cmat, nvec, m = carry
        qc, kc, vc, bc, dc, dmc, qkc, aec = xs
        inter = bc + m[..., None]
        mt = jnp.maximum(inter, dmc)
        w_int = jnp.exp(inter - mt)
        sc = jnp.exp(dc - mt[..., None]) * qkc
        num = w_int[..., None] * jnp.einsum('bhld,bhde->bhle', qc, cmat) + jnp.einsum('bhls,bhse->bhle', sc, vc)
        den = w_int * jnp.einsum('bhld,bhd->bhl', qc, nvec) + jnp.sum(sc, axis=-1)
        hc = num / jnp.maximum(jnp.abs(den), jnp.exp(-mt))[..., None]
        m_new = jnp.maximum(bc[..., -1] + m, jnp.max(aec, axis=-1))
        w_old = jnp.exp(bc[..., -1] + m - m_new)
        kw = kc * jnp.exp(aec - m_new[..., None])[..., None]
        cmat = w_old[..., None, None] * cmat + jnp.einsum('bhld,bhle->bhde', kw, vc)
        nvec = w_old[..., None] * nvec + jnp.sum(kw, axis=-2)
        return (cmat, nvec, m_new), hc

    init = (jnp.zeros((bsz, h, A_DK, A_DV), jnp.float32), jnp.zeros((bsz, h, A_DK), jnp.float32), jnp.zeros((bsz, h), jnp.float32))
    _, hs = lax.scan(step, init, (q, k, v, b, dmat, dmax, qk, a_end))
    return _from_chunks(hs)


def _gdn_chunked(q, k, v, beta, g):
    q, k, v, beta, g = (_to_chunks(t) for t in (q, k, v, beta, g))
    nc, bsz, h = q.shape[:3]
    tril = jnp.tril(jnp.ones((CHUNK, CHUNK), dtype=bool))
    strict = jnp.tril(jnp.ones((CHUNK, CHUNK), dtype=bool), -1)
    gc = jnp.cumsum(g, axis=-1)
    gam = jnp.exp(jnp.where(tril, gc[..., :, None] - gc[..., None, :], -jnp.inf))
    a = jnp.where(strict, beta[..., :, None] * jnp.einsum('nbhid,nbhjd->nbhij', k, k) * gam, 0.0)
    tmat = a + jnp.eye(CHUNK, dtype=a.dtype)
    u = lax.linalg.triangular_solve(tmat, beta[..., None] * v, left_side=True, lower=True, unit_diagonal=True)
    w = lax.linalg.triangular_solve(tmat, (beta * jnp.exp(gc))[..., None] * k, left_side=True, lower=True, unit_diagonal=True)
    attn = jnp.einsum('nbhid,nbhjd->nbhij', q, k) * gam

    def step(state, xs):
        qc, kc, uc, wc, gcc, ac = xs
        v_new = uc - jnp.einsum('bhld,bhde->bhle', wc, state)
        o = jnp.einsum('bhld,bhde->bhle', qc * jnp.exp(gcc)[..., None], state) + jnp.einsum('bhls,bhse->bhle', ac, v_new)
        gl = gcc[..., -1]
        state = jnp.exp(gl)[..., None, None] * state + jnp.einsum('bhld,bhle->bhde', kc * jnp.exp(gl[..., None] - gcc)[..., None], v_new)
        return state, o

    _, os_ = lax.scan(step, jnp.zeros((bsz, h, B_DK, B_DV), jnp.float32), (q, k, u, w, gc, attn))
    return _from_chunks(os_)


def _t5_bucket(rel):
    half = REL_BUCKETS // 2
    exact = half // 2
    n = jnp.abs(rel)
    large = exact + (jnp.log(jnp.maximum(n, 1).astype(jnp.float32) / exact) / math.log(REL_MAX_DIST / exact) * (half - exact)).astype(jnp.int32)
    large = jnp.minimum(large, half - 1)
    return (rel > 0).astype(jnp.int32) * half + jnp.where(n < exact, n, large)


def _dilated_group(q, k, v, dilation, radius, rel_bias):
    bsz, s, h, dh = q.shape
    ls = s // dilation
    nb = -(-ls // radius)
    lp = nb * radius

    def sub(t, lo, hi):
        t = t.reshape(bsz, ls, dilation, h, dh).transpose(0, 3, 2, 1, 4)
        return jnp.pad(t, ((0, 0), (0, 0), (0, 0), (lo, hi), (0, 0)))

    qb = sub(q, 0, lp - ls).reshape(bsz, h, dilation, nb, radius, dh)

    def band(t):
        t = sub(t, radius, lp - ls + radius).reshape(bsz, h, dilation, nb + 2, radius, dh)
        return jnp.concatenate([t[:, :, :, :-2], t[:, :, :, 1:-1], t[:, :, :, 2:]], axis=4)

    kb, vb = band(k), band(v)
    qi = jnp.arange(radius)[:, None]
    kj = jnp.arange(3 * radius)[None, :]
    rel = kj - radius - qi
    kpos = jnp.arange(nb)[:, None, None] * radius + kj - radius
    valid = (jnp.abs(rel) <= radius) & (kpos >= 0) & (kpos < ls)
    bias = jnp.transpose(rel_bias[_t5_bucket(rel * dilation)], (2, 0, 1)).astype(jnp.float32)
    sc = jnp.einsum('bhrnid,bhrnjd->bhrnij', qb, kb).astype(jnp.float32) * (dh ** -0.5) + bias[:, None, None]
    sc = jnp.where(valid, sc, NEG)
    m = jnp.max(sc, axis=-1, keepdims=True)
    p = jnp.exp(sc - m)
    den = jnp.sum(p, axis=-1)
    o = jnp.einsum('bhrnij,bhrnjd->bhrnid', p, vb.astype(jnp.float32)) / den[..., None]
    lse = m[..., 0] + jnp.log(den)
    o = o.reshape(bsz, h, dilation, lp, dh)[:, :, :, :ls].transpose(0, 3, 2, 1, 4).reshape(bsz, s, h, dh)
    lse = lse.reshape(bsz, h, dilation, lp)[:, :, :, :ls].transpose(0, 3, 2, 1).reshape(bsz, s, h)
    return o, lse


def _dilated_attention(q, k, v, rel_bias):
    outs, lses = [], []
    for window, dilation in D_GROUPS:
        o, l = _dilated_group(q, k, v, dilation, window // (2 * dilation), rel_bias)
        outs.append(o)
        lses.append(l)
    wts = jax.nn.softmax(jnp.stack(lses, axis=0), axis=0)
    return jnp.sum(wts[..., None] * jnp.stack(outs, axis=0), axis=0)


def _even_mixer_core(p, m_gate_b, dn_dt_bias, dn_a_log, dn_conv_w, m_norm_g, dn_norm_g):
    bsz, s, _ = p.shape
    f32 = jnp.float32
    mq, mk, mv, mo, mg, dqkv, dg, z = _split(p, EVEN_SPLITS)
    q = mq.reshape(bsz, s, A_HEADS, A_DK)
    k = mk.reshape(bsz, s, A_HEADS, A_DK) * (A_DK ** -0.5)
    v = mv.reshape(bsz, s, A_HEADS, A_DV)
    gt = mg.reshape(bsz, s, 4, A_HEADS) + m_gate_b
    logf = jax.nn.log_sigmoid(gt[:, :, 2:4])
    h_fwd = _mlstm_chunkwise(q, k, v, gt[:, :, 0], logf[:, :, 0])
    h_bwd = _flip(_mlstm_chunkwise(_flip(q), _flip(k), _flip(v), _flip(gt[:, :, 1]), _flip(logf[:, :, 1])))
    out_a = jax.nn.sigmoid(mo) * _head_rms(h_fwd + h_bwd, m_norm_g)
    qkv = jax.nn.silu(_dwconv(dqkv, dn_conv_w))
    bq, bk, bv = _split(qkv, (B_HEADS * B_DK, B_HEADS * B_DK, B_HEADS * B_DV))
    q = _l2n(bq.reshape(bsz, s, B_HEADS, B_DK)) * (B_DK ** -0.5)
    k = _l2n(bk.reshape(bsz, s, B_HEADS, B_DK))
    v = bv.reshape(bsz, s, B_HEADS, B_DV)
    gb = dg.reshape(bsz, s, 4, B_HEADS)
    beta = jax.nn.sigmoid(gb[:, :, 0:2])
    decay = -jnp.exp(dn_a_log) * jax.nn.softplus(gb[:, :, 2:4] + dn_dt_bias)
    o_fwd = _gdn_chunked(q, k, v, beta[:, :, 0], decay[:, :, 0])
    o_bwd = _flip(_gdn_chunked(_flip(q), _flip(k), _flip(v), _flip(beta[:, :, 1]), _flip(decay[:, :, 1])))
    out_b = _head_rms(o_fwd + o_bwd, dn_norm_g)
    return jnp.concatenate([out_a, out_b], axis=-1), z


def _odd_mixer_core(p, dw_w, dw_b, ln_g, ln_b, rel_bias):
    bsz, s, _ = p.shape
    ga, gb, aq, ak, av, z = _split(p, ODD_SPLITS)
    u = _dwconv(ga * jax.nn.sigmoid(gb), dw_w) + dw_b
    out_c = jax.nn.silu(_layernorm(u, ln_g, ln_b))
    shp = (bsz, s, D_HEADS, D_DH)
    out_d = _dilated_attention(aq.reshape(shp), ak.reshape(shp), av.reshape(shp), rel_bias).reshape(bsz, s, D_HEADS * D_DH)
    return jnp.concatenate([out_c, out_d], axis=-1), z


def kernel(x, c, norm_g, ada_w, ada_b, ev_w_in, ev_m_gate_b, ev_dn_dt_bias, ev_dn_a_log, ev_dn_conv_w, ev_m_norm_g, ev_dn_norm_g, ev_w_out, od_w_in, od_dw_w, od_dw_b, od_ln_g, od_ln_b, od_w_out, rel_bias, final_g):
    assert DEPTH == 2, "the final RMSNorm is fused into the (last) odd layer's output projection"
    assert all(window // (2 * dil) == R_ATT for window, dil in D_GROUPS)
    d = x.shape[-1]
    mod = _adaln(c, ada_w, ada_b)
    for layer in range(DEPTH):
        shift, scale, gate = (mod[layer, :, i * d:(i + 1) * d][:, None, :] for i in range(3))
        j = layer // 2
        if layer % 2 == 0:
            x = _even_layer(x, norm_g[layer], scale, shift, gate, ev_w_in[j], ev_m_gate_b[j], ev_dn_dt_bias[j], ev_dn_a_log[j],
                            ev_dn_conv_w[j], ev_m_norm_g[j], ev_dn_norm_g[j], ev_w_out[j])
        else:
            x = _odd_layer_final(x, norm_g[layer], scale, shift, gate, od_w_in[j], od_dw_w[j], od_dw_b[j], od_ln_g[j], od_ln_b[j],
                                 rel_bias, od_w_out[j], final_g)
    return x


def _even_layer(x, norm_g, scale, shift, gate, w_in, m_gate_b, dn_dt_bias, dn_a_log, dn_conv_w, m_norm_g, dn_norm_g, w_out):
    bf16 = jnp.bfloat16
    mq, mk, mv, mo, mg, dqkv, dg, z = _split(w_in, EVEN_SPLITS)
    w = jnp.concatenate([mq, mv, dqkv], axis=1).astype(bf16)
    wkt = (mk * (A_DK ** -0.5)).T.astype(bf16)
    wg = jnp.concatenate([mg, dg], axis=1).astype(bf16)
    wz = jnp.concatenate([mo, z], axis=1).astype(bf16)
    pq, pkt, pv, bq, bk, bv, bkt, mc, mr, gc, gr = _inproj_even(x, norm_g, scale, shift, w, wkt, wg.T, dn_conv_w, m_gate_b, dn_a_log, dn_dt_bias)
    hf, hb = _mlstm(pq, pkt, pv, mc, mr)
    of, ob = _gdn(bq, bk, bkt, bv, gc, gr)
    return _outproj_even(hf, hb, of, ob, x, norm_g, scale, shift, gate, m_norm_g, dn_norm_g, wz, w_out.astype(bf16))


def _odd_layer_final(x, norm_g, scale, shift, gate, w_in, dw_w, dw_b, ln_g, ln_b, rel_bias, w_out, final_g):
    bf16 = jnp.bfloat16
    ga, gb, aq, ak, av, z = _split(w_in, ODD_SPLITS)
    w = jnp.concatenate([ga, gb, aq * (D_DH ** -0.5), ak, av], axis=1).astype(bf16)
    out_c, pq, pk, pv = _inproj_odd(x, norm_g, scale, shift, w, dw_w, dw_b, ln_g, ln_b)
    og, lg = zip(*[_dilated_group_call(qd, kd, vd, rel_bias, dil) for qd, kd, vd, dil in zip(pq, pk, pv, DILATIONS)])
    return _outproj_odd_final(out_c, og, lg, x, norm_g, scale, shift, gate, final_g, z.astype(bf16), w_out.astype(bf16))
```

```python
import math

import jax
import jax.numpy as jnp
import numpy as np
from jax import lax
from jax.experimental import pallas as pl
from jax.experimental.pallas import tpu as pltpu

DEPTH = 2
A_HEADS = 4
A_DK = 64
A_DV = 128
B_HEADS = 4
B_DK = 128
B_DV = 128
B_CONV = 5
C_WIDTH = 512
C_CONV = 31
D_HEADS = 8
D_DH = 64
D_GROUPS = ((128, 1), (512, 4), (2048, 16))
REL_BUCKETS = 32
REL_MAX_DIST = 1024
EPS = 1e-6
NEG = -1e30
MIX_EVEN = A_HEADS * A_DV + B_HEADS * B_DV
MIX_ODD = C_WIDTH + D_HEADS * D_DH
B_QKV = B_HEADS * (2 * B_DK + B_DV)
EVEN_SPLITS = (A_HEADS * A_DK, A_HEADS * A_DK, A_HEADS * A_DV, A_HEADS * A_DV, 4 * A_HEADS, B_QKV, 4 * B_HEADS, MIX_EVEN)
ODD_SPLITS = (C_WIDTH, C_WIDTH, D_HEADS * D_DH, D_HEADS * D_DH, D_HEADS * D_DH, MIX_ODD)
DILATIONS = tuple(dil for _, dil in D_GROUPS)

LANES = 128
SUBLANES = 8
VMEM_LIMIT = 56 * 1024 * 1024
TM_PROJ = 512
HALO_X = 16
SUB_C = 64
L_MLSTM = 256
C_GDN = 64
T_GDN_STEP = 256
B_REC_STEP = 2
TQ_ATT = 128
TB_ATT = 1024
R_ATT = 64

_EV_COLS = {"mq": (0, 256), "mv": (256, 768), "dqkv": (768, 2304)}
_OD_COLS = {"ga": (0, 512), "gb": (512, 1024), "aq": (1024, 1536), "ak": (1536, 2048), "av": (2048, 2560)}


def _log_sigmoid(t):
    return jnp.minimum(t, 0.0) - jnp.log(1.0 + jnp.exp(-jnp.abs(t)))


def _softplus(t):
    return jnp.maximum(t, 0.0) + jnp.log1p(jnp.exp(-jnp.abs(t)))


def _modulated_rms_val(x, g, scale, shift):
    y = x * lax.rsqrt(jnp.mean(x * x, axis=-1, keepdims=True) + EPS)
    return ((y * g) * (1.0 + scale) + shift).astype(jnp.bfloat16)


def _modulated_rms(x_ref, g_ref, sc_ref, sh_ref):
    return _modulated_rms_val(x_ref[0], g_ref[...], sc_ref[0], sh_ref[0])


def _seg_scan_lanes(x, seg, reverse, op, fill):
    n = x.shape[1]
    pos = lax.broadcasted_iota(jnp.int32, x.shape, 1) % seg
    k = 1
    while k < seg:
        if reverse:
            x = op(x, jnp.where(pos < seg - k, pltpu.roll(x, n - k, axis=1), fill))
        else:
            x = op(x, jnp.where(pos >= k, pltpu.roll(x, k, axis=1), fill))
        k *= 2
    return x


def _seg_cumsum_lanes(x, seg, reverse):
    return _seg_scan_lanes(x, seg, reverse, jnp.add, 0.0)


def _seg_cummax_lanes(x, seg, reverse):
    return _seg_scan_lanes(x, seg, reverse, jnp.maximum, -jnp.inf)


def _conv_taps(xe, w_ref, cs, width, t, halo):
    n = t + 2 * halo
    acc = None
    for j in range(width):
        off = halo - width // 2 + j
        shifted = xe[off:off + t] if off % SUBLANES == 0 else pltpu.roll(xe, n - off, axis=0)[0:t]
        term = shifted * w_ref[j:j + 1, cs]
        acc = term if acc is None else acc + term
    return acc


def _zero_outside(d, t, halo, first, last):
    return jnp.concatenate([jnp.where(first, 0.0, d[:halo]), d[halo:halo + t], jnp.where(last, 0.0, d[halo + t:])], axis=0)


def _halo_specs(tm, s, d):
    hb = tm // HALO_X
    return [pl.BlockSpec((1, HALO_X, d), lambda b, i: (b, jnp.maximum(i * hb - 1, 0), 0)),
            pl.BlockSpec((1, HALO_X, d), lambda b, i: (b, jnp.minimum((i + 1) * hb, s // HALO_X - 1), 0))]


def _mirror_maps(nblocks):
    return (lambda b, n: (b, n, 0), lambda b, n: (b, nblocks - 1 - n, 0),
            lambda b, n: (b, 0, n), lambda b, n: (b, 0, nblocks - 1 - n))


def _adaln_kernel(c_ref, w_ref, b_ref, o_ref):
    c = c_ref[...]
    cs = (c * jax.nn.sigmoid(c)).astype(jnp.bfloat16)
    o_ref[0] = jnp.dot(cs, w_ref[0].astype(jnp.bfloat16), preferred_element_type=jnp.float32) + b_ref[0]


def _adaln(c, ada_w, ada_b):
    depth, d, n3 = ada_w.shape
    bsz = c.shape[0]
    tn = 1024
    return pl.pallas_call(
        _adaln_kernel,
        grid=(depth, n3 // tn),
        in_specs=[pl.BlockSpec((bsz, d), lambda l, j: (0, 0)), pl.BlockSpec((1, d, tn), lambda l, j: (l, 0, j)),
                  pl.BlockSpec((1, 1, tn), lambda l, j: (l, 0, j))],
        out_specs=pl.BlockSpec((1, bsz, tn), lambda l, j: (l, 0, j)),
        out_shape=jax.ShapeDtypeStruct((depth, bsz, n3), jnp.float32),
        compiler_params=pltpu.CompilerParams(dimension_semantics=("parallel", "parallel")),
        name="adaln",
    )(c, ada_w, ada_b.reshape(depth, 1, n3))


def _inproj_even_kernel(x_ref, xp_ref, xn_ref, g_ref, sc_ref, sh_ref, w_ref, wkt_ref, wgt_ref, cw_ref, mb_ref, a_ref, dt_ref,
                        mq_ref, mkt_ref, mv_ref, bq_ref, bk_ref, bv_ref, bkt_ref, mc_ref, mr_ref, gc_ref, gr_ref):
    i = pl.program_id(1)
    nt = pl.num_programs(1)
    tm = x_ref.shape[1]
    f32 = jnp.float32
    x_ext = jnp.concatenate([xp_ref[0], x_ref[0], xn_ref[0]], axis=0)
    h_ext = _modulated_rms_val(x_ext, g_ref[...], sc_ref[0], sh_ref[0])
    h = h_ext[HALO_X:HALO_X + tm]
    xpose = (((1,), (1,)), ((), ()))
    for name, o_ref in (("mq", mq_ref), ("mv", mv_ref)):
        lo, hi = _EV_COLS[name]
        o_ref[0] = jnp.dot(h, w_ref[:, lo:hi], preferred_element_type=f32).astype(o_ref.dtype)
    mkt_ref[0] = lax.dot_general(wkt_ref[...], h, xpose, preferred_element_type=f32).astype(mkt_ref.dtype)
    gates_t = lax.dot_general(wgt_ref[...], h, xpose, preferred_element_type=f32)
    na = A_HEADS
    gm = gates_t[:16] + mb_ref[...]
    logf = _log_sigmoid(gm[2 * na:])
    b_f, b_b = _seg_cumsum_lanes(logf[:na], L_MLSTM, False), _seg_cumsum_lanes(logf[na:], L_MLSTM, True)
    pm_f = _seg_cummax_lanes(gm[:na] - b_f, L_MLSTM, False)
    pm_b = _seg_cummax_lanes(gm[na:2 * na] - b_b, L_MLSTM, True)
    mr_ref[0] = jnp.concatenate([gm[:2 * na], b_f, b_b], axis=0)
    mc_ref[0] = jnp.concatenate([b_f, b_b, pm_f, pm_b], axis=0).T
    nh = B_HEADS
    dgt = gates_t[16:]
    dec = -jnp.exp(a_ref[...]) * _softplus(dgt[2 * nh:] + dt_ref[...])
    gr = jnp.concatenate([jax.nn.sigmoid(dgt[:2 * nh]), _seg_cumsum_lanes(dec[:nh], C_GDN, False),
                          _seg_cumsum_lanes(dec[nh:], C_GDN, True)], axis=0)
    gr_ref[0] = gr
    gc_ref[0] = gr.T
    hd = B_HEADS * B_DK
    for part, o_ref in enumerate((bq_ref, bk_ref, bv_ref)):
        lo = _EV_COLS["dqkv"][0] + part * hd
        cs = slice(part * hd, (part + 1) * hd)
        d = jnp.dot(h_ext, w_ref[:, lo:lo + hd], preferred_element_type=f32)
        acc = _conv_taps(_zero_outside(d, tm, HALO_X, i == 0, i == nt - 1), cw_ref, cs, B_CONV, tm, HALO_X)
        y = acc * jax.nn.sigmoid(acc)
        for hh in range(B_HEADS):
            yh = y[:, hh * B_DK:(hh + 1) * B_DK]
            if part == 0:
                yh = yh * lax.rsqrt(jnp.sum(yh * yh, axis=-1, keepdims=True) + EPS) * (B_DK ** -0.5)
            elif part == 1:
                yh = yh * lax.rsqrt(jnp.sum(yh * yh, axis=-1, keepdims=True) + EPS)
            o_ref[0, :, hh * B_DK:(hh + 1) * B_DK] = yh.astype(o_ref.dtype)
            if part == 1:
                bkt_ref[0, hh * B_DK:(hh + 1) * B_DK, :] = yh.T.astype(bkt_ref.dtype)


def _inproj_even(x, g, scale, shift, w, wkt, wgt, conv_w, m_gate_b, a_log, dt_bias):
    bsz, s, d = x.shape
    tm = TM_PROJ
    assert tm % L_MLSTM == 0 and tm % C_GDN == 0
    tok = lambda b, i: (b, i, 0)
    tok_t = lambda b, i: (b, 0, i)
    const = lambda b, i: (0, 0)
    bvec = lambda b, i: (b, 0, 0)
    bf16, f32 = jnp.bfloat16, jnp.float32
    hk, hv, hd = A_HEADS * A_DK, A_HEADS * A_DV, B_HEADS * B_DK
    tok_specs = lambda wd: pl.BlockSpec((1, tm, wd), tok)
    gate_specs = [pl.BlockSpec((1, tm, 16), tok), pl.BlockSpec((1, 16, tm), tok_t)]
    gate_shapes = [jax.ShapeDtypeStruct((bsz, s, 16), f32), jax.ShapeDtypeStruct((bsz, 16, s), f32)]
    return pl.pallas_call(
        _inproj_even_kernel,
        grid=(bsz, s // tm),
        in_specs=[pl.BlockSpec((1, tm, d), tok)] + _halo_specs(tm, s, d) + [pl.BlockSpec((1, d), const), pl.BlockSpec((1, 1, d), bvec),
                  pl.BlockSpec((1, 1, d), bvec), pl.BlockSpec(w.shape, const), pl.BlockSpec(wkt.shape, const), pl.BlockSpec(wgt.shape, const),
                  pl.BlockSpec(conv_w.shape, const), pl.BlockSpec((16, 1), const), pl.BlockSpec((8, 1), const), pl.BlockSpec((8, 1), const)],
        out_specs=[tok_specs(hk), pl.BlockSpec((1, hk, tm), tok_t), tok_specs(hv), tok_specs(hd), tok_specs(hd), tok_specs(hd),
                   pl.BlockSpec((1, hd, tm), tok_t)] + gate_specs + gate_specs,
        out_shape=[jax.ShapeDtypeStruct((bsz, s, hk), bf16), jax.ShapeDtypeStruct((bsz, hk, s), bf16), jax.ShapeDtypeStruct((bsz, s, hv), bf16)]
        + [jax.ShapeDtypeStruct((bsz, s, hd), bf16)] * 3 + [jax.ShapeDtypeStruct((bsz, hd, s), bf16)] + gate_shapes + gate_shapes,
        compiler_params=pltpu.CompilerParams(dimension_semantics=("parallel", "parallel"), vmem_limit_bytes=VMEM_LIMIT),
        name="inproj_even",
    )(x, x, x, g.reshape(1, d), scale, shift, w, wkt, wgt, conv_w, m_gate_b.reshape(16, 1), a_log.reshape(8, 1), dt_bias.reshape(8, 1))


def _inproj_odd_kernel(x_ref, xp_ref, xn_ref, g_ref, sc_ref, sh_ref, w_ref, cw_ref, cb_ref, lg_ref, lb_ref, oc_ref, *rest):
    out_refs, plane_ref = rest[:-1], rest[-1]
    i = pl.program_id(1)
    nt = pl.num_programs(1)
    f32 = jnp.float32
    tm = x_ref.shape[1]
    nd = D_HEADS * D_DH
    x_ext = jnp.concatenate([xp_ref[0], x_ref[0], xn_ref[0]], axis=0)
    h_ext = _modulated_rms_val(x_ext, g_ref[...], sc_ref[0], sh_ref[0])
    h = h_ext[HALO_X:HALO_X + tm]
    dot = lambda name: jnp.dot(h, w_ref[:, _OD_COLS[name][0]:_OD_COLS[name][1]], preferred_element_type=f32)
    dot_ext = lambda name: jnp.dot(h_ext, w_ref[:, _OD_COLS[name][0]:_OD_COLS[name][1]], preferred_element_type=f32)
    xe = _zero_outside(dot_ext("ga") * jax.nn.sigmoid(dot_ext("gb")), tm, HALO_X, i == 0, i == nt - 1)
    n = tm + 2 * HALO_X
    phases = [xe] + [pltpu.roll(xe, n - b, axis=0) for b in range(1, SUBLANES)]
    for r0 in range(0, tm, SUB_C):
        acc = None
        for j in range(C_CONV):
            a, b = divmod(HALO_X - C_CONV // 2 + j, SUBLANES)
            lo = a * SUBLANES + r0
            term = phases[b][lo:lo + SUB_C] * cw_ref[j:j + 1, :]
            acc = term if acc is None else acc + term
        u = acc + cb_ref[...]
        uc = u - jnp.mean(u, axis=-1, keepdims=True)
        y = uc * lax.rsqrt(jnp.mean(uc * uc, axis=-1, keepdims=True) + EPS) * lg_ref[...] + lb_ref[...]
        oc_ref[0, r0:r0 + SUB_C, :] = y * jax.nn.sigmoid(y)
    for a, name in enumerate(("aq", "ak", "av")):
        r = dot(name)
        group_refs = out_refs[a * len(DILATIONS):(a + 1) * len(DILATIONS)]
        for j in range(nd // LANES):
            plane_ref[a, j] = r[:, j * LANES:(j + 1) * LANES]
        for dil, o_ref in zip(DILATIONS, group_refs):
            if dil == 1:
                o_ref[0, 0] = r.astype(o_ref.dtype)
                continue
            for res in range(dil):
                for j in range(nd // LANES):
                    o_ref[0, res, :, j * LANES:(j + 1) * LANES] = plane_ref[a, j, pl.ds(res, tm // dil, stride=dil), :].astype(o_ref.dtype)


def _inproj_odd(x, g, scale, shift, w, dw_w, dw_b, ln_g, ln_b):
    bsz, s, d = x.shape
    tm = TM_PROJ
    tok = lambda b, i: (b, i, 0)
    const = lambda b, i: (0, 0)
    bvec = lambda b, i: (b, 0, 0)
    bf16, f32 = jnp.bfloat16, jnp.float32
    nd = D_HEADS * D_DH
    cw = C_WIDTH
    att_specs = [pl.BlockSpec((1, dil, tm // dil, nd), lambda b, i: (b, 0, i, 0)) for dil in DILATIONS] * 3
    att_shapes = [jax.ShapeDtypeStruct((bsz, dil, s // dil, nd), bf16) for dil in DILATIONS] * 3
    outs = pl.pallas_call(
        _inproj_odd_kernel,
        grid=(bsz, s // tm),
        in_specs=[pl.BlockSpec((1, tm, d), tok)] + _halo_specs(tm, s, d) + [pl.BlockSpec((1, d), const), pl.BlockSpec((1, 1, d), bvec),
                  pl.BlockSpec((1, 1, d), bvec), pl.BlockSpec(w.shape, const), pl.BlockSpec((C_CONV, cw), const)]
        + [pl.BlockSpec((1, cw), const)] * 3,
        out_specs=[pl.BlockSpec((1, tm, cw), tok)] + att_specs,
        out_shape=[jax.ShapeDtypeStruct((bsz, s, cw), f32)] + att_shapes,
        scratch_shapes=[pltpu.VMEM((3, nd // LANES, tm, LANES), f32)],
        compiler_params=pltpu.CompilerParams(dimension_semantics=("parallel", "parallel"), vmem_limit_bytes=VMEM_LIMIT),
        name="inproj_odd",
    )(x, x, x, g.reshape(1, d), scale, shift, w, dw_w, dw_b.reshape(1, cw), ln_g.reshape(1, cw), ln_b.reshape(1, cw))
    ng = len(DILATIONS)
    return outs[0], outs[1:1 + ng], outs[1 + ng:1 + 2 * ng], outs[1 + 2 * ng:]


def _head_rms_cols(t, g, width):
    parts = []
    for h in range(t.shape[1] // width):
        th = t[:, h * width:(h + 1) * width]
        parts.append(th * lax.rsqrt(jnp.mean(th * th, axis=-1, keepdims=True) + EPS))
    return jnp.concatenate(parts, axis=1) * g


def _outproj_even_kernel(hf_ref, hb_ref, of_ref, ob_ref, x_ref, g_ref, sc_ref, sh_ref, gate_ref, mg_ref, dg_ref, wz_ref, w_ref, o_ref):
    f32, bf16 = jnp.float32, jnp.bfloat16
    na = A_HEADS * A_DV
    h = _modulated_rms(x_ref, g_ref, sc_ref, sh_ref)
    mo = jnp.dot(h, wz_ref[:, :na], preferred_element_type=f32)
    z = jnp.dot(h, wz_ref[:, na:], preferred_element_type=f32)
    sz = z * jax.nn.sigmoid(z)
    out_a = jax.nn.sigmoid(mo) * _head_rms_cols(hf_ref[0] + hb_ref[0], mg_ref[...], A_DV)
    out_b = _head_rms_cols(of_ref[0] + ob_ref[0], dg_ref[...], B_DV)
    y = jnp.dot((out_a * sz[:, :na]).astype(bf16), w_ref[:na, :], preferred_element_type=f32)
    y = y + jnp.dot((out_b * sz[:, na:]).astype(bf16), w_ref[na:, :], preferred_element_type=f32)
    o_ref[0] = x_ref[0] + gate_ref[0] * y


def _outproj_even(hf, hb, of, ob, x, norm_g, scale, shift, gate, m_norm_g, dn_norm_g, wz, w):
    bsz, s, d = x.shape
    tm = TM_PROJ
    tok = lambda b, i: (b, i, 0)
    const = lambda b, i: (0, 0)
    bvec = lambda b, i: (b, 0, 0)
    na, nb = A_HEADS * A_DV, B_HEADS * B_DV
    return pl.pallas_call(
        _outproj_even_kernel,
        grid=(bsz, s // tm),
        in_specs=[pl.BlockSpec((1, tm, na), tok)] * 2 + [pl.BlockSpec((1, tm, nb), tok)] * 2 + [pl.BlockSpec((1, tm, d), tok),
                  pl.BlockSpec((1, d), const), pl.BlockSpec((1, 1, d), bvec), pl.BlockSpec((1, 1, d), bvec), pl.BlockSpec((1, 1, d), bvec),
                  pl.BlockSpec((1, na), const), pl.BlockSpec((1, nb), const), pl.BlockSpec(wz.shape, const), pl.BlockSpec(w.shape, const)],
        out_specs=pl.BlockSpec((1, tm, d), tok),
        out_shape=jax.ShapeDtypeStruct((bsz, s, d), jnp.float32),
        compiler_params=pltpu.CompilerParams(dimension_semantics=("parallel", "parallel"), vmem_limit_bytes=VMEM_LIMIT),
        name="outproj_even",
    )(hf, hb, of, ob, x, norm_g.reshape(1, d), scale, shift, gate, m_norm_g.reshape(1, na), dn_norm_g.reshape(1, nb), wz, w)


def _outproj_odd_kernel(oc_ref, o1_ref, o2_ref, o3_ref, l1_ref, l2_ref, l3_ref, x_ref, g_ref, sc_ref, sh_ref, gate_ref, fg_ref,
                        wz_ref, w_ref, o_ref, nat_ref):
    f32, bf16 = jnp.float32, jnp.bfloat16
    tm = x_ref.shape[1]
    npl = D_HEADS * D_DH // LANES
    z = jnp.dot(_modulated_rms(x_ref, g_ref, sc_ref, sh_ref), wz_ref[...], preferred_element_type=f32)
    sz = z * jax.nn.sigmoid(z)
    groups = []
    for gi, (dil, og_ref, lg_ref) in enumerate(zip(DILATIONS, (o1_ref, o2_ref, o3_ref), (l1_ref, l2_ref, l3_ref))):
        if dil == 1:
            groups.append(([og_ref[0, 0, :, j * LANES:(j + 1) * LANES] for j in range(npl)], lg_ref[0, 0]))
            continue
        for res in range(dil):
            rows = pl.ds(res, tm // dil, stride=dil)
            for j in range(npl):
                nat_ref[gi, j, rows, :] = og_ref[0, res, :, j * LANES:(j + 1) * LANES]
            nat_ref[gi, npl, rows, :] = lg_ref[0, res]
        groups.append(([nat_ref[gi, j] for j in range(npl)], nat_ref[gi, npl]))
    (p1, l1), (p2, l2), (p3, l3) = groups
    lm = jnp.maximum(jnp.maximum(l1, l2), l3)
    e1, e2, e3 = jnp.exp(l1 - lm), jnp.exp(l2 - lm), jnp.exp(l3 - lm)
    inv = 1.0 / (e1 + e2 + e3)
    low = lax.broadcasted_iota(jnp.int32, (tm, LANES), 1) < D_DH
    planes = []
    for j in range(npl):
        acc = None
        for e, p in ((e1, p1), (e2, p2), (e3, p3)):
            wgt = e * inv
            term = jnp.where(low, wgt[:, 2 * j:2 * j + 1], wgt[:, 2 * j + 1:2 * j + 2]) * p[j]
            acc = term if acc is None else acc + term
        planes.append(acc)
    out_d = jnp.concatenate(planes, axis=1)
    y = jnp.dot((oc_ref[0] * sz[:, :C_WIDTH]).astype(bf16), w_ref[:C_WIDTH, :], preferred_element_type=f32)
    y = y + jnp.dot((out_d * sz[:, C_WIDTH:]).astype(bf16), w_ref[C_WIDTH:, :], preferred_element_type=f32)
    xn = x_ref[0] + gate_ref[0] * y
    o_ref[0] = xn * lax.rsqrt(jnp.mean(xn * xn, axis=-1, keepdims=True) + EPS) * fg_ref[...]


def _outproj_odd_final(oc, og, lg, x, norm_g, scale, shift, gate, final_g, wz, w):
    bsz, s, d = x.shape
    tm = TM_PROJ
    tok = lambda b, i: (b, i, 0)
    const = lambda b, i: (0, 0)
    bvec = lambda b, i: (b, 0, 0)
    nd = D_HEADS * D_DH
    res_major = lambda width: [pl.BlockSpec((1, dil, tm // dil, width), lambda b, i: (b, 0, i, 0)) for dil in DILATIONS]
    return pl.pallas_call(
        _outproj_odd_kernel,
        grid=(bsz, s // tm),
        in_specs=[pl.BlockSpec((1, tm, C_WIDTH), tok)] + res_major(nd) + res_major(LANES)
        + [pl.BlockSpec((1, tm, d), tok), pl.BlockSpec((1, d), const), pl.BlockSpec((1, 1, d), bvec), pl.BlockSpec((1, 1, d), bvec),
           pl.BlockSpec((1, 1, d), bvec), pl.BlockSpec((1, d), const), pl.BlockSpec(wz.shape, const), pl.BlockSpec(w.shape, const)],
        out_specs=pl.BlockSpec((1, tm, d), tok),
        out_shape=jax.ShapeDtypeStruct((bsz, s, d), jnp.float32),
        scratch_shapes=[pltpu.VMEM((len(DILATIONS), nd // LANES + 1, tm, LANES), jnp.float32)],
        compiler_params=pltpu.CompilerParams(dimension_semantics=("parallel", "parallel"), vmem_limit_bytes=VMEM_LIMIT),
        name="outproj_odd",
    )(oc, *og, *lg, x, norm_g.reshape(1, d), scale, shift, gate, final_g.reshape(1, d), wz, w)


def _dilated_kernel(q_ref, kc_ref, kp_ref, kn_ref, vc_ref, vp_ref, vn_ref, bias_ref, o_ref, lse_ref, kx_ref, vx_ref):
    i = pl.program_id(2)
    nt = pl.num_programs(2)
    tb = q_ref.shape[1]
    tq = TQ_ATT
    nk = tq + 2 * R_ATT
    nsub = tb // tq
    f32, bf16 = jnp.float32, jnp.bfloat16
    kx_ref[0:R_ATT, :] = kp_ref[0]
    kx_ref[R_ATT:R_ATT + tb, :] = kc_ref[0]
    kx_ref[R_ATT + tb:, :] = kn_ref[0]
    vx_ref[0:R_ATT, :] = vp_ref[0]
    vx_ref[R_ATT:R_ATT + tb, :] = vc_ref[0]
    vx_ref[R_ATT + tb:, :] = vn_ref[0]
    kj = lax.broadcasted_iota(jnp.int32, (tq, nk), 1)
    lane = lax.broadcasted_iota(jnp.int32, (tq, LANES), 1)
    low = lane < D_DH
    heads = [(pr, hi) for pr in range(D_HEADS // 2) for hi in (False, True)]
    for sub in range(nsub):
        qs = slice(sub * tq, (sub + 1) * tq)
        ks = slice(sub * tq, sub * tq + nk)
        outside = None
        if sub == 0:
            outside = (kj < R_ATT) & (i == 0)
        if sub == nsub - 1:
            after = (kj >= R_ATT + tq) & (i == nt - 1)
            outside = after if outside is None else outside | after
        scs = []
        for pr, hi in heads:
            ps = slice(pr * LANES, (pr + 1) * LANES)
            qp = q_ref[0, qs, ps]
            qh = jnp.where(low != hi, qp, jnp.zeros_like(qp))
            sc = lax.dot_general(qh, kx_ref[ks, ps], (((1,), (1,)), ((), ())), preferred_element_type=f32) + bias_ref[2 * pr + int(hi)]
            scs.append(sc if outside is None else jnp.where(outside, NEG, sc))
        ms = [jnp.max(sc, axis=-1, keepdims=True) for sc in scs]
        ps_ = [jnp.exp(sc - m) for sc, m in zip(scs, ms)]
        dens = [jnp.sum(p, axis=-1, keepdims=True) for p in ps_]
        pvs = [jnp.dot(p.astype(bf16), vx_ref[ks, pr * LANES:(pr + 1) * LANES], preferred_element_type=f32) for (pr, _), p in zip(heads, ps_)]
        lse_all = jnp.zeros((tq, LANES), f32)
        for pr in range(D_HEADS // 2):
            lo, hi = 2 * pr, 2 * pr + 1
            o_ref[0, qs, pr * LANES:(pr + 1) * LANES] = jnp.where(low, pvs[lo] / dens[lo], pvs[hi] / dens[hi])
            lse_all = jnp.where(lane == lo, ms[lo] + jnp.log(dens[lo]), lse_all)
            lse_all = jnp.where(lane == hi, ms[hi] + jnp.log(dens[hi]), lse_all)
        lse_ref[0, qs, :] = lse_all


def _dilated_bias(rel_bias, dilation, tq):
    half = REL_BUCKETS // 2
    exact = half // 2
    qi = jnp.arange(tq)[:, None]
    kj = jnp.arange(tq + 2 * R_ATT)[None, :]
    rel = kj - R_ATT - qi
    reld = rel * dilation
    n = jnp.abs(reld)
    large = exact + (jnp.log(jnp.maximum(n, 1).astype(jnp.float32) / exact) / math.log(REL_MAX_DIST / exact) * (half - exact)).astype(jnp.int32)
    large = jnp.minimum(large, half - 1)
    bucket = (reld > 0).astype(jnp.int32) * half + jnp.where(n < exact, n, large)
    bias = jnp.zeros((rel_bias.shape[1],) + bucket.shape, jnp.float32)
    for b in range(REL_BUCKETS):
        bias = jnp.where((bucket == b)[None], rel_bias[b].astype(jnp.float32)[:, None, None], bias)
    return jnp.where((jnp.abs(rel) <= R_ATT)[None], bias, NEG)


def _dilated_group_call(q, k, v, rel_bias, dilation):
    bsz, dil, ls, nd = q.shape
    assert dil == dilation
    tb, tq = min(TB_ATT, ls), TQ_ATT
    assert ls % tb == 0 and tb % tq == 0
    nt = ls // tb
    hb = tb // R_ATT
    nk = tq + 2 * R_ATT
    cur = lambda b, r, i: (b, r, i, 0)
    prev = lambda b, r, i: (b, r, jnp.maximum(i * hb - 1, 0), 0)
    nxt = lambda b, r, i: (b, r, jnp.minimum((i + 1) * hb, ls // R_ATT - 1), 0)
    kv_specs = [pl.BlockSpec((1, None, tb, nd), cur), pl.BlockSpec((1, None, R_ATT, nd), prev), pl.BlockSpec((1, None, R_ATT, nd), nxt)]
    return pl.pallas_call(
        _dilated_kernel,
        grid=(bsz, dilation, nt),
        in_specs=[pl.BlockSpec((1, None, tb, nd), cur)] + kv_specs + kv_specs + [pl.BlockSpec((D_HEADS, tq, nk), lambda b, r, i: (0, 0, 0))],
        out_specs=[pl.BlockSpec((1, None, tb, nd), cur), pl.BlockSpec((1, None, tb, LANES), cur)],
        out_shape=[jax.ShapeDtypeStruct((bsz, dilation, ls, nd), jnp.float32), jax.ShapeDtypeStruct((bsz, dilation, ls, LANES), jnp.float32)],
        scratch_shapes=[pltpu.VMEM((tb + 2 * R_ATT, nd), jnp.bfloat16)] * 2,
        compiler_params=pltpu.CompilerParams(dimension_semantics=("parallel", "parallel", "parallel"), vmem_limit_bytes=VMEM_LIMIT),
        name=f"dilated_d{dilation}",
    )(q, k, k, k, v, v, v, _dilated_bias(rel_bias, dilation, tq))


def _mlstm_kernel(qf_ref, kf_ref, vf_ref, gf_ref, gtf_ref, qb_ref, kb_ref, vb_ref, gb_ref, gtb_ref,
                  hf_ref, hb_ref, c_ref, m_ref):
    @pl.when(pl.program_id(1) == 0)
    def _():
        c_ref[...] = jnp.zeros_like(c_ref)
        m_ref[...] = jnp.zeros_like(m_ref)

    nbat, ln = qf_ref.shape[0], qf_ref.shape[1]
    f32, bf16 = jnp.float32, jnp.bfloat16
    row = lax.broadcasted_iota(jnp.int32, (ln, ln), 0)
    col = lax.broadcasted_iota(jnp.int32, (ln, ln), 1)
    ones_blk = jnp.ones((ln, A_DV), bf16)
    fwd_refs, bwd_refs = (qf_ref, kf_ref, vf_ref, gf_ref, gtf_ref), (qb_ref, kb_ref, vb_ref, gb_ref, gtb_ref)
    probs = []
    na = A_HEADS
    for bi in range(nbat):
        for d, (q_ref, kt_ref, v_ref, gc_ref, gr_ref), h_ref in ((0, fwd_refs, hf_ref), (1, bwd_refs, hb_ref)):
            mask = (row >= col) if d == 0 else (row <= col)
            for h in range(na):
                r = d * na + h
                probs.append(dict(
                    bi=bi, slot=bi * 2 * na + r, h=h, mask=mask, last=ln - 1 if d == 0 else 0, h_ref=h_ref,
                    q=q_ref[bi, :, h * A_DK:(h + 1) * A_DK],
                    kt=kt_ref[bi, h * A_DK:(h + 1) * A_DK, :],
                    vaug=jnp.concatenate([v_ref[bi, :, h * A_DV:(h + 1) * A_DV], ones_blk], axis=1),
                    bcol=gc_ref[bi, :, r:r + 1], pmcol=gc_ref[bi, :, 2 * na + r:2 * na + r + 1],
                    brow=gr_ref[bi, 2 * na + r:2 * na + r + 1, :],
                    vrow=gr_ref[bi, r:r + 1, :] - gr_ref[bi, 2 * na + r:2 * na + r + 1, :]))
    for p in probs:
        p["m_old"] = m_ref[p["slot"]:p["slot"] + 1, 0:1]
        p["caug"] = c_ref[p["slot"]]
        p["qk"] = jnp.dot(p["q"], p["kt"], preferred_element_type=f32)
    for p in probs:
        p["qc"] = jnp.dot(p["q"], p["caug"].astype(bf16), preferred_element_type=f32)
    for p in probs:
        mstab = jnp.maximum(p["m_old"], p["pmcol"])
        p["w_int"] = jnp.exp(p["m_old"] - mstab)
        p["emt"] = jnp.exp(-(p["bcol"] + mstab))
        p["sc"] = (jnp.exp(jnp.where(p["mask"], p["vrow"] - mstab, -jnp.inf)) * p["qk"]).astype(bf16)
    for p in probs:
        tot = p["w_int"] * p["qc"] + jnp.dot(p["sc"], p["vaug"], preferred_element_type=f32)
        den = jnp.maximum(jnp.abs(tot[:, A_DV:]), p["emt"])
        p["h_ref"][p["bi"], :, p["h"] * A_DV:(p["h"] + 1) * A_DV] = tot[:, :A_DV] / den
    for p in probs:
        last, brow, vrow = p["last"], p["brow"], p["vrow"]
        btot = brow[:, last:last + 1]
        m_new = btot + jnp.maximum(p["m_old"], jnp.max(vrow, axis=-1, keepdims=True))
        w_old = jnp.exp(btot + p["m_old"] - m_new)
        kwt = (p["kt"].astype(f32) * jnp.exp(btot + vrow - m_new)).astype(bf16)
        c_ref[p["slot"]] = w_old * p["caug"] + jnp.dot(kwt, p["vaug"], preferred_element_type=f32)
        m_ref[p["slot"]:p["slot"] + 1, :] = jnp.broadcast_to(m_new, (1, m_ref.shape[1]))


def _mlstm(q, kt, v, gc, gr):
    bsz, s, _ = q.shape
    ln = min(L_MLSTM, s)
    nc = s // ln
    bb = B_REC_STEP if bsz % B_REC_STEP == 0 else 1
    hk, hv = A_HEADS * A_DK, A_HEADS * A_DV
    fwd, bwd, fwd_t, bwd_t = _mirror_maps(nc)
    def specs(im, im_t):
        return [pl.BlockSpec((bb, ln, hk), im), pl.BlockSpec((bb, hk, ln), im_t), pl.BlockSpec((bb, ln, hv), im),
                pl.BlockSpec((bb, ln, 16), im), pl.BlockSpec((bb, 16, ln), im_t)]
    return pl.pallas_call(
        _mlstm_kernel,
        grid=(bsz // bb, nc),
        in_specs=specs(fwd, fwd_t) + specs(bwd, bwd_t),
        out_specs=[pl.BlockSpec((bb, ln, hv), fwd), pl.BlockSpec((bb, ln, hv), bwd)],
        out_shape=[jax.ShapeDtypeStruct((bsz, s, hv), jnp.float32)] * 2,
        scratch_shapes=[pltpu.VMEM((bb * 2 * A_HEADS, A_DK, 2 * A_DV), jnp.float32), pltpu.VMEM((bb * 2 * A_HEADS, LANES), jnp.float32)],
        compiler_params=pltpu.CompilerParams(dimension_semantics=("parallel", "arbitrary"), vmem_limit_bytes=VMEM_LIMIT),
        name="mlstm",
    )(q, kt, v, gc, gr, q, kt, v, gc, gr)


def _tri_inverse_many(a_list, masks):
    eye, m16, m32, m64 = masks
    f32, bf16 = jnp.float32, jnp.bfloat16
    mm = lambda x, y: jnp.dot(x.astype(bf16), y.astype(bf16), preferred_element_type=f32)
    ads = [jnp.where(m16, a, 0.0) for a in a_list]
    xs = [eye - ad for ad in ads]
    ps = [mm(ad, ad) for ad in ads]
    for stage in range(3):
        xs = [x + mm(x, p) for x, p in zip(xs, ps)]
        if stage < 2:
            ps = [mm(p, p) for p in ps]
    for lo, hi in ((m16, m32), (m32, m64)):
        off = hi & ~lo
        ys = [mm(jnp.where(off, a, 0.0), x) for a, x in zip(a_list, xs)]
        xs = [x - mm(x, y) for x, y in zip(xs, ys)]
    return xs


def _gdn_kernel(qf_ref, kf_ref, ktf_ref, vf_ref, gcf_ref, grf_ref, qb_ref, kb_ref, ktb_ref, vb_ref, gcb_ref, grb_ref,
                of_ref, ob_ref, s_ref):
    @pl.when(pl.program_id(1) == 0)
    def _():
        s_ref[...] = jnp.zeros_like(s_ref)

    nbat, t = qf_ref.shape[0], qf_ref.shape[1]
    c = C_GDN
    f32, bf16 = jnp.float32, jnp.bfloat16
    row = lax.broadcasted_iota(jnp.int32, (c, c), 0)
    col = lax.broadcasted_iota(jnp.int32, (c, c), 1)
    eye = (row == col).astype(f32)
    blk = lambda w: (row // w) == (col // w)
    masks = (eye, blk(16), blk(32), blk(64))
    nh, nchunk = B_HEADS, t // c
    dir_refs = ((qf_ref, kf_ref, ktf_ref, vf_ref, gcf_ref, grf_ref, of_ref), (qb_ref, kb_ref, ktb_ref, vb_ref, gcb_ref, grb_ref, ob_ref))
    probs = [(bi, d, h, ci) for bi in range(nbat) for d in range(2) for h in range(nh) for ci in range(nchunk)]
    xpose = (((1,), (1,)), ((), ()))

    def load(bi, d, h, ci):
        q_ref, k_ref, kt_ref, v_ref, gc_ref, gr_ref, _ = dir_refs[d]
        rs, cs = slice(ci * c, (ci + 1) * c), slice(h * B_DK, (h + 1) * B_DK)
        return dict(
            q=q_ref[bi, rs, cs], k=k_ref[bi, rs, cs], kt=kt_ref[bi, cs, rs], v=v_ref[bi, rs, cs],
            beta=gc_ref[bi, rs, d * nh + h:d * nh + h + 1],
            gcol=gc_ref[bi, rs, (2 + d) * nh + h:(2 + d) * nh + h + 1],
            grow=gr_ref[bi, (2 + d) * nh + h:(2 + d) * nh + h + 1, rs])

    data = [load(*p) for p in probs]
    for (_, d, _, _), p in zip(probs, data):
        incl = (row >= col) if d == 0 else (row <= col)
        p["gam"] = jnp.exp(jnp.where(incl, p["gcol"] - p["grow"], -jnp.inf))
    for p in data:
        p["kk"] = lax.dot_general(p["k"], p["k"], xpose, preferred_element_type=f32)
    for p in data:
        p["qk"] = lax.dot_general(p["q"], p["k"], xpose, preferred_element_type=f32)
    a_list = []
    for (_, d, _, _), p in zip(probs, data):
        strict = (row > col) if d == 0 else (row < col)
        a_list.append(jnp.where(strict, p["beta"] * p["kk"] * p["gam"], 0.0))
    tinvs = _tri_inverse_many(a_list, masks)
    for p, tinv in zip(data, tinvs):
        p["egc"] = jnp.exp(p["gcol"])
        rhs = jnp.concatenate([p["beta"] * p["v"].astype(f32), (p["beta"] * p["egc"]) * p["k"].astype(f32)], axis=1).astype(bf16)
        p["uw"] = jnp.dot(tinv.astype(bf16), rhs, preferred_element_type=f32)
        p["attn"] = (p["qk"] * p["gam"]).astype(bf16)
    index = {p: i for i, p in enumerate(probs)}
    chains = [(bi, d, h) for bi in range(nbat) for d in range(2) for h in range(nh)]
    slot = lambda bi, d, h: (bi * 2 + d) * nh + h
    states = [s_ref[slot(*ch)] for ch in chains]
    for step in range(nchunk):
        cur = [data[index[(bi, d, h, step if d == 0 else nchunk - 1 - step)]] for bi, d, h in chains]
        wss = []
        for p, state in zip(cur, states):
            wq = jnp.concatenate([p["uw"][:, B_DV:], p["q"].astype(f32) * p["egc"]], axis=0).astype(bf16)
            wss.append(jnp.dot(wq, state.astype(bf16), preferred_element_type=f32))
        v_news = [(p["uw"][:, :B_DV] - ws[:c]).astype(bf16) for p, ws in zip(cur, wss)]
        for (bi, d, h), p, ws, v_new in zip(chains, cur, wss, v_news):
            ci = step if d == 0 else nchunk - 1 - step
            dir_refs[d][6][bi, ci * c:(ci + 1) * c, h * B_DV:(h + 1) * B_DV] = ws[c:] + jnp.dot(p["attn"], v_new, preferred_element_type=f32)
        new_states = []
        for (bi, d, h), p, state, v_new in zip(chains, cur, states, v_news):
            last = c - 1 if d == 0 else 0
            gl = p["grow"][:, last:last + 1]
            kdt = (p["kt"].astype(f32) * jnp.exp(gl - p["grow"])).astype(bf16)
            new_states.append(jnp.exp(gl) * state + jnp.dot(kdt, v_new, preferred_element_type=f32))
        states = new_states
    for ch, state in zip(chains, states):
        s_ref[slot(*ch)] = state


def _gdn(q, k, kt, v, gc, gr):
    bsz, s, hd = q.shape
    t = min(T_GDN_STEP, s)
    nb = s // t
    bb = B_REC_STEP if bsz % B_REC_STEP == 0 else 1
    fwd, bwd, fwd_t, bwd_t = _mirror_maps(nb)
    def specs(im, im_t):
        return [pl.BlockSpec((bb, t, hd), im)] * 2 + [pl.BlockSpec((bb, hd, t), im_t), pl.BlockSpec((bb, t, hd), im),
                                                      pl.BlockSpec((bb, t, 16), im), pl.BlockSpec((bb, 16, t), im_t)]
    return pl.pallas_call(
        _gdn_kernel,
        grid=(bsz // bb, nb),
        in_specs=specs(fwd, fwd_t) + specs(bwd, bwd_t),
        out_specs=[pl.BlockSpec((bb, t, hd), fwd), pl.BlockSpec((bb, t, hd), bwd)],
        out_shape=[jax.ShapeDtypeStruct((bsz, s, hd), jnp.float32)] * 2,
        scratch_shapes=[pltpu.VMEM((bb * 2 * B_HEADS, B_DK, B_DV), jnp.float32)],
        compiler_params=pltpu.CompilerParams(dimension_semantics=("parallel", "arbitrary"), vmem_limit_bytes=VMEM_LIMIT),
        name="gdn",
    )(q, k, kt, v, gc, gr, q, k, kt, v, gc, gr)


def _split(p, sizes):
    return jnp.split(p, np.cumsum(sizes)[:-1].tolist(), axis=-1)


def _even_layer(x, norm_g, scale, shift, gate, w_in, m_gate_b, dn_dt_bias, dn_a_log, dn_conv_w, m_norm_g, dn_norm_g, w_out):
    bf16 = jnp.bfloat16
    mq, mk, mv, mo, mg, dqkv, dg, z = _split(w_in, EVEN_SPLITS)
    w = jnp.concatenate([mq, mv, dqkv], axis=1).astype(bf16)
    wkt = (mk * (A_DK ** -0.5)).T.astype(bf16)
    wg = jnp.concatenate([mg, dg], axis=1).astype(bf16)
    wz = jnp.concatenate([mo, z], axis=1).astype(bf16)
    pq, pkt, pv, bq, bk, bv, bkt, mc, mr, gc, gr = _inproj_even(x, norm_g, scale, shift, w, wkt, wg.T, dn_conv_w, m_gate_b, dn_a_log, dn_dt_bias)
    hf, hb = _mlstm(pq, pkt, pv, mc, mr)
    of, ob = _gdn(bq, bk, bkt, bv, gc, gr)
    return _outproj_even(hf, hb, of, ob, x, norm_g, scale, shift, gate, m_norm_g, dn_norm_g, wz, w_out.astype(bf16))


def _odd_layer_final(x, norm_g, scale, shift, gate, w_in, dw_w, dw_b, ln_g, ln_b, rel_bias, w_out, final_g):
    bf16 = jnp.bfloat16
    ga, gb, aq, ak, av, z = _split(w_in, ODD_SPLITS)
    w = jnp.concatenate([ga, gb, aq * (D_DH ** -0.5), ak, av], axis=1).astype(bf16)
    out_c, pq, pk, pv = _inproj_odd(x, norm_g, scale, shift, w, dw_w, dw_b, ln_g, ln_b)
    og, lg = zip(*[_dilated_group_call(qd, kd, vd, rel_bias, dil) for qd, kd, vd, dil in zip(pq, pk, pv, DILATIONS)])
    return _outproj_odd_final(out_c, og, lg, x, norm_g, scale, shift, gate, final_g, z.astype(bf16), w_out.astype(bf16))


def kernel(x, c, norm_g, ada_w, ada_b, ev_w_in, ev_m_gate_b, ev_dn_dt_bias, ev_dn_a_log, ev_dn_conv_w, ev_m_norm_g, ev_dn_norm_g, ev_w_out, od_w_in, od_dw_w, od_dw_b, od_ln_g, od_ln_b, od_w_out, rel_bias, final_g):
    assert DEPTH == 2, "the final RMSNorm is fused into the (last) odd layer's output projection"
    assert all(window // (2 * dil) == R_ATT for window, dil in D_GROUPS)
    d = x.shape[-1]
    mod = _adaln(c, ada_w, ada_b)
    for layer in range(DEPTH):
        shift, scale, gate = (mod[layer, :, i * d:(i + 1) * d][:, None, :] for i in range(3))
        j = layer // 2
        if layer % 2 == 0:
            x = _even_layer(x, norm_g[layer], scale, shift, gate, ev_w_in[j], ev_m_gate_b[j], ev_dn_dt_bias[j], ev_dn_a_log[j],
                            ev_dn_conv_w[j], ev_m_norm_g[j], ev_dn_norm_g[j], ev_w_out[j])
        else:
            x = _odd_layer_final(x, norm_g[layer], scale, shift, gate, od_w_in[j], od_dw_w[j], od_dw_b[j], od_ln_g[j], od_ln_b[j],
                                 rel_bias, od_w_out[j], final_g)
    return x
```

```python
import math

import jax
import jax.numpy as jnp
import numpy as np
from jax import lax
from jax.experimental import pallas as pl
from jax.experimental.pallas import tpu as pltpu

DEPTH = 2
A_HEADS = 4
A_DK = 64
A_DV = 128
B_HEADS = 4
B_DK = 128
B_DV = 128
B_CONV = 5
C_WIDTH = 512
C_CONV = 31
D_HEADS = 8
D_DH = 64
D_GROUPS = ((128, 1), (512, 4), (2048, 16))
REL_BUCKETS = 32
REL_MAX_DIST = 1024
EPS = 1e-6
NEG = -1e30
MIX_EVEN = A_HEADS * A_DV + B_HEADS * B_DV
MIX_ODD = C_WIDTH + D_HEADS * D_DH
B_QKV = B_HEADS * (2 * B_DK + B_DV)
EVEN_SPLITS = (A_HEADS * A_DK, A_HEADS * A_DK, A_HEADS * A_DV, A_HEADS * A_DV, 4 * A_HEADS, B_QKV, 4 * B_HEADS, MIX_EVEN)
ODD_SPLITS = (C_WIDTH, C_WIDTH, D_HEADS * D_DH, D_HEADS * D_DH, D_HEADS * D_DH, MIX_ODD)
DILATIONS = tuple(dil for _, dil in D_GROUPS)

LANES = 128
SUBLANES = 8
VMEM_LIMIT = 56 * 1024 * 1024
TM_PROJ = 512
HALO_X = 16
SUB_C = 64
L_MLSTM = 256
C_GDN = 64
T_GDN_STEP = 256
B_REC_STEP = 2
TQ_ATT = 128
TB_ATT = 1024
R_ATT = 64

_EV_COLS = {"mq": (0, 256), "mv": (256, 768), "dqkv": (768, 2304)}
_OD_COLS = {"ga": (0, 512), "gb": (512, 1024), "aq": (1024, 1536), "ak": (1536, 2048), "av": (2048, 2560)}


def _log_sigmoid(t):
    return jnp.minimum(t, 0.0) - jnp.log(1.0 + jnp.exp(-jnp.abs(t)))


def _softplus(t):
    return jnp.maximum(t, 0.0) + jnp.log1p(jnp.exp(-jnp.abs(t)))


def _modulated_rms_val(x, g, scale, shift):
    y = x * lax.rsqrt(jnp.mean(x * x, axis=-1, keepdims=True) + EPS)
    return ((y * g) * (1.0 + scale) + shift).astype(jnp.bfloat16)


def _modulated_rms(x_ref, g_ref, sc_ref, sh_ref):
    return _modulated_rms_val(x_ref[0], g_ref[...], sc_ref[0], sh_ref[0])


def _seg_scan_lanes(x, seg, reverse, op, fill):
    n = x.shape[1]
    pos = lax.broadcasted_iota(jnp.int32, x.shape, 1) % seg
    k = 1
    while k < seg:
        if reverse:
            x = op(x, jnp.where(pos < seg - k, pltpu.roll(x, n - k, axis=1), fill))
        else:
            x = op(x, jnp.where(pos >= k, pltpu.roll(x, k, axis=1), fill))
        k *= 2
    return x


def _seg_cumsum_lanes(x, seg, reverse):
    return _seg_scan_lanes(x, seg, reverse, jnp.add, 0.0)


def _seg_cummax_lanes(x, seg, reverse):
    return _seg_scan_lanes(x, seg, reverse, jnp.maximum, -jnp.inf)


def _conv_taps(xe, w_ref, cs, width, t, halo):
    n = t + 2 * halo
    acc = None
    for j in range(width):
        off = halo - width // 2 + j
        shifted = xe[off:off + t] if off % SUBLANES == 0 else pltpu.roll(xe, n - off, axis=0)[0:t]
        term = shifted * w_ref[j:j + 1, cs]
        acc = term if acc is None else acc + term
    return acc


def _zero_outside(d, t, halo, first, last):
    return jnp.concatenate([jnp.where(first, 0.0, d[:halo]), d[halo:halo + t], jnp.where(last, 0.0, d[halo + t:])], axis=0)


def _halo_specs(tm, s, d):
    hb = tm // HALO_X
    return [pl.BlockSpec((1, HALO_X, d), lambda b, i: (b, jnp.maximum(i * hb - 1, 0), 0)),
            pl.BlockSpec((1, HALO_X, d), lambda b, i: (b, jnp.minimum((i + 1) * hb, s // HALO_X - 1), 0))]


def _mirror_maps(nblocks):
    return (lambda b, n: (b, n, 0), lambda b, n: (b, nblocks - 1 - n, 0),
            lambda b, n: (b, 0, n), lambda b, n: (b, 0, nblocks - 1 - n))


def _adaln_kernel(c_ref, w_ref, b_ref, o_ref):
    c = c_ref[...]
    cs = (c * jax.nn.sigmoid(c)).astype(jnp.bfloat16)
    o_ref[0] = jnp.dot(cs, w_ref[0].astype(jnp.bfloat16), preferred_element_type=jnp.float32) + b_ref[0]


def _adaln(c, ada_w, ada_b):
    depth, d, n3 = ada_w.shape
    bsz = c.shape[0]
    tn = 1024
    return pl.pallas_call(
        _adaln_kernel,
        grid=(depth, n3 // tn),
        in_specs=[pl.BlockSpec((bsz, d), lambda l, j: (0, 0)), pl.BlockSpec((1, d, tn), lambda l, j: (l, 0, j)),
                  pl.BlockSpec((1, 1, tn), lambda l, j: (l, 0, j))],
        out_specs=pl.BlockSpec((1, bsz, tn), lambda l, j: (l, 0, j)),
        out_shape=jax.ShapeDtypeStruct((depth, bsz, n3), jnp.float32),
        compiler_params=pltpu.CompilerParams(dimension_semantics=("parallel", "parallel")),
        name="adaln",
    )(c, ada_w, ada_b.reshape(depth, 1, n3))


def _inproj_even_kernel(x_ref, xp_ref, xn_ref, g_ref, sc_ref, sh_ref, w_ref, wkt_ref, wgt_ref, cw_ref, mb_ref, a_ref, dt_ref,
                        mq_ref, mkt_ref, mv_ref, bq_ref, bk_ref, bv_ref, bkt_ref, mc_ref, mr_ref, gc_ref, gr_ref, h_ref):
    i = pl.program_id(1)
    nt = pl.num_programs(1)
    tm = x_ref.shape[1]
    f32 = jnp.float32
    x_ext = jnp.concatenate([xp_ref[0], x_ref[0], xn_ref[0]], axis=0)
    h_ext = _modulated_rms_val(x_ext, g_ref[...], sc_ref[0], sh_ref[0])
    h = h_ext[HALO_X:HALO_X + tm]
    h_ref[0] = h
    xpose = (((1,), (1,)), ((), ()))

    def mlstm_operand(name, o_ref):
        lo, hi = _EV_COLS[name]
        o_ref[0] = jnp.dot(h, w_ref[:, lo:hi], preferred_element_type=f32).astype(o_ref.dtype)

    def gates():
        mkt_ref[0] = lax.dot_general(wkt_ref[...], h, xpose, preferred_element_type=f32).astype(mkt_ref.dtype)
        gates_t = lax.dot_general(wgt_ref[...], h, xpose, preferred_element_type=f32)
        na = A_HEADS
        gm = gates_t[:16] + mb_ref[...]
        logf = _log_sigmoid(gm[2 * na:])
        b_f, b_b = _seg_cumsum_lanes(logf[:na], L_MLSTM, False), _seg_cumsum_lanes(logf[na:], L_MLSTM, True)
        pm_f = _seg_cummax_lanes(gm[:na] - b_f, L_MLSTM, False)
        pm_b = _seg_cummax_lanes(gm[na:2 * na] - b_b, L_MLSTM, True)
        mr_ref[0] = jnp.concatenate([gm[:2 * na], b_f, b_b], axis=0)
        mc_ref[0] = jnp.concatenate([b_f, b_b, pm_f, pm_b], axis=0).T
        nh = B_HEADS
        dgt = gates_t[16:]
        dec = -jnp.exp(a_ref[...]) * _softplus(dgt[2 * nh:] + dt_ref[...])
        gr = jnp.concatenate([jax.nn.sigmoid(dgt[:2 * nh]), _seg_cumsum_lanes(dec[:nh], C_GDN, False),
                              _seg_cumsum_lanes(dec[nh:], C_GDN, True)], axis=0)
        gr_ref[0] = gr
        gc_ref[0] = gr.T

    mlstm_operand("mq", mq_ref)
    mlstm_operand("mv", mv_ref)
    gates()
    hd = B_HEADS * B_DK
    for part, o_ref in enumerate((bq_ref, bk_ref, bv_ref)):
        lo = _EV_COLS["dqkv"][0] + part * hd
        cs = slice(part * hd, (part + 1) * hd)
        d = jnp.dot(h_ext, w_ref[:, lo:lo + hd], preferred_element_type=f32)
        acc = _conv_taps(_zero_outside(d, tm, HALO_X, i == 0, i == nt - 1), cw_ref, cs, B_CONV, tm, HALO_X)
        y = acc * jax.nn.sigmoid(acc)
        for hh in range(B_HEADS):
            yh = y[:, hh * B_DK:(hh + 1) * B_DK]
            if part == 0:
                yh = yh * lax.rsqrt(jnp.sum(yh * yh, axis=-1, keepdims=True) + EPS) * (B_DK ** -0.5)
            elif part == 1:
                yh = yh * lax.rsqrt(jnp.sum(yh * yh, axis=-1, keepdims=True) + EPS)
            o_ref[0, :, hh * B_DK:(hh + 1) * B_DK] = yh.astype(o_ref.dtype)
            if part == 1:
                bkt_ref[0, hh * B_DK:(hh + 1) * B_DK, :] = yh.T.astype(bkt_ref.dtype)


def _inproj_even(x, g, scale, shift, w, wkt, wgt, conv_w, m_gate_b, a_log, dt_bias):
    bsz, s, d = x.shape
    tm = TM_PROJ
    assert tm % L_MLSTM == 0 and tm % C_GDN == 0
    tok = lambda b, i: (b, i, 0)
    tok_t = lambda b, i: (b, 0, i)
    const = lambda b, i: (0, 0)
    bvec = lambda b, i: (b, 0, 0)
    bf16, f32 = jnp.bfloat16, jnp.float32
    hk, hv, hd = A_HEADS * A_DK, A_HEADS * A_DV, B_HEADS * B_DK
    tok_specs = lambda wd: pl.BlockSpec((1, tm, wd), tok)
    gate_specs = [pl.BlockSpec((1, tm, 16), tok), pl.BlockSpec((1, 16, tm), tok_t)]
    gate_shapes = [jax.ShapeDtypeStruct((bsz, s, 16), f32), jax.ShapeDtypeStruct((bsz, 16, s), f32)]
    return pl.pallas_call(
        _inproj_even_kernel,
        grid=(bsz, s // tm),
        in_specs=[pl.BlockSpec((1, tm, d), tok)] + _halo_specs(tm, s, d) + [pl.BlockSpec((1, d), const), pl.BlockSpec((1, 1, d), bvec),
                  pl.BlockSpec((1, 1, d), bvec), pl.BlockSpec(w.shape, const), pl.BlockSpec(wkt.shape, const), pl.BlockSpec(wgt.shape, const),
                  pl.BlockSpec(conv_w.shape, const), pl.BlockSpec((16, 1), const), pl.BlockSpec((8, 1), const), pl.BlockSpec((8, 1), const)],
        out_specs=[tok_specs(hk), pl.BlockSpec((1, hk, tm), tok_t), tok_specs(hv), tok_specs(hd), tok_specs(hd), tok_specs(hd),
                   pl.BlockSpec((1, hd, tm), tok_t)] + gate_specs + gate_specs + [tok_specs(d)],
        out_shape=[jax.ShapeDtypeStruct((bsz, s, hk), bf16), jax.ShapeDtypeStruct((bsz, hk, s), bf16), jax.ShapeDtypeStruct((bsz, s, hv), bf16)]
        + [jax.ShapeDtypeStruct((bsz, s, hd), bf16)] * 3 + [jax.ShapeDtypeStruct((bsz, hd, s), bf16)] + gate_shapes + gate_shapes
        + [jax.ShapeDtypeStruct((bsz, s, d), bf16)],
        compiler_params=pltpu.CompilerParams(dimension_semantics=("parallel", "parallel"), vmem_limit_bytes=VMEM_LIMIT),
        name="inproj_even",
    )(x, x, x, g.reshape(1, d), scale, shift, w, wkt, wgt, conv_w, m_gate_b.reshape(16, 1), a_log.reshape(8, 1), dt_bias.reshape(8, 1))


def _inproj_odd_kernel(x_ref, xp_ref, xn_ref, g_ref, sc_ref, sh_ref, w_ref, cw_ref, cb_ref, lg_ref, lb_ref, oc_ref, h_ref, *rest):
    out_refs, plane_ref = rest[:-1], rest[-1]
    i = pl.program_id(1)
    nt = pl.num_programs(1)
    f32 = jnp.float32
    tm = x_ref.shape[1]
    nd = D_HEADS * D_DH
    x_ext = jnp.concatenate([xp_ref[0], x_ref[0], xn_ref[0]], axis=0)
    h_ext = _modulated_rms_val(x_ext, g_ref[...], sc_ref[0], sh_ref[0])
    h = h_ext[HALO_X:HALO_X + tm]
    h_ref[0] = h
    dot = lambda name: jnp.dot(h, w_ref[:, _OD_COLS[name][0]:_OD_COLS[name][1]], preferred_element_type=f32)
    dot_ext = lambda name: jnp.dot(h_ext, w_ref[:, _OD_COLS[name][0]:_OD_COLS[name][1]], preferred_element_type=f32)
    xe = _zero_outside(dot_ext("ga") * jax.nn.sigmoid(dot_ext("gb")), tm, HALO_X, i == 0, i == nt - 1)
    n = tm + 2 * HALO_X
    phases = [xe] + [pltpu.roll(xe, n - b, axis=0) for b in range(1, SUBLANES)]
    for r0 in range(0, tm, SUB_C):
        acc = None
        for j in range(C_CONV):
            a, b = divmod(HALO_X - C_CONV // 2 + j, SUBLANES)
            lo = a * SUBLANES + r0
            term = phases[b][lo:lo + SUB_C] * cw_ref[j:j + 1, :]
            acc = term if acc is None else acc + term
        u = acc + cb_ref[...]
        uc = u - jnp.mean(u, axis=-1, keepdims=True)
        y = uc * lax.rsqrt(jnp.mean(uc * uc, axis=-1, keepdims=True) + EPS) * lg_ref[...] + lb_ref[...]
        oc_ref[0, r0:r0 + SUB_C, :] = y * jax.nn.sigmoid(y)
    for a, name in enumerate(("aq", "ak", "av")):
        r = dot(name)
        group_refs = out_refs[a * len(DILATIONS):(a + 1) * len(DILATIONS)]
        for j in range(nd // LANES):
            plane_ref[a, j] = r[:, j * LANES:(j + 1) * LANES]
        for dil, o_ref in zip(DILATIONS, group_refs):
            if dil == 1:
                o_ref[0, 0] = r.astype(o_ref.dtype)
                continue
            for res in range(dil):
                for j in range(nd // LANES):
                    o_ref[0, res, :, j * LANES:(j + 1) * LANES] = plane_ref[a, j, pl.ds(res, tm // dil, stride=dil), :].astype(o_ref.dtype)


def _inproj_odd(x, g, scale, shift, w, dw_w, dw_b, ln_g, ln_b):
    bsz, s, d = x.shape
    tm = TM_PROJ
    tok = lambda b, i: (b, i, 0)
    const = lambda b, i: (0, 0)
    bvec = lambda b, i: (b, 0, 0)
    bf16, f32 = jnp.bfloat16, jnp.float32
    nd = D_HEADS * D_DH
    cw = C_WIDTH
    att_specs = [pl.BlockSpec((1, dil, tm // dil, nd), lambda b, i: (b, 0, i, 0)) for dil in DILATIONS] * 3
    att_shapes = [jax.ShapeDtypeStruct((bsz, dil, s // dil, nd), bf16) for dil in DILATIONS] * 3
    outs = pl.pallas_call(
        _inproj_odd_kernel,
        grid=(bsz, s // tm),
        in_specs=[pl.BlockSpec((1, tm, d), tok)] + _halo_specs(tm, s, d) + [pl.BlockSpec((1, d), const), pl.BlockSpec((1, 1, d), bvec),
                  pl.BlockSpec((1, 1, d), bvec), pl.BlockSpec(w.shape, const), pl.BlockSpec((C_CONV, cw), const)]
        + [pl.BlockSpec((1, cw), const)] * 3,
        out_specs=[pl.BlockSpec((1, tm, cw), tok), pl.BlockSpec((1, tm, d), tok)] + att_specs,
        out_shape=[jax.ShapeDtypeStruct((bsz, s, cw), f32), jax.ShapeDtypeStruct((bsz, s, d), bf16)] + att_shapes,
        scratch_shapes=[pltpu.VMEM((3, nd // LANES, tm, LANES), f32)],
        compiler_params=pltpu.CompilerParams(dimension_semantics=("parallel", "parallel"), vmem_limit_bytes=VMEM_LIMIT),
        name="inproj_odd",
    )(x, x, x, g.reshape(1, d), scale, shift, w, dw_w, dw_b.reshape(1, cw), ln_g.reshape(1, cw), ln_b.reshape(1, cw))
    ng = len(DILATIONS)
    return outs[0], outs[1], outs[2:2 + ng], outs[2 + ng:2 + 2 * ng], outs[2 + 2 * ng:]


def _head_rms_cols(t, g, width):
    parts = []
    for h in range(t.shape[1] // width):
        th = t[:, h * width:(h + 1) * width]
        parts.append(th * lax.rsqrt(jnp.mean(th * th, axis=-1, keepdims=True) + EPS))
    return jnp.concatenate(parts, axis=1) * g


def _outproj_even_kernel(hf_ref, hb_ref, of_ref, ob_ref, x_ref, h_ref, gate_ref, mg_ref, dg_ref, wz_ref, w_ref, o_ref):
    f32, bf16 = jnp.float32, jnp.bfloat16
    na = A_HEADS * A_DV
    h = h_ref[0]
    mo = jnp.dot(h, wz_ref[:, :na], preferred_element_type=f32)
    z = jnp.dot(h, wz_ref[:, na:], preferred_element_type=f32)
    sz = z * jax.nn.sigmoid(z)
    out_a = jax.nn.sigmoid(mo) * _head_rms_cols(hf_ref[0] + hb_ref[0], mg_ref[...], A_DV)
    out_b = _head_rms_cols(of_ref[0] + ob_ref[0], dg_ref[...], B_DV)
    y = jnp.dot((out_a * sz[:, :na]).astype(bf16), w_ref[:na, :], preferred_element_type=f32)
    y = y + jnp.dot((out_b * sz[:, na:]).astype(bf16), w_ref[na:, :], preferred_element_type=f32)
    o_ref[0] = x_ref[0] + gate_ref[0] * y


def _outproj_even(hf, hb, of, ob, x, h, gate, m_norm_g, dn_norm_g, wz, w):
    bsz, s, d = x.shape
    tm = TM_PROJ
    tok = lambda b, i: (b, i, 0)
    const = lambda b, i: (0, 0)
    bvec = lambda b, i: (b, 0, 0)
    na, nb = A_HEADS * A_DV, B_HEADS * B_DV
    return pl.pallas_call(
        _outproj_even_kernel,
        grid=(bsz, s // tm),
        in_specs=[pl.BlockSpec((1, tm, na), tok)] * 2 + [pl.BlockSpec((1, tm, nb), tok)] * 2 + [pl.BlockSpec((1, tm, d), tok)] * 2
        + [pl.BlockSpec((1, 1, d), bvec), pl.BlockSpec((1, na), const), pl.BlockSpec((1, nb), const), pl.BlockSpec(wz.shape, const),
           pl.BlockSpec(w.shape, const)],
        out_specs=pl.BlockSpec((1, tm, d), tok),
        out_shape=jax.ShapeDtypeStruct((bsz, s, d), jnp.float32),
        compiler_params=pltpu.CompilerParams(dimension_semantics=("parallel", "parallel"), vmem_limit_bytes=VMEM_LIMIT),
        name="outproj_even",
    )(hf, hb, of, ob, x, h, gate, m_norm_g.reshape(1, na), dn_norm_g.reshape(1, nb), wz, w)


def _outproj_odd_kernel(oc_ref, o1_ref, o2_ref, o3_ref, l1_ref, l2_ref, l3_ref, x_ref, h_ref, gate_ref, fg_ref,
                        wz_ref, w_ref, o_ref, nat_ref):
    f32, bf16 = jnp.float32, jnp.bfloat16
    tm = x_ref.shape[1]
    npl = D_HEADS * D_DH // LANES
    z = jnp.dot(h_ref[0], wz_ref[...], preferred_element_type=f32)
    sz = z * jax.nn.sigmoid(z)
    groups = []
    for gi, (dil, og_ref, lg_ref) in enumerate(zip(DILATIONS, (o1_ref, o2_ref, o3_ref), (l1_ref, l2_ref, l3_ref))):
        if dil == 1:
            groups.append(([og_ref[0, 0, :, j * LANES:(j + 1) * LANES] for j in range(npl)], lg_ref[0, 0]))
            continue
        for res in range(dil):
            rows = pl.ds(res, tm // dil, stride=dil)
            for j in range(npl):
                nat_ref[gi, j, rows, :] = og_ref[0, res, :, j * LANES:(j + 1) * LANES]
            nat_ref[gi, npl, rows, :] = lg_ref[0, res]
        groups.append(([nat_ref[gi, j] for j in range(npl)], nat_ref[gi, npl]))
    (p1, l1), (p2, l2), (p3, l3) = groups
    lm = jnp.maximum(jnp.maximum(l1, l2), l3)
    e1, e2, e3 = jnp.exp(l1 - lm), jnp.exp(l2 - lm), jnp.exp(l3 - lm)
    inv = 1.0 / (e1 + e2 + e3)
    low = lax.broadcasted_iota(jnp.int32, (tm, LANES), 1) < D_DH
    planes = []
    for j in range(npl):
        acc = None
        for e, p in ((e1, p1), (e2, p2), (e3, p3)):
            wgt = e * inv
            term = jnp.where(low, wgt[:, 2 * j:2 * j + 1], wgt[:, 2 * j + 1:2 * j + 2]) * p[j]
            acc = term if acc is None else acc + term
        planes.append(acc)
    out_d = jnp.concatenate(planes, axis=1)
    y = jnp.dot((oc_ref[0] * sz[:, :C_WIDTH]).astype(bf16), w_ref[:C_WIDTH, :], preferred_element_type=f32)
    y = y + jnp.dot((out_d * sz[:, C_WIDTH:]).astype(bf16), w_ref[C_WIDTH:, :], preferred_element_type=f32)
    xn = x_ref[0] + gate_ref[0] * y
    o_ref[0] = xn * lax.rsqrt(jnp.mean(xn * xn, axis=-1, keepdims=True) + EPS) * fg_ref[...]


def _outproj_odd_final(oc, og, lg, x, h, gate, final_g, wz, w):
    bsz, s, d = x.shape
    tm = TM_PROJ
    tok = lambda b, i: (b, i, 0)
    const = lambda b, i: (0, 0)
    bvec = lambda b, i: (b, 0, 0)
    nd = D_HEADS * D_DH
    res_major = lambda width: [pl.BlockSpec((1, dil, tm // dil, width), lambda b, i: (b, 0, i, 0)) for dil in DILATIONS]
    return pl.pallas_call(
        _outproj_odd_kernel,
        grid=(bsz, s // tm),
        in_specs=[pl.BlockSpec((1, tm, C_WIDTH), tok)] + res_major(nd) + res_major(LANES)
        + [pl.BlockSpec((1, tm, d), tok), pl.BlockSpec((1, tm, d), tok), pl.BlockSpec((1, 1, d), bvec), pl.BlockSpec((1, d), const),
           pl.BlockSpec(wz.shape, const), pl.BlockSpec(w.shape, const)],
        out_specs=pl.BlockSpec((1, tm, d), tok),
        out_shape=jax.ShapeDtypeStruct((bsz, s, d), jnp.float32),
        scratch_shapes=[pltpu.VMEM((len(DILATIONS), nd // LANES + 1, tm, LANES), jnp.float32)],
        compiler_params=pltpu.CompilerParams(dimension_semantics=("parallel", "parallel"), vmem_limit_bytes=VMEM_LIMIT),
        name="outproj_odd",
    )(oc, *og, *lg, x, h, gate, final_g.reshape(1, d), wz, w)


def _dilated_kernel(q_ref, kc_ref, kp_ref, kn_ref, vc_ref, vp_ref, vn_ref, bias_ref, o_ref, lse_ref, kx_ref, vx_ref):
    i = pl.program_id(2)
    nt = pl.num_programs(2)
    tb = q_ref.shape[1]
    tq = TQ_ATT
    nk = tq + 2 * R_ATT
    nsub = tb // tq
    f32, bf16 = jnp.float32, jnp.bfloat16
    kx_ref[0:R_ATT, :] = kp_ref[0]
    kx_ref[R_ATT:R_ATT + tb, :] = kc_ref[0]
    kx_ref[R_ATT + tb:, :] = kn_ref[0]
    for pr in range(D_HEADS // 2):
        src, dst = slice(pr * LANES, (pr + 1) * LANES), slice(2 * pr * LANES, (2 * pr + 1) * LANES)
        vx_ref[0:R_ATT, dst] = vp_ref[0, :, src]
        vx_ref[R_ATT:R_ATT + tb, dst] = vc_ref[0, :, src]
        vx_ref[R_ATT + tb:, dst] = vn_ref[0, :, src]
        vx_ref[:, (2 * pr + 1) * LANES:(2 * pr + 2) * LANES] = jnp.ones((tb + 2 * R_ATT, LANES), bf16)
    kj = lax.broadcasted_iota(jnp.int32, (tq, nk), 1)
    lane = lax.broadcasted_iota(jnp.int32, (tq, LANES), 1)
    low = lane < D_DH
    heads = [(pr, hi) for pr in range(D_HEADS // 2) for hi in (False, True)]
    for sub in range(nsub):
        qs = slice(sub * tq, (sub + 1) * tq)
        ks = slice(sub * tq, sub * tq + nk)
        outside = None
        if sub == 0:
            outside = (kj < R_ATT) & (i == 0)
        if sub == nsub - 1:
            after = (kj >= R_ATT + tq) & (i == nt - 1)
            outside = after if outside is None else outside | after
        scs = []
        for pr, hi in heads:
            ps = slice(pr * LANES, (pr + 1) * LANES)
            qp = q_ref[0, qs, ps]
            qh = jnp.where(low != hi, qp, jnp.zeros_like(qp))
            sc = lax.dot_general(qh, kx_ref[ks, ps], (((1,), (1,)), ((), ())), preferred_element_type=f32) + bias_ref[2 * pr + int(hi)]
            scs.append(sc if outside is None else jnp.where(outside, NEG, sc))
        ms = [jnp.max(sc, axis=-1, keepdims=True) for sc in scs]
        ps_ = [jnp.exp(sc - m) for sc, m in zip(scs, ms)]
        pvs = [jnp.dot(p.astype(bf16), vx_ref[ks, 2 * pr * LANES:(2 * pr + 2) * LANES], preferred_element_type=f32) for (pr, _), p in zip(heads, ps_)]
        lse_all = jnp.zeros((tq, LANES), f32)
        for pr in range(D_HEADS // 2):
            lo, hi = 2 * pr, 2 * pr + 1
            (num_lo, den_lo), (num_hi, den_hi) = ((pvs[h][:, :LANES], pvs[h][:, LANES:]) for h in (lo, hi))
            o_ref[0, qs, pr * LANES:(pr + 1) * LANES] = jnp.where(low, num_lo / den_lo, num_hi / den_hi)
            lse_all = jnp.where(lane == lo, ms[lo] + jnp.log(den_lo), lse_all)
            lse_all = jnp.where(lane == hi, ms[hi] + jnp.log(den_hi), lse_all)
        lse_ref[0, qs, :] = lse_all


def _dilated_bias(rel_bias, dilation, tq):
    half = REL_BUCKETS // 2
    exact = half // 2
    qi = jnp.arange(tq)[:, None]
    kj = jnp.arange(tq + 2 * R_ATT)[None, :]
    rel = kj - R_ATT - qi
    reld = rel * dilation
    n = jnp.abs(reld)
    large = exact + (jnp.log(jnp.maximum(n, 1).astype(jnp.float32) / exact) / math.log(REL_MAX_DIST / exact) * (half - exact)).astype(jnp.int32)
    large = jnp.minimum(large, half - 1)
    bucket = (reld > 0).astype(jnp.int32) * half + jnp.where(n < exact, n, large)
    bias = jnp.zeros((rel_bias.shape[1],) + bucket.shape, jnp.float32)
    for b in range(REL_BUCKETS):
        bias = jnp.where((bucket == b)[None], rel_bias[b].astype(jnp.float32)[:, None, None], bias)
    return jnp.where((jnp.abs(rel) <= R_ATT)[None], bias, NEG)


def _dilated_group_call(q, k, v, rel_bias, dilation):
    bsz, dil, ls, nd = q.shape
    assert dil == dilation
    tb, tq = min(TB_ATT, ls), TQ_ATT
    assert ls % tb == 0 and tb % tq == 0
    nt = ls // tb
    hb = tb // R_ATT
    nk = tq + 2 * R_ATT
    cur = lambda b, r, i: (b, r, i, 0)
    prev = lambda b, r, i: (b, r, jnp.maximum(i * hb - 1, 0), 0)
    nxt = lambda b, r, i: (b, r, jnp.minimum((i + 1) * hb, ls // R_ATT - 1), 0)
    kv_specs = [pl.BlockSpec((1, None, tb, nd), cur), pl.BlockSpec((1, None, R_ATT, nd), prev), pl.BlockSpec((1, None, R_ATT, nd), nxt)]
    return pl.pallas_call(
        _dilated_kernel,
        grid=(bsz, dilation, nt),
        in_specs=[pl.BlockSpec((1, None, tb, nd), cur)] + kv_specs + kv_specs + [pl.BlockSpec((D_HEADS, tq, nk), lambda b, r, i: (0, 0, 0))],
        out_specs=[pl.BlockSpec((1, None, tb, nd), cur), pl.BlockSpec((1, None, tb, LANES), cur)],
        out_shape=[jax.ShapeDtypeStruct((bsz, dilation, ls, nd), jnp.float32), jax.ShapeDtypeStruct((bsz, dilation, ls, LANES), jnp.float32)],
        scratch_shapes=[pltpu.VMEM((tb + 2 * R_ATT, nd), jnp.bfloat16), pltpu.VMEM((tb + 2 * R_ATT, 2 * nd), jnp.bfloat16)],
        compiler_params=pltpu.CompilerParams(dimension_semantics=("parallel", "parallel", "parallel"), vmem_limit_bytes=VMEM_LIMIT),
        name=f"dilated_d{dilation}",
    )(q, k, k, k, v, v, v, _dilated_bias(rel_bias, dilation, tq))


def _mlstm_kernel(qf_ref, kf_ref, vf_ref, gf_ref, gtf_ref, qb_ref, kb_ref, vb_ref, gb_ref, gtb_ref,
                  hf_ref, hb_ref, c_ref, m_ref):
    @pl.when(pl.program_id(1) == 0)
    def _():
        c_ref[...] = jnp.zeros_like(c_ref)
        m_ref[...] = jnp.zeros_like(m_ref)

    nbat, ln = qf_ref.shape[0], qf_ref.shape[1]
    f32, bf16 = jnp.float32, jnp.bfloat16
    row = lax.broadcasted_iota(jnp.int32, (ln, ln), 0)
    col = lax.broadcasted_iota(jnp.int32, (ln, ln), 1)
    ones_blk = jnp.ones((ln, A_DV), bf16)
    fwd_refs, bwd_refs = (qf_ref, kf_ref, vf_ref, gf_ref, gtf_ref), (qb_ref, kb_ref, vb_ref, gb_ref, gtb_ref)
    probs = []
    na = A_HEADS
    for bi in range(nbat):
        for d, (q_ref, kt_ref, v_ref, gc_ref, gr_ref), h_ref in ((0, fwd_refs, hf_ref), (1, bwd_refs, hb_ref)):
            mask = (row >= col) if d == 0 else (row <= col)
            for h in range(na):
                r = d * na + h
                probs.append(dict(
                    bi=bi, slot=bi * 2 * na + r, h=h, mask=mask, last=ln - 1 if d == 0 else 0, h_ref=h_ref,
                    q=q_ref[bi, :, h * A_DK:(h + 1) * A_DK],
                    kt=kt_ref[bi, h * A_DK:(h + 1) * A_DK, :],
                    vaug=jnp.concatenate([v_ref[bi, :, h * A_DV:(h + 1) * A_DV], ones_blk], axis=1),
                    bcol=gc_ref[bi, :, r:r + 1], pmcol=gc_ref[bi, :, 2 * na + r:2 * na + r + 1],
                    brow=gr_ref[bi, 2 * na + r:2 * na + r + 1, :],
                    vrow=gr_ref[bi, r:r + 1, :] - gr_ref[bi, 2 * na + r:2 * na + r + 1, :]))
    for p in probs:
        p["m_old"] = m_ref[p["slot"]:p["slot"] + 1, 0:1]
        p["caug"] = c_ref[p["slot"]]
        p["qk"] = jnp.dot(p["q"], p["kt"], preferred_element_type=f32)
    for p in probs:
        p["qc"] = jnp.dot(p["q"], p["caug"].astype(bf16), preferred_element_type=f32)
    for p in probs:
        mstab = jnp.maximum(p["m_old"], p["pmcol"])
        p["w_int"] = jnp.exp(p["m_old"] - mstab)
        p["emt"] = jnp.exp(-(p["bcol"] + mstab))
        p["sc"] = (jnp.exp(jnp.where(p["mask"], p["vrow"] - mstab, -jnp.inf)) * p["qk"]).astype(bf16)
    for p in probs:
        tot = p["w_int"] * p["qc"] + jnp.dot(p["sc"], p["vaug"], preferred_element_type=f32)
        den = jnp.maximum(jnp.abs(tot[:, A_DV:]), p["emt"])
        p["h_ref"][p["bi"], :, p["h"] * A_DV:(p["h"] + 1) * A_DV] = tot[:, :A_DV] / den
    for p in probs:
        last, brow, vrow = p["last"], p["brow"], p["vrow"]
        btot = brow[:, last:last + 1]
        m_new = btot + jnp.maximum(p["m_old"], jnp.max(vrow, axis=-1, keepdims=True))
        w_old = jnp.exp(btot + p["m_old"] - m_new)
        kwt = (p["kt"].astype(f32) * jnp.exp(btot + vrow - m_new)).astype(bf16)
        c_ref[p["slot"]] = w_old * p["caug"] + jnp.dot(kwt, p["vaug"], preferred_element_type=f32)
        m_ref[p["slot"]:p["slot"] + 1, :] = jnp.broadcast_to(m_new, (1, m_ref.shape[1]))


def _mlstm(q, kt, v, gc, gr):
    bsz, s, _ = q.shape
    ln = min(L_MLSTM, s)
    nc = s // ln
    bb = B_REC_STEP if bsz % B_REC_STEP == 0 else 1
    hk, hv = A_HEADS * A_DK, A_HEADS * A_DV
    fwd, bwd, fwd_t, bwd_t = _mirror_maps(nc)
    def specs(im, im_t):
        return [pl.BlockSpec((bb, ln, hk), im), pl.BlockSpec((bb, hk, ln), im_t), pl.BlockSpec((bb, ln, hv), im),
                pl.BlockSpec((bb, ln, 16), im), pl.BlockSpec((bb, 16, ln), im_t)]
    return pl.pallas_call(
        _mlstm_kernel,
        grid=(bsz // bb, nc),
        in_specs=specs(fwd, fwd_t) + specs(bwd, bwd_t),
        out_specs=[pl.BlockSpec((bb, ln, hv), fwd), pl.BlockSpec((bb, ln, hv), bwd)],
        out_shape=[jax.ShapeDtypeStruct((bsz, s, hv), jnp.float32)] * 2,
        scratch_shapes=[pltpu.VMEM((bb * 2 * A_HEADS, A_DK, 2 * A_DV), jnp.float32), pltpu.VMEM((bb * 2 * A_HEADS, LANES), jnp.float32)],
        compiler_params=pltpu.CompilerParams(dimension_semantics=("parallel", "arbitrary"), vmem_limit_bytes=VMEM_LIMIT),
        name="mlstm",
    )(q, kt, v, gc, gr, q, kt, v, gc, gr)


def _half_rows(x, s, odd):
    return jnp.concatenate([x[i * s:(i + 1) * s] for i in range(x.shape[0] // s) if (i % 2 == 1) == odd], axis=0)


def _with_half_rows(sel, rest, s, odd):
    n = 2 * sel.shape[0]
    blocks = []
    for i in range(n // s):
        if (i % 2 == 1) == odd:
            blocks.append(sel[(i // 2) * s:(i // 2 + 1) * s])
        else:
            blocks.append(jnp.zeros((s, sel.shape[1]), sel.dtype) if rest is None else rest[i * s:(i + 1) * s])
    return jnp.concatenate(blocks, axis=0)


def _tri_inverse_many(a_list, lowers, masks):
    eye, m16, m32, m64 = masks
    f32, bf16 = jnp.float32, jnp.bfloat16
    mm = lambda x, y: jnp.dot(x.astype(bf16), y.astype(bf16), preferred_element_type=f32)
    ads = [jnp.where(m16, a, 0.0) for a in a_list]
    xs = [eye - ad for ad in ads]
    ps = [mm(ad, ad) for ad in ads]
    for stage in range(3):
        xs = [x + mm(x, p) for x, p in zip(xs, ps)]
        if stage < 2:
            ps = [mm(p, p) for p in ps]
    for s, lo, hi in ((16, m16, m32), (32, m32, m64)):
        off = hi & ~lo
        ys = [mm(_half_rows(jnp.where(off, a, 0.0), s, low), x) for a, low, x in zip(a_list, lowers, xs)]
        upd = [_half_rows(x, s, low) - mm(_half_rows(x, s, low), _with_half_rows(y, None, s, low))
               for x, low, y in zip(xs, lowers, ys)]
        xs = [_with_half_rows(u, x, s, low) for u, x, low in zip(upd, xs, lowers)]
    return xs


def _gdn_kernel(qf_ref, kf_ref, ktf_ref, vf_ref, gcf_ref, grf_ref, qb_ref, kb_ref, ktb_ref, vb_ref, gcb_ref, grb_ref,
                of_ref, ob_ref, s_ref):
    @pl.when(pl.program_id(1) == 0)
    def _():
        s_ref[...] = jnp.zeros_like(s_ref)

    nbat, t = qf_ref.shape[0], qf_ref.shape[1]
    c = C_GDN
    f32, bf16 = jnp.float32, jnp.bfloat16
    row = lax.broadcasted_iota(jnp.int32, (c, c), 0)
    col = lax.broadcasted_iota(jnp.int32, (c, c), 1)
    eye = (row == col).astype(f32)
    blk = lambda w: (row // w) == (col // w)
    masks = (eye, blk(16), blk(32), blk(64))
    nh, nchunk = B_HEADS, t // c
    dir_refs = ((qf_ref, kf_ref, ktf_ref, vf_ref, gcf_ref, grf_ref, of_ref), (qb_ref, kb_ref, ktb_ref, vb_ref, gcb_ref, grb_ref, ob_ref))
    probs = [(bi, d, h, ci) for bi in range(nbat) for d in range(2) for h in range(nh) for ci in range(nchunk)]
    xpose = (((1,), (1,)), ((), ()))

    def load(bi, d, h, ci):
        q_ref, k_ref, kt_ref, v_ref, gc_ref, gr_ref, _ = dir_refs[d]
        rs, cs = slice(ci * c, (ci + 1) * c), slice(h * B_DK, (h + 1) * B_DK)
        return dict(
            q=q_ref[bi, rs, cs], k=k_ref[bi, rs, cs], kt=kt_ref[bi, cs, rs], v=v_ref[bi, rs, cs],
            beta=gc_ref[bi, rs, d * nh + h:d * nh + h + 1],
            gcol=gc_ref[bi, rs, (2 + d) * nh + h:(2 + d) * nh + h + 1],
            grow=gr_ref[bi, (2 + d) * nh + h:(2 + d) * nh + h + 1, rs])

    data = [load(*p) for p in probs]
    for (_, d, _, _), p in zip(probs, data):
        incl = (row >= col) if d == 0 else (row <= col)
        p["gam"] = jnp.exp(jnp.where(incl, p["gcol"] - p["grow"], -jnp.inf))
    for p in data:
        p["kk"] = lax.dot_general(p["k"], p["k"], xpose, preferred_element_type=f32)
    for p in data:
        p["qk"] = lax.dot_general(p["q"], p["k"], xpose, preferred_element_type=f32)
    a_list = []
    for (_, d, _, _), p in zip(probs, data):
        strict = (row > col) if d == 0 else (row < col)
        a_list.append(jnp.where(strict, p["beta"] * p["kk"] * p["gam"], 0.0))
    tinvs = _tri_inverse_many(a_list, [d == 0 for (_, d, _, _) in probs], masks)
    for p, tinv in zip(data, tinvs):
        p["egc"] = jnp.exp(p["gcol"])
        rhs = jnp.concatenate([p["beta"] * p["v"].astype(f32), (p["beta"] * p["egc"]) * p["k"].astype(f32)], axis=1).astype(bf16)
        p["uw"] = jnp.dot(tinv.astype(bf16), rhs, preferred_element_type=f32)
        p["attn"] = (p["qk"] * p["gam"]).astype(bf16)
    index = {p: i for i, p in enumerate(probs)}
    chains = [(bi, d, h) for bi in range(nbat) for d in range(2) for h in range(nh)]
    slot = lambda bi, d, h: (bi * 2 + d) * nh + h
    states = [s_ref[slot(*ch)] for ch in chains]
    for step in range(nchunk):
        cur = [data[index[(bi, d, h, step if d == 0 else nchunk - 1 - step)]] for bi, d, h in chains]
        wss = []
        for p, state in zip(cur, states):
            wq = jnp.concatenate([p["uw"][:, B_DV:], p["q"].astype(f32) * p["egc"]], axis=0).astype(bf16)
            wss.append(jnp.dot(wq, state.astype(bf16), preferred_element_type=f32))
        v_news = [(p["uw"][:, :B_DV] - ws[:c]).astype(bf16) for p, ws in zip(cur, wss)]
        for (bi, d, h), p, ws, v_new in zip(chains, cur, wss, v_news):
            ci = step if d == 0 else nchunk - 1 - step
            dir_refs[d][6][bi, ci * c:(ci + 1) * c, h * B_DV:(h + 1) * B_DV] = ws[c:] + jnp.dot(p["attn"], v_new, preferred_element_type=f32)
        new_states = []
        for (bi, d, h), p, state, v_new in zip(chains, cur, states, v_news):
            last = c - 1 if d == 0 else 0
            gl = p["grow"][:, last:last + 1]
            kdt = (p["kt"].astype(f32) * jnp.exp(gl - p["grow"])).astype(bf16)
            new_states.append(jnp.exp(gl) * state + jnp.dot(kdt, v_new, preferred_element_type=f32))
        states = new_states
    for ch, state in zip(chains, states):
        s_ref[slot(*ch)] = state


def _gdn(q, k, kt, v, gc, gr):
    bsz, s, hd = q.shape
    t = min(T_GDN_STEP, s)
    nb = s // t
    bb = B_REC_STEP if bsz % B_REC_STEP == 0 else 1
    fwd, bwd, fwd_t, bwd_t = _mirror_maps(nb)
    def specs(im, im_t):
        return [pl.BlockSpec((bb, t, hd), im)] * 2 + [pl.BlockSpec((bb, hd, t), im_t), pl.BlockSpec((bb, t, hd), im),
                                                      pl.BlockSpec((bb, t, 16), im), pl.BlockSpec((bb, 16, t), im_t)]
    return pl.pallas_call(
        _gdn_kernel,
        grid=(bsz // bb, nb),
        in_specs=specs(fwd, fwd_t) + specs(bwd, bwd_t),
        out_specs=[pl.BlockSpec((bb, t, hd), fwd), pl.BlockSpec((bb, t, hd), bwd)],
        out_shape=[jax.ShapeDtypeStruct((bsz, s, hd), jnp.float32)] * 2,
        scratch_shapes=[pltpu.VMEM((bb * 2 * B_HEADS, B_DK, B_DV), jnp.float32)],
        compiler_params=pltpu.CompilerParams(dimension_semantics=("parallel", "arbitrary"), vmem_limit_bytes=VMEM_LIMIT),
        name="gdn",
    )(q, k, kt, v, gc, gr, q, k, kt, v, gc, gr)


def _split(p, sizes):
    return jnp.split(p, np.cumsum(sizes)[:-1].tolist(), axis=-1)


def _even_layer(x, norm_g, scale, shift, gate, w_in, m_gate_b, dn_dt_bias, dn_a_log, dn_conv_w, m_norm_g, dn_norm_g, w_out):
    bf16 = jnp.bfloat16
    mq, mk, mv, mo, mg, dqkv, dg, z = _split(w_in, EVEN_SPLITS)
    w = jnp.concatenate([mq, mv, dqkv], axis=1).astype(bf16)
    wkt = (mk * (A_DK ** -0.5)).T.astype(bf16)
    wg = jnp.concatenate([mg, dg], axis=1).astype(bf16)
    wz = jnp.concatenate([mo, z], axis=1).astype(bf16)
    pq, pkt, pv, bq, bk, bv, bkt, mc, mr, gc, gr, h = _inproj_even(x, norm_g, scale, shift, w, wkt, wg.T, dn_conv_w, m_gate_b, dn_a_log, dn_dt_bias)
    hf, hb = _mlstm(pq, pkt, pv, mc, mr)
    of, ob = _gdn(bq, bk, bkt, bv, gc, gr)
    return _outproj_even(hf, hb, of, ob, x, h, gate, m_norm_g, dn_norm_g, wz, w_out.astype(bf16))


def _odd_layer_final(x, norm_g, scale, shift, gate, w_in, dw_w, dw_b, ln_g, ln_b, rel_bias, w_out, final_g):
    bf16 = jnp.bfloat16
    ga, gb, aq, ak, av, z = _split(w_in, ODD_SPLITS)
    w = jnp.concatenate([ga, gb, aq * (D_DH ** -0.5), ak, av], axis=1).astype(bf16)
    out_c, h, pq, pk, pv = _inproj_odd(x, norm_g, scale, shift, w, dw_w, dw_b, ln_g, ln_b)
    og, lg = zip(*[_dilated_group_call(qd, kd, vd, rel_bias, dil) for qd, kd, vd, dil in zip(pq, pk, pv, DILATIONS)])
    return _outproj_odd_final(out_c, og, lg, x, h, gate, final_g, z.astype(bf16), w_out.astype(bf16))


def kernel(x, c, norm_g, ada_w, ada_b, ev_w_in, ev_m_gate_b, ev_dn_dt_bias, ev_dn_a_log, ev_dn_conv_w, ev_m_norm_g, ev_dn_norm_g, ev_w_out, od_w_in, od_dw_w, od_dw_b, od_ln_g, od_ln_b, od_w_out, rel_bias, final_g):
    assert DEPTH == 2, "the final RMSNorm is fused into the (last) odd layer's output projection"
    assert all(window // (2 * dil) == R_ATT for window, dil in D_GROUPS)
    d = x.shape[-1]
    mod = _adaln(c, ada_w, ada_b)
    for layer in range(DEPTH):
        shift, scale, gate = (mod[layer, :, i * d:(i + 1) * d][:, None, :] for i in range(3))
        j = layer // 2
        if layer % 2 == 0:
            x = _even_layer(x, norm_g[layer], scale, shift, gate, ev_w_in[j], ev_m_gate_b[j], ev_dn_dt_bias[j], ev_dn_a_log[j],
                            ev_dn_conv_w[j], ev_m_norm_g[j], ev_dn_norm_g[j], ev_w_out[j])
        else:
            x = _odd_layer_final(x, norm_g[layer], scale, shift, gate, od_w_in[j], od_dw_w[j], od_dw_b[j], od_ln_g[j], od_ln_b[j],
                                 rel_bias, od_w_out[j], final_g)
    return x
```

```python
import math

import jax
import jax.numpy as jnp
import numpy as np
from jax import lax
from jax.experimental import pallas as pl
from jax.experimental.pallas import tpu as pltpu

DEPTH = 2
A_HEADS = 4
A_DK = 64
A_DV = 128
B_HEADS = 4
B_DK = 128
B_DV = 128
B_CONV = 5
C_WIDTH = 512
C_CONV = 31
D_HEADS = 8
D_DH = 64
D_GROUPS = ((128, 1), (512, 4), (2048, 16))
REL_BUCKETS = 32
REL_MAX_DIST = 1024
EPS = 1e-6
NEG = -1e30
MIX_EVEN = A_HEADS * A_DV + B_HEADS * B_DV
MIX_ODD = C_WIDTH + D_HEADS * D_DH
B_QKV = B_HEADS * (2 * B_DK + B_DV)
EVEN_SPLITS = (A_HEADS * A_DK, A_HEADS * A_DK, A_HEADS * A_DV, A_HEADS * A_DV, 4 * A_HEADS, B_QKV, 4 * B_HEADS, MIX_EVEN)
ODD_SPLITS = (C_WIDTH, C_WIDTH, D_HEADS * D_DH, D_HEADS * D_DH, D_HEADS * D_DH, MIX_ODD)
DILATIONS = tuple(dil for _, dil in D_GROUPS)

LANES = 128
SUBLANES = 8
VMEM_LIMIT = 56 * 1024 * 1024
TM_PROJ = 1024
HALO_X = 16
SUB_C = 64
L_MLSTM = 256
C_GDN = 64
T_GDN_STEP = 256
B_REC_STEP = 2
TQ_ATT = 128
TB_ATT = 1024
R_ATT = 64

_EV_COLS = {"mq": (0, 256), "mv": (256, 768), "dqkv": (768, 2304)}
_OD_COLS = {"ga": (0, 512), "gb": (512, 1024), "aq": (1024, 1536), "ak": (1536, 2048), "av": (2048, 2560)}


def _log_sigmoid(t):
    return jnp.minimum(t, 0.0) - jnp.log(1.0 + jnp.exp(-jnp.abs(t)))


def _softplus(t):
    return jnp.maximum(t, 0.0) + jnp.log1p(jnp.exp(-jnp.abs(t)))


def _modulated_rms_val(x, g, scale, shift):
    y = x * lax.rsqrt(jnp.mean(x * x, axis=-1, keepdims=True) + EPS)
    return ((y * g) * (1.0 + scale) + shift).astype(jnp.bfloat16)


def _seg_scan_lanes(x, seg, reverse, op, fill):
    n = x.shape[1]
    pos = lax.broadcasted_iota(jnp.int32, x.shape, 1) % seg
    k = 1
    while k < seg:
        if reverse:
            x = op(x, jnp.where(pos < seg - k, pltpu.roll(x, n - k, axis=1), fill))
        else:
            x = op(x, jnp.where(pos >= k, pltpu.roll(x, k, axis=1), fill))
        k *= 2
    return x


def _seg_cumsum_lanes(x, seg, reverse):
    return _seg_scan_lanes(x, seg, reverse, jnp.add, 0.0)


def _seg_cummax_lanes(x, seg, reverse):
    return _seg_scan_lanes(x, seg, reverse, jnp.maximum, -jnp.inf)


def _conv_taps(xe, w_ref, cs, width, t, halo):
    n = t + 2 * halo
    acc = None
    for j in range(width):
        off = halo - width // 2 + j
        shifted = xe[off:off + t] if off % SUBLANES == 0 else pltpu.roll(xe, n - off, axis=0)[0:t]
        term = shifted * w_ref[j:j + 1, cs]
        acc = term if acc is None else acc + term
    return acc


def _zero_outside(d, t, halo, first, last):
    return jnp.concatenate([jnp.where(first, 0.0, d[:halo]), d[halo:halo + t], jnp.where(last, 0.0, d[halo + t:])], axis=0)


def _halo_specs(tm, s, d):
    hb = tm // HALO_X
    return [pl.BlockSpec((1, HALO_X, d), lambda b, i: (b, jnp.maximum(i * hb - 1, 0), 0)),
            pl.BlockSpec((1, HALO_X, d), lambda b, i: (b, jnp.minimum((i + 1) * hb, s // HALO_X - 1), 0))]


def _mirror_maps(nblocks):
    return (lambda b, n: (b, n, 0), lambda b, n: (b, nblocks - 1 - n, 0),
            lambda b, n: (b, 0, n), lambda b, n: (b, 0, nblocks - 1 - n))


def _adaln_kernel(c_ref, w_ref, b_ref, o_ref):
    c = c_ref[...]
    cs = (c * jax.nn.sigmoid(c)).astype(jnp.bfloat16)
    o_ref[0] = jnp.dot(cs, w_ref[0].astype(jnp.bfloat16), preferred_element_type=jnp.float32) + b_ref[0]


def _adaln(c, ada_w, ada_b):
    depth, d, n3 = ada_w.shape
    bsz = c.shape[0]
    tn = 1024
    return pl.pallas_call(
        _adaln_kernel,
        grid=(depth, n3 // tn),
        in_specs=[pl.BlockSpec((bsz, d), lambda l, j: (0, 0)), pl.BlockSpec((1, d, tn), lambda l, j: (l, 0, j)),
                  pl.BlockSpec((1, 1, tn), lambda l, j: (l, 0, j))],
        out_specs=pl.BlockSpec((1, bsz, tn), lambda l, j: (l, 0, j)),
        out_shape=jax.ShapeDtypeStruct((depth, bsz, n3), jnp.float32),
        compiler_params=pltpu.CompilerParams(dimension_semantics=("parallel", "parallel")),
        name="adaln",
    )(c, ada_w, ada_b.reshape(depth, 1, n3))


def _inproj_even_kernel(x_ref, xp_ref, xn_ref, g_ref, sc_ref, sh_ref, w_ref, wkt_ref, wgt_ref, cw_ref, mb_ref, a_ref, dt_ref,
                        mq_ref, mkt_ref, mv_ref, bq_ref, bk_ref, bv_ref, bkt_ref, mc_ref, mr_ref, gc_ref, gr_ref, h_ref):
    i = pl.program_id(1)
    nt = pl.num_programs(1)
    tm = x_ref.shape[1]
    f32 = jnp.float32
    x_ext = jnp.concatenate([xp_ref[0], x_ref[0], xn_ref[0]], axis=0)
    h_ext = _modulated_rms_val(x_ext, g_ref[...], sc_ref[0], sh_ref[0])
    h = h_ext[HALO_X:HALO_X + tm]
    h_ref[0] = h
    xpose = (((1,), (1,)), ((), ()))

    def mlstm_operand(name, o_ref):
        lo, hi = _EV_COLS[name]
        o_ref[0] = jnp.dot(h, w_ref[:, lo:hi], preferred_element_type=f32).astype(o_ref.dtype)

    def gates():
        mkt_ref[0] = lax.dot_general(wkt_ref[...], h, xpose, preferred_element_type=f32).astype(mkt_ref.dtype)
        gates_t = lax.dot_general(wgt_ref[...], h, xpose, preferred_element_type=f32)
        na = A_HEADS
        gm = gates_t[:16] + mb_ref[...]
        logf = _log_sigmoid(gm[2 * na:])
        b_f, b_b = _seg_cumsum_lanes(logf[:na], L_MLSTM, False), _seg_cumsum_lanes(logf[na:], L_MLSTM, True)
        pm_f = _seg_cummax_lanes(gm[:na] - b_f, L_MLSTM, False)
        pm_b = _seg_cummax_lanes(gm[na:2 * na] - b_b, L_MLSTM, True)
        mr_ref[0] = jnp.concatenate([gm[:2 * na], b_f, b_b], axis=0)
        mc_ref[0] = jnp.concatenate([b_f, b_b, pm_f, pm_b], axis=0).T
        nh = B_HEADS
        dgt = gates_t[16:]
        dec = -jnp.exp(a_ref[...]) * _softplus(dgt[2 * nh:] + dt_ref[...])
        gr = jnp.concatenate([jax.nn.sigmoid(dgt[:2 * nh]), _seg_cumsum_lanes(dec[:nh], C_GDN, False),
                              _seg_cumsum_lanes(dec[nh:], C_GDN, True)], axis=0)
        gr_ref[0] = gr
        gc_ref[0] = gr.T

    mlstm_operand("mq", mq_ref)
    mlstm_operand("mv", mv_ref)
    gates()
    hd = B_HEADS * B_DK
    for part, o_ref in enumerate((bq_ref, bk_ref, bv_ref)):
        lo = _EV_COLS["dqkv"][0] + part * hd
        cs = slice(part * hd, (part + 1) * hd)
        d = jnp.dot(h_ext, w_ref[:, lo:lo + hd], preferred_element_type=f32)
        acc = _conv_taps(_zero_outside(d, tm, HALO_X, i == 0, i == nt - 1), cw_ref, cs, B_CONV, tm, HALO_X)
        y = acc * jax.nn.sigmoid(acc)
        for hh in range(B_HEADS):
            yh = y[:, hh * B_DK:(hh + 1) * B_DK]
            if part == 0:
                yh = yh * lax.rsqrt(jnp.sum(yh * yh, axis=-1, keepdims=True) + EPS) * (B_DK ** -0.5)
            elif part == 1:
                yh = yh * lax.rsqrt(jnp.sum(yh * yh, axis=-1, keepdims=True) + EPS)
            o_ref[0, :, hh * B_DK:(hh + 1) * B_DK] = yh.astype(o_ref.dtype)
            if part == 1:
                bkt_ref[0, hh * B_DK:(hh + 1) * B_DK, :] = yh.T.astype(bkt_ref.dtype)


def _inproj_even(x, g, scale, shift, w, wkt, wgt, conv_w, m_gate_b, a_log, dt_bias):
    bsz, s, d = x.shape
    tm = TM_PROJ
    assert tm % L_MLSTM == 0 and tm % C_GDN == 0
    tok = lambda b, i: (b, i, 0)
    tok_t = lambda b, i: (b, 0, i)
    const = lambda b, i: (0, 0)
    bvec = lambda b, i: (b, 0, 0)
    bf16, f32 = jnp.bfloat16, jnp.float32
    hk, hv, hd = A_HEADS * A_DK, A_HEADS * A_DV, B_HEADS * B_DK
    tok_specs = lambda wd: pl.BlockSpec((1, tm, wd), tok)
    gate_specs = [pl.BlockSpec((1, tm, 16), tok), pl.BlockSpec((1, 16, tm), tok_t)]
    gate_shapes = [jax.ShapeDtypeStruct((bsz, s, 16), f32), jax.ShapeDtypeStruct((bsz, 16, s), f32)]
    return pl.pallas_call(
        _inproj_even_kernel,
        grid=(bsz, s // tm),
        in_specs=[pl.BlockSpec((1, tm, d), tok)] + _halo_specs(tm, s, d) + [pl.BlockSpec((1, d), const), pl.BlockSpec((1, 1, d), bvec),
                  pl.BlockSpec((1, 1, d), bvec), pl.BlockSpec(w.shape, const), pl.BlockSpec(wkt.shape, const), pl.BlockSpec(wgt.shape, const),
                  pl.BlockSpec(conv_w.shape, const), pl.BlockSpec((16, 1), const), pl.BlockSpec((8, 1), const), pl.BlockSpec((8, 1), const)],
        out_specs=[tok_specs(hk), pl.BlockSpec((1, hk, tm), tok_t), tok_specs(hv), tok_specs(hd), tok_specs(hd), tok_specs(hd),
                   pl.BlockSpec((1, hd, tm), tok_t)] + gate_specs + gate_specs + [tok_specs(d)],
        out_shape=[jax.ShapeDtypeStruct((bsz, s, hk), bf16), jax.ShapeDtypeStruct((bsz, hk, s), bf16), jax.ShapeDtypeStruct((bsz, s, hv), bf16)]
        + [jax.ShapeDtypeStruct((bsz, s, hd), bf16)] * 3 + [jax.ShapeDtypeStruct((bsz, hd, s), bf16)] + gate_shapes + gate_shapes
        + [jax.ShapeDtypeStruct((bsz, s, d), bf16)],
        compiler_params=pltpu.CompilerParams(dimension_semantics=("parallel", "parallel"), vmem_limit_bytes=VMEM_LIMIT),
        name="inproj_even",
    )(x, x, x, g.reshape(1, d), scale, shift, w, wkt, wgt, conv_w, m_gate_b.reshape(16, 1), a_log.reshape(8, 1), dt_bias.reshape(8, 1))


def _inproj_odd_kernel(x_ref, xp_ref, xn_ref, g_ref, sc_ref, sh_ref, w_ref, cw_ref, cb_ref, lg_ref, lb_ref, oc_ref, h_ref, *rest):
    out_refs, plane_ref = rest[:-1], rest[-1]
    i = pl.program_id(1)
    nt = pl.num_programs(1)
    f32 = jnp.float32
    tm = x_ref.shape[1]
    nd = D_HEADS * D_DH
    x_ext = jnp.concatenate([xp_ref[0], x_ref[0], xn_ref[0]], axis=0)
    h_ext = _modulated_rms_val(x_ext, g_ref[...], sc_ref[0], sh_ref[0])
    h = h_ext[HALO_X:HALO_X + tm]
    h_ref[0] = h
    dot = lambda name: jnp.dot(h, w_ref[:, _OD_COLS[name][0]:_OD_COLS[name][1]], preferred_element_type=f32)
    dot_ext = lambda name: jnp.dot(h_ext, w_ref[:, _OD_COLS[name][0]:_OD_COLS[name][1]], preferred_element_type=f32)
    xe = _zero_outside(dot_ext("ga") * jax.nn.sigmoid(dot_ext("gb")), tm, HALO_X, i == 0, i == nt - 1)
    n = tm + 2 * HALO_X
    phases = [xe] + [pltpu.roll(xe, n - b, axis=0) for b in range(1, SUBLANES)]
    for r0 in range(0, tm, SUB_C):
        acc = None
        for j in range(C_CONV):
            a, b = divmod(HALO_X - C_CONV // 2 + j, SUBLANES)
            lo = a * SUBLANES + r0
            term = phases[b][lo:lo + SUB_C] * cw_ref[j:j + 1, :]
            acc = term if acc is None else acc + term
        u = acc + cb_ref[...]
        uc = u - jnp.mean(u, axis=-1, keepdims=True)
        y = uc * lax.rsqrt(jnp.mean(uc * uc, axis=-1, keepdims=True) + EPS) * lg_ref[...] + lb_ref[...]
        oc_ref[0, r0:r0 + SUB_C, :] = y * jax.nn.sigmoid(y)
    for a, name in enumerate(("aq", "ak", "av")):
        r = dot(name)
        group_refs = out_refs[a * len(DILATIONS):(a + 1) * len(DILATIONS)]
        for j in range(nd // LANES):
            plane_ref[a, j] = r[:, j * LANES:(j + 1) * LANES]
        for dil, o_ref in zip(DILATIONS, group_refs):
            if dil == 1:
                o_ref[0, 0] = r.astype(o_ref.dtype)
                continue
            for res in range(dil):
                for j in range(nd // LANES):
                    o_ref[0, res, :, j * LANES:(j + 1) * LANES] = plane_ref[a, j, pl.ds(res, tm // dil, stride=dil), :].astype(o_ref.dtype)


def _inproj_odd(x, g, scale, shift, w, dw_w, dw_b, ln_g, ln_b):
    bsz, s, d = x.shape
    tm = TM_PROJ
    tok = lambda b, i: (b, i, 0)
    const = lambda b, i: (0, 0)
    bvec = lambda b, i: (b, 0, 0)
    bf16, f32 = jnp.bfloat16, jnp.float32
    nd = D_HEADS * D_DH
    cw = C_WIDTH
    att_specs = [pl.BlockSpec((1, dil, tm // dil, nd), lambda b, i: (b, 0, i, 0)) for dil in DILATIONS] * 3
    att_shapes = [jax.ShapeDtypeStruct((bsz, dil, s // dil, nd), bf16) for dil in DILATIONS] * 3
    outs = pl.pallas_call(
        _inproj_odd_kernel,
        grid=(bsz, s // tm),
        in_specs=[pl.BlockSpec((1, tm, d), tok)] + _halo_specs(tm, s, d) + [pl.BlockSpec((1, d), const), pl.BlockSpec((1, 1, d), bvec),
                  pl.BlockSpec((1, 1, d), bvec), pl.BlockSpec(w.shape, const), pl.BlockSpec((C_CONV, cw), const)]
        + [pl.BlockSpec((1, cw), const)] * 3,
        out_specs=[pl.BlockSpec((1, tm, cw), tok), pl.BlockSpec((1, tm, d), tok)] + att_specs,
        out_shape=[jax.ShapeDtypeStruct((bsz, s, cw), f32), jax.ShapeDtypeStruct((bsz, s, d), bf16)] + att_shapes,
        scratch_shapes=[pltpu.VMEM((3, nd // LANES, tm, LANES), f32)],
        compiler_params=pltpu.CompilerParams(dimension_semantics=("parallel", "parallel"), vmem_limit_bytes=VMEM_LIMIT),
        name="inproj_odd",
    )(x, x, x, g.reshape(1, d), scale, shift, w, dw_w, dw_b.reshape(1, cw), ln_g.reshape(1, cw), ln_b.reshape(1, cw))
    ng = len(DILATIONS)
    return outs[0], outs[1], outs[2:2 + ng], outs[2 + ng:2 + 2 * ng], outs[2 + 2 * ng:]


def _head_rms_cols(t, g, width):
    parts = []
    for h in range(t.shape[1] // width):
        th = t[:, h * width:(h + 1) * width]
        parts.append(th * lax.rsqrt(jnp.mean(th * th, axis=-1, keepdims=True) + EPS))
    return jnp.concatenate(parts, axis=1) * g


def _outproj_even_kernel(hf_ref, hb_ref, of_ref, ob_ref, x_ref, h_ref, gate_ref, mg_ref, dg_ref, wz_ref, w_ref, o_ref):
    f32, bf16 = jnp.float32, jnp.bfloat16
    na = A_HEADS * A_DV
    h = h_ref[0]
    mo = jnp.dot(h, wz_ref[:, :na], preferred_element_type=f32)
    z = jnp.dot(h, wz_ref[:, na:], preferred_element_type=f32)
    sz = z * jax.nn.sigmoid(z)
    out_a = jax.nn.sigmoid(mo) * _head_rms_cols(hf_ref[0] + hb_ref[0], mg_ref[...], A_DV)
    out_b = _head_rms_cols(of_ref[0] + ob_ref[0], dg_ref[...], B_DV)
    y = jnp.dot((out_a * sz[:, :na]).astype(bf16), w_ref[:na, :], preferred_element_type=f32)
    y = y + jnp.dot((out_b * sz[:, na:]).astype(bf16), w_ref[na:, :], preferred_element_type=f32)
    o_ref[0] = x_ref[0] + gate_ref[0] * y


def _outproj_even(hf, hb, of, ob, x, h, gate, m_norm_g, dn_norm_g, wz, w):
    bsz, s, d = x.shape
    tm = TM_PROJ
    tok = lambda b, i: (b, i, 0)
    const = lambda b, i: (0, 0)
    bvec = lambda b, i: (b, 0, 0)
    na, nb = A_HEADS * A_DV, B_HEADS * B_DV
    return pl.pallas_call(
        _outproj_even_kernel,
        grid=(bsz, s // tm),
        in_specs=[pl.BlockSpec((1, tm, na), tok)] * 2 + [pl.BlockSpec((1, tm, nb), tok)] * 2 + [pl.BlockSpec((1, tm, d), tok)] * 2
        + [pl.BlockSpec((1, 1, d), bvec), pl.BlockSpec((1, na), const), pl.BlockSpec((1, nb), const), pl.BlockSpec(wz.shape, const),
           pl.BlockSpec(w.shape, const)],
        out_specs=pl.BlockSpec((1, tm, d), tok),
        out_shape=jax.ShapeDtypeStruct((bsz, s, d), jnp.float32),
        compiler_params=pltpu.CompilerParams(dimension_semantics=("parallel", "parallel"), vmem_limit_bytes=VMEM_LIMIT),
        name="outproj_even",
    )(hf, hb, of, ob, x, h, gate, m_norm_g.reshape(1, na), dn_norm_g.reshape(1, nb), wz, w)


def _outproj_odd_kernel(oc_ref, o1_ref, o2_ref, o3_ref, l1_ref, l2_ref, l3_ref, x_ref, h_ref, gate_ref, fg_ref,
                        wz_ref, w_ref, o_ref, nat_ref):
    f32, bf16 = jnp.float32, jnp.bfloat16
    tm = x_ref.shape[1]
    npl = D_HEADS * D_DH // LANES
    z = jnp.dot(h_ref[0], wz_ref[...], preferred_element_type=f32)
    sz = z * jax.nn.sigmoid(z)
    groups = []
    for gi, (dil, og_ref, lg_ref) in enumerate(zip(DILATIONS, (o1_ref, o2_ref, o3_ref), (l1_ref, l2_ref, l3_ref))):
        if dil == 1:
            groups.append(([og_ref[0, 0, :, j * LANES:(j + 1) * LANES].astype(f32) for j in range(npl)], lg_ref[0, 0]))
            continue
        for res in range(dil):
            rows = pl.ds(res, tm // dil, stride=dil)
            for j in range(npl):
                nat_ref[gi, j, rows, :] = og_ref[0, res, :, j * LANES:(j + 1) * LANES].astype(f32)
            nat_ref[gi, npl, rows, :] = lg_ref[0, res]
        groups.append(([nat_ref[gi, j] for j in range(npl)], nat_ref[gi, npl]))
    (p1, l1), (p2, l2), (p3, l3) = groups
    lm = jnp.maximum(jnp.maximum(l1, l2), l3)
    e1, e2, e3 = jnp.exp(l1 - lm), jnp.exp(l2 - lm), jnp.exp(l3 - lm)
    inv = 1.0 / (e1 + e2 + e3)
    low = lax.broadcasted_iota(jnp.int32, (tm, LANES), 1) < D_DH
    planes = []
    for j in range(npl):
        acc = None
        for e, p in ((e1, p1), (e2, p2), (e3, p3)):
            wgt = e * inv
            term = jnp.where(low, wgt[:, 2 * j:2 * j + 1], wgt[:, 2 * j + 1:2 * j + 2]) * p[j]
            acc = term if acc is None else acc + term
        planes.append(acc)
    out_d = jnp.concatenate(planes, axis=1)
    y = jnp.dot((oc_ref[0] * sz[:, :C_WIDTH]).astype(bf16), w_ref[:C_WIDTH, :], preferred_element_type=f32)
    y = y + jnp.dot((out_d * sz[:, C_WIDTH:]).astype(bf16), w_ref[C_WIDTH:, :], preferred_element_type=f32)
    xn = x_ref[0] + gate_ref[0] * y
    o_ref[0] = xn * lax.rsqrt(jnp.mean(xn * xn, axis=-1, keepdims=True) + EPS) * fg_ref[...]


def _outproj_odd_final(oc, og, lg, x, h, gate, final_g, wz, w):
    bsz, s, d = x.shape
    tm = TM_PROJ
    tok = lambda b, i: (b, i, 0)
    const = lambda b, i: (0, 0)
    bvec = lambda b, i: (b, 0, 0)
    nd = D_HEADS * D_DH
    res_major = lambda width: [pl.BlockSpec((1, dil, tm // dil, width), lambda b, i: (b, 0, i, 0)) for dil in DILATIONS]
    return pl.pallas_call(
        _outproj_odd_kernel,
        grid=(bsz, s // tm),
        in_specs=[pl.BlockSpec((1, tm, C_WIDTH), tok)] + res_major(nd) + res_major(LANES)
        + [pl.BlockSpec((1, tm, d), tok), pl.BlockSpec((1, tm, d), tok), pl.BlockSpec((1, 1, d), bvec), pl.BlockSpec((1, d), const),
           pl.BlockSpec(wz.shape, const), pl.BlockSpec(w.shape, const)],
        out_specs=pl.BlockSpec((1, tm, d), tok),
        out_shape=jax.ShapeDtypeStruct((bsz, s, d), jnp.float32),
        scratch_shapes=[pltpu.VMEM((len(DILATIONS), nd // LANES + 1, tm, LANES), jnp.float32)],
        compiler_params=pltpu.CompilerParams(dimension_semantics=("parallel", "parallel"), vmem_limit_bytes=VMEM_LIMIT),
        name="outproj_odd",
    )(oc, *og, *lg, x, h, gate, final_g.reshape(1, d), wz, w)


def _dilated_kernel(q_ref, kc_ref, kp_ref, kn_ref, vc_ref, vp_ref, vn_ref, bias_ref, o_ref, lse_ref, kx_ref, vx_ref):
    i = pl.program_id(2)
    nt = pl.num_programs(2)
    tb = q_ref.shape[1]
    tq = TQ_ATT
    nk = tq + 2 * R_ATT
    nsub = tb // tq
    f32, bf16 = jnp.float32, jnp.bfloat16
    kx_ref[0:R_ATT, :] = kp_ref[0]
    kx_ref[R_ATT:R_ATT + tb, :] = kc_ref[0]
    kx_ref[R_ATT + tb:, :] = kn_ref[0]
    for pr in range(D_HEADS // 2):
        src, dst = slice(pr * LANES, (pr + 1) * LANES), slice(2 * pr * LANES, (2 * pr + 1) * LANES)
        vx_ref[0:R_ATT, dst] = vp_ref[0, :, src]
        vx_ref[R_ATT:R_ATT + tb, dst] = vc_ref[0, :, src]
        vx_ref[R_ATT + tb:, dst] = vn_ref[0, :, src]
        vx_ref[:, (2 * pr + 1) * LANES:(2 * pr + 2) * LANES] = jnp.ones((tb + 2 * R_ATT, LANES), bf16)
    kj = lax.broadcasted_iota(jnp.int32, (tq, nk), 1)
    lane = lax.broadcasted_iota(jnp.int32, (tq, LANES), 1)
    low = lane < D_DH
    heads = [(pr, hi) for pr in range(D_HEADS // 2) for hi in (False, True)]
    for sub in range(nsub):
        qs = slice(sub * tq, (sub + 1) * tq)
        ks = slice(sub * tq, sub * tq + nk)
        outside = None
        if sub == 0:
            outside = (kj < R_ATT) & (i == 0)
        if sub == nsub - 1:
            after = (kj >= R_ATT + tq) & (i == nt - 1)
            outside = after if outside is None else outside | after
        scs = []
        for pr, hi in heads:
            ps = slice(pr * LANES, (pr + 1) * LANES)
            qp = q_ref[0, qs, ps]
            qh = jnp.where(low != hi, qp, jnp.zeros_like(qp))
            sc = lax.dot_general(qh, kx_ref[ks, ps], (((1,), (1,)), ((), ())), preferred_element_type=f32) + bias_ref[2 * pr + int(hi)]
            scs.append(sc if outside is None else jnp.where(outside, NEG, sc))
        ms = [jnp.max(sc, axis=-1, keepdims=True) for sc in scs]
        ps_ = [jnp.exp(sc - m) for sc, m in zip(scs, ms)]
        pvs = [jnp.dot(p.astype(bf16), vx_ref[ks, 2 * pr * LANES:(2 * pr + 2) * LANES], preferred_element_type=f32) for (pr, _), p in zip(heads, ps_)]
        lse_all = jnp.zeros((tq, LANES), f32)
        for pr in range(D_HEADS // 2):
            lo, hi = 2 * pr, 2 * pr + 1
            (num_lo, den_lo), (num_hi, den_hi) = ((pvs[h][:, :LANES], pvs[h][:, LANES:]) for h in (lo, hi))
            o_ref[0, qs, pr * LANES:(pr + 1) * LANES] = jnp.where(low, num_lo / den_lo, num_hi / den_hi).astype(o_ref.dtype)
            lse_all = jnp.where(lane == lo, ms[lo] + jnp.log(den_lo), lse_all)
            lse_all = jnp.where(lane == hi, ms[hi] + jnp.log(den_hi), lse_all)
        lse_ref[0, qs, :] = lse_all


def _dilated_bias(rel_bias, dilation, tq):
    half = REL_BUCKETS // 2
    exact = half // 2
    qi = jnp.arange(tq)[:, None]
    kj = jnp.arange(tq + 2 * R_ATT)[None, :]
    rel = kj - R_ATT - qi
    reld = rel * dilation
    n = jnp.abs(reld)
    large = exact + (jnp.log(jnp.maximum(n, 1).astype(jnp.float32) / exact) / math.log(REL_MAX_DIST / exact) * (half - exact)).astype(jnp.int32)
    large = jnp.minimum(large, half - 1)
    bucket = (reld > 0).astype(jnp.int32) * half + jnp.where(n < exact, n, large)
    bias = jnp.zeros((rel_bias.shape[1],) + bucket.shape, jnp.float32)
    for b in range(REL_BUCKETS):
        bias = jnp.where((bucket == b)[None], rel_bias[b].astype(jnp.float32)[:, None, None], bias)
    return jnp.where((jnp.abs(rel) <= R_ATT)[None], bias, NEG)


def _dilated_group_call(q, k, v, rel_bias, dilation):
    bsz, dil, ls, nd = q.shape
    assert dil == dilation
    tb, tq = min(TB_ATT, ls), TQ_ATT
    assert ls % tb == 0 and tb % tq == 0
    nt = ls // tb
    hb = tb // R_ATT
    nk = tq + 2 * R_ATT
    cur = lambda b, r, i: (b, r, i, 0)
    prev = lambda b, r, i: (b, r, jnp.maximum(i * hb - 1, 0), 0)
    nxt = lambda b, r, i: (b, r, jnp.minimum((i + 1) * hb, ls // R_ATT - 1), 0)
    kv_specs = [pl.BlockSpec((1, None, tb, nd), cur), pl.BlockSpec((1, None, R_ATT, nd), prev), pl.BlockSpec((1, None, R_ATT, nd), nxt)]
    return pl.pallas_call(
        _dilated_kernel,
        grid=(bsz, dilation, nt),
        in_specs=[pl.BlockSpec((1, None, tb, nd), cur)] + kv_specs + kv_specs + [pl.BlockSpec((D_HEADS, tq, nk), lambda b, r, i: (0, 0, 0))],
        out_specs=[pl.BlockSpec((1, None, tb, nd), cur), pl.BlockSpec((1, None, tb, LANES), cur)],
        out_shape=[jax.ShapeDtypeStruct((bsz, dilation, ls, nd), jnp.bfloat16), jax.ShapeDtypeStruct((bsz, dilation, ls, LANES), jnp.float32)],
        scratch_shapes=[pltpu.VMEM((tb + 2 * R_ATT, nd), jnp.bfloat16), pltpu.VMEM((tb + 2 * R_ATT, 2 * nd), jnp.bfloat16)],
        compiler_params=pltpu.CompilerParams(dimension_semantics=("parallel", "parallel", "parallel"), vmem_limit_bytes=VMEM_LIMIT),
        name=f"dilated_d{dilation}",
    )(q, k, k, k, v, v, v, _dilated_bias(rel_bias, dilation, tq))


def _mlstm_kernel(qf_ref, kf_ref, vf_ref, gf_ref, gtf_ref, qb_ref, kb_ref, vb_ref, gb_ref, gtb_ref,
                  hf_ref, hb_ref, c_ref, m_ref):
    @pl.when(pl.program_id(1) == 0)
    def _():
        c_ref[...] = jnp.zeros_like(c_ref)
        m_ref[...] = jnp.zeros_like(m_ref)

    nbat, ln = qf_ref.shape[0], qf_ref.shape[1]
    f32, bf16 = jnp.float32, jnp.bfloat16
    row = lax.broadcasted_iota(jnp.int32, (ln, ln), 0)
    col = lax.broadcasted_iota(jnp.int32, (ln, ln), 1)
    ones_blk = jnp.ones((ln, A_DV), bf16)
    fwd_refs, bwd_refs = (qf_ref, kf_ref, vf_ref, gf_ref, gtf_ref), (qb_ref, kb_ref, vb_ref, gb_ref, gtb_ref)
    probs = []
    na = A_HEADS
    for bi in range(nbat):
        for d, (q_ref, kt_ref, v_ref, gc_ref, gr_ref), h_ref in ((0, fwd_refs, hf_ref), (1, bwd_refs, hb_ref)):
            mask = (row >= col) if d == 0 else (row <= col)
            for h in range(na):
                r = d * na + h
                probs.append(dict(
                    bi=bi, slot=bi * 2 * na + r, h=h, mask=mask, last=ln - 1 if d == 0 else 0, h_ref=h_ref,
                    q=q_ref[bi, :, h * A_DK:(h + 1) * A_DK],
                    kt=kt_ref[bi, h * A_DK:(h + 1) * A_DK, :],
                    vaug=jnp.concatenate([v_ref[bi, :, h * A_DV:(h + 1) * A_DV], ones_blk], axis=1),
                    bcol=gc_ref[bi, :, r:r + 1], pmcol=gc_ref[bi, :, 2 * na + r:2 * na + r + 1],
                    brow=gr_ref[bi, 2 * na + r:2 * na + r + 1, :],
                    vrow=gr_ref[bi, r:r + 1, :] - gr_ref[bi, 2 * na + r:2 * na + r + 1, :]))
    for p in probs:
        p["m_old"] = m_ref[p["slot"]:p["slot"] + 1, 0:1]
        p["caug"] = c_ref[p["slot"]]
        p["qk"] = jnp.dot(p["q"], p["kt"], preferred_element_type=f32)
    for p in probs:
        p["qc"] = jnp.dot(p["q"], p["caug"].astype(bf16), preferred_element_type=f32)
    for p in probs:
        mstab = jnp.maximum(p["m_old"], p["pmcol"])
        p["w_int"] = jnp.exp(p["m_old"] - mstab)
        p["emt"] = jnp.exp(-(p["bcol"] + mstab))
        p["sc"] = (jnp.exp(jnp.where(p["mask"], p["vrow"] - mstab, -jnp.inf)) * p["qk"]).astype(bf16)
    for p in probs:
        tot = p["w_int"] * p["qc"] + jnp.dot(p["sc"], p["vaug"], preferred_element_type=f32)
        den = jnp.maximum(jnp.abs(tot[:, A_DV:]), p["emt"])
        p["h_ref"][p["bi"], :, p["h"] * A_DV:(p["h"] + 1) * A_DV] = tot[:, :A_DV] / den
    for p in probs:
        last, brow, vrow = p["last"], p["brow"], p["vrow"]
        btot = brow[:, last:last + 1]
        m_new = btot + jnp.maximum(p["m_old"], jnp.max(vrow, axis=-1, keepdims=True))
        w_old = jnp.exp(btot + p["m_old"] - m_new)
        kwt = (p["kt"].astype(f32) * jnp.exp(btot + vrow - m_new)).astype(bf16)
        c_ref[p["slot"]] = w_old * p["caug"] + jnp.dot(kwt, p["vaug"], preferred_element_type=f32)
        m_ref[p["slot"]:p["slot"] + 1, :] = jnp.broadcast_to(m_new, (1, m_ref.shape[1]))


def _mlstm(q, kt, v, gc, gr):
    bsz, s, _ = q.shape
    ln = min(L_MLSTM, s)
    nc = s // ln
    bb = B_REC_STEP if bsz % B_REC_STEP == 0 else 1
    hk, hv = A_HEADS * A_DK, A_HEADS * A_DV
    fwd, bwd, fwd_t, bwd_t = _mirror_maps(nc)
    def specs(im, im_t):
        return [pl.BlockSpec((bb, ln, hk), im), pl.BlockSpec((bb, hk, ln), im_t), pl.BlockSpec((bb, ln, hv), im),
                pl.BlockSpec((bb, ln, 16), im), pl.BlockSpec((bb, 16, ln), im_t)]
    return pl.pallas_call(
        _mlstm_kernel,
        grid=(bsz // bb, nc),
        in_specs=specs(fwd, fwd_t) + specs(bwd, bwd_t),
        out_specs=[pl.BlockSpec((bb, ln, hv), fwd), pl.BlockSpec((bb, ln, hv), bwd)],
        out_shape=[jax.ShapeDtypeStruct((bsz, s, hv), jnp.float32)] * 2,
        scratch_shapes=[pltpu.VMEM((bb * 2 * A_HEADS, A_DK, 2 * A_DV), jnp.float32), pltpu.VMEM((bb * 2 * A_HEADS, LANES), jnp.float32)],
        compiler_params=pltpu.CompilerParams(dimension_semantics=("parallel", "arbitrary"), vmem_limit_bytes=VMEM_LIMIT),
        name="mlstm",
    )(q, kt, v, gc, gr, q, kt, v, gc, gr)


def _half_rows(x, s, odd):
    return jnp.concatenate([x[i * s:(i + 1) * s] for i in range(x.shape[0] // s) if (i % 2 == 1) == odd], axis=0)


def _with_half_rows(sel, rest, s, odd):
    n = 2 * sel.shape[0]
    blocks = []
    for i in range(n // s):
        if (i % 2 == 1) == odd:
            blocks.append(sel[(i // 2) * s:(i // 2 + 1) * s])
        else:
            blocks.append(jnp.zeros((s, sel.shape[1]), sel.dtype) if rest is None else rest[i * s:(i + 1) * s])
    return jnp.concatenate(blocks, axis=0)


def _tri_inverse_many(a_list, lowers, masks):
    eye, m16, m32, m64 = masks
    f32, bf16 = jnp.float32, jnp.bfloat16
    mm = lambda x, y: jnp.dot(x.astype(bf16), y.astype(bf16), preferred_element_type=f32)
    ads = [jnp.where(m16, a, 0.0) for a in a_list]
    xs = [eye - ad for ad in ads]
    ps = [mm(ad, ad) for ad in ads]
    for stage in range(3):
        xs = [x + mm(x, p) for x, p in zip(xs, ps)]
        if stage < 2:
            ps = [mm(p, p) for p in ps]
    for s, lo, hi in ((16, m16, m32), (32, m32, m64)):
        off = hi & ~lo
        ys = [mm(_half_rows(jnp.where(off, a, 0.0), s, low), x) for a, low, x in zip(a_list, lowers, xs)]
        upd = [_half_rows(x, s, low) - mm(_half_rows(x, s, low), _with_half_rows(y, None, s, low))
               for x, low, y in zip(xs, lowers, ys)]
        xs = [_with_half_rows(u, x, s, low) for u, x, low in zip(upd, xs, lowers)]
    return xs


def _gdn_kernel(qf_ref, kf_ref, ktf_ref, vf_ref, gcf_ref, grf_ref, qb_ref, kb_ref, ktb_ref, vb_ref, gcb_ref, grb_ref,
                of_ref, ob_ref, s_ref):
    @pl.when(pl.program_id(1) == 0)
    def _():
        s_ref[...] = jnp.zeros_like(s_ref)

    nbat, t = qf_ref.shape[0], qf_ref.shape[1]
    c = C_GDN
    f32, bf16 = jnp.float32, jnp.bfloat16
    row = lax.broadcasted_iota(jnp.int32, (c, c), 0)
    col = lax.broadcasted_iota(jnp.int32, (c, c), 1)
    eye = (row == col).astype(f32)
    blk = lambda w: (row // w) == (col // w)
    masks = (eye, blk(16), blk(32), blk(64))
    nh, nchunk = B_HEADS, t // c
    dir_refs = ((qf_ref, kf_ref, ktf_ref, vf_ref, gcf_ref, grf_ref, of_ref), (qb_ref, kb_ref, ktb_ref, vb_ref, gcb_ref, grb_ref, ob_ref))
    probs = [(bi, d, h, ci) for bi in range(nbat) for d in range(2) for h in range(nh) for ci in range(nchunk)]
    xpose = (((1,), (1,)), ((), ()))

    def load(bi, d, h, ci):
        q_ref, k_ref, kt_ref, v_ref, gc_ref, gr_ref, _ = dir_refs[d]
        rs, cs = slice(ci * c, (ci + 1) * c), slice(h * B_DK, (h + 1) * B_DK)
        return dict(
            q=q_ref[bi, rs, cs], k=k_ref[bi, rs, cs], kt=kt_ref[bi, cs, rs], v=v_ref[bi, rs, cs],
            beta=gc_ref[bi, rs, d * nh + h:d * nh + h + 1],
            gcol=gc_ref[bi, rs, (2 + d) * nh + h:(2 + d) * nh + h + 1],
            grow=gr_ref[bi, (2 + d) * nh + h:(2 + d) * nh + h + 1, rs])

    data = [load(*p) for p in probs]
    for (_, d, _, _), p in zip(probs, data):
        incl = (row >= col) if d == 0 else (row <= col)
        p["gam"] = jnp.exp(jnp.where(incl, p["gcol"] - p["grow"], -jnp.inf))
    for p in data:
        p["kk"] = lax.dot_general(p["k"], p["k"], xpose, preferred_element_type=f32)
    for p in data:
        p["qk"] = lax.dot_general(p["q"], p["k"], xpose, preferred_element_type=f32)
    a_list = []
    for (_, d, _, _), p in zip(probs, data):
        strict = (row > col) if d == 0 else (row < col)
        a_list.append(jnp.where(strict, p["beta"] * p["kk"] * p["gam"], 0.0))
    tinvs = _tri_inverse_many(a_list, [d == 0 for (_, d, _, _) in probs], masks)
    for p, tinv in zip(data, tinvs):
        p["egc"] = jnp.exp(p["gcol"])
        rhs = jnp.concatenate([p["beta"] * p["v"].astype(f32), (p["beta"] * p["egc"]) * p["k"].astype(f32)], axis=1).astype(bf16)
        p["uw"] = jnp.dot(tinv.astype(bf16), rhs, preferred_element_type=f32)
        p["attn"] = (p["qk"] * p["gam"]).astype(bf16)
    index = {p: i for i, p in enumerate(probs)}
    chains = [(bi, d, h) for bi in range(nbat) for d in range(2) for h in range(nh)]
    slot = lambda bi, d, h: (bi * 2 + d) * nh + h
    states = [s_ref[slot(*ch)] for ch in chains]
    for step in range(nchunk):
        cur = [data[index[(bi, d, h, step if d == 0 else nchunk - 1 - step)]] for bi, d, h in chains]
        wss = []
        for p, state in zip(cur, states):
            wq = jnp.concatenate([p["uw"][:, B_DV:], p["q"].astype(f32) * p["egc"]], axis=0).astype(bf16)
            wss.append(jnp.dot(wq, state.astype(bf16), preferred_element_type=f32))
        v_news = [(p["uw"][:, :B_DV] - ws[:c]).astype(bf16) for p, ws in zip(cur, wss)]
        for (bi, d, h), p, ws, v_new in zip(chains, cur, wss, v_news):
            ci = step if d == 0 else nchunk - 1 - step
            dir_refs[d][6][bi, ci * c:(ci + 1) * c, h * B_DV:(h + 1) * B_DV] = ws[c:] + jnp.dot(p["attn"], v_new, preferred_element_type=f32)
        new_states = []
        for (bi, d, h), p, state, v_new in zip(chains, cur, states, v_news):
            last = c - 1 if d == 0 else 0
            gl = p["grow"][:, last:last + 1]
            kdt = (p["kt"].astype(f32) * jnp.exp(gl - p["grow"])).astype(bf16)
            new_states.append(jnp.exp(gl) * state + jnp.dot(kdt, v_new, preferred_element_type=f32))
        states = new_states
    for ch, state in zip(chains, states):
        s_ref[slot(*ch)] = state


def _gdn(q, k, kt, v, gc, gr):
    bsz, s, hd = q.shape
    t = min(T_GDN_STEP, s)
    nb = s // t
    bb = B_REC_STEP if bsz % B_REC_STEP == 0 else 1
    fwd, bwd, fwd_t, bwd_t = _mirror_maps(nb)
    def specs(im, im_t):
        return [pl.BlockSpec((bb, t, hd), im)] * 2 + [pl.BlockSpec((bb, hd, t), im_t), pl.BlockSpec((bb, t, hd), im),
                                                      pl.BlockSpec((bb, t, 16), im), pl.BlockSpec((bb, 16, t), im_t)]
    return pl.pallas_call(
        _gdn_kernel,
        grid=(bsz // bb, nb),
        in_specs=specs(fwd, fwd_t) + specs(bwd, bwd_t),
        out_specs=[pl.BlockSpec((bb, t, hd), fwd), pl.BlockSpec((bb, t, hd), bwd)],
        out_shape=[jax.ShapeDtypeStruct((bsz, s, hd), jnp.float32)] * 2,
        scratch_shapes=[pltpu.VMEM((bb * 2 * B_HEADS, B_DK, B_DV), jnp.float32)],
        compiler_params=pltpu.CompilerParams(dimension_semantics=("parallel", "arbitrary"), vmem_limit_bytes=VMEM_LIMIT),
        name="gdn",
    )(q, k, kt, v, gc, gr, q, k, kt, v, gc, gr)


def _split(p, sizes):
    return jnp.split(p, np.cumsum(sizes)[:-1].tolist(), axis=-1)


def _even_layer(x, norm_g, scale, shift, gate, w_in, m_gate_b, dn_dt_bias, dn_a_log, dn_conv_w, m_norm_g, dn_norm_g, w_out):
    bf16 = jnp.bfloat16
    mq, mk, mv, mo, mg, dqkv, dg, z = _split(w_in, EVEN_SPLITS)
    w = jnp.concatenate([mq, mv, dqkv], axis=1).astype(bf16)
    wkt = (mk * (A_DK ** -0.5)).T.astype(bf16)
    wg = jnp.concatenate([mg, dg], axis=1).astype(bf16)
    wz = jnp.concatenate([mo, z], axis=1).astype(bf16)
    pq, pkt, pv, bq, bk, bv, bkt, mc, mr, gc, gr, h = _inproj_even(x, norm_g, scale, shift, w, wkt, wg.T, dn_conv_w, m_gate_b, dn_a_log, dn_dt_bias)
    hf, hb = _mlstm(pq, pkt, pv, mc, mr)
    of, ob = _gdn(bq, bk, bkt, bv, gc, gr)
    return _outproj_even(hf, hb, of, ob, x, h, gate, m_norm_g, dn_norm_g, wz, w_out.astype(bf16))


def _odd_layer_final(x, norm_g, scale, shift, gate, w_in, dw_w, dw_b, ln_g, ln_b, rel_bias, w_out, final_g):
    bf16 = jnp.bfloat16
    ga, gb, aq, ak, av, z = _split(w_in, ODD_SPLITS)
    w = jnp.concatenate([ga, gb, aq * (D_DH ** -0.5), ak, av], axis=1).astype(bf16)
    out_c, h, pq, pk, pv = _inproj_odd(x, norm_g, scale, shift, w, dw_w, dw_b, ln_g, ln_b)
    og, lg = zip(*[_dilated_group_call(qd, kd, vd, rel_bias, dil) for qd, kd, vd, dil in zip(pq, pk, pv, DILATIONS)])
    return _outproj_odd_final(out_c, og, lg, x, h, gate, final_g, z.astype(bf16), w_out.astype(bf16))


def kernel(x, c, norm_g, ada_w, ada_b, ev_w_in, ev_m_gate_b, ev_dn_dt_bias, ev_dn_a_log, ev_dn_conv_w, ev_m_norm_g, ev_dn_norm_g, ev_w_out, od_w_in, od_dw_w, od_dw_b, od_ln_g, od_ln_b, od_w_out, rel_bias, final_g):
    assert DEPTH == 2, "the final RMSNorm is fused into the (last) odd layer's output projection"
    assert all(window // (2 * dil) == R_ATT for window, dil in D_GROUPS)
    d = x.shape[-1]
    mod = _adaln(c, ada_w, ada_b)
    for layer in range(DEPTH):
        shift, scale, gate = (mod[layer, :, i * d:(i + 1) * d][:, None, :] for i in range(3))
        j = layer // 2
        if layer % 2 == 0:
            x = _even_layer(x, norm_g[layer], scale, shift, gate, ev_w_in[j], ev_m_gate_b[j], ev_dn_dt_bias[j], ev_dn_a_log[j],
                            ev_dn_conv_w[j], ev_m_norm_g[j], ev_dn_norm_g[j], ev_w_out[j])
        else:
            x = _odd_layer_final(x, norm_g[layer], scale, shift, gate, od_w_in[j], od_dw_w[j], od_dw_b[j], od_ln_g[j], od_ln_b[j],
                                 rel_bias, od_w_out[j], final_g)
    return x
```

```python
import math

import jax
import jax.numpy as jnp
import numpy as np
from jax import lax
from jax.experimental import pallas as pl
from jax.experimental.pallas import tpu as pltpu

DEPTH = 2
A_HEADS = 4
A_DK = 64
A_DV = 128
B_HEADS = 4
B_DK = 128
B_DV = 128
B_CONV = 5
C_WIDTH = 512
C_CONV = 31
D_HEADS = 8
D_DH = 64
D_GROUPS = ((128, 1), (512, 4), (2048, 16))
REL_BUCKETS = 32
REL_MAX_DIST = 1024
EPS = 1e-6
NEG = -1e30
MIX_EVEN = A_HEADS * A_DV + B_HEADS * B_DV
MIX_ODD = C_WIDTH + D_HEADS * D_DH
B_QKV = B_HEADS * (2 * B_DK + B_DV)
EVEN_SPLITS = (A_HEADS * A_DK, A_HEADS * A_DK, A_HEADS * A_DV, A_HEADS * A_DV, 4 * A_HEADS, B_QKV, 4 * B_HEADS, MIX_EVEN)
ODD_SPLITS = (C_WIDTH, C_WIDTH, D_HEADS * D_DH, D_HEADS * D_DH, D_HEADS * D_DH, MIX_ODD)
DILATIONS = tuple(dil for _, dil in D_GROUPS)

LANES = 128
SUBLANES = 8
VMEM_LIMIT = 56 * 1024 * 1024
TM_PROJ = 1024
HALO_X = 16
SUB_C = 64
L_MLSTM = 256
C_GDN = 64
T_GDN_STEP = 256
B_REC_STEP = 2
TQ_ATT = 128
TB_ATT = 1024
R_ATT = 64

_EV_COLS = {"mq": (0, 256), "mv": (256, 768), "dqkv": (768, 2304)}
_OD_COLS = {"ga": (0, 512), "gb": (512, 1024), "aq": (1024, 1536), "ak": (1536, 2048), "av": (2048, 2560)}


def _log_sigmoid(t):
    return jnp.minimum(t, 0.0) - jnp.log(1.0 + jnp.exp(-jnp.abs(t)))


def _softplus(t):
    return jnp.maximum(t, 0.0) + jnp.log1p(jnp.exp(-jnp.abs(t)))


def _modulated_rms_val(x, g, scale, shift):
    y = x * lax.rsqrt(jnp.mean(x * x, axis=-1, keepdims=True) + EPS)
    return ((y * g) * (1.0 + scale) + shift).astype(jnp.bfloat16)


def _seg_scan_lanes(x, seg, reverse, op, fill):
    n = x.shape[1]
    pos = lax.broadcasted_iota(jnp.int32, x.shape, 1) % seg
    k = 1
    while k < seg:
        if reverse:
            x = op(x, jnp.where(pos < seg - k, pltpu.roll(x, n - k, axis=1), fill))
        else:
            x = op(x, jnp.where(pos >= k, pltpu.roll(x, k, axis=1), fill))
        k *= 2
    return x


def _seg_cumsum_lanes(x, seg, reverse):
    return _seg_scan_lanes(x, seg, reverse, jnp.add, 0.0)


def _seg_cummax_lanes(x, seg, reverse):
    return _seg_scan_lanes(x, seg, reverse, jnp.maximum, -jnp.inf)


def _conv_taps(xe, w_ref, cs, width, t, halo):
    n = t + 2 * halo
    acc = None
    for j in range(width):
        off = halo - width // 2 + j
        shifted = xe[off:off + t] if off % SUBLANES == 0 else pltpu.roll(xe, n - off, axis=0)[0:t]
        term = shifted * w_ref[j:j + 1, cs]
        acc = term if acc is None else acc + term
    return acc


def _zero_outside(d, t, halo, first, last):
    return jnp.concatenate([jnp.where(first, 0.0, d[:halo]), d[halo:halo + t], jnp.where(last, 0.0, d[halo + t:])], axis=0)


def _halo_specs(tm, s, d):
    hb = tm // HALO_X
    return [pl.BlockSpec((1, HALO_X, d), lambda b, i: (b, jnp.maximum(i * hb - 1, 0), 0)),
            pl.BlockSpec((1, HALO_X, d), lambda b, i: (b, jnp.minimum((i + 1) * hb, s // HALO_X - 1), 0))]


def _fuse_operands(n, which):
    return [i in which for i in range(n)]


def _mirror_maps(nblocks):
    return (lambda b, n: (b, n, 0), lambda b, n: (b, nblocks - 1 - n, 0),
            lambda b, n: (b, 0, n), lambda b, n: (b, 0, nblocks - 1 - n))


def _adaln_kernel(c_ref, w_ref, b_ref, o_ref):
    c = c_ref[...]
    cs = (c * jax.nn.sigmoid(c)).astype(jnp.bfloat16)
    o_ref[0] = jnp.dot(cs, w_ref[0].astype(jnp.bfloat16), preferred_element_type=jnp.float32) + b_ref[0]


def _adaln(c, ada_w, ada_b):
    depth, d, n3 = ada_w.shape
    bsz = c.shape[0]
    tn = 1024
    return pl.pallas_call(
        _adaln_kernel,
        grid=(depth, n3 // tn),
        in_specs=[pl.BlockSpec((bsz, d), lambda l, j: (0, 0)), pl.BlockSpec((1, d, tn), lambda l, j: (l, 0, j)),
                  pl.BlockSpec((1, 1, tn), lambda l, j: (l, 0, j))],
        out_specs=pl.BlockSpec((1, bsz, tn), lambda l, j: (l, 0, j)),
        out_shape=jax.ShapeDtypeStruct((depth, bsz, n3), jnp.float32),
        compiler_params=pltpu.CompilerParams(dimension_semantics=("parallel", "parallel")),
        name="adaln",
    )(c, ada_w, ada_b.reshape(depth, 1, n3))


def _inproj_even_kernel(x_ref, xp_ref, xn_ref, g_ref, sc_ref, sh_ref, w_ref, wkt_ref, wgt_ref, cw_ref, mb_ref, a_ref, dt_ref,
                        mq_ref, mkt_ref, mv_ref, bq_ref, bk_ref, bv_ref, bkt_ref, mc_ref, mr_ref, gc_ref, gr_ref, h_ref):
    i = pl.program_id(1)
    nt = pl.num_programs(1)
    tm = x_ref.shape[1]
    f32 = jnp.float32
    x_ext = jnp.concatenate([xp_ref[0], x_ref[0], xn_ref[0]], axis=0)
    h_ext = _modulated_rms_val(x_ext, g_ref[...], sc_ref[0], sh_ref[0])
    h = h_ext[HALO_X:HALO_X + tm]
    h_ref[0] = h
    xpose = (((1,), (1,)), ((), ()))

    def mlstm_operand(name, o_ref):
        lo, hi = _EV_COLS[name]
        o_ref[0] = jnp.dot(h, w_ref[:, lo:hi], preferred_element_type=f32).astype(o_ref.dtype)

    def gates():
        mkt_ref[0] = lax.dot_general(wkt_ref[...], h, xpose, preferred_element_type=f32).astype(mkt_ref.dtype)
        gates_t = lax.dot_general(wgt_ref[...], h, xpose, preferred_element_type=f32)
        na = A_HEADS
        gm = gates_t[:16] + mb_ref[...]
        logf = _log_sigmoid(gm[2 * na:])
        b_f, b_b = _seg_cumsum_lanes(logf[:na], L_MLSTM, False), _seg_cumsum_lanes(logf[na:], L_MLSTM, True)
        pm_f = _seg_cummax_lanes(gm[:na] - b_f, L_MLSTM, False)
        pm_b = _seg_cummax_lanes(gm[na:2 * na] - b_b, L_MLSTM, True)
        mr_ref[0] = jnp.concatenate([gm[:2 * na], b_f, b_b], axis=0)
        mc_ref[0] = jnp.concatenate([b_f, b_b, pm_f, pm_b], axis=0).T
        nh = B_HEADS
        dgt = gates_t[16:]
        dec = -jnp.exp(a_ref[...]) * _softplus(dgt[2 * nh:] + dt_ref[...])
        gr = jnp.concatenate([jax.nn.sigmoid(dgt[:2 * nh]), _seg_cumsum_lanes(dec[:nh], C_GDN, False),
                              _seg_cumsum_lanes(dec[nh:], C_GDN, True)], axis=0)
        gr_ref[0] = gr
        gc_ref[0] = gr.T

    mlstm_operand("mq", mq_ref)
    mlstm_operand("mv", mv_ref)
    gates()
    hd = B_HEADS * B_DK
    for part, o_ref in enumerate((bq_ref, bk_ref, bv_ref)):
        lo = _EV_COLS["dqkv"][0] + part * hd
        cs = slice(part * hd, (part + 1) * hd)
        d = jnp.dot(h_ext, w_ref[:, lo:lo + hd], preferred_element_type=f32)
        acc = _conv_taps(_zero_outside(d, tm, HALO_X, i == 0, i == nt - 1), cw_ref, cs, B_CONV, tm, HALO_X)
        y = acc * jax.nn.sigmoid(acc)
        for hh in range(B_HEADS):
            yh = y[:, hh * B_DK:(hh + 1) * B_DK]
            if part == 0:
                yh = yh * lax.rsqrt(jnp.sum(yh * yh, axis=-1, keepdims=True) + EPS) * (B_DK ** -0.5)
            elif part == 1:
                yh = yh * lax.rsqrt(jnp.sum(yh * yh, axis=-1, keepdims=True) + EPS)
            o_ref[0, :, hh * B_DK:(hh + 1) * B_DK] = yh.astype(o_ref.dtype)
            if part == 1:
                bkt_ref[0, hh * B_DK:(hh + 1) * B_DK, :] = yh.T.astype(bkt_ref.dtype)


def _inproj_even(x, g, scale, shift, w, wkt, wgt, conv_w, m_gate_b, a_log, dt_bias):
    bsz, s, d = x.shape
    tm = TM_PROJ
    assert tm % L_MLSTM == 0 and tm % C_GDN == 0
    tok = lambda b, i: (b, i, 0)
    tok_t = lambda b, i: (b, 0, i)
    const = lambda b, i: (0, 0)
    bvec = lambda b, i: (b, 0, 0)
    bf16, f32 = jnp.bfloat16, jnp.float32
    hk, hv, hd = A_HEADS * A_DK, A_HEADS * A_DV, B_HEADS * B_DK
    tok_specs = lambda wd: pl.BlockSpec((1, tm, wd), tok)
    gate_specs = [pl.BlockSpec((1, tm, 16), tok), pl.BlockSpec((1, 16, tm), tok_t)]
    gate_shapes = [jax.ShapeDtypeStruct((bsz, s, 16), f32), jax.ShapeDtypeStruct((bsz, 16, s), f32)]
    return pl.pallas_call(
        _inproj_even_kernel,
        grid=(bsz, s // tm),
        in_specs=[pl.BlockSpec((1, tm, d), tok)] + _halo_specs(tm, s, d) + [pl.BlockSpec((1, d), const), pl.BlockSpec((1, 1, d), bvec),
                  pl.BlockSpec((1, 1, d), bvec), pl.BlockSpec(w.shape, const), pl.BlockSpec(wkt.shape, const), pl.BlockSpec(wgt.shape, const),
                  pl.BlockSpec(conv_w.shape, const), pl.BlockSpec((16, 1), const), pl.BlockSpec((8, 1), const), pl.BlockSpec((8, 1), const)],
        out_specs=[tok_specs(hk), pl.BlockSpec((1, hk, tm), tok_t), tok_specs(hv), tok_specs(hd), tok_specs(hd), tok_specs(hd),
                   pl.BlockSpec((1, hd, tm), tok_t)] + gate_specs + gate_specs + [tok_specs(d)],
        out_shape=[jax.ShapeDtypeStruct((bsz, s, hk), bf16), jax.ShapeDtypeStruct((bsz, hk, s), bf16), jax.ShapeDtypeStruct((bsz, s, hv), bf16)]
        + [jax.ShapeDtypeStruct((bsz, s, hd), bf16)] * 3 + [jax.ShapeDtypeStruct((bsz, hd, s), bf16)] + gate_shapes + gate_shapes
        + [jax.ShapeDtypeStruct((bsz, s, d), bf16)],
        compiler_params=pltpu.CompilerParams(dimension_semantics=("parallel", "parallel"), vmem_limit_bytes=VMEM_LIMIT,
                                             allow_input_fusion=_fuse_operands(13, (6, 7, 8))),
        name="inproj_even",
    )(x, x, x, g.reshape(1, d), scale, shift, w, wkt, wgt, conv_w, m_gate_b.reshape(16, 1), a_log.reshape(8, 1), dt_bias.reshape(8, 1))


def _inproj_odd_kernel(x_ref, xp_ref, xn_ref, g_ref, sc_ref, sh_ref, w_ref, cw_ref, cb_ref, lg_ref, lb_ref, oc_ref, h_ref, *rest):
    out_refs, plane_ref = rest[:-1], rest[-1]
    i = pl.program_id(1)
    nt = pl.num_programs(1)
    f32 = jnp.float32
    tm = x_ref.shape[1]
    nd = D_HEADS * D_DH
    x_ext = jnp.concatenate([xp_ref[0], x_ref[0], xn_ref[0]], axis=0)
    h_ext = _modulated_rms_val(x_ext, g_ref[...], sc_ref[0], sh_ref[0])
    h = h_ext[HALO_X:HALO_X + tm]
    h_ref[0] = h
    dot = lambda name: jnp.dot(h, w_ref[:, _OD_COLS[name][0]:_OD_COLS[name][1]], preferred_element_type=f32)
    dot_ext = lambda name: jnp.dot(h_ext, w_ref[:, _OD_COLS[name][0]:_OD_COLS[name][1]], preferred_element_type=f32)
    xe = _zero_outside(dot_ext("ga") * jax.nn.sigmoid(dot_ext("gb")), tm, HALO_X, i == 0, i == nt - 1)
    n = tm + 2 * HALO_X
    phases = [xe] + [pltpu.roll(xe, n - b, axis=0) for b in range(1, SUBLANES)]
    for r0 in range(0, tm, SUB_C):
        acc = None
        for j in range(C_CONV):
            a, b = divmod(HALO_X - C_CONV // 2 + j, SUBLANES)
            lo = a * SUBLANES + r0
            term = phases[b][lo:lo + SUB_C] * cw_ref[j:j + 1, :]
            acc = term if acc is None else acc + term
        u = acc + cb_ref[...]
        uc = u - jnp.mean(u, axis=-1, keepdims=True)
        y = uc * lax.rsqrt(jnp.mean(uc * uc, axis=-1, keepdims=True) + EPS) * lg_ref[...] + lb_ref[...]
        oc_ref[0, r0:r0 + SUB_C, :] = y * jax.nn.sigmoid(y)
    for a, name in enumerate(("aq", "ak", "av")):
        r = dot(name)
        group_refs = out_refs[a * len(DILATIONS):(a + 1) * len(DILATIONS)]
        for j in range(nd // LANES):
            plane_ref[a, j] = r[:, j * LANES:(j + 1) * LANES]
        for dil, o_ref in zip(DILATIONS, group_refs):
            if dil == 1:
                o_ref[0, 0] = r.astype(o_ref.dtype)
                continue
            for res in range(dil):
                for j in range(nd // LANES):
                    o_ref[0, res, :, j * LANES:(j + 1) * LANES] = plane_ref[a, j, pl.ds(res, tm // dil, stride=dil), :].astype(o_ref.dtype)


def _inproj_odd(x, g, scale, shift, w, dw_w, dw_b, ln_g, ln_b):
    bsz, s, d = x.shape
    tm = TM_PROJ
    tok = lambda b, i: (b, i, 0)
    const = lambda b, i: (0, 0)
    bvec = lambda b, i: (b, 0, 0)
    bf16, f32 = jnp.bfloat16, jnp.float32
    nd = D_HEADS * D_DH
    cw = C_WIDTH
    att_specs = [pl.BlockSpec((1, dil, tm // dil, nd), lambda b, i: (b, 0, i, 0)) for dil in DILATIONS] * 3
    att_shapes = [jax.ShapeDtypeStruct((bsz, dil, s // dil, nd), bf16) for dil in DILATIONS] * 3
    outs = pl.pallas_call(
        _inproj_odd_kernel,
        grid=(bsz, s // tm),
        in_specs=[pl.BlockSpec((1, tm, d), tok)] + _halo_specs(tm, s, d) + [pl.BlockSpec((1, d), const), pl.BlockSpec((1, 1, d), bvec),
                  pl.BlockSpec((1, 1, d), bvec), pl.BlockSpec(w.shape, const), pl.BlockSpec((C_CONV, cw), const)]
        + [pl.BlockSpec((1, cw), const)] * 3,
        out_specs=[pl.BlockSpec((1, tm, cw), tok), pl.BlockSpec((1, tm, d), tok)] + att_specs,
        out_shape=[jax.ShapeDtypeStruct((bsz, s, cw), f32), jax.ShapeDtypeStruct((bsz, s, d), bf16)] + att_shapes,
        scratch_shapes=[pltpu.VMEM((3, nd // LANES, tm, LANES), f32)],
        compiler_params=pltpu.CompilerParams(dimension_semantics=("parallel", "parallel"), vmem_limit_bytes=VMEM_LIMIT,
                                             allow_input_fusion=_fuse_operands(11, (6,))),
        name="inproj_odd",
    )(x, x, x, g.reshape(1, d), scale, shift, w, dw_w, dw_b.reshape(1, cw), ln_g.reshape(1, cw), ln_b.reshape(1, cw))
    ng = len(DILATIONS)
    return outs[0], outs[1], outs[2:2 + ng], outs[2 + ng:2 + 2 * ng], outs[2 + 2 * ng:]


def _head_rms_cols(t, g, width):
    parts = []
    for h in range(t.shape[1] // width):
        th = t[:, h * width:(h + 1) * width]
        parts.append(th * lax.rsqrt(jnp.mean(th * th, axis=-1, keepdims=True) + EPS))
    return jnp.concatenate(parts, axis=1) * g


def _outproj_even_kernel(hf_ref, hb_ref, of_ref, ob_ref, x_ref, h_ref, gate_ref, mg_ref, dg_ref, wz_ref, w_ref, o_ref):
    f32, bf16 = jnp.float32, jnp.bfloat16
    na = A_HEADS * A_DV
    h = h_ref[0]
    mo = jnp.dot(h, wz_ref[:, :na], preferred_element_type=f32)
    z = jnp.dot(h, wz_ref[:, na:], preferred_element_type=f32)
    sz = z * jax.nn.sigmoid(z)
    out_a = jax.nn.sigmoid(mo) * _head_rms_cols(hf_ref[0].astype(f32) + hb_ref[0].astype(f32), mg_ref[...], A_DV)
    out_b = _head_rms_cols(of_ref[0].astype(f32) + ob_ref[0].astype(f32), dg_ref[...], B_DV)
    y = jnp.dot((out_a * sz[:, :na]).astype(bf16), w_ref[:na, :], preferred_element_type=f32)
    y = y + jnp.dot((out_b * sz[:, na:]).astype(bf16), w_ref[na:, :], preferred_element_type=f32)
    o_ref[0] = x_ref[0] + gate_ref[0] * y


def _outproj_even(hf, hb, of, ob, x, h, gate, m_norm_g, dn_norm_g, wz, w):
    bsz, s, d = x.shape
    tm = TM_PROJ
    tok = lambda b, i: (b, i, 0)
    const = lambda b, i: (0, 0)
    bvec = lambda b, i: (b, 0, 0)
    na, nb = A_HEADS * A_DV, B_HEADS * B_DV
    return pl.pallas_call(
        _outproj_even_kernel,
        grid=(bsz, s // tm),
        in_specs=[pl.BlockSpec((1, tm, na), tok)] * 2 + [pl.BlockSpec((1, tm, nb), tok)] * 2 + [pl.BlockSpec((1, tm, d), tok)] * 2
        + [pl.BlockSpec((1, 1, d), bvec), pl.BlockSpec((1, na), const), pl.BlockSpec((1, nb), const), pl.BlockSpec(wz.shape, const),
           pl.BlockSpec(w.shape, const)],
        out_specs=pl.BlockSpec((1, tm, d), tok),
        out_shape=jax.ShapeDtypeStruct((bsz, s, d), jnp.float32),
        compiler_params=pltpu.CompilerParams(dimension_semantics=("parallel", "parallel"), vmem_limit_bytes=VMEM_LIMIT,
                                             allow_input_fusion=_fuse_operands(11, (9, 10))),
        name="outproj_even",
    )(hf, hb, of, ob, x, h, gate, m_norm_g.reshape(1, na), dn_norm_g.reshape(1, nb), wz, w)


def _outproj_odd_kernel(oc_ref, o1_ref, o2_ref, o3_ref, l1_ref, l2_ref, l3_ref, x_ref, h_ref, gate_ref, fg_ref,
                        wz_ref, w_ref, o_ref, nat_ref):
    f32, bf16 = jnp.float32, jnp.bfloat16
    tm = x_ref.shape[1]
    npl = D_HEADS * D_DH // LANES
    z = jnp.dot(h_ref[0], wz_ref[...], preferred_element_type=f32)
    sz = z * jax.nn.sigmoid(z)
    groups = []
    for gi, (dil, og_ref, lg_ref) in enumerate(zip(DILATIONS, (o1_ref, o2_ref, o3_ref), (l1_ref, l2_ref, l3_ref))):
        if dil == 1:
            groups.append(([og_ref[0, 0, :, j * LANES:(j + 1) * LANES].astype(f32) for j in range(npl)], lg_ref[0, 0]))
            continue
        for res in range(dil):
            rows = pl.ds(res, tm // dil, stride=dil)
            for j in range(npl):
                nat_ref[gi, j, rows, :] = og_ref[0, res, :, j * LANES:(j + 1) * LANES].astype(f32)
            nat_ref[gi, npl, rows, :] = lg_ref[0, res]
        groups.append(([nat_ref[gi, j] for j in range(npl)], nat_ref[gi, npl]))
    (p1, l1), (p2, l2), (p3, l3) = groups
    lm = jnp.maximum(jnp.maximum(l1, l2), l3)
    e1, e2, e3 = jnp.exp(l1 - lm), jnp.exp(l2 - lm), jnp.exp(l3 - lm)
    inv = 1.0 / (e1 + e2 + e3)
    low = lax.broadcasted_iota(jnp.int32, (tm, LANES), 1) < D_DH
    planes = []
    for j in range(npl):
        acc = None
        for e, p in ((e1, p1), (e2, p2), (e3, p3)):
            wgt = e * inv
            term = jnp.where(low, wgt[:, 2 * j:2 * j + 1], wgt[:, 2 * j + 1:2 * j + 2]) * p[j]
            acc = term if acc is None else acc + term
        planes.append(acc)
    out_d = jnp.concatenate(planes, axis=1)
    y = jnp.dot((oc_ref[0] * sz[:, :C_WIDTH]).astype(bf16), w_ref[:C_WIDTH, :], preferred_element_type=f32)
    y = y + jnp.dot((out_d * sz[:, C_WIDTH:]).astype(bf16), w_ref[C_WIDTH:, :], preferred_element_type=f32)
    xn = x_ref[0] + gate_ref[0] * y
    o_ref[0] = xn * lax.rsqrt(jnp.mean(xn * xn, axis=-1, keepdims=True) + EPS) * fg_ref[...]


def _outproj_odd_final(oc, og, lg, x, h, gate, final_g, wz, w):
    bsz, s, d = x.shape
    tm = TM_PROJ
    tok = lambda b, i: (b, i, 0)
    const = lambda b, i: (0, 0)
    bvec = lambda b, i: (b, 0, 0)
    nd = D_HEADS * D_DH
    res_major = lambda width: [pl.BlockSpec((1, dil, tm // dil, width), lambda b, i: (b, 0, i, 0)) for dil in DILATIONS]
    return pl.pallas_call(
        _outproj_odd_kernel,
        grid=(bsz, s // tm),
        in_specs=[pl.BlockSpec((1, tm, C_WIDTH), tok)] + res_major(nd) + res_major(LANES)
        + [pl.BlockSpec((1, tm, d), tok), pl.BlockSpec((1, tm, d), tok), pl.BlockSpec((1, 1, d), bvec), pl.BlockSpec((1, d), const),
           pl.BlockSpec(wz.shape, const), pl.BlockSpec(w.shape, const)],
        out_specs=pl.BlockSpec((1, tm, d), tok),
        out_shape=jax.ShapeDtypeStruct((bsz, s, d), jnp.float32),
        scratch_shapes=[pltpu.VMEM((len(DILATIONS), nd // LANES + 1, tm, LANES), jnp.float32)],
        compiler_params=pltpu.CompilerParams(dimension_semantics=("parallel", "parallel"), vmem_limit_bytes=VMEM_LIMIT,
                                             allow_input_fusion=_fuse_operands(13, (11, 12))),
        name="outproj_odd",
    )(oc, *og, *lg, x, h, gate, final_g.reshape(1, d), wz, w)


def _dilated_kernel(q_ref, kc_ref, kp_ref, kn_ref, vc_ref, vp_ref, vn_ref, bias_ref, o_ref, lse_ref, kx_ref, vx_ref):
    i = pl.program_id(2)
    nt = pl.num_programs(2)
    tb = q_ref.shape[1]
    tq = TQ_ATT
    nk = tq + 2 * R_ATT
    nsub = tb // tq
    f32, bf16 = jnp.float32, jnp.bfloat16
    kx_ref[0:R_ATT, :] = kp_ref[0]
    kx_ref[R_ATT:R_ATT + tb, :] = kc_ref[0]
    kx_ref[R_ATT + tb:, :] = kn_ref[0]
    for pr in range(D_HEADS // 2):
        src, dst = slice(pr * LANES, (pr + 1) * LANES), slice(2 * pr * LANES, (2 * pr + 1) * LANES)
        vx_ref[0:R_ATT, dst] = vp_ref[0, :, src]
        vx_ref[R_ATT:R_ATT + tb, dst] = vc_ref[0, :, src]
        vx_ref[R_ATT + tb:, dst] = vn_ref[0, :, src]
        vx_ref[:, (2 * pr + 1) * LANES:(2 * pr + 2) * LANES] = jnp.ones((tb + 2 * R_ATT, LANES), bf16)
    kj = lax.broadcasted_iota(jnp.int32, (tq, nk), 1)
    lane = lax.broadcasted_iota(jnp.int32, (tq, LANES), 1)
    low = lane < D_DH
    heads = [(pr, hi) for pr in range(D_HEADS // 2) for hi in (False, True)]
    for sub in range(nsub):
        qs = slice(sub * tq, (sub + 1) * tq)
        ks = slice(sub * tq, sub * tq + nk)
        outside = None
        if sub == 0:
            outside = (kj < R_ATT) & (i == 0)
        if sub == nsub - 1:
            after = (kj >= R_ATT + tq) & (i == nt - 1)
            outside = after if outside is None else outside | after
        scs = []
        for pr, hi in heads:
            ps = slice(pr * LANES, (pr + 1) * LANES)
            qp = q_ref[0, qs, ps]
            qh = jnp.where(low != hi, qp, jnp.zeros_like(qp))
            sc = lax.dot_general(qh, kx_ref[ks, ps], (((1,), (1,)), ((), ())), preferred_element_type=f32) + bias_ref[2 * pr + int(hi)]
            scs.append(sc if outside is None else jnp.where(outside, NEG, sc))
        ms = [jnp.max(sc, axis=-1, keepdims=True) for sc in scs]
        ps_ = [jnp.exp(sc - m) for sc, m in zip(scs, ms)]
        pvs = [jnp.dot(p.astype(bf16), vx_ref[ks, 2 * pr * LANES:(2 * pr + 2) * LANES], preferred_element_type=f32) for (pr, _), p in zip(heads, ps_)]
        lse_all = jnp.zeros((tq, LANES), f32)
        for pr in range(D_HEADS // 2):
            lo, hi = 2 * pr, 2 * pr + 1
            (num_lo, den_lo), (num_hi, den_hi) = ((pvs[h][:, :LANES], pvs[h][:, LANES:]) for h in (lo, hi))
            o_ref[0, qs, pr * LANES:(pr + 1) * LANES] = jnp.where(low, num_lo / den_lo, num_hi / den_hi).astype(o_ref.dtype)
            lse_all = jnp.where(lane == lo, ms[lo] + jnp.log(den_lo), lse_all)
            lse_all = jnp.where(lane == hi, ms[hi] + jnp.log(den_hi), lse_all)
        lse_ref[0, qs, :] = lse_all


def _dilated_bias(rel_bias, dilation, tq):
    half = REL_BUCKETS // 2
    exact = half // 2
    qi = jnp.arange(tq)[:, None]
    kj = jnp.arange(tq + 2 * R_ATT)[None, :]
    rel = kj - R_ATT - qi
    reld = rel * dilation
    n = jnp.abs(reld)
    large = exact + (jnp.log(jnp.maximum(n, 1).astype(jnp.float32) / exact) / math.log(REL_MAX_DIST / exact) * (half - exact)).astype(jnp.int32)
    large = jnp.minimum(large, half - 1)
    bucket = (reld > 0).astype(jnp.int32) * half + jnp.where(n < exact, n, large)
    bias = jnp.zeros((rel_bias.shape[1],) + bucket.shape, jnp.float32)
    for b in range(REL_BUCKETS):
        bias = jnp.where((bucket == b)[None], rel_bias[b].astype(jnp.float32)[:, None, None], bias)
    return jnp.where((jnp.abs(rel) <= R_ATT)[None], bias, NEG)


def _dilated_group_call(q, k, v, rel_bias, dilation):
    bsz, dil, ls, nd = q.shape
    assert dil == dilation
    tb, tq = min(TB_ATT, ls), TQ_ATT
    assert ls % tb == 0 and tb % tq == 0
    nt = ls // tb
    hb = tb // R_ATT
    nk = tq + 2 * R_ATT
    cur = lambda b, r, i: (b, r, i, 0)
    prev = lambda b, r, i: (b, r, jnp.maximum(i * hb - 1, 0), 0)
    nxt = lambda b, r, i: (b, r, jnp.minimum((i + 1) * hb, ls // R_ATT - 1), 0)
    kv_specs = [pl.BlockSpec((1, None, tb, nd), cur), pl.BlockSpec((1, None, R_ATT, nd), prev), pl.BlockSpec((1, None, R_ATT, nd), nxt)]
    return pl.pallas_call(
        _dilated_kernel,
        grid=(bsz, dilation, nt),
        in_specs=[pl.BlockSpec((1, None, tb, nd), cur)] + kv_specs + kv_specs + [pl.BlockSpec((D_HEADS, tq, nk), lambda b, r, i: (0, 0, 0))],
        out_specs=[pl.BlockSpec((1, None, tb, nd), cur), pl.BlockSpec((1, None, tb, LANES), cur)],
        out_shape=[jax.ShapeDtypeStruct((bsz, dilation, ls, nd), jnp.bfloat16), jax.ShapeDtypeStruct((bsz, dilation, ls, LANES), jnp.float32)],
        scratch_shapes=[pltpu.VMEM((tb + 2 * R_ATT, nd), jnp.bfloat16), pltpu.VMEM((tb + 2 * R_ATT, 2 * nd), jnp.bfloat16)],
        compiler_params=pltpu.CompilerParams(dimension_semantics=("parallel", "parallel", "parallel"), vmem_limit_bytes=VMEM_LIMIT),
        name=f"dilated_d{dilation}",
    )(q, k, k, k, v, v, v, _dilated_bias(rel_bias, dilation, tq))


def _mlstm_kernel(qf_ref, kf_ref, vf_ref, gf_ref, gtf_ref, qb_ref, kb_ref, vb_ref, gb_ref, gtb_ref,
                  hf_ref, hb_ref, c_ref, m_ref):
    @pl.when(pl.program_id(1) == 0)
    def _():
        c_ref[...] = jnp.zeros_like(c_ref)
        m_ref[...] = jnp.zeros_like(m_ref)

    nbat, ln = qf_ref.shape[0], qf_ref.shape[1]
    f32, bf16 = jnp.float32, jnp.bfloat16
    row = lax.broadcasted_iota(jnp.int32, (ln, ln), 0)
    col = lax.broadcasted_iota(jnp.int32, (ln, ln), 1)
    ones_blk = jnp.ones((ln, A_DV), bf16)
    fwd_refs, bwd_refs = (qf_ref, kf_ref, vf_ref, gf_ref, gtf_ref), (qb_ref, kb_ref, vb_ref, gb_ref, gtb_ref)
    probs = []
    na = A_HEADS
    for bi in range(nbat):
        for d, (q_ref, kt_ref, v_ref, gc_ref, gr_ref), h_ref in ((0, fwd_refs, hf_ref), (1, bwd_refs, hb_ref)):
            mask = (row >= col) if d == 0 else (row <= col)
            for h in range(na):
                r = d * na + h
                probs.append(dict(
                    bi=bi, slot=bi * 2 * na + r, h=h, mask=mask, last=ln - 1 if d == 0 else 0, h_ref=h_ref,
                    q=q_ref[bi, :, h * A_DK:(h + 1) * A_DK],
                    kt=kt_ref[bi, h * A_DK:(h + 1) * A_DK, :],
                    vaug=jnp.concatenate([v_ref[bi, :, h * A_DV:(h + 1) * A_DV], ones_blk], axis=1),
                    bcol=gc_ref[bi, :, r:r + 1], pmcol=gc_ref[bi, :, 2 * na + r:2 * na + r + 1],
                    brow=gr_ref[bi, 2 * na + r:2 * na + r + 1, :],
                    vrow=gr_ref[bi, r:r + 1, :] - gr_ref[bi, 2 * na + r:2 * na + r + 1, :]))
    for p in probs:
        p["m_old"] = m_ref[p["slot"]:p["slot"] + 1, 0:1]
        p["caug"] = c_ref[p["slot"]]
        p["qk"] = jnp.dot(p["q"], p["kt"], preferred_element_type=f32)
    for p in probs:
        p["qc"] = jnp.dot(p["q"], p["caug"].astype(bf16), preferred_element_type=f32)
    for p in probs:
        mstab = jnp.maximum(p["m_old"], p["pmcol"])
        p["w_int"] = jnp.exp(p["m_old"] - mstab)
        p["emt"] = jnp.exp(-(p["bcol"] + mstab))
        p["sc"] = (jnp.exp(jnp.where(p["mask"], p["vrow"] - mstab, -jnp.inf)) * p["qk"]).astype(bf16)
    for p in probs:
        tot = p["w_int"] * p["qc"] + jnp.dot(p["sc"], p["vaug"], preferred_element_type=f32)
        den = jnp.maximum(jnp.abs(tot[:, A_DV:]), p["emt"])
        p["h_ref"][p["bi"], :, p["h"] * A_DV:(p["h"] + 1) * A_DV] = (tot[:, :A_DV] / den).astype(p["h_ref"].dtype)
    for p in probs:
        last, brow, vrow = p["last"], p["brow"], p["vrow"]
        btot = brow[:, last:last + 1]
        m_new = btot + jnp.maximum(p["m_old"], jnp.max(vrow, axis=-1, keepdims=True))
        w_old = jnp.exp(btot + p["m_old"] - m_new)
        kwt = (p["kt"].astype(f32) * jnp.exp(btot + vrow - m_new)).astype(bf16)
        c_ref[p["slot"]] = w_old * p["caug"] + jnp.dot(kwt, p["vaug"], preferred_element_type=f32)
        m_ref[p["slot"]:p["slot"] + 1, :] = jnp.broadcast_to(m_new, (1, m_ref.shape[1]))


def _mlstm(q, kt, v, gc, gr):
    bsz, s, _ = q.shape
    ln = min(L_MLSTM, s)
    nc = s // ln
    bb = B_REC_STEP if bsz % B_REC_STEP == 0 else 1
    hk, hv = A_HEADS * A_DK, A_HEADS * A_DV
    fwd, bwd, fwd_t, bwd_t = _mirror_maps(nc)
    def specs(im, im_t):
        return [pl.BlockSpec((bb, ln, hk), im), pl.BlockSpec((bb, hk, ln), im_t), pl.BlockSpec((bb, ln, hv), im),
                pl.BlockSpec((bb, ln, 16), im), pl.BlockSpec((bb, 16, ln), im_t)]
    return pl.pallas_call(
        _mlstm_kernel,
        grid=(bsz // bb, nc),
        in_specs=specs(fwd, fwd_t) + specs(bwd, bwd_t),
        out_specs=[pl.BlockSpec((bb, ln, hv), fwd), pl.BlockSpec((bb, ln, hv), bwd)],
        out_shape=[jax.ShapeDtypeStruct((bsz, s, hv), jnp.bfloat16)] * 2,
        scratch_shapes=[pltpu.VMEM((bb * 2 * A_HEADS, A_DK, 2 * A_DV), jnp.float32), pltpu.VMEM((bb * 2 * A_HEADS, LANES), jnp.float32)],
        compiler_params=pltpu.CompilerParams(dimension_semantics=("parallel", "arbitrary"), vmem_limit_bytes=VMEM_LIMIT),
        name="mlstm",
    )(q, kt, v, gc, gr, q, kt, v, gc, gr)


def _half_rows(x, s, odd):
    return jnp.concatenate([x[i * s:(i + 1) * s] for i in range(x.shape[0] // s) if (i % 2 == 1) == odd], axis=0)


def _with_half_rows(sel, rest, s, odd):
    n = 2 * sel.shape[0]
    blocks = []
    for i in range(n // s):
        if (i % 2 == 1) == odd:
            blocks.append(sel[(i // 2) * s:(i // 2 + 1) * s])
        else:
            blocks.append(jnp.zeros((s, sel.shape[1]), sel.dtype) if rest is None else rest[i * s:(i + 1) * s])
    return jnp.concatenate(blocks, axis=0)


def _tri_inverse_many(a_list, lowers, masks):
    eye, m16, m32, m64 = masks
    f32, bf16 = jnp.float32, jnp.bfloat16
    mm = lambda x, y: jnp.dot(x.astype(bf16), y.astype(bf16), preferred_element_type=f32)
    ads = [jnp.where(m16, a, 0.0) for a in a_list]
    xs = [eye - ad for ad in ads]
    ps = [mm(ad, ad) for ad in ads]
    for stage in range(3):
        xs = [x + mm(x, p) for x, p in zip(xs, ps)]
        if stage < 2:
            ps = [mm(p, p) for p in ps]
    for s, lo, hi in ((16, m16, m32), (32, m32, m64)):
        off = hi & ~lo
        ys = [mm(_half_rows(jnp.where(off, a, 0.0), s, low), x) for a, low, x in zip(a_list, lowers, xs)]
        upd = [_half_rows(x, s, low) - mm(_half_rows(x, s, low), _with_half_rows(y, None, s, low))
               for x, low, y in zip(xs, lowers, ys)]
        xs = [_with_half_rows(u, x, s, low) for u, x, low in zip(upd, xs, lowers)]
    return xs


def _gdn_kernel(qf_ref, kf_ref, ktf_ref, vf_ref, gcf_ref, grf_ref, qb_ref, kb_ref, ktb_ref, vb_ref, gcb_ref, grb_ref,
                of_ref, ob_ref, s_ref):
    @pl.when(pl.program_id(1) == 0)
    def _():
        s_ref[...] = jnp.zeros_like(s_ref)

    nbat, t = qf_ref.shape[0], qf_ref.shape[1]
    c = C_GDN
    f32, bf16 = jnp.float32, jnp.bfloat16
    row = lax.broadcasted_iota(jnp.int32, (c, c), 0)
    col = lax.broadcasted_iota(jnp.int32, (c, c), 1)
    eye = (row == col).astype(f32)
    blk = lambda w: (row // w) == (col // w)
    masks = (eye, blk(16), blk(32), blk(64))
    nh, nchunk = B_HEADS, t // c
    dir_refs = ((qf_ref, kf_ref, ktf_ref, vf_ref, gcf_ref, grf_ref, of_ref), (qb_ref, kb_ref, ktb_ref, vb_ref, gcb_ref, grb_ref, ob_ref))
    probs = [(bi, d, h, ci) for bi in range(nbat) for d in range(2) for h in range(nh) for ci in range(nchunk)]
    xpose = (((1,), (1,)), ((), ()))

    def load(bi, d, h, ci):
        q_ref, k_ref, kt_ref, v_ref, gc_ref, gr_ref, _ = dir_refs[d]
        rs, cs = slice(ci * c, (ci + 1) * c), slice(h * B_DK, (h + 1) * B_DK)
        return dict(
            q=q_ref[bi, rs, cs], k=k_ref[bi, rs, cs], kt=kt_ref[bi, cs, rs], v=v_ref[bi, rs, cs],
            beta=gc_ref[bi, rs, d * nh + h:d * nh + h + 1],
            gcol=gc_ref[bi, rs, (2 + d) * nh + h:(2 + d) * nh + h + 1],
            grow=gr_ref[bi, (2 + d) * nh + h:(2 + d) * nh + h + 1, rs])

    data = [load(*p) for p in probs]
    for (_, d, _, _), p in zip(probs, data):
        incl = (row >= col) if d == 0 else (row <= col)
        p["gam"] = jnp.exp(jnp.where(incl, p["gcol"] - p["grow"], -jnp.inf))
    for p in data:
        p["kk"] = lax.dot_general(p["k"], p["k"], xpose, preferred_element_type=f32)
    for p in data:
        p["qk"] = lax.dot_general(p["q"], p["k"], xpose, preferred_element_type=f32)
    a_list = []
    for (_, d, _, _), p in zip(probs, data):
        strict = (row > col) if d == 0 else (row < col)
        a_list.append(jnp.where(strict, p["beta"] * p["kk"] * p["gam"], 0.0))
    tinvs = _tri_inverse_many(a_list, [d == 0 for (_, d, _, _) in probs], masks)
    for p, tinv in zip(data, tinvs):
        p["egc"] = jnp.exp(p["gcol"])
        rhs = jnp.concatenate([p["beta"] * p["v"].astype(f32), (p["beta"] * p["egc"]) * p["k"].astype(f32)], axis=1).astype(bf16)
        p["uw"] = jnp.dot(tinv.astype(bf16), rhs, preferred_element_type=f32)
        p["attn"] = (p["qk"] * p["gam"]).astype(bf16)
    index = {p: i for i, p in enumerate(probs)}
    chains = [(bi, d, h) for bi in range(nbat) for d in range(2) for h in range(nh)]
    slot = lambda bi, d, h: (bi * 2 + d) * nh + h
    states = [s_ref[slot(*ch)] for ch in chains]
    for step in range(nchunk):
        cur = [data[index[(bi, d, h, step if d == 0 else nchunk - 1 - step)]] for bi, d, h in chains]
        wss = []
        for p, state in zip(cur, states):
            wq = jnp.concatenate([p["uw"][:, B_DV:], p["q"].astype(f32) * p["egc"]], axis=0).astype(bf16)
            wss.append(jnp.dot(wq, state.astype(bf16), preferred_element_type=f32))
        v_news = [(p["uw"][:, :B_DV] - ws[:c]).astype(bf16) for p, ws in zip(cur, wss)]
        for (bi, d, h), p, ws, v_new in zip(chains, cur, wss, v_news):
            ci = step if d == 0 else nchunk - 1 - step
            o = ws[c:] + jnp.dot(p["attn"], v_new, preferred_element_type=f32)
            dir_refs[d][6][bi, ci * c:(ci + 1) * c, h * B_DV:(h + 1) * B_DV] = o.astype(dir_refs[d][6].dtype)
        new_states = []
        for (bi, d, h), p, state, v_new in zip(chains, cur, states, v_news):
            last = c - 1 if d == 0 else 0
            gl = p["grow"][:, last:last + 1]
            kdt = (p["kt"].astype(f32) * jnp.exp(gl - p["grow"])).astype(bf16)
            new_states.append(jnp.exp(gl) * state + jnp.dot(kdt, v_new, preferred_element_type=f32))
        states = new_states
    for ch, state in zip(chains, states):
        s_ref[slot(*ch)] = state


def _gdn(q, k, kt, v, gc, gr):
    bsz, s, hd = q.shape
    t = min(T_GDN_STEP, s)
    nb = s // t
    bb = B_REC_STEP if bsz % B_REC_STEP == 0 else 1
    fwd, bwd, fwd_t, bwd_t = _mirror_maps(nb)
    def specs(im, im_t):
        return [pl.BlockSpec((bb, t, hd), im)] * 2 + [pl.BlockSpec((bb, hd, t), im_t), pl.BlockSpec((bb, t, hd), im),
                                                      pl.BlockSpec((bb, t, 16), im), pl.BlockSpec((bb, 16, t), im_t)]
    return pl.pallas_call(
        _gdn_kernel,
        grid=(bsz // bb, nb),
        in_specs=specs(fwd, fwd_t) + specs(bwd, bwd_t),
        out_specs=[pl.BlockSpec((bb, t, hd), fwd), pl.BlockSpec((bb, t, hd), bwd)],
        out_shape=[jax.ShapeDtypeStruct((bsz, s, hd), jnp.bfloat16)] * 2,
        scratch_shapes=[pltpu.VMEM((bb * 2 * B_HEADS, B_DK, B_DV), jnp.float32)],
        compiler_params=pltpu.CompilerParams(dimension_semantics=("parallel", "arbitrary"), vmem_limit_bytes=VMEM_LIMIT),
        name="gdn",
    )(q, k, kt, v, gc, gr, q, k, kt, v, gc, gr)


def _split(p, sizes):
    return jnp.split(p, np.cumsum(sizes)[:-1].tolist(), axis=-1)


def _even_layer(x, norm_g, scale, shift, gate, w_in, m_gate_b, dn_dt_bias, dn_a_log, dn_conv_w, m_norm_g, dn_norm_g, w_out):
    bf16 = jnp.bfloat16
    mq, mk, mv, mo, mg, dqkv, dg, z = _split(w_in, EVEN_SPLITS)
    w = jnp.concatenate([mq, mv, dqkv], axis=1).astype(bf16)
    wkt = (mk * (A_DK ** -0.5)).T.astype(bf16)
    wg = jnp.concatenate([mg, dg], axis=1).astype(bf16)
    wz = jnp.concatenate([mo, z], axis=1).astype(bf16)
    pq, pkt, pv, bq, bk, bv, bkt, mc, mr, gc, gr, h = _inproj_even(x, norm_g, scale, shift, w, wkt, wg.T, dn_conv_w, m_gate_b, dn_a_log, dn_dt_bias)
    hf, hb = _mlstm(pq, pkt, pv, mc, mr)
    of, ob = _gdn(bq, bk, bkt, bv, gc, gr)
    return _outproj_even(hf, hb, of, ob, x, h, gate, m_norm_g, dn_norm_g, wz, w_out.astype(bf16))


def _odd_layer_final(x, norm_g, scale, shift, gate, w_in, dw_w, dw_b, ln_g, ln_b, rel_bias, w_out, final_g):
    bf16 = jnp.bfloat16
    ga, gb, aq, ak, av, z = _split(w_in, ODD_SPLITS)
    w = jnp.concatenate([ga, gb, aq * (D_DH ** -0.5), ak, av], axis=1).astype(bf16)
    out_c, h, pq, pk, pv = _inproj_odd(x, norm_g, scale, shift, w, dw_w, dw_b, ln_g, ln_b)
    og, lg = zip(*[_dilated_group_call(qd, kd, vd, rel_bias, dil) for qd, kd, vd, dil in zip(pq, pk, pv, DILATIONS)])
    return _outproj_odd_final(out_c, og, lg, x, h, gate, final_g, z.astype(bf16), w_out.astype(bf16))


def kernel(x, c, norm_g, ada_w, ada_b, ev_w_in, ev_m_gate_b, ev_dn_dt_bias, ev_dn_a_log, ev_dn_conv_w, ev_m_norm_g, ev_dn_norm_g, ev_w_out, od_w_in, od_dw_w, od_dw_b, od_ln_g, od_ln_b, od_w_out, rel_bias, final_g):
    assert DEPTH == 2, "the final RMSNorm is fused into the (last) odd layer's output projection"
    assert all(window // (2 * dil) == R_ATT for window, dil in D_GROUPS)
    d = x.shape[-1]
    mod = _adaln(c, ada_w, ada_b)
    for layer in range(DEPTH):
        shift, scale, gate = (mod[layer, :, i * d:(i + 1) * d][:, None, :] for i in range(3))
        j = layer // 2
        if layer % 2 == 0:
            x = _even_layer(x, norm_g[layer], scale, shift, gate, ev_w_in[j], ev_m_gate_b[j], ev_dn_dt_bias[j], ev_dn_a_log[j],
                            ev_dn_conv_w[j], ev_m_norm_g[j], ev_dn_norm_g[j], ev_w_out[j])
        else:
            x = _odd_layer_final(x, norm_g[layer], scale, shift, gate, od_w_in[j], od_dw_w[j], od_dw_b[j], od_ln_g[j], od_ln_b[j],
                                 rel_bias, od_w_out[j], final_g)
    return x
```
